```python
import math
import jax
import jax.numpy as jnp
from jax import lax
import numpy as np

D_MODEL = 1024
BATCH = 2
SEQ = 16384
DEPTH = 2

GRID_W = 64
CTX_LEN = 256
EPS = 1e-6
ROPE_BASE = 10000.0

NA_HEADS = 4
NA_HEAD_DIM = 64
NA_WIN_R = 8
NA_WIN_C = 16
NA_W = NA_HEADS * NA_HEAD_DIM

DIFF_HEADS = 4
DIFF_HEAD_DIM = 32
DIFF_QK_W = DIFF_HEADS * 2 * DIFF_HEAD_DIM
DIFF_V_W = DIFF_HEADS * 2 * DIFF_HEAD_DIM
DIFF_BLOCK = 128

POOL_WINDOWS = (2, 4, 8, 16)
POOL_GROUPS = len(POOL_WINDOWS)
POOL_GROUP_W = 64
POOL_W = POOL_GROUPS * POOL_GROUP_W

FNET_W = 256

N_BRANCH = 4
BRANCH_W = 256

OFF_NA_Q = 0
OFF_DF_Q = OFF_NA_Q + NA_W
OFF_POOL = OFF_DF_Q + DIFF_QK_W
OFF_FNET = OFF_POOL + POOL_W
OFF_GATE = OFF_FNET + FNET_W
OFF_KV = OFF_GATE + N_BRANCH * D_MODEL
KV_W = 2 * NA_W + DIFF_QK_W + DIFF_V_W
IN_COLS = OFF_KV + KV_W

N_EXPERTS = 16
EC_FACTOR = 2
EXPERT_FF = 1408

kernel_name = "hybrid_gated_branch_diffusion_block"


def _rmsnorm(x, g):
    x32 = x.astype(jnp.float32)
    y = x32 * lax.rsqrt(jnp.mean(x32 * x32, axis=-1, keepdims=True) + EPS)
    return (y * g.astype(jnp.float32)).astype(x.dtype)


def _modulate(h, shift, scale):
    return h * (1 + scale) + shift


def _rope_tables(pos, dim):
    half = dim // 2
    inv = ROPE_BASE ** (-jnp.arange(half, dtype=jnp.float32) / half)
    ang = pos.astype(jnp.float32)[:, None] * inv[None, :]
    return jnp.cos(ang), jnp.sin(ang)


def _rotate_half(x, cos, sin):
    half = x.shape[-1] // 2
    x1, x2 = x[..., :half], x[..., half:]
    return jnp.concatenate([x1 * cos - x2 * sin, x1 * sin + x2 * cos], axis=-1)


def _axial_rope(x, rope):
    cos_r, sin_r, cos_c, sin_c = rope
    a = x.shape[-1] // 2
    e = lambda z: z[:, None, None, :]
    x32 = x.astype(jnp.float32)
    y = jnp.concatenate([_rotate_half(x32[..., :a], e(cos_r), e(sin_r)),
                         _rotate_half(x32[..., a:], e(cos_c), e(sin_c))], axis=-1)
    return y.astype(x.dtype)


def _split_q(p, na_q_g, df_q_g):
    B, N, _ = p.shape
    na_q = _rmsnorm(p[..., OFF_NA_Q:OFF_DF_Q].reshape(B, N, NA_HEADS, NA_HEAD_DIM), na_q_g)
    df_q = _rmsnorm(p[..., OFF_DF_Q:OFF_POOL].reshape(B, N, DIFF_HEADS, 2, DIFF_HEAD_DIM), df_q_g)
    return na_q, df_q, p[..., OFF_POOL:OFF_FNET], p[..., OFF_FNET:OFF_GATE], p[..., OFF_GATE:OFF_KV]


def _split_kv(kv, na_k_g, df_k_g):
    B, N, _ = kv.shape
    na_k = _rmsnorm(kv[..., :NA_W].reshape(B, N, NA_HEADS, NA_HEAD_DIM), na_k_g)
    na_v = kv[..., NA_W:2 * NA_W].reshape(B, N, NA_HEADS, NA_HEAD_DIM)
    o = 2 * NA_W
    df_k = _rmsnorm(kv[..., o:o + DIFF_QK_W].reshape(B, N, DIFF_HEADS, 2, DIFF_HEAD_DIM), df_k_g)
    df_v = kv[..., o + DIFF_QK_W:].reshape(B, N, DIFF_HEADS, 2 * DIFF_HEAD_DIM)
    return na_k, na_v, df_k, df_v


def _dense_attn(q, k, v):
    B, Q, H, d = q.shape
    s = jnp.einsum('bqhd,bkhd->bhqk', q, k).astype(jnp.float32) * (d ** -0.5)
    p = jax.nn.softmax(s, axis=-1).astype(v.dtype)
    return jnp.einsum('bhqk,bkhd->bqhd', p, v).reshape(B, Q, H * d)


def _na_latent(q, k, v, kc, vc, rpb):
    B, T, H, dh = q.shape
    rows = T // GRID_W
    kr = min(NA_WIN_R, rows)
    scale = dh ** -0.5
    qg = q.reshape(B, rows, GRID_W, H, dh).transpose(1, 0, 3, 2, 4)
    kg = k.reshape(B, rows, GRID_W, H, dh).transpose(0, 3, 1, 2, 4)
    vg = v.reshape(B, rows, GRID_W, H, dh).transpose(0, 3, 1, 2, 4)
    kct = kc.transpose(0, 2, 1, 3)
    vct = vc.transpose(0, 2, 1, 3)
    col = jnp.arange(GRID_W)
    cs = jnp.clip(col - NA_WIN_C // 2, 0, GRID_W - NA_WIN_C)
    col_idx = cs[:, None] + jnp.arange(NA_WIN_C)[None, :]
    ci = col_idx - col[:, None] + (NA_WIN_C - 1)
    n_loc = kr * NA_WIN_C

    def row_block(args):
        r, qr = args
        rs = jnp.clip(r - kr // 2, 0, rows - kr)
        kw = lax.dynamic_slice_in_dim(kg, rs, kr, axis=2)[:, :, :, col_idx]
        vw = lax.dynamic_slice_in_dim(vg, rs, kr, axis=2)[:, :, :, col_idx]
        ri = rs + jnp.arange(kr) - r + (NA_WIN_R - 1)
        bias = rpb[:, ri[None, :, None], ci[:, None, :]]
        s_loc = (jnp.einsum('bhqd,bhiqjd->bhqij', qr, kw).astype(jnp.float32) * scale
                 + bias.astype(jnp.float32)[None]).reshape(B, H, GRID_W, n_loc)
        s_ctx = jnp.einsum('bhqd,bhkd->bhqk', qr, kct).astype(jnp.float32) * scale
        p = jax.nn.softmax(jnp.concatenate([s_loc, s_ctx], axis=-1), axis=-1).astype(qr.dtype)
        p_loc = p[..., :n_loc].reshape(B, H, GRID_W, kr, NA_WIN_C)
        return (jnp.einsum('bhqij,bhiqjd->bhqd', p_loc, vw)
                + jnp.einsum('bhqk,bhkd->bhqd', p[..., n_loc:], vct))

    o = lax.map(row_block, (jnp.arange(rows), qg))
    return o.transpose(1, 0, 3, 2, 4).reshape(B, T, H * dh)


def _diff_core(qh, kh, vh, lam):
    scale = qh.shape[-1] ** -0.5
    s = jnp.einsum('bhmqd,bhmkd->bhmqk', qh, kh).astype(jnp.float32) * scale
    p = jax.nn.softmax(s, axis=-1)
    a = (p[:, :, 0] - lam * p[:, :, 1]).astype(vh.dtype)
    return jnp.einsum('bhqk,bhkd->bhqd', a, vh)


def _diff_latent(q, k, v, kc, vc, lam):
    B, T, H, _, dq = q.shape
    nblk = T // DIFF_BLOCK
    kall = jnp.concatenate([k, kc], axis=1).transpose(0, 2, 3, 1, 4)
    vall = jnp.concatenate([v, vc], axis=1).transpose(0, 2, 1, 3)
    qb = q.reshape(B, nblk, DIFF_BLOCK, H, 2, dq).transpose(1, 0, 3, 4, 2, 5)
    o = lax.map(lambda qi: _diff_core(qi, kall, vall, lam), qb)
    return o.transpose(1, 2, 0, 3, 4).reshape(B, H, T, 2 * dq)


def _diff_post(o, sub_g, lam_init):
    B, H, N, dv = o.shape
    o = _rmsnorm(o, sub_g) * (1.0 - lam_init)
    return o.transpose(0, 2, 1, 3).reshape(B, N, H * dv)


def _pool_mixer(u, pool_w, pool_scale):
    N = u.shape[1]
    u32 = u.astype(jnp.float32)
    csum = jnp.concatenate([jnp.zeros_like(u32[:, :1]), jnp.cumsum(u32, axis=1)], axis=1)
    t = jnp.arange(N)
    outs = []
    for gi, w in enumerate(POOL_WINDOWS):
        lo = jnp.clip(t - w // 2, 0, N)
        hi = jnp.clip(t + w // 2, 0, N)
        cnt = (hi - lo).astype(jnp.float32)[None, :, None]
        sl = slice(gi * POOL_GROUP_W, (gi + 1) * POOL_GROUP_W)
        cg = csum[..., sl]
        mean = (cg[:, hi] - cg[:, lo]) / cnt
        outs.append((mean - u32[..., sl]) @ pool_w[gi].astype(jnp.float32))
    y = jnp.concatenate(outs, axis=-1) * pool_scale.astype(jnp.float32)
    return y.astype(u.dtype)


def _fourier_mixer(u, w):
    f = jnp.fft.fft2(u.astype(jnp.float32), axes=(1, 2), norm='ortho').real
    return f.astype(u.dtype) @ w


def _merge(ys, gate_logits, w_br, w_o):
    acc = None
    for i, y in enumerate(ys):
        g = jax.nn.sigmoid(gate_logits[..., i * D_MODEL:(i + 1) * D_MODEL].astype(jnp.float32))
        term = g.astype(y.dtype) * (y @ w_br[i])
        acc = term if acc is None else acc + term
    return acc @ w_o


def _ec_ffn(h, w_router, w_gate, w_up, w_down):
    B, N, D = h.shape
    cap = max(1, EC_FACTOR * N // N_EXPERTS)
    aff = jax.nn.softmax((h @ w_router).astype(jnp.float32), axis=-1)
    g, idx = lax.top_k(aff.transpose(0, 2, 1), cap)
    bidx = jnp.arange(B)[:, None, None]
    xe = h[bidx, idx]
    hid = (jax.nn.silu(jnp.einsum('becd,edf->becf', xe, w_gate))
           * jnp.einsum('becd,edf->becf', xe, w_up))
    ye = jnp.einsum('becf,efd->becd', hid, w_down) * g[..., None].astype(h.dtype)
    return jnp.zeros_like(h).at[bidx, idx].add(ye)


def setup_inputs(seed: int = 0) -> dict:
    key = jax.random.key(seed)
    ks = jax.random.split(key, 32)
    f32 = jnp.float32
    L, D, E, F = DEPTH, D_MODEL, N_EXPERTS, EXPERT_FF
    nrm = lambda k, shape, s: jax.random.normal(k, shape, f32) * s
    return {
        "x": nrm(ks[0], (BATCH, SEQ, D), 1.0),
        "c": nrm(ks[1], (BATCH, D), 1.0),
        "ctx": nrm(ks[2], (BATCH, CTX_LEN, D), 1.0),
        "c_ctx": nrm(ks[3], (D,), 1.0),
        "w_ada": nrm(ks[4], (L, D, 6 * D), 0.5 * D ** -0.5),
        "b_ada": nrm(ks[5], (L, 6 * D), 0.02),
        "g_mix": 1.0 + nrm(ks[6], (L, D), 0.05),
        "g_ffn": 1.0 + nrm(ks[7], (L, D), 0.05),
        "w_in": nrm(ks[8], (L, D, IN_COLS), D ** -0.5),
        "na_q_g": 1.0 + nrm(ks[9], (L, NA_HEAD_DIM), 0.05),
        "na_k_g": 1.0 + nrm(ks[10], (L, NA_HEAD_DIM), 0.05),
        "na_rpb": nrm(ks[11], (L, NA_HEADS, 2 * NA_WIN_R - 1, 2 * NA_WIN_C - 1), 0.1),
        "df_q_g": 1.0 + nrm(ks[12], (L, DIFF_HEAD_DIM), 0.05),
        "df_k_g": 1.0 + nrm(ks[13], (L, DIFF_HEAD_DIM), 0.05),
        "df_lambda": nrm(ks[14], (L, 4, DIFF_HEAD_DIM), 0.1),
        "df_subln_g": 1.0 + nrm(ks[15], (L, 2 * DIFF_HEAD_DIM), 0.05),
        "pool_w": nrm(ks[16], (L, POOL_GROUPS, POOL_GROUP_W, POOL_GROUP_W), POOL_GROUP_W ** -0.5),
        "pool_scale": 1.0 + nrm(ks[17], (L, POOL_W), 0.05),
        "fnet_w": nrm(ks[18], (L, FNET_W, FNET_W), FNET_W ** -0.5),
        "w_branch": nrm(ks[19], (L, N_BRANCH, BRANCH_W, D), BRANCH_W ** -0.5),
        "w_out": nrm(ks[20], (L, D, D), D ** -0.5),
        "w_router": nrm(ks[21], (L, D, E), D ** -0.5),
        "w_gate_e": nrm(ks[22], (L, E, D, F), D ** -0.5),
        "w_up_e": nrm(ks[23], (L, E, D, F), D ** -0.5),
        "w_down_e": nrm(ks[24], (L, E, F, D), F ** -0.5),
    }


def reference(x, c, ctx, c_ctx, w_ada, b_ada, g_mix, g_ffn, w_in, na_q_g, na_k_g, na_rpb,
              df_q_g, df_k_g, df_lambda, df_subln_g, pool_w, pool_scale, fnet_w, w_branch,
              w_out, w_router, w_gate_e, w_up_e, w_down_e):
    T = x.shape[1]
    t = jnp.arange(T)
    rope = (*_rope_tables(t // GRID_W, DIFF_HEAD_DIM // 2),
            *_rope_tables(t % GRID_W, DIFF_HEAD_DIM // 2))
    s_c = jax.nn.silu(c)
    s_cc = jax.nn.silu(c_ctx)
    for l in range(DEPTH):
        last = l == DEPTH - 1
        mod = (s_c @ w_ada[l] + b_ada[l])[:, None, :]
        sh1, sc1, gt1, sh2, sc2, gt2 = jnp.split(mod, 6, axis=-1)
        cmod = s_cc @ w_ada[l] + b_ada[l]
        csh1, csc1, cgt1, csh2, csc2, cgt2 = jnp.split(cmod, 6, axis=-1)
        lam_init = 0.8 - 0.6 * math.exp(-0.3 * l)
        lv = df_lambda[l].astype(jnp.float32)
        lam = jnp.exp(jnp.sum(lv[0] * lv[1])) - jnp.exp(jnp.sum(lv[2] * lv[3])) + lam_init

        h = _modulate(_rmsnorm(x, g_mix[l]), sh1, sc1)
        hc = _modulate(_rmsnorm(ctx, g_mix[l]), csh1, csc1)
        pl = h @ w_in[l]
        na_q, df_q, pool_in, fnet_in, gate_l = _split_q(pl, na_q_g[l], df_q_g[l])
        na_k, na_v, df_k, df_v = _split_kv(pl[..., OFF_KV:], na_k_g[l], df_k_g[l])
        df_q = _axial_rope(df_q, rope)
        df_k = _axial_rope(df_k, rope)
        pc = hc @ (w_in[l][:, OFF_KV:] if last else w_in[l])
        na_kc, na_vc, df_kc, df_vc = _split_kv(pc[..., -KV_W:], na_k_g[l], df_k_g[l])

        y_na = _na_latent(na_q, na_k, na_v, na_kc, na_vc, na_rpb[l])
        y_df = _diff_post(_diff_latent(df_q, df_k, df_v, df_kc, df_vc, lam), df_subln_g[l], lam_init)
        y_pool = _pool_mixer(pool_in, pool_w[l], pool_scale[l])
        y_fnet = _fourier_mixer(fnet_in, fnet_w[l])
        x_mix = _merge((y_na, y_df, y_pool, y_fnet), gate_l, w_branch[l], w_out[l])

        if not last:
            na_qc, df_qc, pool_c, fnet_c, gate_c = _split_q(pc, na_q_g[l], df_q_g[l])
            yc_na = _dense_attn(na_qc, na_kc, na_vc)
            yc_df = _diff_post(_diff_core(df_qc.transpose(0, 2, 3, 1, 4),
                                          df_kc.transpose(0, 2, 3, 1, 4),
                                          df_vc.transpose(0, 2, 1, 3), lam),
                               df_subln_g[l], lam_init)
            yc_pool = _pool_mixer(pool_c, pool_w[l], pool_scale[l])
            yc_fnet = _fourier_mixer(fnet_c, fnet_w[l])
            ctx_mix = _merge((yc_na, yc_df, yc_pool, yc_fnet), gate_c, w_branch[l], w_out[l])

        x = x + gt1 * x_mix

        h2 = _modulate(_rmsnorm(x, g_ffn[l]), sh2, sc2)
        x = x + gt2 * _ec_ffn(h2, w_router[l], w_gate_e[l], w_up_e[l], w_down_e[l])

        if not last:
            ctx = ctx + cgt1 * ctx_mix
            hc2 = _modulate(_rmsnorm(ctx, g_ffn[l]), csh2, csc2)
            ctx = ctx + cgt2 * _ec_ffn(hc2, w_router[l], w_gate_e[l], w_up_e[l], w_down_e[l])
    return x
```

```python
import functools
import math

import jax
import jax.numpy as jnp
from jax import lax
from jax.experimental import pallas as pl
from jax.experimental.pallas import tpu as pltpu

F32 = jnp.float32
BF16 = jnp.bfloat16

D_MODEL = 1024
DEPTH = 2
GRID_W = 64
EPS = 1e-6
ROPE_BASE = 10000.0

NA_HEADS = 4
NA_HEAD_DIM = 64
NA_WIN_R = 8
NA_WIN_C = 16
NA_W = NA_HEADS * NA_HEAD_DIM

DIFF_HEADS = 4
DIFF_HEAD_DIM = 32
DIFF_QK_W = DIFF_HEADS * 2 * DIFF_HEAD_DIM
DIFF_V_W = DIFF_HEADS * 2 * DIFF_HEAD_DIM

POOL_WINDOWS = (2, 4, 8, 16)
POOL_GROUP_W = 64
POOL_W = len(POOL_WINDOWS) * POOL_GROUP_W
FNET_W = 256
N_BRANCH = 4
BRANCH_W = 256

OFF_NA_Q = 0
OFF_DF_Q = OFF_NA_Q + NA_W
OFF_POOL = OFF_DF_Q + DIFF_QK_W
OFF_FNET = OFF_POOL + POOL_W
OFF_GATE = OFF_FNET + FNET_W
OFF_KV = OFF_GATE + N_BRANCH * D_MODEL
KV_W = 2 * NA_W + DIFF_QK_W + DIFF_V_W
IN_COLS = OFF_KV + KV_W

N_EXPERTS = 16
EC_FACTOR = 2
EXPERT_FF = 1408

VMEM_LIMIT_BYTES = 56 * 1024 * 1024
LANES = 128
NEG_BIG = -1e30


def _params(*sem):
    return pltpu.CompilerParams(dimension_semantics=sem, vmem_limit_bytes=VMEM_LIMIT_BYTES)


def _split_bf16(a):
    hi = a.astype(BF16)
    lo = (a - hi.astype(F32)).astype(BF16)
    return hi, lo


def _dot(a, b):
    return jnp.dot(a, b, preferred_element_type=F32)


def _dot_nt(a, b):
    return lax.dot_general(a, b, (((1,), (1,)), ((), ())), preferred_element_type=F32)


def _group_rmsnorm(p, bd_ref, g):
    hi, lo = _split_bf16(p * p)
    ms = _dot(hi, bd_ref[...]) + _dot(lo, bd_ref[...])
    return p * lax.rsqrt(ms + EPS) * g


def _rope256(y, cos, s_next, s_prev):
    outs = []
    for half in range(2):
        z = y[:, half * LANES:(half + 1) * LANES]
        outs.append(z * cos + pltpu.roll(z, LANES - 8, 1) * s_next + pltpu.roll(z, 8, 1) * s_prev)
    return jnp.concatenate(outs, axis=1)


def _inproj_kernel(*refs, use_rope):
    if use_rope:
        (x_ref, g_ref, sh_ref, sc_ref, w_ref, gq_ref, gk_ref, gdq_ref, gdk_ref, bd64_ref, bd32_ref,
         cos_ref, sn_ref, sp_ref,
         naq_ref, dfq_ref, pool_ref, fnet_ref, gate_ref, nakv_ref, dfk_ref, dfv_ref) = refs
    else:
        (x_ref, g_ref, sh_ref, sc_ref, w_ref, gq_ref, gk_ref, gdq_ref, gdk_ref, bd64_ref, bd32_ref,
         naq_ref, dfq_ref, pool_ref, fnet_ref, gate_ref, nakv_ref, dfk_ref, dfv_ref) = refs
    x = x_ref[0]
    ms = jnp.mean(x * x, axis=-1, keepdims=True)
    y = x * lax.rsqrt(ms + EPS) * g_ref[...]
    h = (y * (1.0 + sc_ref[0]) + sh_ref[0]).astype(BF16)

    def seg(lo, n):
        return _dot(h, w_ref[:, lo:lo + n])

    def rope(v):
        if not use_rope:
            return v
        return _rope256(v, cos_ref[...], sn_ref[...], sp_ref[...])

    naq = _group_rmsnorm(seg(OFF_NA_Q, NA_W), bd64_ref, gq_ref[...])
    naq_ref[0] = (naq * (NA_HEAD_DIM ** -0.5)).astype(BF16)
    dfq = rope(_group_rmsnorm(seg(OFF_DF_Q, DIFF_QK_W), bd32_ref, gdq_ref[...]))
    dfq_ref[0] = (dfq * (DIFF_HEAD_DIM ** -0.5)).astype(BF16)
    pool_ref[0] = seg(OFF_POOL, POOL_W)
    fnet_ref[0] = seg(OFF_FNET, FNET_W)
    for j in range(0, N_BRANCH * D_MODEL, 512):
        gate_ref[0, :, j:j + 512] = seg(OFF_GATE + j, 512)
    nak = _group_rmsnorm(seg(OFF_KV, NA_W), bd64_ref, gk_ref[...])
    nakv_ref[0, :, 0:NA_W] = nak.astype(BF16)
    nakv_ref[0, :, NA_W:2 * NA_W] = seg(OFF_KV + NA_W, NA_W).astype(BF16)
    dfk = rope(_group_rmsnorm(seg(OFF_KV + 2 * NA_W, DIFF_QK_W), bd32_ref, gdk_ref[...]))
    dfk_ref[0] = dfk.astype(BF16)
    dfv_ref[0] = seg(OFF_KV + 2 * NA_W + DIFF_QK_W, DIFF_V_W).astype(BF16)


def _block_diag_mean(width, group):
    i = jnp.arange(width)
    return jnp.where((i[:, None] // group) == (i[None, :] // group), 1.0 / group, 0.0).astype(BF16)


def _rope_tables(T):
    t = jnp.arange(T)
    j = jnp.arange(LANES)
    jj = j % DIFF_HEAD_DIM
    quarter = DIFF_HEAD_DIM // 4
    use_row = jj < DIFF_HEAD_DIM // 2
    first = (jj % (DIFF_HEAD_DIM // 2)) < quarter
    inv = ROPE_BASE ** (-(jj % quarter).astype(F32) / quarter)
    pos = jnp.where(use_row[None, :], (t // GRID_W)[:, None], (t % GRID_W)[:, None]).astype(F32)
    ang = pos * inv[None, :]
    cos, sin = jnp.cos(ang), jnp.sin(ang)
    s_next = jnp.where(first[None, :], -sin, 0.0)
    s_prev = jnp.where(first[None, :], 0.0, sin)
    return cos, s_next, s_prev


def _inproj(x, g, sh, sc, w_bf16, gq, gk, gdq, gdk, rope, tm):
    B, T, D = x.shape
    use_rope = rope is not None
    row = lambda b, i: (b, i, 0)
    const2 = lambda b, i: (0, 0)
    perb = lambda b, i: (b, 0, 0)
    in_specs = [
        pl.BlockSpec((1, tm, D), row),
        pl.BlockSpec((1, D), const2),
        pl.BlockSpec((1, 1, D), perb),
        pl.BlockSpec((1, 1, D), perb),
        pl.BlockSpec((D, IN_COLS), const2),
        pl.BlockSpec((1, NA_W), const2), pl.BlockSpec((1, NA_W), const2),
        pl.BlockSpec((1, DIFF_QK_W), const2), pl.BlockSpec((1, DIFF_QK_W), const2),
        pl.BlockSpec((NA_W, NA_W), const2), pl.BlockSpec((DIFF_QK_W, DIFF_QK_W), const2),
    ]
    args = [x, g, sh, sc, w_bf16, gq, gk, gdq, gdk,
            _block_diag_mean(NA_W, NA_HEAD_DIM), _block_diag_mean(DIFF_QK_W, DIFF_HEAD_DIM)]
    if use_rope:
        in_specs += [pl.BlockSpec((tm, LANES), lambda b, i: (i, 0))] * 3
        args += list(rope)
    widths = [(NA_W, BF16), (DIFF_QK_W, BF16), (POOL_W, F32), (FNET_W, F32), (N_BRANCH * D_MODEL, F32),
              (2 * NA_W, BF16), (DIFF_QK_W, BF16), (DIFF_V_W, BF16)]
    out_shape = [jax.ShapeDtypeStruct((B, T, w), dt) for w, dt in widths]
    out_specs = [pl.BlockSpec((1, tm, w), row) for w, _ in widths]
    return pl.pallas_call(
        functools.partial(_inproj_kernel, use_rope=use_rope),
        grid=(B, T // tm), in_specs=in_specs, out_specs=out_specs, out_shape=out_shape,
        compiler_params=_params("parallel", "parallel"), name="inproj",
    )(*args)


def _na_kernel(*refs):
    q_ref = refs[0]
    kv_refs = refs[1:1 + NA_WIN_R]
    ckv_ref, bias_ref, o_ref = refs[1 + NA_WIN_R:]
    q = q_ref[0]
    kv = jnp.concatenate([r[0] for r in kv_refs], axis=0)
    ckv = ckv_ref[0]
    for h in range(NA_HEADS):
        ks = slice(h * NA_HEAD_DIM, (h + 1) * NA_HEAD_DIM)
        vs = slice(NA_W + h * NA_HEAD_DIM, NA_W + (h + 1) * NA_HEAD_DIM)
        qh = q[:, ks]
        s = _dot_nt(qh, kv[:, ks]) + bias_ref[0, h]
        sc = _dot_nt(qh, ckv[:, ks])
        m = jnp.maximum(jnp.max(s, axis=-1, keepdims=True), jnp.max(sc, axis=-1, keepdims=True))
        p = jnp.exp(s - m)
        pc = jnp.exp(sc - m)
        l = jnp.sum(p, axis=-1, keepdims=True) + jnp.sum(pc, axis=-1, keepdims=True)
        o = _dot(p.astype(BF16), kv[:, vs]) + _dot(pc.astype(BF16), ckv[:, vs])
        o_ref[0, :, ks] = o / l


def _na_bias_table(rpb):
    col = jnp.arange(GRID_W)
    cs = jnp.clip(col - NA_WIN_C // 2, 0, GRID_W - NA_WIN_C)
    kc = jnp.arange(GRID_W)
    valid = (kc[None, :] >= cs[:, None]) & (kc[None, :] < cs[:, None] + NA_WIN_C)
    ci = jnp.clip(kc[None, :] - col[:, None] + (NA_WIN_C - 1), 0, 2 * NA_WIN_C - 2)
    dq = jnp.arange(NA_WIN_R)
    ri = jnp.arange(NA_WIN_R)[None, :] - dq[:, None] + (NA_WIN_R - 1)
    tab = rpb.astype(F32)[:, ri][:, :, :, ci]
    tab = jnp.where(valid[None, None, None], tab, NEG_BIG)
    tab = tab.transpose(1, 0, 3, 2, 4)
    return tab.reshape(NA_WIN_R, NA_HEADS, GRID_W, NA_WIN_R * GRID_W)


def _na_latent(naq, nakv, cnakv, rpb):
    B, T, _ = naq.shape
    rows = T // GRID_W
    kr = min(NA_WIN_R, rows)
    assert kr == NA_WIN_R
    ctx_len = cnakv.shape[1]
    rs = lambda r: jnp.clip(r - kr // 2, 0, rows - kr)
    in_specs = [pl.BlockSpec((1, GRID_W, NA_W), lambda b, r: (b, r, 0))]
    for i in range(kr):
        in_specs.append(pl.BlockSpec((1, GRID_W, 2 * NA_W), lambda b, r, i=i: (b, rs(r) + i, 0)))
    in_specs.append(pl.BlockSpec((1, ctx_len, 2 * NA_W), lambda b, r: (b, 0, 0)))
    in_specs.append(pl.BlockSpec((1, NA_HEADS, GRID_W, kr * GRID_W), lambda b, r: (r - rs(r), 0, 0, 0)))
    return pl.pallas_call(
        _na_kernel, grid=(B, rows), in_specs=in_specs,
        out_specs=pl.BlockSpec((1, GRID_W, NA_W), lambda b, r: (b, r, 0)),
        out_shape=jax.ShapeDtypeStruct((B, T, NA_W), F32),
        compiler_params=_params("parallel", "parallel"), name="na_latent",
    )(naq, *([nakv] * kr), cnakv, _na_bias_table(rpb))


def _diff_kernel(lam_ref, g_ref, q_ref, kt_ref, v_ref, o_ref, m_sc, l_sc, acc_sc, *, lam_init):
    ki = pl.program_id(3)

    @pl.when(ki == 0)
    def _():
        m_sc[...] = jnp.full(m_sc.shape, -jnp.inf, F32)
        l_sc[...] = jnp.zeros(l_sc.shape, F32)
        acc_sc[...] = jnp.zeros(acc_sc.shape, F32)

    v = v_ref[0, 0]
    for mi in range(2):
        s = _dot(q_ref[0, 0, mi], kt_ref[0, 0, mi])
        m_prev = m_sc[mi]
        m_new = jnp.maximum(m_prev, jnp.max(s, axis=-1, keepdims=True))
        alpha = jnp.exp(m_prev - m_new)
        p = jnp.exp(s - m_new)
        l_sc[mi] = alpha * l_sc[mi] + jnp.sum(p, axis=-1, keepdims=True)
        acc_sc[mi] = alpha * acc_sc[mi] + _dot(p.astype(BF16), v)
        m_sc[mi] = m_new

    @pl.when(ki == pl.num_programs(3) - 1)
    def _():
        lv = lam_ref[...]
        lam = (jnp.exp(jnp.sum(lv[0:1] * lv[1:2], axis=-1, keepdims=True))
               - jnp.exp(jnp.sum(lv[2:3] * lv[3:4], axis=-1, keepdims=True)) + lam_init)
        o = acc_sc[0] / l_sc[0] - lam * (acc_sc[1] / l_sc[1])
        ms = jnp.mean(o * o, axis=-1, keepdims=True)
        o_ref[0, 0] = o * lax.rsqrt(ms + EPS) * g_ref[...] * (1.0 - lam_init)


def _diff_attn(q, kt, v, lam_params, subln_g, lam_init, tq, tk):
    B, H, _, Tq, dq = q.shape
    Tk = kt.shape[-1]
    dv = v.shape[-1]
    return pl.pallas_call(
        functools.partial(_diff_kernel, lam_init=lam_init),
        grid=(B, H, Tq // tq, Tk // tk),
        in_specs=[
            pl.BlockSpec((4, dq), lambda b, h, i, k: (0, 0)),
            pl.BlockSpec((1, dv), lambda b, h, i, k: (0, 0)),
            pl.BlockSpec((1, 1, 2, tq, dq), lambda b, h, i, k: (b, h, 0, i, 0)),
            pl.BlockSpec((1, 1, 2, dq, tk), lambda b, h, i, k: (b, h, 0, 0, k)),
            pl.BlockSpec((1, 1, tk, dv), lambda b, h, i, k: (b, h, k, 0)),
        ],
        out_specs=pl.BlockSpec((1, 1, tq, dv), lambda b, h, i, k: (b, h, i, 0)),
        out_shape=jax.ShapeDtypeStruct((B, H, Tq, dv), F32),
        scratch_shapes=[pltpu.VMEM((2, tq, 1), F32), pltpu.VMEM((2, tq, 1), F32),
                        pltpu.VMEM((2, tq, dv), F32)],
        compiler_params=_params("parallel", "parallel", "parallel", "arbitrary"), name="diff_attn",
    )(lam_params, subln_g, q, kt, v)


def _diff_layouts(dfq, dfk, dfv):
    B, Tq, _ = dfq.shape
    Tk = dfk.shape[1]
    q = dfq.reshape(B, Tq, DIFF_HEADS, 2, DIFF_HEAD_DIM).transpose(0, 2, 3, 1, 4)
    kt = dfk.reshape(B, Tk, DIFF_HEADS, 2, DIFF_HEAD_DIM).transpose(0, 2, 3, 4, 1)
    v = dfv.reshape(B, Tk, DIFF_HEADS, 2 * DIFF_HEAD_DIM).transpose(0, 2, 1, 3)
    return q, kt, v


def _merge_kernel(x_ref, gt_ref, yna_ref, ydf_ref, ypool_ref, f_ref, gate_ref, wbr_ref, wf_ref, wo_ref, o_ref):
    yf = _dot(f_ref[0].astype(BF16), wf_ref[...])
    ys = (yna_ref[0], ydf_ref[0], ypool_ref[0], yf)
    acc = None
    for i, y in enumerate(ys):
        g = jax.nn.sigmoid(gate_ref[0, :, i * D_MODEL:(i + 1) * D_MODEL])
        term = g * _dot(y.astype(BF16), wbr_ref[i])
        acc = term if acc is None else acc + term
    o_ref[0] = x_ref[0] + gt_ref[0] * _dot(acc.astype(BF16), wo_ref[...])


def _merge(x, gt, y_na, y_df, y_pool, f_real, gate, w_br, w_f, w_o, tm):
    B, T, D = x.shape
    row = lambda b, i: (b, i, 0)
    return pl.pallas_call(
        _merge_kernel, grid=(B, T // tm),
        in_specs=[
            pl.BlockSpec((1, tm, D), row),
            pl.BlockSpec((1, 1, D), lambda b, i: (b, 0, 0)),
            pl.BlockSpec((1, tm, BRANCH_W), row), pl.BlockSpec((1, tm, BRANCH_W), row),
            pl.BlockSpec((1, tm, BRANCH_W), row), pl.BlockSpec((1, tm, BRANCH_W), row),
            pl.BlockSpec((1, tm, N_BRANCH * D), row),
            pl.BlockSpec((N_BRANCH, BRANCH_W, D), lambda b, i: (0, 0, 0)),
            pl.BlockSpec((FNET_W, FNET_W), lambda b, i: (0, 0)),
            pl.BlockSpec((D, D), lambda b, i: (0, 0)),
        ],
        out_specs=pl.BlockSpec((1, tm, D), row),
        out_shape=jax.ShapeDtypeStruct((B, T, D), F32),
        compiler_params=_params("parallel", "parallel"), name="merge",
    )(x, gt, y_na, y_df, y_pool, f_real, gate, w_br, w_f, w_o)


def _router_kernel(x_ref, g_ref, sh_ref, sc_ref, wrh_ref, wrl_ref, h_ref, aff_ref):
    x = x_ref[0]
    ms = jnp.mean(x * x, axis=-1, keepdims=True)
    h = x * lax.rsqrt(ms + EPS) * g_ref[...] * (1.0 + sc_ref[0]) + sh_ref[0]
    h_ref[0] = h.astype(BF16)
    hi, lo = _split_bf16(h)
    logits = _dot(hi, wrh_ref[...]) + _dot(lo, wrh_ref[...]) + _dot(hi, wrl_ref[...])
    logits = logits - jnp.max(logits, axis=-1, keepdims=True)
    e = jnp.exp(logits)
    aff_ref[0] = e / jnp.sum(e, axis=-1, keepdims=True)


def _router(x, g, sh, sc, w_router, tm):
    B, T, D = x.shape
    E = w_router.shape[1]
    wrh, wrl = _split_bf16(w_router)
    row = lambda b, i: (b, i, 0)
    return pl.pallas_call(
        _router_kernel, grid=(B, T // tm),
        in_specs=[
            pl.BlockSpec((1, tm, D), row), pl.BlockSpec((1, D), lambda b, i: (0, 0)),
            pl.BlockSpec((1, 1, D), lambda b, i: (b, 0, 0)), pl.BlockSpec((1, 1, D), lambda b, i: (b, 0, 0)),
            pl.BlockSpec((D, E), lambda b, i: (0, 0)), pl.BlockSpec((D, E), lambda b, i: (0, 0)),
        ],
        out_specs=[pl.BlockSpec((1, tm, D), row), pl.BlockSpec((1, tm, E), row)],
        out_shape=[jax.ShapeDtypeStruct((B, T, D), BF16), jax.ShapeDtypeStruct((B, T, E), F32)],
        compiler_params=_params("parallel", "parallel"), name="router",
    )(x, g, sh, sc, wrh, wrl)


FF_CHUNKS = ((0, 512), (512, 512), (1024, 384))


def _expert_kernel(xe_ref, g_ref, wg_ref, wu_ref, wd_ref, o_ref):
    xe = xe_ref[0, 0]
    acc = None
    for lo, n in FF_CHUNKS:
        a = _dot(xe, wg_ref[0, :, lo:lo + n])
        u = _dot(xe, wu_ref[0, :, lo:lo + n])
        hid = (a * jax.nn.sigmoid(a) * u).astype(BF16)
        part = _dot(hid, wd_ref[0, lo:lo + n, :])
        acc = part if acc is None else acc + part
    o_ref[0, 0] = acc * g_ref[0, 0]


def _experts(xe, g, wg, wu, wd, tm):
    B, E, cap, D = xe.shape
    F = wg.shape[-1]
    assert F == EXPERT_FF
    tile = lambda e, b, i: (b, e, i, 0)
    return pl.pallas_call(
        _expert_kernel, grid=(E, B, cap // tm),
        in_specs=[
            pl.BlockSpec((1, 1, tm, D), tile), pl.BlockSpec((1, 1, tm, 1), tile),
            pl.BlockSpec((1, D, F), lambda e, b, i: (e, 0, 0)),
            pl.BlockSpec((1, D, F), lambda e, b, i: (e, 0, 0)),
            pl.BlockSpec((1, F, D), lambda e, b, i: (e, 0, 0)),
        ],
        out_specs=pl.BlockSpec((1, 1, tm, D), tile),
        out_shape=jax.ShapeDtypeStruct((B, E, cap, D), F32),
        compiler_params=_params("parallel", "parallel", "parallel"), name="experts",
    )(xe, g, wg, wu, wd)


def _ec_ffn(x, g, sh, sc, w_router, wg, wu, wd, tm_router, tm_expert):
    B, N, D = x.shape
    cap = max(1, EC_FACTOR * N // N_EXPERTS)
    h, aff = _router(x, g, sh, sc, w_router, tm_router)
    gsel, idx = lax.top_k(aff.transpose(0, 2, 1), cap)
    bidx = jnp.arange(B)[:, None, None]
    xe = h[bidx, idx]
    ye = _experts(xe, gsel[..., None], wg, wu, wd, min(tm_expert, cap))
    return jnp.zeros((B, N, D), F32).at[bidx, idx].add(ye)


def _pool_mixer(u, pool_w, pool_scale):
    N = u.shape[1]
    csum = jnp.concatenate([jnp.zeros_like(u[:, :1]), jnp.cumsum(u, axis=1)], axis=1)
    t = jnp.arange(N)
    outs = []
    for gi, w in enumerate(POOL_WINDOWS):
        lo = jnp.clip(t - w // 2, 0, N)
        hi = jnp.clip(t + w // 2, 0, N)
        cnt = (hi - lo).astype(F32)[None, :, None]
        sl = slice(gi * POOL_GROUP_W, (gi + 1) * POOL_GROUP_W)
        cg = csum[..., sl]
        mean = (cg[:, hi] - cg[:, lo]) / cnt
        outs.append((mean - u[..., sl]) @ pool_w[gi])
    return jnp.concatenate(outs, axis=-1) * pool_scale


def _fourier_real(u):
    return jnp.fft.fft2(u, axes=(1, 2), norm='ortho').real


def _ctx_dense_attn(q, k, v):
    B, Q, _ = q.shape
    qh = q.reshape(B, Q, NA_HEADS, NA_HEAD_DIM)
    kh = k.reshape(B, -1, NA_HEADS, NA_HEAD_DIM)
    vh = v.reshape(B, -1, NA_HEADS, NA_HEAD_DIM)
    s = jnp.einsum('bqhd,bkhd->bhqk', qh, kh, preferred_element_type=F32)
    p = jax.nn.softmax(s, axis=-1).astype(BF16)
    return jnp.einsum('bhqk,bkhd->bqhd', p, vh, preferred_element_type=F32).reshape(B, Q, NA_W)


def _heads_to_tokens(o):
    B, H, N, dv = o.shape
    return o.transpose(0, 2, 1, 3).reshape(B, N, H * dv)


def _tile(g, n):
    return jnp.tile(g.astype(F32), n)[None, :]


def kernel(x, c, ctx, c_ctx, w_ada, b_ada, g_mix, g_ffn, w_in, na_q_g, na_k_g, na_rpb, df_q_g, df_k_g,
           df_lambda, df_subln_g, pool_w, pool_scale, fnet_w, w_branch, w_out, w_router, w_gate_e, w_up_e,
           w_down_e):
    B, T, D = x.shape
    ctx_len = ctx.shape[1]
    rope = _rope_tables(T)
    s_c = jax.nn.silu(c)
    s_cc = jax.nn.silu(c_ctx)
    tm = 256
    tq, tk = 512, 640
    for l in range(DEPTH):
        last = l == DEPTH - 1
        lam_init = 0.8 - 0.6 * math.exp(-0.3 * l)
        mod = jnp.dot(s_c, w_ada[l], precision=lax.Precision.HIGHEST) + b_ada[l]
        sh1, sc1, gt1, sh2, sc2, gt2 = [m[:, None, :] for m in jnp.split(mod, 6, axis=-1)]
        cmod = jnp.dot(s_cc, w_ada[l], precision=lax.Precision.HIGHEST) + b_ada[l]
        bc = lambda m: jnp.broadcast_to(m[None, None, :], (B, 1, D))
        csh1, csc1, cgt1, csh2, csc2, cgt2 = [bc(m) for m in jnp.split(cmod, 6, axis=-1)]

        w_bf = w_in[l].astype(BF16)
        gq, gk = _tile(na_q_g[l], NA_HEADS), _tile(na_k_g[l], NA_HEADS)
        gdq, gdk = _tile(df_q_g[l], 2 * DIFF_HEADS), _tile(df_k_g[l], 2 * DIFF_HEADS)
        gmix = g_mix[l][None, :]
        gffn = g_ffn[l][None, :]
        subg = df_subln_g[l][None, :].astype(F32)
        wbr = w_branch[l].astype(BF16)
        wf = fnet_w[l].astype(BF16)
        wo = w_out[l].astype(BF16)
        wg, wu, wd = w_gate_e[l].astype(BF16), w_up_e[l].astype(BF16), w_down_e[l].astype(BF16)

        naq, dfq, pool_in, fnet_in, gate, nakv, dfk, dfv = _inproj(
            x, gmix, sh1, sc1, w_bf, gq, gk, gdq, gdk, rope, tm)
        (cnaq, cdfq, cpool_in, cfnet_in, cgate, cnakv, cdfk, cdfv) = _inproj(
            ctx, gmix, csh1, csc1, w_bf, gq, gk, gdq, gdk, None, ctx_len)

        y_na = _na_latent(naq, nakv, cnakv, na_rpb[l])
        q5, kt5, v4 = _diff_layouts(dfq, jnp.concatenate([dfk, cdfk], axis=1),
                                    jnp.concatenate([dfv, cdfv], axis=1))
        y_df = _heads_to_tokens(_diff_attn(q5, kt5, v4, df_lambda[l].astype(F32), subg, lam_init, tq, tk))
        y_pool = _pool_mixer(pool_in, pool_w[l], pool_scale[l])
        f_real = _fourier_real(fnet_in)
        x_new = _merge(x, gt1, y_na, y_df, y_pool, f_real, gate, wbr, wf, wo, tm)

        if not last:
            yc_na = _ctx_dense_attn(cnaq, cnakv[..., :NA_W], cnakv[..., NA_W:])
            cq5, ckt5, cv4 = _diff_layouts(cdfq, cdfk, cdfv)
            yc_df = _heads_to_tokens(_diff_attn(cq5, ckt5, cv4, df_lambda[l].astype(F32), subg, lam_init,
                                                ctx_len, ctx_len))
            yc_pool = _pool_mixer(cpool_in, pool_w[l], pool_scale[l])
            fc_real = _fourier_real(cfnet_in)
            ctx_new = _merge(ctx, cgt1, yc_na, yc_df, yc_pool, fc_real, cgate, wbr, wf, wo, ctx_len)

        x = x_new
        x = x + gt2 * _ec_ffn(x, gffn, sh2, sc2, w_router[l], wg, wu, wd, tm, 1024)
        if not last:
            ctx = ctx_new
            ctx = ctx + cgt2 * _ec_ffn(ctx, gffn, csh2, csc2, w_router[l], wg, wu, wd, ctx_len, 1024)
    return x
```

```python
import functools
import math

import jax
import jax.numpy as jnp
from jax import lax
from jax.experimental import pallas as pl
from jax.experimental.pallas import tpu as pltpu

F32 = jnp.float32
BF16 = jnp.bfloat16

D_MODEL = 1024
DEPTH = 2
GRID_W = 64
EPS = 1e-6
ROPE_BASE = 10000.0

NA_HEADS = 4
NA_HEAD_DIM = 64
NA_WIN_R = 8
NA_WIN_C = 16
NA_W = NA_HEADS * NA_HEAD_DIM

DIFF_HEADS = 4
DIFF_HEAD_DIM = 32
DIFF_QK_W = DIFF_HEADS * 2 * DIFF_HEAD_DIM
DIFF_V_W = DIFF_HEADS * 2 * DIFF_HEAD_DIM

POOL_WINDOWS = (2, 4, 8, 16)
POOL_GROUP_W = 64
POOL_W = len(POOL_WINDOWS) * POOL_GROUP_W
FNET_W = 256
N_BRANCH = 4
BRANCH_W = 256

OFF_NA_Q = 0
OFF_DF_Q = OFF_NA_Q + NA_W
OFF_POOL = OFF_DF_Q + DIFF_QK_W
OFF_FNET = OFF_POOL + POOL_W
OFF_GATE = OFF_FNET + FNET_W
OFF_KV = OFF_GATE + N_BRANCH * D_MODEL
KV_W = 2 * NA_W + DIFF_QK_W + DIFF_V_W
IN_COLS = OFF_KV + KV_W

N_EXPERTS = 16
EC_FACTOR = 2
EXPERT_FF = 1408

VMEM_LIMIT_BYTES = 56 * 1024 * 1024
LANES = 128
NEG_BIG = -1e30


def _params(*sem):
    return pltpu.CompilerParams(dimension_semantics=sem, vmem_limit_bytes=VMEM_LIMIT_BYTES)


def _split_bf16(a):
    hi = a.astype(BF16)
    lo = (a - hi.astype(F32)).astype(BF16)
    return hi, lo


def _dot(a, b):
    return jnp.dot(a, b, preferred_element_type=F32)


def _dot_nt(a, b):
    return lax.dot_general(a, b, (((1,), (1,)), ((), ())), preferred_element_type=F32)


def _group_rmsnorm(p, bd_ref, g):
    hi, lo = _split_bf16(p * p)
    ms = _dot(hi, bd_ref[...]) + _dot(lo, bd_ref[...])
    return p * lax.rsqrt(ms + EPS) * g


def _rope256(y, cos, s_next, s_prev):
    outs = []
    for half in range(2):
        z = y[:, half * LANES:(half + 1) * LANES]
        outs.append(z * cos + pltpu.roll(z, LANES - 8, 1) * s_next + pltpu.roll(z, 8, 1) * s_prev)
    return jnp.concatenate(outs, axis=1)


def _inproj_kernel(*refs, use_rope):
    if use_rope:
        (x_ref, g_ref, sh_ref, sc_ref, w_ref, gq_ref, gk_ref, gdq_ref, gdk_ref, bd64_ref, bd32_ref,
         cos_ref, sn_ref, sp_ref,
         naq_ref, dfq_ref, pool_ref, fnet_ref, gate_ref, nakv_ref, dfk_ref, dfv_ref) = refs
    else:
        (x_ref, g_ref, sh_ref, sc_ref, w_ref, gq_ref, gk_ref, gdq_ref, gdk_ref, bd64_ref, bd32_ref,
         naq_ref, dfq_ref, pool_ref, fnet_ref, gate_ref, nakv_ref, dfk_ref, dfv_ref) = refs
    x = x_ref[0]
    ms = jnp.mean(x * x, axis=-1, keepdims=True)
    y = x * lax.rsqrt(ms + EPS) * g_ref[...]
    h = (y * (1.0 + sc_ref[0]) + sh_ref[0]).astype(BF16)

    def seg(lo, n):
        return _dot(h, w_ref[:, lo:lo + n])

    def rope(v):
        if not use_rope:
            return v
        return _rope256(v, cos_ref[...], sn_ref[...], sp_ref[...])

    naq = _group_rmsnorm(seg(OFF_NA_Q, NA_W), bd64_ref, gq_ref[...])
    naq_ref[0] = (naq * (NA_HEAD_DIM ** -0.5)).astype(BF16)
    dfq = rope(_group_rmsnorm(seg(OFF_DF_Q, DIFF_QK_W), bd32_ref, gdq_ref[...]))
    dfq_ref[0] = (dfq * (math.log2(math.e) * DIFF_HEAD_DIM ** -0.5)).astype(BF16)
    pool_ref[0] = seg(OFF_POOL, POOL_W)
    fnet_ref[0] = seg(OFF_FNET, FNET_W)
    for j in range(0, N_BRANCH * D_MODEL, 512):
        gate_ref[0, :, j:j + 512] = seg(OFF_GATE + j, 512)
    nak = _group_rmsnorm(seg(OFF_KV, NA_W), bd64_ref, gk_ref[...])
    nakv_ref[0, :, 0:NA_W] = nak.astype(BF16)
    nakv_ref[0, :, NA_W:2 * NA_W] = seg(OFF_KV + NA_W, NA_W).astype(BF16)
    dfk = rope(_group_rmsnorm(seg(OFF_KV + 2 * NA_W, DIFF_QK_W), bd32_ref, gdk_ref[...]))
    dfk_ref[0] = dfk.astype(BF16)
    dfv_ref[0] = seg(OFF_KV + 2 * NA_W + DIFF_QK_W, DIFF_V_W).astype(BF16)


def _block_diag_mean(width, group):
    i = jnp.arange(width)
    return jnp.where((i[:, None] // group) == (i[None, :] // group), 1.0 / group, 0.0).astype(BF16)


def _rope_tables(T):
    t = jnp.arange(T)
    j = jnp.arange(LANES)
    jj = j % DIFF_HEAD_DIM
    quarter = DIFF_HEAD_DIM // 4
    use_row = jj < DIFF_HEAD_DIM // 2
    first = (jj % (DIFF_HEAD_DIM // 2)) < quarter
    inv = ROPE_BASE ** (-(jj % quarter).astype(F32) / quarter)
    pos = jnp.where(use_row[None, :], (t // GRID_W)[:, None], (t % GRID_W)[:, None]).astype(F32)
    ang = pos * inv[None, :]
    cos, sin = jnp.cos(ang), jnp.sin(ang)
    s_next = jnp.where(first[None, :], -sin, 0.0)
    s_prev = jnp.where(first[None, :], 0.0, sin)
    return cos, s_next, s_prev


def _inproj(x, g, sh, sc, w_bf16, gq, gk, gdq, gdk, rope, tm):
    B, T, D = x.shape
    use_rope = rope is not None
    row = lambda b, i: (b, i, 0)
    const2 = lambda b, i: (0, 0)
    perb = lambda b, i: (b, 0, 0)
    in_specs = [
        pl.BlockSpec((1, tm, D), row),
        pl.BlockSpec((1, D), const2),
        pl.BlockSpec((1, 1, D), perb),
        pl.BlockSpec((1, 1, D), perb),
        pl.BlockSpec((D, IN_COLS), const2),
        pl.BlockSpec((1, NA_W), const2), pl.BlockSpec((1, NA_W), const2),
        pl.BlockSpec((1, DIFF_QK_W), const2), pl.BlockSpec((1, DIFF_QK_W), const2),
        pl.BlockSpec((NA_W, NA_W), const2), pl.BlockSpec((DIFF_QK_W, DIFF_QK_W), const2),
    ]
    args = [x, g, sh, sc, w_bf16, gq, gk, gdq, gdk,
            _block_diag_mean(NA_W, NA_HEAD_DIM), _block_diag_mean(DIFF_QK_W, DIFF_HEAD_DIM)]
    if use_rope:
        in_specs += [pl.BlockSpec((tm, LANES), lambda b, i: (i, 0))] * 3
        args += list(rope)
    widths = [(NA_W, BF16), (DIFF_QK_W, BF16), (POOL_W, F32), (FNET_W, F32), (N_BRANCH * D_MODEL, F32),
              (2 * NA_W, BF16), (DIFF_QK_W, BF16), (DIFF_V_W, BF16)]
    out_shape = [jax.ShapeDtypeStruct((B, T, w), dt) for w, dt in widths]
    out_specs = [pl.BlockSpec((1, tm, w), row) for w, _ in widths]
    return pl.pallas_call(
        functools.partial(_inproj_kernel, use_rope=use_rope),
        grid=(B, T // tm), in_specs=in_specs, out_specs=out_specs, out_shape=out_shape,
        compiler_params=_params("parallel", "parallel"), name="inproj",
    )(*args)


def _na_kernel(*refs):
    q_ref = refs[0]
    kv_refs = refs[1:1 + NA_WIN_R]
    ckv_ref, bias_ref, o_ref = refs[1 + NA_WIN_R:]
    q = q_ref[0]
    kv = jnp.concatenate([r[0] for r in kv_refs], axis=0)
    ckv = ckv_ref[0]
    for h in range(NA_HEADS):
        ks = slice(h * NA_HEAD_DIM, (h + 1) * NA_HEAD_DIM)
        vs = slice(NA_W + h * NA_HEAD_DIM, NA_W + (h + 1) * NA_HEAD_DIM)
        qh = q[:, ks]
        s = _dot_nt(qh, kv[:, ks]) + bias_ref[0, h]
        sc = _dot_nt(qh, ckv[:, ks])
        m = jnp.maximum(jnp.max(s, axis=-1, keepdims=True), jnp.max(sc, axis=-1, keepdims=True))
        p = jnp.exp(s - m)
        pc = jnp.exp(sc - m)
        l = jnp.sum(p, axis=-1, keepdims=True) + jnp.sum(pc, axis=-1, keepdims=True)
        o = _dot(p.astype(BF16), kv[:, vs]) + _dot(pc.astype(BF16), ckv[:, vs])
        o_ref[0, :, ks] = o / l


def _na_bias_table(rpb):
    col = jnp.arange(GRID_W)
    cs = jnp.clip(col - NA_WIN_C // 2, 0, GRID_W - NA_WIN_C)
    kc = jnp.arange(GRID_W)
    valid = (kc[None, :] >= cs[:, None]) & (kc[None, :] < cs[:, None] + NA_WIN_C)
    ci = jnp.clip(kc[None, :] - col[:, None] + (NA_WIN_C - 1), 0, 2 * NA_WIN_C - 2)
    dq = jnp.arange(NA_WIN_R)
    ri = jnp.arange(NA_WIN_R)[None, :] - dq[:, None] + (NA_WIN_R - 1)
    tab = rpb.astype(F32)[:, ri][:, :, :, ci]
    tab = jnp.where(valid[None, None, None], tab, NEG_BIG)
    tab = tab.transpose(1, 0, 3, 2, 4)
    return tab.reshape(NA_WIN_R, NA_HEADS, GRID_W, NA_WIN_R * GRID_W)


def _na_latent(naq, nakv, cnakv, rpb):
    B, T, _ = naq.shape
    rows = T // GRID_W
    kr = min(NA_WIN_R, rows)
    assert kr == NA_WIN_R
    ctx_len = cnakv.shape[1]
    rs = lambda r: jnp.clip(r - kr // 2, 0, rows - kr)
    in_specs = [pl.BlockSpec((1, GRID_W, NA_W), lambda b, r: (b, r, 0))]
    for i in range(kr):
        in_specs.append(pl.BlockSpec((1, GRID_W, 2 * NA_W), lambda b, r, i=i: (b, rs(r) + i, 0)))
    in_specs.append(pl.BlockSpec((1, ctx_len, 2 * NA_W), lambda b, r: (b, 0, 0)))
    in_specs.append(pl.BlockSpec((1, NA_HEADS, GRID_W, kr * GRID_W), lambda b, r: (r - rs(r), 0, 0, 0)))
    return pl.pallas_call(
        _na_kernel, grid=(B, rows), in_specs=in_specs,
        out_specs=pl.BlockSpec((1, GRID_W, NA_W), lambda b, r: (b, r, 0)),
        out_shape=jax.ShapeDtypeStruct((B, T, NA_W), F32),
        compiler_params=_params("parallel", "parallel"), name="na_latent",
    )(naq, *([nakv] * kr), cnakv, _na_bias_table(rpb))


DIFF_FIXED_SHIFT_MAX = 60.0


def _diff_kernel(shift_ref, lam_ref, g_ref, qt_ref, k_ref, vt_ref, o_ref, qx_sc, m_sc, l_sc, acc_sc, *,
                 lam_init, tk, nk, online):
    h = pl.program_id(1)
    qt = qt_ref[0]
    tq = qt.shape[1]
    grp = lax.broadcasted_iota(jnp.int32, qt.shape, 0) // DIFF_HEAD_DIM
    for mi in range(2):
        qx_sc[mi] = jnp.where(grp == 2 * h + mi, qt, jnp.zeros_like(qt))
    m_sc[...] = jnp.full(m_sc.shape, -jnp.inf, F32)
    l_sc[...] = jnp.zeros(l_sc.shape, F32)
    acc_sc[...] = jnp.zeros(acc_sc.shape, F32)

    def body(k, carry):
        off = pl.multiple_of(k * tk, tk)
        kb = k_ref[0, pl.ds(off, tk), :]
        vtb = vt_ref[0, :, pl.ds(off, tk)]
        for mi in range(2):
            s = _dot(kb, qx_sc[mi])
            if online:
                m_prev = m_sc[mi]
                m_new = jnp.maximum(m_prev, jnp.max(s, axis=0, keepdims=True))
                alpha = jnp.exp2(m_prev - m_new)
                p = jnp.exp2(s - m_new)
                l_sc[mi] = alpha * l_sc[mi] + p.reshape(tk // 8, 8, tq).sum(axis=0)
                acc_sc[mi] = alpha * acc_sc[mi] + _dot(vtb, p.astype(BF16))
                m_sc[mi] = m_new
            else:
                p = jnp.exp2(s - shift_ref[0])
                l_sc[mi] += p.reshape(tk // 8, 8, tq).sum(axis=0)
                acc_sc[mi] += _dot(vtb, p.astype(BF16))
        return carry

    lax.fori_loop(0, nk, body, 0)

    lv = lam_ref[...]
    lam = (jnp.exp(jnp.sum(lv[0:1] * lv[1:2], axis=-1, keepdims=True))
           - jnp.exp(jnp.sum(lv[2:3] * lv[3:4], axis=-1, keepdims=True)) + lam_init)
    l0 = jnp.sum(l_sc[0], axis=0, keepdims=True)
    l1 = jnp.sum(l_sc[1], axis=0, keepdims=True)
    o = acc_sc[0] / l0 - lam * (acc_sc[1] / l1)
    ms = jnp.mean(o * o, axis=0, keepdims=True)
    o_ref[0] = o * lax.rsqrt(ms + EPS) * g_ref[...] * (1.0 - lam_init)


def _diff_call(shift, qt, k, vt, lam_params, subln_g, lam_init, tq, tk, online):
    B, W, Tq = qt.shape
    Tk = k.shape[1]
    dv = W // DIFF_HEADS
    return pl.pallas_call(
        functools.partial(_diff_kernel, lam_init=lam_init, tk=tk, nk=Tk // tk, online=online),
        grid=(B, DIFF_HEADS, Tq // tq),
        in_specs=[
            pl.BlockSpec(memory_space=pltpu.SMEM),
            pl.BlockSpec((4, DIFF_HEAD_DIM), lambda b, h, i: (0, 0)),
            pl.BlockSpec((dv, 1), lambda b, h, i: (0, 0)),
            pl.BlockSpec((1, W, tq), lambda b, h, i: (b, 0, i)),
            pl.BlockSpec((1, Tk, W), lambda b, h, i: (b, 0, 0)),
            pl.BlockSpec((1, dv, Tk), lambda b, h, i: (b, h, 0)),
        ],
        out_specs=pl.BlockSpec((1, dv, tq), lambda b, h, i: (b, h, i)),
        out_shape=jax.ShapeDtypeStruct((B, W, Tq), F32),
        scratch_shapes=[pltpu.VMEM((2, W, tq), BF16), pltpu.VMEM((2, 1, tq), F32),
                        pltpu.VMEM((2, 8, tq), F32), pltpu.VMEM((2, dv, tq), F32)],
        compiler_params=_params("parallel", "parallel", "parallel"),
        name="diff_attn_online" if online else "diff_attn_fixed",
    )(shift, lam_params, subln_g, qt, k, vt)


def _diff_attn(dfq, dfk, dfv, gq, gk, lam_params, subln_g, lam_init, tq, tk):
    qt = dfq.transpose(0, 2, 1)
    vt = dfv.transpose(0, 2, 1)
    bound = (1.02 * math.log2(math.e) * DIFF_HEAD_DIM ** 0.5) * jnp.max(jnp.abs(gq)) * jnp.max(jnp.abs(gk))
    shift = bound.astype(F32).reshape(1)
    args = (shift, qt, dfk, vt, lam_params, subln_g, lam_init, tq, tk)
    out = lax.cond(bound <= DIFF_FIXED_SHIFT_MAX,
                   lambda: _diff_call(*args, online=False), lambda: _diff_call(*args, online=True))
    return out.transpose(0, 2, 1)


def _merge_kernel(x_ref, gt_ref, yna_ref, ydf_ref, ypool_ref, f_ref, gate_ref, wbr_ref, wf_ref, wo_ref, o_ref):
    yf = _dot(f_ref[0].astype(BF16), wf_ref[...])
    ys = (yna_ref[0], ydf_ref[0], ypool_ref[0], yf)
    acc = None
    for i, y in enumerate(ys):
        g = jax.nn.sigmoid(gate_ref[0, :, i * D_MODEL:(i + 1) * D_MODEL])
        term = g * _dot(y.astype(BF16), wbr_ref[i])
        acc = term if acc is None else acc + term
    o_ref[0] = x_ref[0] + gt_ref[0] * _dot(acc.astype(BF16), wo_ref[...])


def _merge(x, gt, y_na, y_df, y_pool, f_real, gate, w_br, w_f, w_o, tm):
    B, T, D = x.shape
    row = lambda b, i: (b, i, 0)
    return pl.pallas_call(
        _merge_kernel, grid=(B, T // tm),
        in_specs=[
            pl.BlockSpec((1, tm, D), row),
            pl.BlockSpec((1, 1, D), lambda b, i: (b, 0, 0)),
            pl.BlockSpec((1, tm, BRANCH_W), row), pl.BlockSpec((1, tm, BRANCH_W), row),
            pl.BlockSpec((1, tm, BRANCH_W), row), pl.BlockSpec((1, tm, BRANCH_W), row),
            pl.BlockSpec((1, tm, N_BRANCH * D), row),
            pl.BlockSpec((N_BRANCH, BRANCH_W, D), lambda b, i: (0, 0, 0)),
            pl.BlockSpec((FNET_W, FNET_W), lambda b, i: (0, 0)),
            pl.BlockSpec((D, D), lambda b, i: (0, 0)),
        ],
        out_specs=pl.BlockSpec((1, tm, D), row),
        out_shape=jax.ShapeDtypeStruct((B, T, D), F32),
        compiler_params=_params("parallel", "parallel"), name="merge",
    )(x, gt, y_na, y_df, y_pool, f_real, gate, w_br, w_f, w_o)


def _router_kernel(x_ref, g_ref, sh_ref, sc_ref, wrh_ref, wrl_ref, h_ref, aff_ref):
    x = x_ref[0]
    ms = jnp.mean(x * x, axis=-1, keepdims=True)
    h = x * lax.rsqrt(ms + EPS) * g_ref[...] * (1.0 + sc_ref[0]) + sh_ref[0]
    h_ref[0] = h.astype(BF16)
    hi, lo = _split_bf16(h)
    logits = _dot(hi, wrh_ref[...]) + _dot(lo, wrh_ref[...]) + _dot(hi, wrl_ref[...])
    logits = logits - jnp.max(logits, axis=-1, keepdims=True)
    e = jnp.exp(logits)
    aff_ref[0] = e / jnp.sum(e, axis=-1, keepdims=True)


def _router(x, g, sh, sc, w_router, tm):
    B, T, D = x.shape
    E = w_router.shape[1]
    wrh, wrl = _split_bf16(w_router)
    row = lambda b, i: (b, i, 0)
    return pl.pallas_call(
        _router_kernel, grid=(B, T // tm),
        in_specs=[
            pl.BlockSpec((1, tm, D), row), pl.BlockSpec((1, D), lambda b, i: (0, 0)),
            pl.BlockSpec((1, 1, D), lambda b, i: (b, 0, 0)), pl.BlockSpec((1, 1, D), lambda b, i: (b, 0, 0)),
            pl.BlockSpec((D, E), lambda b, i: (0, 0)), pl.BlockSpec((D, E), lambda b, i: (0, 0)),
        ],
        out_specs=[pl.BlockSpec((1, tm, D), row), pl.BlockSpec((1, tm, E), row)],
        out_shape=[jax.ShapeDtypeStruct((B, T, D), BF16), jax.ShapeDtypeStruct((B, T, E), F32)],
        compiler_params=_params("parallel", "parallel"), name="router",
    )(x, g, sh, sc, wrh, wrl)


FF_CHUNKS = ((0, 512), (512, 512), (1024, 384))


def _expert_kernel(xe_ref, g_ref, wg_ref, wu_ref, wd_ref, o_ref):
    xe = xe_ref[0, 0]
    acc = None
    for lo, n in FF_CHUNKS:
        a = _dot(xe, wg_ref[0, :, lo:lo + n])
        u = _dot(xe, wu_ref[0, :, lo:lo + n])
        hid = (a * jax.nn.sigmoid(a) * u).astype(BF16)
        part = _dot(hid, wd_ref[0, lo:lo + n, :])
        acc = part if acc is None else acc + part
    o_ref[0, 0] = acc * g_ref[0, 0]


def _experts(xe, g, wg, wu, wd, tm):
    B, E, cap, D = xe.shape
    F = wg.shape[-1]
    assert F == EXPERT_FF
    tile = lambda e, b, i: (b, e, i, 0)
    return pl.pallas_call(
        _expert_kernel, grid=(E, B, cap // tm),
        in_specs=[
            pl.BlockSpec((1, 1, tm, D), tile), pl.BlockSpec((1, 1, tm, 1), tile),
            pl.BlockSpec((1, D, F), lambda e, b, i: (e, 0, 0)),
            pl.BlockSpec((1, D, F), lambda e, b, i: (e, 0, 0)),
            pl.BlockSpec((1, F, D), lambda e, b, i: (e, 0, 0)),
        ],
        out_specs=pl.BlockSpec((1, 1, tm, D), tile),
        out_shape=jax.ShapeDtypeStruct((B, E, cap, D), F32),
        compiler_params=_params("parallel", "parallel", "parallel"), name="experts",
    )(xe, g, wg, wu, wd)


def _ec_ffn(x, g, sh, sc, w_router, wg, wu, wd, tm_router, tm_expert):
    B, N, D = x.shape
    cap = max(1, EC_FACTOR * N // N_EXPERTS)
    h, aff = _router(x, g, sh, sc, w_router, tm_router)
    gsel, idx = lax.top_k(aff.transpose(0, 2, 1), cap)
    bidx = jnp.arange(B)[:, None, None]
    xe = h[bidx, idx]
    ye = _experts(xe, gsel[..., None], wg, wu, wd, min(tm_expert, cap))
    return jnp.zeros((B, N, D), F32).at[bidx, idx].add(ye)


def _pool_mixer(u, pool_w, pool_scale):
    N = u.shape[1]
    csum = jnp.concatenate([jnp.zeros_like(u[:, :1]), jnp.cumsum(u, axis=1)], axis=1)
    t = jnp.arange(N)
    outs = []
    for gi, w in enumerate(POOL_WINDOWS):
        lo = jnp.clip(t - w // 2, 0, N)
        hi = jnp.clip(t + w // 2, 0, N)
        cnt = (hi - lo).astype(F32)[None, :, None]
        sl = slice(gi * POOL_GROUP_W, (gi + 1) * POOL_GROUP_W)
        cg = csum[..., sl]
        mean = (cg[:, hi] - cg[:, lo]) / cnt
        outs.append((mean - u[..., sl]) @ pool_w[gi])
    return jnp.concatenate(outs, axis=-1) * pool_scale


def _fourier_real(u):
    return jnp.fft.fft2(u, axes=(1, 2), norm='ortho').real


def _ctx_dense_attn(q, k, v):
    B, Q, _ = q.shape
    qh = q.reshape(B, Q, NA_HEADS, NA_HEAD_DIM)
    kh = k.reshape(B, -1, NA_HEADS, NA_HEAD_DIM)
    vh = v.reshape(B, -1, NA_HEADS, NA_HEAD_DIM)
    s = jnp.einsum('bqhd,bkhd->bhqk', qh, kh, preferred_element_type=F32)
    p = jax.nn.softmax(s, axis=-1).astype(BF16)
    return jnp.einsum('bhqk,bkhd->bqhd', p, vh, preferred_element_type=F32).reshape(B, Q, NA_W)


def _tile(g, n):
    return jnp.tile(g.astype(F32), n)[None, :]


def kernel(x, c, ctx, c_ctx, w_ada, b_ada, g_mix, g_ffn, w_in, na_q_g, na_k_g, na_rpb, df_q_g, df_k_g,
           df_lambda, df_subln_g, pool_w, pool_scale, fnet_w, w_branch, w_out, w_router, w_gate_e, w_up_e,
           w_down_e):
    B, T, D = x.shape
    ctx_len = ctx.shape[1]
    rope = _rope_tables(T)
    s_c = jax.nn.silu(c)
    s_cc = jax.nn.silu(c_ctx)
    tm = 256
    tq, tk = 1024, 1280
    for l in range(DEPTH):
        last = l == DEPTH - 1
        lam_init = 0.8 - 0.6 * math.exp(-0.3 * l)
        mod = jnp.dot(s_c, w_ada[l], precision=lax.Precision.HIGHEST) + b_ada[l]
        sh1, sc1, gt1, sh2, sc2, gt2 = [m[:, None, :] for m in jnp.split(mod, 6, axis=-1)]
        cmod = jnp.dot(s_cc, w_ada[l], precision=lax.Precision.HIGHEST) + b_ada[l]
        bc = lambda m: jnp.broadcast_to(m[None, None, :], (B, 1, D))
        csh1, csc1, cgt1, csh2, csc2, cgt2 = [bc(m) for m in jnp.split(cmod, 6, axis=-1)]

        w_bf = w_in[l].astype(BF16)
        gq, gk = _tile(na_q_g[l], NA_HEADS), _tile(na_k_g[l], NA_HEADS)
        gdq, gdk = _tile(df_q_g[l], 2 * DIFF_HEADS), _tile(df_k_g[l], 2 * DIFF_HEADS)
        gmix = g_mix[l][None, :]
        gffn = g_ffn[l][None, :]
        subg = df_subln_g[l][:, None].astype(F32)
        wbr = w_branch[l].astype(BF16)
        wf = fnet_w[l].astype(BF16)
        wo = w_out[l].astype(BF16)
        wg, wu, wd = w_gate_e[l].astype(BF16), w_up_e[l].astype(BF16), w_down_e[l].astype(BF16)

        naq, dfq, pool_in, fnet_in, gate, nakv, dfk, dfv = _inproj(
            x, gmix, sh1, sc1, w_bf, gq, gk, gdq, gdk, rope, tm)
        (cnaq, cdfq, cpool_in, cfnet_in, cgate, cnakv, cdfk, cdfv) = _inproj(
            ctx, gmix, csh1, csc1, w_bf, gq, gk, gdq, gdk, None, ctx_len)

        y_na = _na_latent(naq, nakv, cnakv, na_rpb[l])
        lamp = df_lambda[l].astype(F32)
        y_df = _diff_attn(dfq, jnp.concatenate([dfk, cdfk], axis=1), jnp.concatenate([dfv, cdfv], axis=1),
                          df_q_g[l], df_k_g[l], lamp, subg, lam_init, tq, tk)
        y_pool = _pool_mixer(pool_in, pool_w[l], pool_scale[l])
        f_real = _fourier_real(fnet_in)
        x_new = _merge(x, gt1, y_na, y_df, y_pool, f_real, gate, wbr, wf, wo, tm)

        if not last:
            yc_na = _ctx_dense_attn(cnaq, cnakv[..., :NA_W], cnakv[..., NA_W:])
            yc_df = _diff_attn(cdfq, cdfk, cdfv, df_q_g[l], df_k_g[l], lamp, subg, lam_init, ctx_len, ctx_len)
            yc_pool = _pool_mixer(cpool_in, pool_w[l], pool_scale[l])
            fc_real = _fourier_real(cfnet_in)
            ctx_new = _merge(ctx, cgt1, yc_na, yc_df, yc_pool, fc_real, cgate, wbr, wf, wo, ctx_len)

        x = x_new
        x = x + gt2 * _ec_ffn(x, gffn, sh2, sc2, w_router[l], wg, wu, wd, tm, 1024)
        if not last:
            ctx = ctx_new
            ctx = ctx + cgt2 * _ec_ffn(ctx, gffn, csh2, csc2, w_router[l], wg, wu, wd, ctx_len, 1024)
    return x
```

```python
import functools
import math

import jax
import jax.numpy as jnp
import numpy as np
from jax import lax
from jax.experimental import pallas as pl
from jax.experimental.pallas import tpu as pltpu

F32 = jnp.float32
BF16 = jnp.bfloat16

D_MODEL = 1024
DEPTH = 2
GRID_W = 64
EPS = 1e-6
ROPE_BASE = 10000.0

NA_HEADS = 4
NA_HEAD_DIM = 64
NA_WIN_R = 8
NA_WIN_C = 16
NA_W = NA_HEADS * NA_HEAD_DIM

DIFF_HEADS = 4
DIFF_HEAD_DIM = 32
DIFF_QK_W = DIFF_HEADS * 2 * DIFF_HEAD_DIM
DIFF_V_W = DIFF_HEADS * 2 * DIFF_HEAD_DIM

POOL_WINDOWS = (2, 4, 8, 16)
POOL_GROUP_W = 64
POOL_W = len(POOL_WINDOWS) * POOL_GROUP_W
FNET_W = 256
N_BRANCH = 4
BRANCH_W = 256

OFF_NA_Q = 0
OFF_DF_Q = OFF_NA_Q + NA_W
OFF_POOL = OFF_DF_Q + DIFF_QK_W
OFF_FNET = OFF_POOL + POOL_W
OFF_GATE = OFF_FNET + FNET_W
OFF_KV = OFF_GATE + N_BRANCH * D_MODEL
KV_W = 2 * NA_W + DIFF_QK_W + DIFF_V_W
IN_COLS = OFF_KV + KV_W

N_EXPERTS = 16
EC_FACTOR = 2
EXPERT_FF = 1408

VMEM_LIMIT_BYTES = 56 * 1024 * 1024
LANES = 128
NEG_BIG = -1e30


def _params(*sem):
    return pltpu.CompilerParams(dimension_semantics=sem, vmem_limit_bytes=VMEM_LIMIT_BYTES)


def _split_bf16(a):
    hi = a.astype(BF16)
    lo = (a - hi.astype(F32)).astype(BF16)
    return hi, lo


def _dot(a, b):
    return jnp.dot(a, b, preferred_element_type=F32)


def _dot_nt(a, b):
    return lax.dot_general(a, b, (((1,), (1,)), ((), ())), preferred_element_type=F32)


def _group_rmsnorm(p, bd_ref, g):
    hi, lo = _split_bf16(p * p)
    ms = _dot(hi, bd_ref[...]) + _dot(lo, bd_ref[...])
    return p * lax.rsqrt(ms + EPS) * g


def _rope256(y, cos, s_next, s_prev):
    outs = []
    for half in range(2):
        z = y[:, half * LANES:(half + 1) * LANES]
        outs.append(z * cos + pltpu.roll(z, LANES - 8, 1) * s_next + pltpu.roll(z, 8, 1) * s_prev)
    return jnp.concatenate(outs, axis=1)


def _inproj_kernel(*refs, use_rope):
    if use_rope:
        (x_ref, g_ref, sh_ref, sc_ref, w_ref, gq_ref, gk_ref, gdq_ref, gdk_ref, bd64_ref, bd32_ref,
         cos_ref, sn_ref, sp_ref,
         naq_ref, dfq_ref, pool_ref, fnet_ref, gate_ref, nakv_ref, dfk_ref, dfv_ref) = refs
    else:
        (x_ref, g_ref, sh_ref, sc_ref, w_ref, gq_ref, gk_ref, gdq_ref, gdk_ref, bd64_ref, bd32_ref,
         naq_ref, dfq_ref, pool_ref, fnet_ref, gate_ref, nakv_ref, dfk_ref, dfv_ref) = refs
    x = x_ref[0]
    ms = jnp.mean(x * x, axis=-1, keepdims=True)
    y = x * lax.rsqrt(ms + EPS) * g_ref[...]
    h = (y * (1.0 + sc_ref[0]) + sh_ref[0]).astype(BF16)

    def seg(lo, n):
        return _dot(h, w_ref[:, lo:lo + n])

    def rope(v):
        if not use_rope:
            return v
        return _rope256(v, cos_ref[...], sn_ref[...], sp_ref[...])

    naq = _group_rmsnorm(seg(OFF_NA_Q, NA_W), bd64_ref, gq_ref[...])
    naq_ref[0] = (naq * (NA_HEAD_DIM ** -0.5)).astype(BF16)
    dfq = rope(_group_rmsnorm(seg(OFF_DF_Q, DIFF_QK_W), bd32_ref, gdq_ref[...]))
    dfq_ref[0] = (dfq * (math.log2(math.e) * DIFF_HEAD_DIM ** -0.5)).astype(BF16)
    pool_ref[0] = seg(OFF_POOL, POOL_W)
    fnet_ref[0] = seg(OFF_FNET, FNET_W)
    for j in range(0, N_BRANCH * D_MODEL, 512):
        gate_ref[0, :, j:j + 512] = seg(OFF_GATE + j, 512)
    nak = _group_rmsnorm(seg(OFF_KV, NA_W), bd64_ref, gk_ref[...])
    nakv_ref[0, :, 0:NA_W] = nak.astype(BF16)
    nakv_ref[0, :, NA_W:2 * NA_W] = seg(OFF_KV + NA_W, NA_W).astype(BF16)
    dfk = rope(_group_rmsnorm(seg(OFF_KV + 2 * NA_W, DIFF_QK_W), bd32_ref, gdk_ref[...]))
    dfk_ref[0] = dfk.astype(BF16)
    dfv_ref[0] = seg(OFF_KV + 2 * NA_W + DIFF_QK_W, DIFF_V_W).astype(BF16)


def _block_diag_mean(width, group):
    i = jnp.arange(width)
    return jnp.where((i[:, None] // group) == (i[None, :] // group), 1.0 / group, 0.0).astype(BF16)


def _rope_tables(T):
    t = jnp.arange(T)
    j = jnp.arange(LANES)
    jj = j % DIFF_HEAD_DIM
    quarter = DIFF_HEAD_DIM // 4
    use_row = jj < DIFF_HEAD_DIM // 2
    first = (jj % (DIFF_HEAD_DIM // 2)) < quarter
    inv = ROPE_BASE ** (-(jj % quarter).astype(F32) / quarter)
    pos = jnp.where(use_row[None, :], (t // GRID_W)[:, None], (t % GRID_W)[:, None]).astype(F32)
    ang = pos * inv[None, :]
    cos, sin = jnp.cos(ang), jnp.sin(ang)
    s_next = jnp.where(first[None, :], -sin, 0.0)
    s_prev = jnp.where(first[None, :], 0.0, sin)
    return cos, s_next, s_prev


def _inproj(x, g, sh, sc, w_bf16, gq, gk, gdq, gdk, rope, tm):
    B, T, D = x.shape
    use_rope = rope is not None
    row = lambda b, i: (b, i, 0)
    const2 = lambda b, i: (0, 0)
    perb = lambda b, i: (b, 0, 0)
    in_specs = [
        pl.BlockSpec((1, tm, D), row),
        pl.BlockSpec((1, D), const2),
        pl.BlockSpec((1, 1, D), perb),
        pl.BlockSpec((1, 1, D), perb),
        pl.BlockSpec((D, IN_COLS), const2),
        pl.BlockSpec((1, NA_W), const2), pl.BlockSpec((1, NA_W), const2),
        pl.BlockSpec((1, DIFF_QK_W), const2), pl.BlockSpec((1, DIFF_QK_W), const2),
        pl.BlockSpec((NA_W, NA_W), const2), pl.BlockSpec((DIFF_QK_W, DIFF_QK_W), const2),
    ]
    args = [x, g, sh, sc, w_bf16, gq, gk, gdq, gdk,
            _block_diag_mean(NA_W, NA_HEAD_DIM), _block_diag_mean(DIFF_QK_W, DIFF_HEAD_DIM)]
    if use_rope:
        in_specs += [pl.BlockSpec((tm, LANES), lambda b, i: (i, 0))] * 3
        args += list(rope)
    widths = [(NA_W, BF16), (DIFF_QK_W, BF16), (POOL_W, F32), (FNET_W, F32), (N_BRANCH * D_MODEL, F32),
              (2 * NA_W, BF16), (DIFF_QK_W, BF16), (DIFF_V_W, BF16)]
    out_shape = [jax.ShapeDtypeStruct((B, T, w), dt) for w, dt in widths]
    out_specs = [pl.BlockSpec((1, tm, w), row) for w, _ in widths]
    return pl.pallas_call(
        functools.partial(_inproj_kernel, use_rope=use_rope),
        grid=(B, T // tm), in_specs=in_specs, out_specs=out_specs, out_shape=out_shape,
        compiler_params=_params("parallel", "parallel"), name="inproj",
    )(*args)


NA_ROWS_PER_STEP = 4


def _na_kernel(q_ref, kv0_ref, kv1_ref, kv2_ref, ckv_ref, bias_ref, o_ref):
    q = q_ref[0]
    kv = jnp.concatenate([kv0_ref[0], kv1_ref[0], kv2_ref[0]], axis=0)
    ckv = ckv_ref[0]
    for h in range(NA_HEADS):
        ks = slice(h * NA_HEAD_DIM, (h + 1) * NA_HEAD_DIM)
        vs = slice(NA_W + h * NA_HEAD_DIM, NA_W + (h + 1) * NA_HEAD_DIM)
        qh = q[:, ks]
        s = _dot_nt(qh, kv[:, ks]) + bias_ref[0, h]
        sc = _dot_nt(qh, ckv[:, ks])
        m = jnp.maximum(jnp.max(s, axis=-1, keepdims=True), jnp.max(sc, axis=-1, keepdims=True))
        p = jnp.exp(s - m)
        pc = jnp.exp(sc - m)
        l = jnp.sum(p, axis=-1, keepdims=True) + jnp.sum(pc, axis=-1, keepdims=True)
        o = _dot(p.astype(BF16), kv[:, vs]) + _dot(pc.astype(BF16), ckv[:, vs])
        o_ref[0, :, ks] = o / l


def _na_bias_table(rpb, rows):
    R = NA_ROWS_PER_STEP
    nb = rows // R
    col = jnp.arange(GRID_W)
    cs = jnp.clip(col - NA_WIN_C // 2, 0, GRID_W - NA_WIN_C)
    col_ok = (col[None, :] >= cs[:, None]) & (col[None, :] < cs[:, None] + NA_WIN_C)
    ci = jnp.clip(col[None, :] - col[:, None] + (NA_WIN_C - 1), 0, 2 * NA_WIN_C - 2)
    j = jnp.array([0, 1, nb - 1])[:, None, None]
    r = R * j + jnp.arange(R)[None, :, None]
    krow = R * (j - 1) + jnp.arange(3 * R)[None, None, :]
    rs = jnp.clip(r - NA_WIN_R // 2, 0, rows - NA_WIN_R)
    row_ok = (krow >= rs) & (krow < rs + NA_WIN_R) & (krow >= 0) & (krow < rows)
    ri = jnp.clip(krow - r + (NA_WIN_R - 1), 0, 2 * NA_WIN_R - 2)
    tab = rpb.astype(F32)[:, ri][:, :, :, :, ci]
    ok = row_ok[None, :, :, :, None, None] & col_ok[None, None, None, None]
    tab = jnp.where(ok, tab, NEG_BIG)
    tab = tab.transpose(1, 0, 2, 4, 3, 5)
    return tab.reshape(3, NA_HEADS, R * GRID_W, 3 * R * GRID_W)


def _na_latent(naq, nakv, cnakv, rpb):
    B, T, _ = naq.shape
    rows = T // GRID_W
    R = NA_ROWS_PER_STEP
    assert rows % R == 0 and rows // R >= 3 and rows >= NA_WIN_R and R + NA_WIN_R <= 3 * R
    nb = rows // R
    tb = R * GRID_W
    ctx_len = cnakv.shape[1]
    pattern = lambda j: (j > 0).astype(jnp.int32) + (j == nb - 1).astype(jnp.int32)
    kv_spec = lambda d: pl.BlockSpec((1, tb, 2 * NA_W), lambda b, j: (b, jnp.clip(j + d, 0, nb - 1), 0))
    return pl.pallas_call(
        _na_kernel, grid=(B, nb),
        in_specs=[
            pl.BlockSpec((1, tb, NA_W), lambda b, j: (b, j, 0)),
            kv_spec(-1), kv_spec(0), kv_spec(1),
            pl.BlockSpec((1, ctx_len, 2 * NA_W), lambda b, j: (b, 0, 0)),
            pl.BlockSpec((1, NA_HEADS, tb, 3 * tb), lambda b, j: (pattern(j), 0, 0, 0)),
        ],
        out_specs=pl.BlockSpec((1, tb, NA_W), lambda b, j: (b, j, 0)),
        out_shape=jax.ShapeDtypeStruct((B, T, NA_W), F32),
        compiler_params=_params("parallel", "parallel"), name="na_latent",
    )(naq, nakv, nakv, nakv, cnakv, _na_bias_table(rpb, rows))


DIFF_FIXED_SHIFT_MAX = 60.0


def _diff_kernel(shift_ref, lam_ref, g_ref, qt_ref, k_ref, vt_ref, o_ref, qx_sc, m_sc, l_sc, acc_sc, *,
                 lam_init, tk, nk, online):
    h = pl.program_id(1)
    qt = qt_ref[0]
    tq = qt.shape[1]
    grp = lax.broadcasted_iota(jnp.int32, qt.shape, 0) // DIFF_HEAD_DIM
    for mi in range(2):
        qx_sc[mi] = jnp.where(grp == 2 * h + mi, qt, jnp.zeros_like(qt))
    m_sc[...] = jnp.full(m_sc.shape, -jnp.inf, F32)
    l_sc[...] = jnp.zeros(l_sc.shape, F32)
    acc_sc[...] = jnp.zeros(acc_sc.shape, F32)

    def body(k, carry):
        off = pl.multiple_of(k * tk, tk)
        kb = k_ref[0, pl.ds(off, tk), :]
        vtb = vt_ref[0, :, pl.ds(off, tk)]
        for mi in range(2):
            s = _dot(kb, qx_sc[mi])
            if online:
                m_prev = m_sc[mi]
                m_new = jnp.maximum(m_prev, jnp.max(s, axis=0, keepdims=True))
                alpha = jnp.exp2(m_prev - m_new)
                p = jnp.exp2(s - m_new)
                l_sc[mi] = alpha * l_sc[mi] + p.reshape(tk // 8, 8, tq).sum(axis=0)
                acc_sc[mi] = alpha * acc_sc[mi] + _dot(vtb, p.astype(BF16))
                m_sc[mi] = m_new
            else:
                p = jnp.exp2(s - shift_ref[0])
                l_sc[mi] += p.reshape(tk // 8, 8, tq).sum(axis=0)
                acc_sc[mi] += _dot(vtb, p.astype(BF16))
        return carry

    lax.fori_loop(0, nk, body, 0)

    lv = lam_ref[...]
    lam = (jnp.exp(jnp.sum(lv[0:1] * lv[1:2], axis=-1, keepdims=True))
           - jnp.exp(jnp.sum(lv[2:3] * lv[3:4], axis=-1, keepdims=True)) + lam_init)
    l0 = jnp.sum(l_sc[0], axis=0, keepdims=True)
    l1 = jnp.sum(l_sc[1], axis=0, keepdims=True)
    o = acc_sc[0] / l0 - lam * (acc_sc[1] / l1)
    ms = jnp.mean(o * o, axis=0, keepdims=True)
    o_ref[0] = o * lax.rsqrt(ms + EPS) * g_ref[...] * (1.0 - lam_init)


def _diff_call(shift, qt, k, vt, lam_params, subln_g, lam_init, tq, tk, online):
    B, W, Tq = qt.shape
    Tk = k.shape[1]
    dv = W // DIFF_HEADS
    return pl.pallas_call(
        functools.partial(_diff_kernel, lam_init=lam_init, tk=tk, nk=Tk // tk, online=online),
        grid=(B, DIFF_HEADS, Tq // tq),
        in_specs=[
            pl.BlockSpec(memory_space=pltpu.SMEM),
            pl.BlockSpec((4, DIFF_HEAD_DIM), lambda b, h, i: (0, 0)),
            pl.BlockSpec((dv, 1), lambda b, h, i: (0, 0)),
            pl.BlockSpec((1, W, tq), lambda b, h, i: (b, 0, i)),
            pl.BlockSpec((1, Tk, W), lambda b, h, i: (b, 0, 0)),
            pl.BlockSpec((1, dv, Tk), lambda b, h, i: (b, h, 0)),
        ],
        out_specs=pl.BlockSpec((1, dv, tq), lambda b, h, i: (b, h, i)),
        out_shape=jax.ShapeDtypeStruct((B, W, Tq), F32),
        scratch_shapes=[pltpu.VMEM((2, W, tq), BF16), pltpu.VMEM((2, 1, tq), F32),
                        pltpu.VMEM((2, 8, tq), F32), pltpu.VMEM((2, dv, tq), F32)],
        compiler_params=_params("parallel", "parallel", "parallel"),
        name="diff_attn_online" if online else "diff_attn_fixed",
    )(shift, lam_params, subln_g, qt, k, vt)


def _diff_attn(dfq, dfk, dfv, gq, gk, lam_params, subln_g, lam_init, tq, tk):
    qt = dfq.transpose(0, 2, 1)
    vt = dfv.transpose(0, 2, 1)
    bound = (1.02 * math.log2(math.e) * DIFF_HEAD_DIM ** 0.5) * jnp.max(jnp.abs(gq)) * jnp.max(jnp.abs(gk))
    shift = bound.astype(F32).reshape(1)
    args = (shift, qt, dfk, vt, lam_params, subln_g, lam_init, tq, tk)
    out = lax.cond(bound <= DIFF_FIXED_SHIFT_MAX,
                   lambda: _diff_call(*args, online=False), lambda: _diff_call(*args, online=True))
    return out.transpose(0, 2, 1)


POOL_HALO = 8


def _pool_mixer(u, prev, nxt, wp_ref, scale, n_total):
    tm = u.shape[0]
    i = pl.program_id(1)
    prev = jnp.where(i > 0, prev, 0.0)
    nxt = jnp.where(i < pl.num_programs(1) - 1, nxt, 0.0)
    ext = jnp.concatenate([prev, u, nxt], axis=0)
    sh = lambda d: ext[POOL_HALO + d:POOL_HALO + d + tm]
    t = i * tm + lax.broadcasted_iota(jnp.int32, (tm, 1), 0)
    lane_grp = lax.broadcasted_iota(jnp.int32, (1, POOL_W), 1) // POOL_GROUP_W
    win = None
    mean = None
    prev_half = 0
    for gi, w in enumerate(POOL_WINDOWS):
        half = w // 2
        for d in range(prev_half, half):
            add = sh(d) + sh(-d - 1)
            win = add if win is None else win + add
        prev_half = half
        cnt = (jnp.minimum(t + half, n_total) - jnp.maximum(t - half, 0)).astype(F32)
        m_w = win / cnt
        mean = m_w if mean is None else jnp.where(lane_grp >= gi, m_w, mean)
    return _dot((mean - u).astype(BF16), wp_ref[...]) * scale


def _merge_kernel(x_ref, gt_ref, yna_ref, ydf_ref, u_ref, up_ref, un_ref, f_ref, gate_ref, wbr_ref, wp_ref,
                  ps_ref, wf_ref, wo_ref, o_ref, *, n_total):
    yf = _dot(f_ref[0].astype(BF16), wf_ref[...])
    ypool = _pool_mixer(u_ref[0], up_ref[0], un_ref[0], wp_ref, ps_ref[...], n_total)
    ys = (yna_ref[0], ydf_ref[0], ypool, yf)
    acc = None
    for i, y in enumerate(ys):
        g = jax.nn.sigmoid(gate_ref[0, :, i * D_MODEL:(i + 1) * D_MODEL])
        term = g * _dot(y.astype(BF16), wbr_ref[i])
        acc = term if acc is None else acc + term
    o_ref[0] = x_ref[0] + gt_ref[0] * _dot(acc.astype(BF16), wo_ref[...])


def _merge(x, gt, y_na, y_df, pool_in, f_real, gate, w_br, w_pool_bd, pool_scale, w_f, w_o, tm):
    B, T, D = x.shape
    assert tm % POOL_HALO == 0 and POOL_HALO >= max(POOL_WINDOWS) // 2
    hb = tm // POOL_HALO
    row = lambda b, i: (b, i, 0)
    const2 = lambda b, i: (0, 0)
    return pl.pallas_call(
        functools.partial(_merge_kernel, n_total=T), grid=(B, T // tm),
        in_specs=[
            pl.BlockSpec((1, tm, D), row),
            pl.BlockSpec((1, 1, D), lambda b, i: (b, 0, 0)),
            pl.BlockSpec((1, tm, BRANCH_W), row), pl.BlockSpec((1, tm, BRANCH_W), row),
            pl.BlockSpec((1, tm, POOL_W), row),
            pl.BlockSpec((1, POOL_HALO, POOL_W), lambda b, i: (b, jnp.maximum(i * hb - 1, 0), 0)),
            pl.BlockSpec((1, POOL_HALO, POOL_W), lambda b, i: (b, jnp.minimum((i + 1) * hb, T // POOL_HALO - 1), 0)),
            pl.BlockSpec((1, tm, FNET_W), row),
            pl.BlockSpec((1, tm, N_BRANCH * D), row),
            pl.BlockSpec((N_BRANCH, BRANCH_W, D), lambda b, i: (0, 0, 0)),
            pl.BlockSpec((POOL_W, POOL_W), const2), pl.BlockSpec((1, POOL_W), const2),
            pl.BlockSpec((FNET_W, FNET_W), const2),
            pl.BlockSpec((D, D), const2),
        ],
        out_specs=pl.BlockSpec((1, tm, D), row),
        out_shape=jax.ShapeDtypeStruct((B, T, D), F32),
        compiler_params=_params("parallel", "parallel"), name="merge",
    )(x, gt, y_na, y_df, pool_in, pool_in, pool_in, f_real, gate, w_br, w_pool_bd, pool_scale, w_f, w_o)


def _router_kernel(x_ref, g_ref, sh_ref, sc_ref, wrh_ref, wrl_ref, h_ref, aff_ref):
    x = x_ref[0]
    ms = jnp.mean(x * x, axis=-1, keepdims=True)
    h = x * lax.rsqrt(ms + EPS) * g_ref[...] * (1.0 + sc_ref[0]) + sh_ref[0]
    h_ref[0] = h.astype(BF16)
    hi, lo = _split_bf16(h)
    logits = _dot(hi, wrh_ref[...]) + _dot(lo, wrh_ref[...]) + _dot(hi, wrl_ref[...])
    logits = logits - jnp.max(logits, axis=-1, keepdims=True)
    e = jnp.exp(logits)
    aff_ref[0] = e / jnp.sum(e, axis=-1, keepdims=True)


def _router(x, g, sh, sc, w_router, tm):
    B, T, D = x.shape
    E = w_router.shape[1]
    wrh, wrl = _split_bf16(w_router)
    row = lambda b, i: (b, i, 0)
    return pl.pallas_call(
        _router_kernel, grid=(B, T // tm),
        in_specs=[
            pl.BlockSpec((1, tm, D), row), pl.BlockSpec((1, D), lambda b, i: (0, 0)),
            pl.BlockSpec((1, 1, D), lambda b, i: (b, 0, 0)), pl.BlockSpec((1, 1, D), lambda b, i: (b, 0, 0)),
            pl.BlockSpec((D, E), lambda b, i: (0, 0)), pl.BlockSpec((D, E), lambda b, i: (0, 0)),
        ],
        out_specs=[pl.BlockSpec((1, tm, D), row), pl.BlockSpec((1, tm, E), row)],
        out_shape=[jax.ShapeDtypeStruct((B, T, D), BF16), jax.ShapeDtypeStruct((B, T, E), F32)],
        compiler_params=_params("parallel", "parallel"), name="router",
    )(x, g, sh, sc, wrh, wrl)


FF_CHUNKS = ((0, 512), (512, 512), (1024, 384))


def _expert_kernel(xe_ref, g_ref, wg_ref, wu_ref, wd_ref, o_ref):
    xe = xe_ref[0, 0]
    acc = None
    for lo, n in FF_CHUNKS:
        a = _dot(xe, wg_ref[0, :, lo:lo + n])
        u = _dot(xe, wu_ref[0, :, lo:lo + n])
        hid = (a * jax.nn.sigmoid(a) * u).astype(BF16)
        part = _dot(hid, wd_ref[0, lo:lo + n, :])
        acc = part if acc is None else acc + part
    o_ref[0, 0] = acc * g_ref[0, 0]


def _experts(xe, g, wg, wu, wd, tm):
    B, E, cap, D = xe.shape
    F = wg.shape[-1]
    assert F == EXPERT_FF
    tile = lambda e, b, i: (b, e, i, 0)
    return pl.pallas_call(
        _expert_kernel, grid=(E, B, cap // tm),
        in_specs=[
            pl.BlockSpec((1, 1, tm, D), tile), pl.BlockSpec((1, 1, tm, 1), tile),
            pl.BlockSpec((1, D, F), lambda e, b, i: (e, 0, 0)),
            pl.BlockSpec((1, D, F), lambda e, b, i: (e, 0, 0)),
            pl.BlockSpec((1, F, D), lambda e, b, i: (e, 0, 0)),
        ],
        out_specs=pl.BlockSpec((1, 1, tm, D), tile),
        out_shape=jax.ShapeDtypeStruct((B, E, cap, D), F32),
        compiler_params=_params("parallel", "parallel", "parallel"), name="experts",
    )(xe, g, wg, wu, wd)


def _ec_ffn(x, g, sh, sc, w_router, wg, wu, wd, tm_router, tm_expert):
    B, N, D = x.shape
    cap = max(1, EC_FACTOR * N // N_EXPERTS)
    h, aff = _router(x, g, sh, sc, w_router, tm_router)
    gsel, idx = lax.top_k(aff.transpose(0, 2, 1), cap)
    bidx = jnp.arange(B)[:, None, None]
    xe = h[bidx, idx]
    ye = _experts(xe, gsel[..., None], wg, wu, wd, min(tm_expert, cap))
    return jnp.zeros((B, N, D), F32).at[bidx, idx].add(ye)


DFT_RADIX = 128


def _dot3(a_hi, a_lo, b_hi, b_lo):
    return _dot(a_hi, b_hi) + _dot(a_lo, b_hi) + _dot(a_hi, b_lo)


def _cos_sin(n_rows, n_cols, period):
    k = (np.arange(n_rows)[:, None] * np.arange(n_cols)[None, :]) % period
    ang = 2.0 * np.pi * k.astype(np.float64) / period
    return np.cos(ang), np.sin(ang)


def _const_split(m):
    return _split_bf16(jnp.asarray(m, F32))


def _chan_dft_kernel(u_ref, mh_ref, ml_ref, xr_ref, xi_ref):
    uh, ul = _split_bf16(u_ref[0])
    y = _dot3(uh, ul, mh_ref[...], ml_ref[...])
    c = xr_ref.shape[-1]
    xr_ref[0] = y[:, :c]
    xi_ref[0] = y[:, c:]


def _chan_dft(u, tm):
    B, N, C = u.shape
    cc, sc = _cos_sin(C, C, C)
    mh, ml = _const_split(np.concatenate([cc, -sc], axis=1))
    row = lambda b, i: (b, i, 0)
    const2 = lambda b, i: (0, 0)
    return pl.pallas_call(
        _chan_dft_kernel, grid=(B, N // tm),
        in_specs=[pl.BlockSpec((1, tm, C), row), pl.BlockSpec((C, 2 * C), const2), pl.BlockSpec((C, 2 * C), const2)],
        out_specs=[pl.BlockSpec((1, tm, C), row)] * 2,
        out_shape=[jax.ShapeDtypeStruct((B, N, C), F32)] * 2,
        compiler_params=_params("parallel", "parallel"), name="chan_dft",
    )(u, mh, ml)


def _dft_left_kernel(xr_ref, xi_ref, mh_ref, ml_ref, o_ref, *, scale):
    xh, xl = _split_bf16(jnp.concatenate([xr_ref[0], xi_ref[0]], axis=0))
    o_ref[0] = _dot3(mh_ref[...], ml_ref[...], xh, xl) * scale


def _dft_left(xr, xi, m, scale, tn):
    B, K, cols = xr.shape
    R = m.shape[0]
    mh, ml = _const_split(m)
    col = lambda b, j: (b, 0, j)
    const2 = lambda b, j: (0, 0)
    return pl.pallas_call(
        functools.partial(_dft_left_kernel, scale=scale), grid=(B, cols // tn),
        in_specs=[pl.BlockSpec((1, K, tn), col), pl.BlockSpec((1, K, tn), col),
                  pl.BlockSpec((R, 2 * K), const2), pl.BlockSpec((R, 2 * K), const2)],
        out_specs=pl.BlockSpec((1, R, tn), col),
        out_shape=jax.ShapeDtypeStruct((B, R, cols), F32),
        compiler_params=_params("parallel", "parallel"), name="dft_left",
    )(xr, xi, mh, ml)


DFT_K1_PER_STEP = 8


def _dft_twiddle_kernel(y_ref, tc_ref, ts_ref, dh_ref, dl_ref, o_ref, *, scale):
    tc = tc_ref[0]
    ts = ts_ref[0]
    for kk in range(DFT_K1_PER_STEP):
        yr = y_ref[0, 0, kk]
        yi = y_ref[0, 1, kk]
        c, s = tc[:, kk:kk + 1], ts[:, kk:kk + 1]
        zh, zl = _split_bf16(jnp.concatenate([yr * c + yi * s, yi * c - yr * s], axis=0))
        o_ref[0, kk] = _dot3(dh_ref[...], dl_ref[...], zh, zl) * scale


def _fourier_real(u):
    B, N, C = u.shape
    scale = 1.0 / math.sqrt(N * C)
    xr, xi = _chan_dft(u, min(N, 512))
    if N % (DFT_RADIX * DFT_K1_PER_STEP) != 0:
        assert N <= 1024
        cn, sn = _cos_sin(N, N, N)
        return _dft_left(xr, xi, np.concatenate([cn, sn], axis=1), scale, C)
    n1, n2 = N // DFT_RADIX, DFT_RADIX
    c1, s1 = _cos_sin(n1, n1, n1)
    m1 = np.block([[c1, s1], [-s1, c1]])
    cols = n2 * C
    yy = _dft_left(xr.reshape(B, n1, cols), xi.reshape(B, n1, cols), m1, 1.0, min(cols, 2048))
    yy = yy.reshape(B, 2, n1, n2, C)
    tcos, tsin = _cos_sin(n1, n2, N)
    grp = lambda t: jnp.asarray(t.reshape(n1 // DFT_K1_PER_STEP, DFT_K1_PER_STEP, n2).transpose(0, 2, 1), F32)
    c2, s2 = _cos_sin(n2, n2, n2)
    dh, dl = _const_split(np.concatenate([c2, s2], axis=1))
    fp = pl.pallas_call(
        functools.partial(_dft_twiddle_kernel, scale=scale), grid=(B, n1 // DFT_K1_PER_STEP),
        in_specs=[
            pl.BlockSpec((1, 2, DFT_K1_PER_STEP, n2, C), lambda b, g: (b, 0, g, 0, 0)),
            pl.BlockSpec((1, n2, DFT_K1_PER_STEP), lambda b, g: (g, 0, 0)),
            pl.BlockSpec((1, n2, DFT_K1_PER_STEP), lambda b, g: (g, 0, 0)),
            pl.BlockSpec((n2, 2 * n2), lambda b, g: (0, 0)), pl.BlockSpec((n2, 2 * n2), lambda b, g: (0, 0)),
        ],
        out_specs=pl.BlockSpec((1, DFT_K1_PER_STEP, n2, C), lambda b, g: (b, g, 0, 0)),
        out_shape=jax.ShapeDtypeStruct((B, n1, n2, C), F32),
        compiler_params=_params("parallel", "parallel"), name="dft_twiddle",
    )(yy, grp(tcos), grp(tsin), dh, dl)
    return fp.transpose(0, 2, 1, 3).reshape(B, N, C)


def _ctx_attn_kernel(q_ref, kv_ref, o_ref):
    q = q_ref[0]
    kv = kv_ref[0]
    for h in range(NA_HEADS):
        ks = slice(h * NA_HEAD_DIM, (h + 1) * NA_HEAD_DIM)
        vs = slice(NA_W + h * NA_HEAD_DIM, NA_W + (h + 1) * NA_HEAD_DIM)
        s = _dot_nt(q[:, ks], kv[:, ks])
        p = jnp.exp(s - jnp.max(s, axis=-1, keepdims=True))
        l = jnp.sum(p, axis=-1, keepdims=True)
        o_ref[0, :, ks] = _dot(p.astype(BF16), kv[:, vs]) / l


def _ctx_dense_attn(q, kv):
    B, Q, _ = q.shape
    return pl.pallas_call(
        _ctx_attn_kernel, grid=(B,),
        in_specs=[pl.BlockSpec((1, Q, NA_W), lambda b: (b, 0, 0)),
                  pl.BlockSpec((1, kv.shape[1], 2 * NA_W), lambda b: (b, 0, 0))],
        out_specs=pl.BlockSpec((1, Q, NA_W), lambda b: (b, 0, 0)),
        out_shape=jax.ShapeDtypeStruct((B, Q, NA_W), F32),
        compiler_params=_params("parallel"), name="ctx_attn",
    )(q, kv)


def _tile(g, n):
    return jnp.tile(g.astype(F32), n)[None, :]


def kernel(x, c, ctx, c_ctx, w_ada, b_ada, g_mix, g_ffn, w_in, na_q_g, na_k_g, na_rpb, df_q_g, df_k_g,
           df_lambda, df_subln_g, pool_w, pool_scale, fnet_w, w_branch, w_out, w_router, w_gate_e, w_up_e,
           w_down_e):
    B, T, D = x.shape
    ctx_len = ctx.shape[1]
    rope = _rope_tables(T)
    s_c = jax.nn.silu(c)
    s_cc = jax.nn.silu(c_ctx)
    tm = 256
    tq, tk = 1024, 1280
    for l in range(DEPTH):
        last = l == DEPTH - 1
        lam_init = 0.8 - 0.6 * math.exp(-0.3 * l)
        mod = jnp.dot(s_c, w_ada[l], precision=lax.Precision.HIGHEST) + b_ada[l]
        sh1, sc1, gt1, sh2, sc2, gt2 = [m[:, None, :] for m in jnp.split(mod, 6, axis=-1)]
        cmod = jnp.dot(s_cc, w_ada[l], precision=lax.Precision.HIGHEST) + b_ada[l]
        bc = lambda m: jnp.broadcast_to(m[None, None, :], (B, 1, D))
        csh1, csc1, cgt1, csh2, csc2, cgt2 = [bc(m) for m in jnp.split(cmod, 6, axis=-1)]

        w_bf = w_in[l].astype(BF16)
        gq, gk = _tile(na_q_g[l], NA_HEADS), _tile(na_k_g[l], NA_HEADS)
        gdq, gdk = _tile(df_q_g[l], 2 * DIFF_HEADS), _tile(df_k_g[l], 2 * DIFF_HEADS)
        gmix = g_mix[l][None, :]
        gffn = g_ffn[l][None, :]
        subg = df_subln_g[l][:, None].astype(F32)
        wbr = w_branch[l].astype(BF16)
        wf = fnet_w[l].astype(BF16)
        wpool = jax.scipy.linalg.block_diag(*[pool_w[l, gi] for gi in range(len(POOL_WINDOWS))]).astype(BF16)
        pscale = pool_scale[l][None, :].astype(F32)
        wo = w_out[l].astype(BF16)
        wg, wu, wd = w_gate_e[l].astype(BF16), w_up_e[l].astype(BF16), w_down_e[l].astype(BF16)

        naq, dfq, pool_in, fnet_in, gate, nakv, dfk, dfv = _inproj(
            x, gmix, sh1, sc1, w_bf, gq, gk, gdq, gdk, rope, tm)
        (cnaq, cdfq, cpool_in, cfnet_in, cgate, cnakv, cdfk, cdfv) = _inproj(
            ctx, gmix, csh1, csc1, w_bf, gq, gk, gdq, gdk, None, ctx_len)

        y_na = _na_latent(naq, nakv, cnakv, na_rpb[l])
        lamp = df_lambda[l].astype(F32)
        y_df = _diff_attn(dfq, jnp.concatenate([dfk, cdfk], axis=1), jnp.concatenate([dfv, cdfv], axis=1),
                          df_q_g[l], df_k_g[l], lamp, subg, lam_init, tq, tk)
        f_real = _fourier_real(fnet_in)
        x_new = _merge(x, gt1, y_na, y_df, pool_in, f_real, gate, wbr, wpool, pscale, wf, wo, tm)

        if not last:
            yc_na = _ctx_dense_attn(cnaq, cnakv)
            yc_df = _diff_attn(cdfq, cdfk, cdfv, df_q_g[l], df_k_g[l], lamp, subg, lam_init, ctx_len, ctx_len)
            fc_real = _fourier_real(cfnet_in)
            ctx_new = _merge(ctx, cgt1, yc_na, yc_df, cpool_in, fc_real, cgate, wbr, wpool, pscale, wf, wo,
                             ctx_len)

        x = x_new
        x = x + gt2 * _ec_ffn(x, gffn, sh2, sc2, w_router[l], wg, wu, wd, tm, 1024)
        if not last:
            ctx = ctx_new
            ctx = ctx + cgt2 * _ec_ffn(ctx, gffn, csh2, csc2, w_router[l], wg, wu, wd, ctx_len, 1024)
    return x
```

```python
import functools
import math

import jax
import jax.numpy as jnp
import numpy as np
from jax import lax
from jax.experimental import pallas as pl
from jax.experimental.pallas import tpu as pltpu

F32 = jnp.float32
BF16 = jnp.bfloat16

D_MODEL = 1024
DEPTH = 2
GRID_W = 64
EPS = 1e-6
ROPE_BASE = 10000.0

NA_HEADS = 4
NA_HEAD_DIM = 64
NA_WIN_R = 8
NA_WIN_C = 16
NA_W = NA_HEADS * NA_HEAD_DIM

DIFF_HEADS = 4
DIFF_HEAD_DIM = 32
DIFF_QK_W = DIFF_HEADS * 2 * DIFF_HEAD_DIM
DIFF_V_W = DIFF_HEADS * 2 * DIFF_HEAD_DIM

POOL_WINDOWS = (2, 4, 8, 16)
POOL_GROUP_W = 64
POOL_W = len(POOL_WINDOWS) * POOL_GROUP_W
FNET_W = 256
N_BRANCH = 4
BRANCH_W = 256

OFF_NA_Q = 0
OFF_DF_Q = OFF_NA_Q + NA_W
OFF_POOL = OFF_DF_Q + DIFF_QK_W
OFF_FNET = OFF_POOL + POOL_W
OFF_GATE = OFF_FNET + FNET_W
OFF_KV = OFF_GATE + N_BRANCH * D_MODEL
KV_W = 2 * NA_W + DIFF_QK_W + DIFF_V_W
IN_COLS = OFF_KV + KV_W

N_EXPERTS = 16
EC_FACTOR = 2
EXPERT_FF = 1408

VMEM_LIMIT_BYTES = 56 * 1024 * 1024
LANES = 128
NEG_BIG = -1e30


def _params(*sem):
    return pltpu.CompilerParams(dimension_semantics=sem, vmem_limit_bytes=VMEM_LIMIT_BYTES)


def _split_bf16(a):
    hi = a.astype(BF16)
    lo = (a - hi.astype(F32)).astype(BF16)
    return hi, lo


def _dot(a, b):
    return jnp.dot(a, b, preferred_element_type=F32)


def _dot_nt(a, b):
    return lax.dot_general(a, b, (((1,), (1,)), ((), ())), preferred_element_type=F32)


def _group_rmsnorm(p, bd_ref, g):
    hi, lo = _split_bf16(p * p)
    ms = _dot(hi, bd_ref[...]) + _dot(lo, bd_ref[...])
    return p * lax.rsqrt(ms + EPS) * g


def _rope256(y, cos, s_next, s_prev):
    outs = []
    for half in range(2):
        z = y[:, half * LANES:(half + 1) * LANES]
        outs.append(z * cos + pltpu.roll(z, LANES - 8, 1) * s_next + pltpu.roll(z, 8, 1) * s_prev)
    return jnp.concatenate(outs, axis=1)


def _inproj_kernel(*refs, use_rope):
    if use_rope:
        (x_ref, g_ref, sh_ref, sc_ref, w_ref, gq_ref, gk_ref, gdq_ref, gdk_ref, bd64_ref, bd32_ref,
         cos_ref, sn_ref, sp_ref,
         naq_ref, dfq_ref, pool_ref, fnet_ref, gate_ref, nakv_ref, dfk_ref, dfv_ref) = refs
    else:
        (x_ref, g_ref, sh_ref, sc_ref, w_ref, gq_ref, gk_ref, gdq_ref, gdk_ref, bd64_ref, bd32_ref,
         naq_ref, dfq_ref, pool_ref, fnet_ref, gate_ref, nakv_ref, dfk_ref, dfv_ref) = refs
    x = x_ref[0]
    ms = jnp.mean(x * x, axis=-1, keepdims=True)
    y = x * lax.rsqrt(ms + EPS) * g_ref[...]
    h = (y * (1.0 + sc_ref[0]) + sh_ref[0]).astype(BF16)

    def seg(lo, n):
        return _dot(h, w_ref[:, lo:lo + n])

    def rope(v):
        if not use_rope:
            return v
        return _rope256(v, cos_ref[...], sn_ref[...], sp_ref[...])

    naq = _group_rmsnorm(seg(OFF_NA_Q, NA_W), bd64_ref, gq_ref[...])
    naq_ref[0] = (naq * (NA_HEAD_DIM ** -0.5)).astype(BF16)
    dfq = rope(_group_rmsnorm(seg(OFF_DF_Q, DIFF_QK_W), bd32_ref, gdq_ref[...]))
    dfq_ref[0] = (dfq * (math.log2(math.e) * DIFF_HEAD_DIM ** -0.5)).astype(BF16)
    pool_ref[0] = seg(OFF_POOL, POOL_W)
    fnet_ref[0] = seg(OFF_FNET, FNET_W)
    for j in range(0, N_BRANCH * D_MODEL, 512):
        gate_ref[0, :, j:j + 512] = seg(OFF_GATE + j, 512)
    nak = _group_rmsnorm(seg(OFF_KV, NA_W), bd64_ref, gk_ref[...])
    nakv_ref[0, :, 0:NA_W] = nak.astype(BF16)
    nakv_ref[0, :, NA_W:2 * NA_W] = seg(OFF_KV + NA_W, NA_W).astype(BF16)
    dfk = rope(_group_rmsnorm(seg(OFF_KV + 2 * NA_W, DIFF_QK_W), bd32_ref, gdk_ref[...]))
    dfk_ref[0] = dfk.astype(BF16)
    dfv_ref[0] = seg(OFF_KV + 2 * NA_W + DIFF_QK_W, DIFF_V_W).astype(BF16)


def _block_diag_mean(width, group):
    i = jnp.arange(width)
    return jnp.where((i[:, None] // group) == (i[None, :] // group), 1.0 / group, 0.0).astype(BF16)


def _rope_tables(T):
    t = jnp.arange(T)
    j = jnp.arange(LANES)
    jj = j % DIFF_HEAD_DIM
    quarter = DIFF_HEAD_DIM // 4
    use_row = jj < DIFF_HEAD_DIM // 2
    first = (jj % (DIFF_HEAD_DIM // 2)) < quarter
    inv = ROPE_BASE ** (-(jj % quarter).astype(F32) / quarter)
    pos = jnp.where(use_row[None, :], (t // GRID_W)[:, None], (t % GRID_W)[:, None]).astype(F32)
    ang = pos * inv[None, :]
    cos, sin = jnp.cos(ang), jnp.sin(ang)
    s_next = jnp.where(first[None, :], -sin, 0.0)
    s_prev = jnp.where(first[None, :], 0.0, sin)
    return cos, s_next, s_prev


def _inproj(x, g, sh, sc, w_bf16, gq, gk, gdq, gdk, rope, tm):
    B, T, D = x.shape
    use_rope = rope is not None
    row = lambda b, i: (b, i, 0)
    const2 = lambda b, i: (0, 0)
    perb = lambda b, i: (b, 0, 0)
    in_specs = [
        pl.BlockSpec((1, tm, D), row),
        pl.BlockSpec((1, D), const2),
        pl.BlockSpec((1, 1, D), perb),
        pl.BlockSpec((1, 1, D), perb),
        pl.BlockSpec((D, IN_COLS), const2),
        pl.BlockSpec((1, NA_W), const2), pl.BlockSpec((1, NA_W), const2),
        pl.BlockSpec((1, DIFF_QK_W), const2), pl.BlockSpec((1, DIFF_QK_W), const2),
        pl.BlockSpec((NA_W, NA_W), const2), pl.BlockSpec((DIFF_QK_W, DIFF_QK_W), const2),
    ]
    args = [x, g, sh, sc, w_bf16, gq, gk, gdq, gdk,
            _block_diag_mean(NA_W, NA_HEAD_DIM), _block_diag_mean(DIFF_QK_W, DIFF_HEAD_DIM)]
    if use_rope:
        in_specs += [pl.BlockSpec((tm, LANES), lambda b, i: (i, 0))] * 3
        args += list(rope)
    widths = [(NA_W, BF16), (DIFF_QK_W, BF16), (POOL_W, F32), (FNET_W, F32), (N_BRANCH * D_MODEL, F32),
              (2 * NA_W, BF16), (DIFF_QK_W, BF16), (DIFF_V_W, BF16)]
    out_shape = [jax.ShapeDtypeStruct((B, T, w), dt) for w, dt in widths]
    out_specs = [pl.BlockSpec((1, tm, w), row) for w, _ in widths]
    return pl.pallas_call(
        functools.partial(_inproj_kernel, use_rope=use_rope),
        grid=(B, T // tm), in_specs=in_specs, out_specs=out_specs, out_shape=out_shape,
        compiler_params=_params("parallel", "parallel"), name="inproj",
    )(*args)


NA_ROWS_PER_STEP = 4


def _na_kernel(q_ref, kv0_ref, kv1_ref, kv2_ref, ckv_ref, bias_ref, o_ref):
    q = q_ref[0]
    kv = jnp.concatenate([kv0_ref[0], kv1_ref[0], kv2_ref[0]], axis=0)
    ckv = ckv_ref[0]
    for h in range(NA_HEADS):
        ks = slice(h * NA_HEAD_DIM, (h + 1) * NA_HEAD_DIM)
        vs = slice(NA_W + h * NA_HEAD_DIM, NA_W + (h + 1) * NA_HEAD_DIM)
        qh = q[:, ks]
        s = _dot_nt(qh, kv[:, ks]) + bias_ref[0, h]
        sc = _dot_nt(qh, ckv[:, ks])
        m = jnp.maximum(jnp.max(s, axis=-1, keepdims=True), jnp.max(sc, axis=-1, keepdims=True))
        p = jnp.exp(s - m)
        pc = jnp.exp(sc - m)
        l = jnp.sum(p, axis=-1, keepdims=True) + jnp.sum(pc, axis=-1, keepdims=True)
        o = _dot(p.astype(BF16), kv[:, vs]) + _dot(pc.astype(BF16), ckv[:, vs])
        o_ref[0, :, ks] = o / l


def _na_bias_table(rpb, rows):
    R = NA_ROWS_PER_STEP
    nb = rows // R
    col = jnp.arange(GRID_W)
    cs = jnp.clip(col - NA_WIN_C // 2, 0, GRID_W - NA_WIN_C)
    col_ok = (col[None, :] >= cs[:, None]) & (col[None, :] < cs[:, None] + NA_WIN_C)
    ci = jnp.clip(col[None, :] - col[:, None] + (NA_WIN_C - 1), 0, 2 * NA_WIN_C - 2)
    j = jnp.array([0, 1, nb - 1])[:, None, None]
    r = R * j + jnp.arange(R)[None, :, None]
    krow = R * (j - 1) + jnp.arange(3 * R)[None, None, :]
    rs = jnp.clip(r - NA_WIN_R // 2, 0, rows - NA_WIN_R)
    row_ok = (krow >= rs) & (krow < rs + NA_WIN_R) & (krow >= 0) & (krow < rows)
    ri = jnp.clip(krow - r + (NA_WIN_R - 1), 0, 2 * NA_WIN_R - 2)
    tab = rpb.astype(F32)[:, ri][:, :, :, :, ci]
    ok = row_ok[None, :, :, :, None, None] & col_ok[None, None, None, None]
    tab = jnp.where(ok, tab, NEG_BIG)
    tab = tab.transpose(1, 0, 2, 4, 3, 5)
    return tab.reshape(3, NA_HEADS, R * GRID_W, 3 * R * GRID_W)


def _na_latent(naq, nakv, cnakv, rpb):
    B, T, _ = naq.shape
    rows = T // GRID_W
    R = NA_ROWS_PER_STEP
    assert rows % R == 0 and rows // R >= 3 and rows >= NA_WIN_R and R + NA_WIN_R <= 3 * R
    nb = rows // R
    tb = R * GRID_W
    ctx_len = cnakv.shape[1]
    pattern = lambda j: (j > 0).astype(jnp.int32) + (j == nb - 1).astype(jnp.int32)
    kv_spec = lambda d: pl.BlockSpec((1, tb, 2 * NA_W), lambda b, j: (b, jnp.clip(j + d, 0, nb - 1), 0))
    return pl.pallas_call(
        _na_kernel, grid=(B, nb),
        in_specs=[
            pl.BlockSpec((1, tb, NA_W), lambda b, j: (b, j, 0)),
            kv_spec(-1), kv_spec(0), kv_spec(1),
            pl.BlockSpec((1, ctx_len, 2 * NA_W), lambda b, j: (b, 0, 0)),
            pl.BlockSpec((1, NA_HEADS, tb, 3 * tb), lambda b, j: (pattern(j), 0, 0, 0)),
        ],
        out_specs=pl.BlockSpec((1, tb, NA_W), lambda b, j: (b, j, 0)),
        out_shape=jax.ShapeDtypeStruct((B, T, NA_W), F32),
        compiler_params=_params("parallel", "parallel"), name="na_latent",
    )(naq, nakv, nakv, nakv, cnakv, _na_bias_table(rpb, rows))


DIFF_FIXED_SHIFT_MAX = 60.0


def _diff_kernel(shift_ref, lam_ref, g_ref, qt_ref, k_ref, vt_ref, o_ref, qx_sc, m_sc, l_sc, acc_sc, *,
                 lam_init, tk, nk, online):
    h = pl.program_id(1)
    qt = qt_ref[0]
    tq = qt.shape[1]
    grp = lax.broadcasted_iota(jnp.int32, qt.shape, 0) // DIFF_HEAD_DIM
    for mi in range(2):
        qx_sc[mi] = jnp.where(grp == 2 * h + mi, qt, jnp.zeros_like(qt))
    m_sc[...] = jnp.full(m_sc.shape, -jnp.inf, F32)
    l_sc[...] = jnp.zeros(l_sc.shape, F32)
    acc_sc[...] = jnp.zeros(acc_sc.shape, F32)

    def body(k, carry):
        off = pl.multiple_of(k * tk, tk)
        kb = k_ref[0, pl.ds(off, tk), :]
        vtb = vt_ref[0, :, pl.ds(off, tk)]
        for mi in range(2):
            s = _dot(kb, qx_sc[mi])
            if online:
                m_prev = m_sc[mi]
                m_new = jnp.maximum(m_prev, jnp.max(s, axis=0, keepdims=True))
                alpha = jnp.exp2(m_prev - m_new)
                p = jnp.exp2(s - m_new)
                l_sc[mi] = alpha * l_sc[mi] + p.reshape(tk // 8, 8, tq).sum(axis=0)
                acc_sc[mi] = alpha * acc_sc[mi] + _dot(vtb, p.astype(BF16))
                m_sc[mi] = m_new
            else:
                p = jnp.exp2(s - shift_ref[0])
                l_sc[mi] += p.reshape(tk // 8, 8, tq).sum(axis=0)
                acc_sc[mi] += _dot(vtb, p.astype(BF16))
        return carry

    lax.fori_loop(0, nk, body, 0)

    lv = lam_ref[...]
    lam = (jnp.exp(jnp.sum(lv[0:1] * lv[1:2], axis=-1, keepdims=True))
           - jnp.exp(jnp.sum(lv[2:3] * lv[3:4], axis=-1, keepdims=True)) + lam_init)
    l0 = jnp.sum(l_sc[0], axis=0, keepdims=True)
    l1 = jnp.sum(l_sc[1], axis=0, keepdims=True)
    o = acc_sc[0] / l0 - lam * (acc_sc[1] / l1)
    ms = jnp.mean(o * o, axis=0, keepdims=True)
    o_ref[0] = o * lax.rsqrt(ms + EPS) * g_ref[...] * (1.0 - lam_init)


def _diff_call(shift, qt, k, vt, lam_params, subln_g, lam_init, tq, tk, online):
    B, W, Tq = qt.shape
    Tk = k.shape[1]
    dv = W // DIFF_HEADS
    return pl.pallas_call(
        functools.partial(_diff_kernel, lam_init=lam_init, tk=tk, nk=Tk // tk, online=online),
        grid=(B, DIFF_HEADS, Tq // tq),
        in_specs=[
            pl.BlockSpec(memory_space=pltpu.SMEM),
            pl.BlockSpec((4, DIFF_HEAD_DIM), lambda b, h, i: (0, 0)),
            pl.BlockSpec((dv, 1), lambda b, h, i: (0, 0)),
            pl.BlockSpec((1, W, tq), lambda b, h, i: (b, 0, i)),
            pl.BlockSpec((1, Tk, W), lambda b, h, i: (b, 0, 0)),
            pl.BlockSpec((1, dv, Tk), lambda b, h, i: (b, h, 0)),
        ],
        out_specs=pl.BlockSpec((1, dv, tq), lambda b, h, i: (b, h, i)),
        out_shape=jax.ShapeDtypeStruct((B, W, Tq), F32),
        scratch_shapes=[pltpu.VMEM((2, W, tq), BF16), pltpu.VMEM((2, 1, tq), F32),
                        pltpu.VMEM((2, 8, tq), F32), pltpu.VMEM((2, dv, tq), F32)],
        compiler_params=_params("parallel", "parallel", "parallel"),
        name="diff_attn_online" if online else "diff_attn_fixed",
    )(shift, lam_params, subln_g, qt, k, vt)


def _diff_attn(dfq, dfk, dfv, gq, gk, lam_params, subln_g, lam_init, tq, tk):
    qt = dfq.transpose(0, 2, 1)
    vt = dfv.transpose(0, 2, 1)
    bound = (1.02 * math.log2(math.e) * DIFF_HEAD_DIM ** 0.5) * jnp.max(jnp.abs(gq)) * jnp.max(jnp.abs(gk))
    shift = bound.astype(F32).reshape(1)
    args = (shift, qt, dfk, vt, lam_params, subln_g, lam_init, tq, tk)
    out = lax.cond(bound <= DIFF_FIXED_SHIFT_MAX,
                   lambda: _diff_call(*args, online=False), lambda: _diff_call(*args, online=True))
    return out.transpose(0, 2, 1)


POOL_HALO = 8


def _pool_mixer(u, prev, nxt, wp_ref, scale, n_total):
    tm = u.shape[0]
    i = pl.program_id(1)
    prev = jnp.where(i > 0, prev, 0.0)
    nxt = jnp.where(i < pl.num_programs(1) - 1, nxt, 0.0)
    ext = jnp.concatenate([prev, u, nxt], axis=0)
    sh = lambda d: ext[POOL_HALO + d:POOL_HALO + d + tm]
    t = i * tm + lax.broadcasted_iota(jnp.int32, (tm, 1), 0)
    lane_grp = lax.broadcasted_iota(jnp.int32, (1, POOL_W), 1) // POOL_GROUP_W
    win = None
    mean = None
    prev_half = 0
    for gi, w in enumerate(POOL_WINDOWS):
        half = w // 2
        for d in range(prev_half, half):
            add = sh(d) + sh(-d - 1)
            win = add if win is None else win + add
        prev_half = half
        cnt = (jnp.minimum(t + half, n_total) - jnp.maximum(t - half, 0)).astype(F32)
        m_w = win / cnt
        mean = m_w if mean is None else jnp.where(lane_grp >= gi, m_w, mean)
    return _dot((mean - u).astype(BF16), wp_ref[...]) * scale


def _merge_kernel(x_ref, gt_ref, yna_ref, ydf_ref, u_ref, up_ref, un_ref, f_ref, gate_ref, wbr_ref, wp_ref,
                  ps_ref, wf_ref, wo_ref, o_ref, *, n_total):
    yf = _dot(f_ref[0].astype(BF16), wf_ref[...])
    ypool = _pool_mixer(u_ref[0], up_ref[0], un_ref[0], wp_ref, ps_ref[...], n_total)
    ys = (yna_ref[0], ydf_ref[0], ypool, yf)
    acc = None
    for i, y in enumerate(ys):
        g = jax.nn.sigmoid(gate_ref[0, :, i * D_MODEL:(i + 1) * D_MODEL])
        term = g * _dot(y.astype(BF16), wbr_ref[i])
        acc = term if acc is None else acc + term
    o_ref[0] = x_ref[0] + gt_ref[0] * _dot(acc.astype(BF16), wo_ref[...])


def _merge(x, gt, y_na, y_df, pool_in, f_real, gate, w_br, w_pool_bd, pool_scale, w_f, w_o, tm):
    B, T, D = x.shape
    assert tm % POOL_HALO == 0 and POOL_HALO >= max(POOL_WINDOWS) // 2
    hb = tm // POOL_HALO
    row = lambda b, i: (b, i, 0)
    const2 = lambda b, i: (0, 0)
    return pl.pallas_call(
        functools.partial(_merge_kernel, n_total=T), grid=(B, T // tm),
        in_specs=[
            pl.BlockSpec((1, tm, D), row),
            pl.BlockSpec((1, 1, D), lambda b, i: (b, 0, 0)),
            pl.BlockSpec((1, tm, BRANCH_W), row), pl.BlockSpec((1, tm, BRANCH_W), row),
            pl.BlockSpec((1, tm, POOL_W), row),
            pl.BlockSpec((1, POOL_HALO, POOL_W), lambda b, i: (b, jnp.maximum(i * hb - 1, 0), 0)),
            pl.BlockSpec((1, POOL_HALO, POOL_W), lambda b, i: (b, jnp.minimum((i + 1) * hb, T // POOL_HALO - 1), 0)),
            pl.BlockSpec((1, tm, FNET_W), row),
            pl.BlockSpec((1, tm, N_BRANCH * D), row),
            pl.BlockSpec((N_BRANCH, BRANCH_W, D), lambda b, i: (0, 0, 0)),
            pl.BlockSpec((POOL_W, POOL_W), const2), pl.BlockSpec((1, POOL_W), const2),
            pl.BlockSpec((FNET_W, FNET_W), const2),
            pl.BlockSpec((D, D), const2),
        ],
        out_specs=pl.BlockSpec((1, tm, D), row),
        out_shape=jax.ShapeDtypeStruct((B, T, D), F32),
        compiler_params=_params("parallel", "parallel"), name="merge",
    )(x, gt, y_na, y_df, pool_in, pool_in, pool_in, f_real, gate, w_br, w_pool_bd, pool_scale, w_f, w_o)


TOK_BLOCK = LANES
SEL_GROUP = 8
COMBINE_WIN = 32


def _router_kernel(x_ref, g_ref, sh_ref, sc_ref, wrh_ref, wrl_ref, h_ref, aff_ref):
    x = x_ref[0]
    ms = jnp.mean(x * x, axis=-1, keepdims=True)
    h = x * lax.rsqrt(ms + EPS) * g_ref[...] * (1.0 + sc_ref[0]) + sh_ref[0]
    h_ref[0] = h.astype(BF16)
    hi, lo = _split_bf16(h)
    logits = _dot_nt(wrh_ref[...], hi) + _dot_nt(wrh_ref[...], lo) + _dot_nt(wrl_ref[...], hi)
    logits = logits - jnp.max(logits, axis=0, keepdims=True)
    e = jnp.exp(logits)
    aff_ref[0] = e / jnp.sum(e, axis=0, keepdims=True)


def _router(x, g, sh, sc, w_router, tm):
    B, T, D = x.shape
    E = w_router.shape[1]
    wrh, wrl = _split_bf16(w_router.T)
    row = lambda b, i: (b, i, 0)
    return pl.pallas_call(
        _router_kernel, grid=(B, T // tm),
        in_specs=[
            pl.BlockSpec((1, tm, D), row), pl.BlockSpec((1, D), lambda b, i: (0, 0)),
            pl.BlockSpec((1, 1, D), lambda b, i: (b, 0, 0)), pl.BlockSpec((1, 1, D), lambda b, i: (b, 0, 0)),
            pl.BlockSpec((E, D), lambda b, i: (0, 0)), pl.BlockSpec((E, D), lambda b, i: (0, 0)),
        ],
        out_specs=[pl.BlockSpec((1, tm, D), row), pl.BlockSpec((1, E, tm), lambda b, i: (b, 0, i))],
        out_shape=[jax.ShapeDtypeStruct((B, T, D), BF16), jax.ShapeDtypeStruct((B, E, T), F32)],
        compiler_params=_params("parallel", "parallel"), name="router",
    )(x, g, sh, sc, wrh, wrl)


def _select_kernel(aff_ref, u_ref, ls_ref, idx_ref, gate_ref, pos_ref, off_ref, tot_ref, *, cap):
    G, nb, _ = aff_ref.shape
    aff = aff_ref[...]
    bits = pltpu.bitcast(aff, jnp.int32)
    kf = float(cap)
    ones_f = lambda m: jnp.where(m, 1.0, 0.0)
    count = lambda m: jnp.sum(ones_f(m), axis=(1, 2), keepdims=True)

    def search(it, lo):
        cand = lo | jnp.left_shift(jnp.int32(1), 30 - it)
        return jnp.where(count(bits >= cand) >= kf, cand, lo)

    thr = lax.fori_loop(0, 31, search, jnp.zeros((G, 1, 1), jnp.int32))
    need = kf - count(bits > thr)

    u = u_ref[...]
    ls = ls_ref[...]
    ones_m = jnp.ones((LANES, LANES), BF16)

    def prefix(mf):
        mb = mf.astype(BF16)
        tot = _dot(mb, ones_m)
        off = _dot(ls, tot.astype(BF16))
        return _dot(mb, u) + off, off, tot

    blk_i = lax.broadcasted_iota(jnp.int32, (nb, LANES), 0)
    lane_i = lax.broadcasted_iota(jnp.int32, (nb, LANES), 1)
    s_row = lax.broadcasted_iota(jnp.int32, (1, cap), 1).astype(F32)
    col_nb = lax.broadcasted_iota(jnp.int32, (nb, 1), 0).astype(F32)
    col_l = lax.broadcasted_iota(jnp.int32, (LANES, 1), 0).astype(F32)
    for g in range(G):
        gt = bits[g] > thr[g]
        eq = bits[g] == thr[g]
        eqf = ones_f(eq)
        tie_before = prefix(eqf)[0] - eqf
        sel = gt | (eq & (tie_before < need[g]))
        pin, off, tot = prefix(ones_f(sel))
        pos_ref[g] = jnp.where(sel, pin - 1.0, -1.0).astype(jnp.int32)
        diag = blk_i == lane_i
        off_ref[g] = jnp.sum(jnp.where(diag, off, 0.0), axis=0, keepdims=True).astype(jnp.int32)
        tot_ref[g] = jnp.sum(jnp.where(diag, tot, 0.0), axis=0, keepdims=True).astype(jnp.int32)
        blk = jnp.sum(ones_f((off + tot)[:, 0:1] <= s_row), axis=0, keepdims=True)
        oh = jnp.where(col_nb == blk, 1.0, 0.0).astype(BF16)
        pin_t = pin.T
        p_hi = jnp.floor(pin_t * (1.0 / LANES))
        p_lo = pin_t - p_hi * LANES
        rows = _dot(p_hi.astype(BF16), oh) * LANES + _dot(p_lo.astype(BF16), oh)
        lane = jnp.sum(ones_f(rows <= s_row), axis=0, keepdims=True)
        idx_ref[g] = (blk * LANES + lane).astype(jnp.int32)
        a_t = aff[g].T
        a1 = a_t.astype(BF16)
        r1 = a_t - a1.astype(F32)
        a2 = r1.astype(BF16)
        a3 = (r1 - a2.astype(F32)).astype(BF16)
        arow = _dot(a1, oh) + _dot(a2, oh) + _dot(a3, oh)
        gate_ref[g] = jnp.sum(jnp.where(col_l == lane, arow, 0.0), axis=0, keepdims=True)


def _select(aff, cap):
    B, E, T = aff.shape
    nb = T // TOK_BLOCK
    R = B * E
    G = SEL_GROUP
    i = np.arange(LANES)
    u = jnp.asarray(i[:, None] <= i[None, :], BF16)
    j = np.arange(nb)
    ls = jnp.asarray(j[None, :] < j[:, None], BF16)
    grp = lambda r: (r, 0, 0)
    outs = pl.pallas_call(
        functools.partial(_select_kernel, cap=cap), grid=(R // G,),
        in_specs=[pl.BlockSpec((G, nb, LANES), grp), pl.BlockSpec((LANES, LANES), lambda r: (0, 0)),
                  pl.BlockSpec((nb, nb), lambda r: (0, 0))],
        out_specs=[pl.BlockSpec((G, 1, cap), grp), pl.BlockSpec((G, 1, cap), grp),
                   pl.BlockSpec((G, nb, LANES), grp), pl.BlockSpec((G, 1, LANES), grp),
                   pl.BlockSpec((G, 1, LANES), grp)],
        out_shape=[jax.ShapeDtypeStruct((R, 1, cap), jnp.int32), jax.ShapeDtypeStruct((R, 1, cap), F32),
                   jax.ShapeDtypeStruct((R, nb, LANES), jnp.int32), jax.ShapeDtypeStruct((R, 1, LANES), jnp.int32),
                   jax.ShapeDtypeStruct((R, 1, LANES), jnp.int32)],
        compiler_params=_params("parallel"), name="select",
    )(aff.reshape(R, nb, LANES), u, ls)
    idx, gate, pos, off, tot = outs
    return (idx.reshape(B, E, cap), gate.reshape(B, E, cap), pos.reshape(B, E, T),
            off[:, 0, :nb].reshape(B, E, nb), tot[:, 0, :nb].reshape(B, E, nb))


def _combine_kernel(off_ref, tot_ref, x_ref, gt_ref, pos_ref, ye_hbm, o_ref, buf, xbuf, oh_sc, acc_sc, sem,
                    xsem, *, nt, cap):
    n = pl.program_id(0)
    E, W = N_EXPERTS, COMBINE_WIN

    def window(step, e):
        b = step // nt
        o = off_ref[(b * E + e) * nt + step % nt]
        return b, o, jnp.minimum((o // 8) * 8, cap - W)

    def fetch(step, slot):
        for e in range(E):
            b, _, st = window(step, e)
            pltpu.make_async_copy(ye_hbm.at[b, e, pl.ds(pl.multiple_of(st, 8), W)], buf.at[slot, e],
                                  sem.at[slot]).start()

    @pl.when(n == 0)
    def _():
        fetch(0, 0)

    @pl.when(n + 1 < pl.num_programs(0))
    def _():
        fetch(n + 1, (n + 1) % 2)

    slot = n % 2
    for e in range(E):
        pltpu.make_async_copy(ye_hbm.at[0, 0, pl.ds(0, W)], buf.at[slot, e], sem.at[slot]).wait()

    pos = pos_ref[0]
    w_i = lax.broadcasted_iota(jnp.int32, (1, W), 1)
    for e in range(E):
        _, _, st = window(n, e)
        oh_sc[:, e * W:(e + 1) * W] = jnp.where(pos[:, e:e + 1] == st + w_i, 1.0, 0.0)
    oh_all = oh_sc[...].astype(BF16)
    rh, rl = _split_bf16(buf[slot].reshape(E * W, buf.shape[-1]))
    acc_sc[...] = _dot(oh_all, rh) + _dot(oh_all, rl)

    for e in range(E):
        b, o, st = window(n, e)
        t = tot_ref[(b * E + e) * nt + n % nt]
        n_extra = jnp.maximum(o + t - (st + W) + (W - 1), 0) // W

        def extra(j, carry, e=e, b=b, st=st):
            base = st + W * (j + 1)
            src = jnp.minimum(base, cap - W)
            cp = pltpu.make_async_copy(ye_hbm.at[b, e, pl.ds(pl.multiple_of(src, 8), W)], xbuf, xsem.at[0])
            cp.start()
            cp.wait()
            slot_i = src + w_i
            oh = jnp.where((pos[:, e:e + 1] == slot_i) & (slot_i >= base), 1.0, 0.0).astype(BF16)
            xh, xl = _split_bf16(xbuf[...])
            acc_sc[...] += _dot(oh, xh) + _dot(oh, xl)
            return carry

        lax.fori_loop(0, n_extra, extra, 0)

    o_ref[0] = x_ref[0] + gt_ref[0] * acc_sc[...]


def _combine(x, gt, pos_t, off, tot, ye):
    B, T, D = x.shape
    E, cap = ye.shape[1], ye.shape[2]
    nt = T // TOK_BLOCK
    W = COMBINE_WIN
    assert cap % W == 0 and W % 8 == 0
    tile = lambda n, off_r, tot_r: (n // nt, n % nt, 0)
    grid_spec = pltpu.PrefetchScalarGridSpec(
        num_scalar_prefetch=2, grid=(B * nt,),
        in_specs=[
            pl.BlockSpec((1, TOK_BLOCK, D), tile),
            pl.BlockSpec((1, 1, D), lambda n, off_r, tot_r: (n // nt, 0, 0)),
            pl.BlockSpec((1, TOK_BLOCK, E), tile),
            pl.BlockSpec(memory_space=pl.ANY),
        ],
        out_specs=pl.BlockSpec((1, TOK_BLOCK, D), tile),
        scratch_shapes=[pltpu.VMEM((2, E, W, D), F32), pltpu.VMEM((W, D), F32),
                        pltpu.VMEM((TOK_BLOCK, E * W), F32), pltpu.VMEM((TOK_BLOCK, D), F32),
                        pltpu.SemaphoreType.DMA((2,)), pltpu.SemaphoreType.DMA((1,))],
    )
    return pl.pallas_call(
        functools.partial(_combine_kernel, nt=nt, cap=cap), grid_spec=grid_spec,
        out_shape=jax.ShapeDtypeStruct((B, T, D), F32),
        compiler_params=_params("arbitrary"), name="combine",
    )(off.reshape(-1), tot.reshape(-1), x, gt, pos_t, ye)


FF_CHUNKS = ((0, 512), (512, 512), (1024, 384))


def _expert_kernel(xe_ref, g_ref, wg_ref, wu_ref, wd_ref, o_ref):
    xe = xe_ref[0, 0]
    acc = None
    for lo, n in FF_CHUNKS:
        a = _dot(xe, wg_ref[0, :, lo:lo + n])
        u = _dot(xe, wu_ref[0, :, lo:lo + n])
        hid = (a * jax.nn.sigmoid(a) * u).astype(BF16)
        part = _dot(hid, wd_ref[0, lo:lo + n, :])
        acc = part if acc is None else acc + part
    o_ref[0, 0] = acc * g_ref[0, 0]


def _experts(xe, g, wg, wu, wd, tm):
    B, E, cap, D = xe.shape
    F = wg.shape[-1]
    assert F == EXPERT_FF
    tile = lambda e, b, i: (b, e, i, 0)
    return pl.pallas_call(
        _expert_kernel, grid=(E, B, cap // tm),
        in_specs=[
            pl.BlockSpec((1, 1, tm, D), tile), pl.BlockSpec((1, 1, tm, 1), tile),
            pl.BlockSpec((1, D, F), lambda e, b, i: (e, 0, 0)),
            pl.BlockSpec((1, D, F), lambda e, b, i: (e, 0, 0)),
            pl.BlockSpec((1, F, D), lambda e, b, i: (e, 0, 0)),
        ],
        out_specs=pl.BlockSpec((1, 1, tm, D), tile),
        out_shape=jax.ShapeDtypeStruct((B, E, cap, D), F32),
        compiler_params=_params("parallel", "parallel", "parallel"), name="experts",
    )(xe, g, wg, wu, wd)


def _ec_ffn(x, gt, g, sh, sc, w_router, wg, wu, wd, tm_router, tm_expert):
    B, N, D = x.shape
    cap = max(1, EC_FACTOR * N // N_EXPERTS)
    h, aff = _router(x, g, sh, sc, w_router, tm_router)
    bidx = jnp.arange(B)[:, None, None]
    if (N // TOK_BLOCK) % 8 == 0 and cap % COMBINE_WIN == 0:
        idx, gsel, pos, off, tot = _select(aff, cap)
        ye = _experts(h[bidx, idx], gsel[..., None], wg, wu, wd, min(tm_expert, cap))
        return _combine(x, gt, pos.transpose(0, 2, 1), off, tot, ye)
    gsel, idx = lax.top_k(aff, cap)
    ye = _experts(h[bidx, idx], gsel[..., None], wg, wu, wd, min(tm_expert, cap))
    return x + gt * jnp.zeros((B, N, D), F32).at[bidx, idx].add(ye)


DFT_RADIX = 128


def _dot3(a_hi, a_lo, b_hi, b_lo):
    return _dot(a_hi, b_hi) + _dot(a_lo, b_hi) + _dot(a_hi, b_lo)


def _cos_sin(n_rows, n_cols, period):
    k = (np.arange(n_rows)[:, None] * np.arange(n_cols)[None, :]) % period
    ang = 2.0 * np.pi * k.astype(np.float64) / period
    return np.cos(ang), np.sin(ang)


def _const_split(m):
    return _split_bf16(jnp.asarray(m, F32))


def _chan_dft_kernel(u_ref, mh_ref, ml_ref, xr_ref, xi_ref):
    uh, ul = _split_bf16(u_ref[0])
    y = _dot3(uh, ul, mh_ref[...], ml_ref[...])
    c = xr_ref.shape[-1]
    xr_ref[0] = y[:, :c]
    xi_ref[0] = y[:, c:]


def _chan_dft(u, tm):
    B, N, C = u.shape
    cc, sc = _cos_sin(C, C, C)
    mh, ml = _const_split(np.concatenate([cc, -sc], axis=1))
    row = lambda b, i: (b, i, 0)
    const2 = lambda b, i: (0, 0)
    return pl.pallas_call(
        _chan_dft_kernel, grid=(B, N // tm),
        in_specs=[pl.BlockSpec((1, tm, C), row), pl.BlockSpec((C, 2 * C), const2), pl.BlockSpec((C, 2 * C), const2)],
        out_specs=[pl.BlockSpec((1, tm, C), row)] * 2,
        out_shape=[jax.ShapeDtypeStruct((B, N, C), F32)] * 2,
        compiler_params=_params("parallel", "parallel"), name="chan_dft",
    )(u, mh, ml)


def _dft_left_kernel(xr_ref, xi_ref, mh_ref, ml_ref, o_ref, *, scale):
    xh, xl = _split_bf16(jnp.concatenate([xr_ref[0], xi_ref[0]], axis=0))
    o_ref[0] = _dot3(mh_ref[...], ml_ref[...], xh, xl) * scale


def _dft_left(xr, xi, m, scale, tn):
    B, K, cols = xr.shape
    R = m.shape[0]
    mh, ml = _const_split(m)
    col = lambda b, j: (b, 0, j)
    const2 = lambda b, j: (0, 0)
    return pl.pallas_call(
        functools.partial(_dft_left_kernel, scale=scale), grid=(B, cols // tn),
        in_specs=[pl.BlockSpec((1, K, tn), col), pl.BlockSpec((1, K, tn), col),
                  pl.BlockSpec((R, 2 * K), const2), pl.BlockSpec((R, 2 * K), const2)],
        out_specs=pl.BlockSpec((1, R, tn), col),
        out_shape=jax.ShapeDtypeStruct((B, R, cols), F32),
        compiler_params=_params("parallel", "parallel"), name="dft_left",
    )(xr, xi, mh, ml)


DFT_K1_PER_STEP = 8


def _dft_twiddle_kernel(y_ref, tc_ref, ts_ref, dh_ref, dl_ref, o_ref, *, scale):
    tc = tc_ref[0]
    ts = ts_ref[0]
    for kk in range(DFT_K1_PER_STEP):
        yr = y_ref[0, 0, kk]
        yi = y_ref[0, 1, kk]
        c, s = tc[:, kk:kk + 1], ts[:, kk:kk + 1]
        zh, zl = _split_bf16(jnp.concatenate([yr * c + yi * s, yi * c - yr * s], axis=0))
        o_ref[0, kk] = _dot3(dh_ref[...], dl_ref[...], zh, zl) * scale


def _fourier_real(u):
    B, N, C = u.shape
    scale = 1.0 / math.sqrt(N * C)
    xr, xi = _chan_dft(u, min(N, 512))
    if N % (DFT_RADIX * DFT_K1_PER_STEP) != 0:
        assert N <= 1024
        cn, sn = _cos_sin(N, N, N)
        return _dft_left(xr, xi, np.concatenate([cn, sn], axis=1), scale, C)
    n1, n2 = N // DFT_RADIX, DFT_RADIX
    c1, s1 = _cos_sin(n1, n1, n1)
    m1 = np.block([[c1, s1], [-s1, c1]])
    cols = n2 * C
    yy = _dft_left(xr.reshape(B, n1, cols), xi.reshape(B, n1, cols), m1, 1.0, min(cols, 2048))
    yy = yy.reshape(B, 2, n1, n2, C)
    tcos, tsin = _cos_sin(n1, n2, N)
    grp = lambda t: jnp.asarray(t.reshape(n1 // DFT_K1_PER_STEP, DFT_K1_PER_STEP, n2).transpose(0, 2, 1), F32)
    c2, s2 = _cos_sin(n2, n2, n2)
    dh, dl = _const_split(np.concatenate([c2, s2], axis=1))
    fp = pl.pallas_call(
        functools.partial(_dft_twiddle_kernel, scale=scale), grid=(B, n1 // DFT_K1_PER_STEP),
        in_specs=[
            pl.BlockSpec((1, 2, DFT_K1_PER_STEP, n2, C), lambda b, g: (b, 0, g, 0, 0)),
            pl.BlockSpec((1, n2, DFT_K1_PER_STEP), lambda b, g: (g, 0, 0)),
            pl.BlockSpec((1, n2, DFT_K1_PER_STEP), lambda b, g: (g, 0, 0)),
            pl.BlockSpec((n2, 2 * n2), lambda b, g: (0, 0)), pl.BlockSpec((n2, 2 * n2), lambda b, g: (0, 0)),
        ],
        out_specs=pl.BlockSpec((1, DFT_K1_PER_STEP, n2, C), lambda b, g: (b, g, 0, 0)),
        out_shape=jax.ShapeDtypeStruct((B, n1, n2, C), F32),
        compiler_params=_params("parallel", "parallel"), name="dft_twiddle",
    )(yy, grp(tcos), grp(tsin), dh, dl)
    return fp.transpose(0, 2, 1, 3).reshape(B, N, C)


def _ctx_attn_kernel(q_ref, kv_ref, o_ref):
    q = q_ref[0]
    kv = kv_ref[0]
    for h in range(NA_HEADS):
        ks = slice(h * NA_HEAD_DIM, (h + 1) * NA_HEAD_DIM)
        vs = slice(NA_W + h * NA_HEAD_DIM, NA_W + (h + 1) * NA_HEAD_DIM)
        s = _dot_nt(q[:, ks], kv[:, ks])
        p = jnp.exp(s - jnp.max(s, axis=-1, keepdims=True))
        l = jnp.sum(p, axis=-1, keepdims=True)
        o_ref[0, :, ks] = _dot(p.astype(BF16), kv[:, vs]) / l


def _ctx_dense_attn(q, kv):
    B, Q, _ = q.shape
    return pl.pallas_call(
        _ctx_attn_kernel, grid=(B,),
        in_specs=[pl.BlockSpec((1, Q, NA_W), lambda b: (b, 0, 0)),
                  pl.BlockSpec((1, kv.shape[1], 2 * NA_W), lambda b: (b, 0, 0))],
        out_specs=pl.BlockSpec((1, Q, NA_W), lambda b: (b, 0, 0)),
        out_shape=jax.ShapeDtypeStruct((B, Q, NA_W), F32),
        compiler_params=_params("parallel"), name="ctx_attn",
    )(q, kv)


def _tile(g, n):
    return jnp.tile(g.astype(F32), n)[None, :]


def kernel(x, c, ctx, c_ctx, w_ada, b_ada, g_mix, g_ffn, w_in, na_q_g, na_k_g, na_rpb, df_q_g, df_k_g,
           df_lambda, df_subln_g, pool_w, pool_scale, fnet_w, w_branch, w_out, w_router, w_gate_e, w_up_e,
           w_down_e):
    B, T, D = x.shape
    ctx_len = ctx.shape[1]
    rope = _rope_tables(T)
    s_c = jax.nn.silu(c)
    s_cc = jax.nn.silu(c_ctx)
    tm = 256
    tq, tk = 1024, 1280
    for l in range(DEPTH):
        last = l == DEPTH - 1
        lam_init = 0.8 - 0.6 * math.exp(-0.3 * l)
        mod = jnp.dot(s_c, w_ada[l], precision=lax.Precision.HIGHEST) + b_ada[l]
        sh1, sc1, gt1, sh2, sc2, gt2 = [m[:, None, :] for m in jnp.split(mod, 6, axis=-1)]
        cmod = jnp.dot(s_cc, w_ada[l], precision=lax.Precision.HIGHEST) + b_ada[l]
        bc = lambda m: jnp.broadcast_to(m[None, None, :], (B, 1, D))
        csh1, csc1, cgt1, csh2, csc2, cgt2 = [bc(m) for m in jnp.split(cmod, 6, axis=-1)]

        w_bf = w_in[l].astype(BF16)
        gq, gk = _tile(na_q_g[l], NA_HEADS), _tile(na_k_g[l], NA_HEADS)
        gdq, gdk = _tile(df_q_g[l], 2 * DIFF_HEADS), _tile(df_k_g[l], 2 * DIFF_HEADS)
        gmix = g_mix[l][None, :]
        gffn = g_ffn[l][None, :]
        subg = df_subln_g[l][:, None].astype(F32)
        wbr = w_branch[l].astype(BF16)
        wf = fnet_w[l].astype(BF16)
        wpool = jax.scipy.linalg.block_diag(*[pool_w[l, gi] for gi in range(len(POOL_WINDOWS))]).astype(BF16)
        pscale = pool_scale[l][None, :].astype(F32)
        wo = w_out[l].astype(BF16)
        wg, wu, wd = w_gate_e[l].astype(BF16), w_up_e[l].astype(BF16), w_down_e[l].astype(BF16)

        naq, dfq, pool_in, fnet_in, gate, nakv, dfk, dfv = _inproj(
            x, gmix, sh1, sc1, w_bf, gq, gk, gdq, gdk, rope, tm)
        (cnaq, cdfq, cpool_in, cfnet_in, cgate, cnakv, cdfk, cdfv) = _inproj(
            ctx, gmix, csh1, csc1, w_bf, gq, gk, gdq, gdk, None, ctx_len)

        y_na = _na_latent(naq, nakv, cnakv, na_rpb[l])
        lamp = df_lambda[l].astype(F32)
        y_df = _diff_attn(dfq, jnp.concatenate([dfk, cdfk], axis=1), jnp.concatenate([dfv, cdfv], axis=1),
                          df_q_g[l], df_k_g[l], lamp, subg, lam_init, tq, tk)
        f_real = _fourier_real(fnet_in)
        x_new = _merge(x, gt1, y_na, y_df, pool_in, f_real, gate, wbr, wpool, pscale, wf, wo, tm)

        if not last:
            yc_na = _ctx_dense_attn(cnaq, cnakv)
            yc_df = _diff_attn(cdfq, cdfk, cdfv, df_q_g[l], df_k_g[l], lamp, subg, lam_init, ctx_len, ctx_len)
            fc_real = _fourier_real(cfnet_in)
            ctx_new = _merge(ctx, cgt1, yc_na, yc_df, cpool_in, fc_real, cgate, wbr, wpool, pscale, wf, wo,
                             ctx_len)

        x = x_new
        x = _ec_ffn(x, gt2, gffn, sh2, sc2, w_router[l], wg, wu, wd, tm, 1024)
        if not last:
            ctx = ctx_new
            ctx = _ec_ffn(ctx, cgt2, gffn, csh2, csc2, w_router[l], wg, wu, wd, ctx_len, 1024)
    return x
```

```python
import functools
import math

import jax
import jax.numpy as jnp
import numpy as np
from jax import lax
from jax.experimental import pallas as pl
from jax.experimental.pallas import tpu as pltpu

F32 = jnp.float32
BF16 = jnp.bfloat16

D_MODEL = 1024
DEPTH = 2
GRID_W = 64
EPS = 1e-6
ROPE_BASE = 10000.0

NA_HEADS = 4
NA_HEAD_DIM = 64
NA_WIN_R = 8
NA_WIN_C = 16
NA_W = NA_HEADS * NA_HEAD_DIM

DIFF_HEADS = 4
DIFF_HEAD_DIM = 32
DIFF_QK_W = DIFF_HEADS * 2 * DIFF_HEAD_DIM
DIFF_V_W = DIFF_HEADS * 2 * DIFF_HEAD_DIM

POOL_WINDOWS = (2, 4, 8, 16)
POOL_GROUP_W = 64
POOL_W = len(POOL_WINDOWS) * POOL_GROUP_W
FNET_W = 256
N_BRANCH = 4
BRANCH_W = 256

OFF_NA_Q = 0
OFF_DF_Q = OFF_NA_Q + NA_W
OFF_POOL = OFF_DF_Q + DIFF_QK_W
OFF_FNET = OFF_POOL + POOL_W
OFF_GATE = OFF_FNET + FNET_W
OFF_KV = OFF_GATE + N_BRANCH * D_MODEL
KV_W = 2 * NA_W + DIFF_QK_W + DIFF_V_W
IN_COLS = OFF_KV + KV_W

N_EXPERTS = 16
EC_FACTOR = 2
EXPERT_FF = 1408

VMEM_LIMIT_BYTES = 56 * 1024 * 1024
LANES = 128
NEG_BIG = -1e30


def _params(*sem):
    return pltpu.CompilerParams(dimension_semantics=sem, vmem_limit_bytes=VMEM_LIMIT_BYTES)


def _split_bf16(a):
    hi = a.astype(BF16)
    lo = (a - hi.astype(F32)).astype(BF16)
    return hi, lo


def _dot(a, b):
    return jnp.dot(a, b, preferred_element_type=F32)


def _dot_nt(a, b):
    return lax.dot_general(a, b, (((1,), (1,)), ((), ())), preferred_element_type=F32)


def _group_rmsnorm(p, bd_ref, g):
    hi, lo = _split_bf16(p * p)
    ms = _dot(hi, bd_ref[...]) + _dot(lo, bd_ref[...])
    return p * lax.rsqrt(ms + EPS) * g


def _rope256(y, cos, s_next, s_prev):
    outs = []
    for half in range(2):
        z = y[:, half * LANES:(half + 1) * LANES]
        outs.append(z * cos + pltpu.roll(z, LANES - 8, 1) * s_next + pltpu.roll(z, 8, 1) * s_prev)
    return jnp.concatenate(outs, axis=1)


def _inproj_kernel(*refs, use_rope):
    if use_rope:
        (x_ref, g_ref, sh_ref, sc_ref, w_ref, gq_ref, gk_ref, gdq_ref, gdk_ref, bd64_ref, bd32_ref,
         cos_ref, sn_ref, sp_ref,
         naq_ref, dfq_ref, pool_ref, fnet_ref, gate_ref, nakv_ref, dfk_ref, dfv_ref) = refs
    else:
        (x_ref, g_ref, sh_ref, sc_ref, w_ref, gq_ref, gk_ref, gdq_ref, gdk_ref, bd64_ref, bd32_ref,
         naq_ref, dfq_ref, pool_ref, fnet_ref, gate_ref, nakv_ref, dfk_ref, dfv_ref) = refs
    x = x_ref[0]
    ms = jnp.mean(x * x, axis=-1, keepdims=True)
    y = x * lax.rsqrt(ms + EPS) * g_ref[...]
    h = (y * (1.0 + sc_ref[0]) + sh_ref[0]).astype(BF16)

    def seg(lo, n):
        return _dot(h, w_ref[:, lo:lo + n])

    def rope(v):
        if not use_rope:
            return v
        return _rope256(v, cos_ref[...], sn_ref[...], sp_ref[...])

    naq = _group_rmsnorm(seg(OFF_NA_Q, NA_W), bd64_ref, gq_ref[...])
    naq_ref[0] = (naq * (NA_HEAD_DIM ** -0.5)).astype(BF16)
    dfq = rope(_group_rmsnorm(seg(OFF_DF_Q, DIFF_QK_W), bd32_ref, gdq_ref[...]))
    dfq_ref[0] = (dfq * (math.log2(math.e) * DIFF_HEAD_DIM ** -0.5)).T.astype(BF16)
    pool_ref[0] = seg(OFF_POOL, POOL_W)
    fnet_ref[0] = seg(OFF_FNET, FNET_W)
    for j in range(0, N_BRANCH * D_MODEL, 512):
        gate_ref[0, :, j:j + 512] = jax.nn.sigmoid(seg(OFF_GATE + j, 512)).astype(BF16)
    nak = _group_rmsnorm(seg(OFF_KV, NA_W), bd64_ref, gk_ref[...])
    nakv_ref[0, :, 0:NA_W] = nak.astype(BF16)
    nakv_ref[0, :, NA_W:2 * NA_W] = seg(OFF_KV + NA_W, NA_W).astype(BF16)
    dfk = rope(_group_rmsnorm(seg(OFF_KV + 2 * NA_W, DIFF_QK_W), bd32_ref, gdk_ref[...]))
    dfk_ref[0] = dfk.astype(BF16)
    dfv_ref[0] = seg(OFF_KV + 2 * NA_W + DIFF_QK_W, DIFF_V_W).T.astype(BF16)


def _block_diag_mean(width, group):
    i = jnp.arange(width)
    return jnp.where((i[:, None] // group) == (i[None, :] // group), 1.0 / group, 0.0).astype(BF16)


def _rope_tables(T):
    t = jnp.arange(T)
    j = jnp.arange(LANES)
    jj = j % DIFF_HEAD_DIM
    quarter = DIFF_HEAD_DIM // 4
    use_row = jj < DIFF_HEAD_DIM // 2
    first = (jj % (DIFF_HEAD_DIM // 2)) < quarter
    inv = ROPE_BASE ** (-(jj % quarter).astype(F32) / quarter)
    pos = jnp.where(use_row[None, :], (t // GRID_W)[:, None], (t % GRID_W)[:, None]).astype(F32)
    ang = pos * inv[None, :]
    cos, sin = jnp.cos(ang), jnp.sin(ang)
    s_next = jnp.where(first[None, :], -sin, 0.0)
    s_prev = jnp.where(first[None, :], 0.0, sin)
    return cos, s_next, s_prev


def _inproj(x, g, sh, sc, w_bf16, gq, gk, gdq, gdk, rope, tm):
    B, T, D = x.shape
    use_rope = rope is not None
    row = lambda b, i: (b, i, 0)
    const2 = lambda b, i: (0, 0)
    perb = lambda b, i: (b, 0, 0)
    in_specs = [
        pl.BlockSpec((1, tm, D), row),
        pl.BlockSpec((1, D), const2),
        pl.BlockSpec((1, 1, D), perb),
        pl.BlockSpec((1, 1, D), perb),
        pl.BlockSpec((D, IN_COLS), const2),
        pl.BlockSpec((1, NA_W), const2), pl.BlockSpec((1, NA_W), const2),
        pl.BlockSpec((1, DIFF_QK_W), const2), pl.BlockSpec((1, DIFF_QK_W), const2),
        pl.BlockSpec((NA_W, NA_W), const2), pl.BlockSpec((DIFF_QK_W, DIFF_QK_W), const2),
    ]
    args = [x, g, sh, sc, w_bf16, gq, gk, gdq, gdk,
            _block_diag_mean(NA_W, NA_HEAD_DIM), _block_diag_mean(DIFF_QK_W, DIFF_HEAD_DIM)]
    if use_rope:
        in_specs += [pl.BlockSpec((tm, LANES), lambda b, i: (i, 0))] * 3
        args += list(rope)
    outs = [(NA_W, BF16, False), (DIFF_QK_W, BF16, True), (POOL_W, F32, False), (FNET_W, F32, False),
            (N_BRANCH * D_MODEL, BF16, False), (2 * NA_W, BF16, False), (DIFF_QK_W, BF16, False),
            (DIFF_V_W, BF16, True)]
    col = lambda b, i: (b, 0, i)
    out_shape = [jax.ShapeDtypeStruct((B, w, T) if tr else (B, T, w), dt) for w, dt, tr in outs]
    out_specs = [pl.BlockSpec((1, w, tm), col) if tr else pl.BlockSpec((1, tm, w), row) for w, _, tr in outs]
    return pl.pallas_call(
        functools.partial(_inproj_kernel, use_rope=use_rope),
        grid=(B, T // tm), in_specs=in_specs, out_specs=out_specs, out_shape=out_shape,
        compiler_params=_params("parallel", "parallel"), name="inproj",
    )(*args)


NA_ROWS_PER_STEP = 4


def _na_kernel(q_ref, kv0_ref, kv1_ref, kv2_ref, ckv_ref, bias_ref, o_ref):
    q = q_ref[0]
    kv = jnp.concatenate([kv0_ref[0], kv1_ref[0], kv2_ref[0]], axis=0)
    ckv = ckv_ref[0]
    for h in range(NA_HEADS):
        ks = slice(h * NA_HEAD_DIM, (h + 1) * NA_HEAD_DIM)
        vs = slice(NA_W + h * NA_HEAD_DIM, NA_W + (h + 1) * NA_HEAD_DIM)
        qh = q[:, ks]
        s = _dot_nt(qh, kv[:, ks]) + bias_ref[0, h]
        sc = _dot_nt(qh, ckv[:, ks])
        m = jnp.maximum(jnp.max(s, axis=-1, keepdims=True), jnp.max(sc, axis=-1, keepdims=True))
        p = jnp.exp(s - m)
        pc = jnp.exp(sc - m)
        l = jnp.sum(p, axis=-1, keepdims=True) + jnp.sum(pc, axis=-1, keepdims=True)
        o = _dot(p.astype(BF16), kv[:, vs]) + _dot(pc.astype(BF16), ckv[:, vs])
        o_ref[0, :, ks] = o / l


def _na_bias_table(rpb, rows):
    R = NA_ROWS_PER_STEP
    nb = rows // R
    col = jnp.arange(GRID_W)
    cs = jnp.clip(col - NA_WIN_C // 2, 0, GRID_W - NA_WIN_C)
    col_ok = (col[None, :] >= cs[:, None]) & (col[None, :] < cs[:, None] + NA_WIN_C)
    ci = jnp.clip(col[None, :] - col[:, None] + (NA_WIN_C - 1), 0, 2 * NA_WIN_C - 2)
    j = jnp.array([0, 1, nb - 1])[:, None, None]
    r = R * j + jnp.arange(R)[None, :, None]
    krow = R * (j - 1) + jnp.arange(3 * R)[None, None, :]
    rs = jnp.clip(r - NA_WIN_R // 2, 0, rows - NA_WIN_R)
    row_ok = (krow >= rs) & (krow < rs + NA_WIN_R) & (krow >= 0) & (krow < rows)
    ri = jnp.clip(krow - r + (NA_WIN_R - 1), 0, 2 * NA_WIN_R - 2)
    tab = rpb.astype(F32)[:, ri][:, :, :, :, ci]
    ok = row_ok[None, :, :, :, None, None] & col_ok[None, None, None, None]
    tab = jnp.where(ok, tab, NEG_BIG)
    tab = tab.transpose(1, 0, 2, 4, 3, 5)
    return tab.reshape(3, NA_HEADS, R * GRID_W, 3 * R * GRID_W)


def _na_latent(naq, nakv, cnakv, rpb):
    B, T, _ = naq.shape
    rows = T // GRID_W
    R = NA_ROWS_PER_STEP
    assert rows % R == 0 and rows // R >= 3 and rows >= NA_WIN_R and R + NA_WIN_R <= 3 * R
    nb = rows // R
    tb = R * GRID_W
    ctx_len = cnakv.shape[1]
    pattern = lambda j: (j > 0).astype(jnp.int32) + (j == nb - 1).astype(jnp.int32)
    kv_spec = lambda d: pl.BlockSpec((1, tb, 2 * NA_W), lambda b, j: (b, jnp.clip(j + d, 0, nb - 1), 0))
    return pl.pallas_call(
        _na_kernel, grid=(B, nb),
        in_specs=[
            pl.BlockSpec((1, tb, NA_W), lambda b, j: (b, j, 0)),
            kv_spec(-1), kv_spec(0), kv_spec(1),
            pl.BlockSpec((1, ctx_len, 2 * NA_W), lambda b, j: (b, 0, 0)),
            pl.BlockSpec((1, NA_HEADS, tb, 3 * tb), lambda b, j: (pattern(j), 0, 0, 0)),
        ],
        out_specs=pl.BlockSpec((1, tb, NA_W), lambda b, j: (b, j, 0)),
        out_shape=jax.ShapeDtypeStruct((B, T, NA_W), F32),
        compiler_params=_params("parallel", "parallel"), name="na_latent",
    )(naq, nakv, nakv, nakv, cnakv, _na_bias_table(rpb, rows))


DIFF_FIXED_SHIFT_MAX = 60.0


def _diff_kernel(shift_ref, lam_ref, g_ref, qt_ref, k_ref, vt_ref, o_ref, qx_sc, m_sc, l_sc, acc_sc, *,
                 lam_init, tk, nk, online):
    h = pl.program_id(1)
    qt = qt_ref[0]
    tq = qt.shape[1]
    grp = lax.broadcasted_iota(jnp.int32, qt.shape, 0) // DIFF_HEAD_DIM
    for mi in range(2):
        qx_sc[mi] = jnp.where(grp == 2 * h + mi, qt, jnp.zeros_like(qt))
    m_sc[...] = jnp.full(m_sc.shape, -jnp.inf, F32)
    l_sc[...] = jnp.zeros(l_sc.shape, F32)
    acc_sc[...] = jnp.zeros(acc_sc.shape, F32)

    def body(k, carry):
        off = pl.multiple_of(k * tk, tk)
        kb = k_ref[0, pl.ds(off, tk), :]
        vtb = vt_ref[0, :, pl.ds(off, tk)]
        for mi in range(2):
            s = _dot(kb, qx_sc[mi])
            if online:
                m_prev = m_sc[mi]
                m_new = jnp.maximum(m_prev, jnp.max(s, axis=0, keepdims=True))
                alpha = jnp.exp2(m_prev - m_new)
                p = jnp.exp2(s - m_new)
                l_sc[mi] = alpha * l_sc[mi] + p.reshape(tk // 8, 8, tq).sum(axis=0)
                acc_sc[mi] = alpha * acc_sc[mi] + _dot(vtb, p.astype(BF16))
                m_sc[mi] = m_new
            else:
                p = jnp.exp2(s - shift_ref[0])
                l_sc[mi] += p.reshape(tk // 8, 8, tq).sum(axis=0)
                acc_sc[mi] += _dot(vtb, p.astype(BF16))
        return carry

    lax.fori_loop(0, nk, body, 0)

    lv = lam_ref[...]
    lam = (jnp.exp(jnp.sum(lv[0:1] * lv[1:2], axis=-1, keepdims=True))
           - jnp.exp(jnp.sum(lv[2:3] * lv[3:4], axis=-1, keepdims=True)) + lam_init)
    l0 = jnp.sum(l_sc[0], axis=0, keepdims=True)
    l1 = jnp.sum(l_sc[1], axis=0, keepdims=True)
    o = acc_sc[0] / l0 - lam * (acc_sc[1] / l1)
    ms = jnp.mean(o * o, axis=0, keepdims=True)
    o_ref[0] = o * lax.rsqrt(ms + EPS) * g_ref[...] * (1.0 - lam_init)


def _diff_call(shift, qt, k, vt, lam_params, subln_g, lam_init, tq, tk, online):
    B, W, Tq = qt.shape
    Tk = k.shape[1]
    dv = W // DIFF_HEADS
    return pl.pallas_call(
        functools.partial(_diff_kernel, lam_init=lam_init, tk=tk, nk=Tk // tk, online=online),
        grid=(B, DIFF_HEADS, Tq // tq),
        in_specs=[
            pl.BlockSpec(memory_space=pltpu.SMEM),
            pl.BlockSpec((4, DIFF_HEAD_DIM), lambda b, h, i: (0, 0)),
            pl.BlockSpec((dv, 1), lambda b, h, i: (0, 0)),
            pl.BlockSpec((1, W, tq), lambda b, h, i: (b, 0, i)),
            pl.BlockSpec((1, Tk, W), lambda b, h, i: (b, 0, 0)),
            pl.BlockSpec((1, dv, Tk), lambda b, h, i: (b, h, 0)),
        ],
        out_specs=pl.BlockSpec((1, dv, tq), lambda b, h, i: (b, h, i)),
        out_shape=jax.ShapeDtypeStruct((B, W, Tq), F32),
        scratch_shapes=[pltpu.VMEM((2, W, tq), BF16), pltpu.VMEM((2, 1, tq), F32),
                        pltpu.VMEM((2, 8, tq), F32), pltpu.VMEM((2, dv, tq), F32)],
        compiler_params=_params("parallel", "parallel", "parallel"),
        name="diff_attn_online" if online else "diff_attn_fixed",
    )(shift, lam_params, subln_g, qt, k, vt)


def _diff_attn(qt, dfk, vt, gq, gk, lam_params, subln_g, lam_init, tq, tk):
    bound = (1.02 * math.log2(math.e) * DIFF_HEAD_DIM ** 0.5) * jnp.max(jnp.abs(gq)) * jnp.max(jnp.abs(gk))
    shift = bound.astype(F32).reshape(1)
    args = (shift, qt, dfk, vt, lam_params, subln_g, lam_init, tq, tk)
    return lax.cond(bound <= DIFF_FIXED_SHIFT_MAX,
                    lambda: _diff_call(*args, online=False), lambda: _diff_call(*args, online=True))


POOL_HALO = 8


def _pool_mixer(u, prev, nxt, wp_ref, scale, n_total):
    tm = u.shape[0]
    i = pl.program_id(1)
    prev = jnp.where(i > 0, prev, 0.0)
    nxt = jnp.where(i < pl.num_programs(1) - 1, nxt, 0.0)
    ext = jnp.concatenate([prev, u, nxt], axis=0)
    sh = lambda d: ext[POOL_HALO + d:POOL_HALO + d + tm]
    t = i * tm + lax.broadcasted_iota(jnp.int32, (tm, 1), 0)
    lane_grp = lax.broadcasted_iota(jnp.int32, (1, POOL_W), 1) // POOL_GROUP_W
    win = None
    mean = None
    prev_half = 0
    for gi, w in enumerate(POOL_WINDOWS):
        half = w // 2
        for d in range(prev_half, half):
            add = sh(d) + sh(-d - 1)
            win = add if win is None else win + add
        prev_half = half
        cnt = (jnp.minimum(t + half, n_total) - jnp.maximum(t - half, 0)).astype(F32)
        m_w = win / cnt
        mean = m_w if mean is None else jnp.where(lane_grp >= gi, m_w, mean)
    return _dot((mean - u).astype(BF16), wp_ref[...]) * scale


def _merge_kernel(x_ref, gt_ref, yna_ref, ydf_ref, u_ref, up_ref, un_ref, f_ref, gate_ref, wbr_ref, wp_ref,
                  ps_ref, wf_ref, wo_ref, o_ref, *, n_total):
    yf = _dot(f_ref[0].astype(BF16), wf_ref[...])
    ypool = _pool_mixer(u_ref[0], up_ref[0], un_ref[0], wp_ref, ps_ref[...], n_total)
    ys = (yna_ref[0], ydf_ref[0].T, ypool, yf)
    acc = None
    for i, y in enumerate(ys):
        g = gate_ref[0, :, i * D_MODEL:(i + 1) * D_MODEL].astype(F32)
        term = g * _dot(y.astype(BF16), wbr_ref[i])
        acc = term if acc is None else acc + term
    o_ref[0] = x_ref[0] + gt_ref[0] * _dot(acc.astype(BF16), wo_ref[...])


def _merge(x, gt, y_na, y_df, pool_in, f_real, gate, w_br, w_pool_bd, pool_scale, w_f, w_o, tm):
    B, T, D = x.shape
    assert tm % POOL_HALO == 0 and POOL_HALO >= max(POOL_WINDOWS) // 2
    hb = tm // POOL_HALO
    row = lambda b, i: (b, i, 0)
    const2 = lambda b, i: (0, 0)
    return pl.pallas_call(
        functools.partial(_merge_kernel, n_total=T), grid=(B, T // tm),
        in_specs=[
            pl.BlockSpec((1, tm, D), row),
            pl.BlockSpec((1, 1, D), lambda b, i: (b, 0, 0)),
            pl.BlockSpec((1, tm, BRANCH_W), row), pl.BlockSpec((1, BRANCH_W, tm), lambda b, i: (b, 0, i)),
            pl.BlockSpec((1, tm, POOL_W), row),
            pl.BlockSpec((1, POOL_HALO, POOL_W), lambda b, i: (b, jnp.maximum(i * hb - 1, 0), 0)),
            pl.BlockSpec((1, POOL_HALO, POOL_W), lambda b, i: (b, jnp.minimum((i + 1) * hb, T // POOL_HALO - 1), 0)),
            pl.BlockSpec((1, tm, FNET_W), row),
            pl.BlockSpec((1, tm, N_BRANCH * D), row),
            pl.BlockSpec((N_BRANCH, BRANCH_W, D), lambda b, i: (0, 0, 0)),
            pl.BlockSpec((POOL_W, POOL_W), const2), pl.BlockSpec((1, POOL_W), const2),
            pl.BlockSpec((FNET_W, FNET_W), const2),
            pl.BlockSpec((D, D), const2),
        ],
        out_specs=pl.BlockSpec((1, tm, D), row),
        out_shape=jax.ShapeDtypeStruct((B, T, D), F32),
        compiler_params=_params("parallel", "parallel"), name="merge",
    )(x, gt, y_na, y_df, pool_in, pool_in, pool_in, f_real, gate, w_br, w_pool_bd, pool_scale, w_f, w_o)


TOK_BLOCK = LANES
SEL_GROUP = 8
COMBINE_WIN = 32


def _router_kernel(x_ref, g_ref, sh_ref, sc_ref, wrh_ref, wrl_ref, h_ref, aff_ref):
    x = x_ref[0]
    ms = jnp.mean(x * x, axis=-1, keepdims=True)
    h = x * lax.rsqrt(ms + EPS) * g_ref[...] * (1.0 + sc_ref[0]) + sh_ref[0]
    h_ref[0] = h.astype(BF16)
    hi, lo = _split_bf16(h)
    logits = _dot_nt(wrh_ref[...], hi) + _dot_nt(wrh_ref[...], lo) + _dot_nt(wrl_ref[...], hi)
    logits = logits - jnp.max(logits, axis=0, keepdims=True)
    e = jnp.exp(logits)
    aff_ref[0] = e / jnp.sum(e, axis=0, keepdims=True)


def _router(x, g, sh, sc, w_router, tm):
    B, T, D = x.shape
    E = w_router.shape[1]
    wrh, wrl = _split_bf16(w_router.T)
    row = lambda b, i: (b, i, 0)
    return pl.pallas_call(
        _router_kernel, grid=(B, T // tm),
        in_specs=[
            pl.BlockSpec((1, tm, D), row), pl.BlockSpec((1, D), lambda b, i: (0, 0)),
            pl.BlockSpec((1, 1, D), lambda b, i: (b, 0, 0)), pl.BlockSpec((1, 1, D), lambda b, i: (b, 0, 0)),
            pl.BlockSpec((E, D), lambda b, i: (0, 0)), pl.BlockSpec((E, D), lambda b, i: (0, 0)),
        ],
        out_specs=[pl.BlockSpec((1, tm, D), row), pl.BlockSpec((1, E, tm), lambda b, i: (b, 0, i))],
        out_shape=[jax.ShapeDtypeStruct((B, T, D), BF16), jax.ShapeDtypeStruct((B, E, T), F32)],
        compiler_params=_params("parallel", "parallel"), name="router",
    )(x, g, sh, sc, wrh, wrl)


def _select_kernel(aff_ref, u_ref, ls_ref, idx_ref, gate_ref, pos_ref, off_ref, tot_ref, *, cap):
    G, nb, _ = aff_ref.shape
    aff = aff_ref[...]
    bits = pltpu.bitcast(aff, jnp.int32)
    kf = float(cap)
    ones_f = lambda m: jnp.where(m, 1.0, 0.0)
    count = lambda m: jnp.sum(ones_f(m), axis=(1, 2), keepdims=True)

    def search(it, lo):
        cand = lo | jnp.left_shift(jnp.int32(1), 30 - it)
        return jnp.where(count(bits >= cand) >= kf, cand, lo)

    thr = lax.fori_loop(0, 31, search, jnp.zeros((G, 1, 1), jnp.int32))
    need = kf - count(bits > thr)

    u = u_ref[...]
    ls = ls_ref[...]
    ones_m = jnp.ones((LANES, LANES), BF16)

    def prefix(mf):
        mb = mf.astype(BF16)
        tot = _dot(mb, ones_m)
        off = _dot(ls, tot.astype(BF16))
        return _dot(mb, u) + off, off, tot

    blk_i = lax.broadcasted_iota(jnp.int32, (nb, LANES), 0)
    lane_i = lax.broadcasted_iota(jnp.int32, (nb, LANES), 1)
    s_row = lax.broadcasted_iota(jnp.int32, (1, cap), 1).astype(F32)
    col_nb = lax.broadcasted_iota(jnp.int32, (nb, 1), 0).astype(F32)
    col_l = lax.broadcasted_iota(jnp.int32, (LANES, 1), 0).astype(F32)
    for g in range(G):
        gt = bits[g] > thr[g]
        eq = bits[g] == thr[g]
        eqf = ones_f(eq)
        tie_before = prefix(eqf)[0] - eqf
        sel = gt | (eq & (tie_before < need[g]))
        pin, off, tot = prefix(ones_f(sel))
        pos_ref[g] = jnp.where(sel, pin - 1.0, -1.0).astype(jnp.int32)
        diag = blk_i == lane_i
        off_ref[g] = jnp.sum(jnp.where(diag, off, 0.0), axis=0, keepdims=True).astype(jnp.int32)
        tot_ref[g] = jnp.sum(jnp.where(diag, tot, 0.0), axis=0, keepdims=True).astype(jnp.int32)
        blk = jnp.sum(ones_f((off + tot)[:, 0:1] <= s_row), axis=0, keepdims=True)
        oh = jnp.where(col_nb == blk, 1.0, 0.0).astype(BF16)
        pin_t = pin.T
        p_hi = jnp.floor(pin_t * (1.0 / LANES))
        p_lo = pin_t - p_hi * LANES
        rows = _dot(p_hi.astype(BF16), oh) * LANES + _dot(p_lo.astype(BF16), oh)
        lane = jnp.sum(ones_f(rows <= s_row), axis=0, keepdims=True)
        idx_ref[g] = (blk * LANES + lane).astype(jnp.int32)
        a_t = aff[g].T
        a1 = a_t.astype(BF16)
        r1 = a_t - a1.astype(F32)
        a2 = r1.astype(BF16)
        a3 = (r1 - a2.astype(F32)).astype(BF16)
        arow = _dot(a1, oh) + _dot(a2, oh) + _dot(a3, oh)
        gate_ref[g] = jnp.sum(jnp.where(col_l == lane, arow, 0.0), axis=0, keepdims=True)


def _select(aff, cap):
    B, E, T = aff.shape
    nb = T // TOK_BLOCK
    R = B * E
    G = SEL_GROUP
    i = np.arange(LANES)
    u = jnp.asarray(i[:, None] <= i[None, :], BF16)
    j = np.arange(nb)
    ls = jnp.asarray(j[None, :] < j[:, None], BF16)
    grp = lambda r: (r, 0, 0)
    outs = pl.pallas_call(
        functools.partial(_select_kernel, cap=cap), grid=(R // G,),
        in_specs=[pl.BlockSpec((G, nb, LANES), grp), pl.BlockSpec((LANES, LANES), lambda r: (0, 0)),
                  pl.BlockSpec((nb, nb), lambda r: (0, 0))],
        out_specs=[pl.BlockSpec((G, 1, cap), grp), pl.BlockSpec((G, 1, cap), grp),
                   pl.BlockSpec((G, nb, LANES), grp), pl.BlockSpec((G, 1, LANES), grp),
                   pl.BlockSpec((G, 1, LANES), grp)],
        out_shape=[jax.ShapeDtypeStruct((R, 1, cap), jnp.int32), jax.ShapeDtypeStruct((R, 1, cap), F32),
                   jax.ShapeDtypeStruct((R, nb, LANES), jnp.int32), jax.ShapeDtypeStruct((R, 1, LANES), jnp.int32),
                   jax.ShapeDtypeStruct((R, 1, LANES), jnp.int32)],
        compiler_params=_params("parallel"), name="select",
    )(aff.reshape(R, nb, LANES), u, ls)
    idx, gate, pos, off, tot = outs
    return (idx.reshape(B, E, cap), gate.reshape(B, E, cap), pos.reshape(B, E, T),
            off[:, 0, :nb].reshape(B, E, nb), tot[:, 0, :nb].reshape(B, E, nb))


def _combine_kernel(off_ref, tot_ref, x_ref, gt_ref, pos_ref, ye_hbm, o_ref, buf, xbuf, oh_sc, acc_sc, sem,
                    xsem, *, nt, cap):
    n = pl.program_id(0)
    E, W = N_EXPERTS, COMBINE_WIN

    def window(step, e):
        b = step // nt
        o = off_ref[(b * E + e) * nt + step % nt]
        return b, o, jnp.minimum((o // 8) * 8, cap - W)

    def fetch(step, slot):
        for e in range(E):
            b, _, st = window(step, e)
            pltpu.make_async_copy(ye_hbm.at[b, e, pl.ds(pl.multiple_of(st, 8), W)], buf.at[slot, e],
                                  sem.at[slot]).start()

    @pl.when(n == 0)
    def _():
        fetch(0, 0)

    @pl.when(n + 1 < pl.num_programs(0))
    def _():
        fetch(n + 1, (n + 1) % 2)

    slot = n % 2
    for e in range(E):
        pltpu.make_async_copy(ye_hbm.at[0, 0, pl.ds(0, W)], buf.at[slot, e], sem.at[slot]).wait()

    pos = pos_ref[0]
    w_i = lax.broadcasted_iota(jnp.int32, (1, W), 1)
    for e in range(E):
        _, _, st = window(n, e)
        oh_sc[:, e * W:(e + 1) * W] = jnp.where(pos[:, e:e + 1] == st + w_i, 1.0, 0.0)
    oh_all = oh_sc[...].astype(BF16)
    rh, rl = _split_bf16(buf[slot].reshape(E * W, buf.shape[-1]))
    acc_sc[...] = _dot(oh_all, rh) + _dot(oh_all, rl)

    for e in range(E):
        b, o, st = window(n, e)
        t = tot_ref[(b * E + e) * nt + n % nt]
        n_extra = jnp.maximum(o + t - (st + W) + (W - 1), 0) // W

        def extra(j, carry, e=e, b=b, st=st):
            base = st + W * (j + 1)
            src = jnp.minimum(base, cap - W)
            cp = pltpu.make_async_copy(ye_hbm.at[b, e, pl.ds(pl.multiple_of(src, 8), W)], xbuf, xsem.at[0])
            cp.start()
            cp.wait()
            slot_i = src + w_i
            oh = jnp.where((pos[:, e:e + 1] == slot_i) & (slot_i >= base), 1.0, 0.0).astype(BF16)
            xh, xl = _split_bf16(xbuf[...])
            acc_sc[...] += _dot(oh, xh) + _dot(oh, xl)
            return carry

        lax.fori_loop(0, n_extra, extra, 0)

    o_ref[0] = x_ref[0] + gt_ref[0] * acc_sc[...]


def _combine(x, gt, pos_t, off, tot, ye):
    B, T, D = x.shape
    E, cap = ye.shape[1], ye.shape[2]
    nt = T // TOK_BLOCK
    W = COMBINE_WIN
    assert cap % W == 0 and W % 8 == 0
    tile = lambda n, off_r, tot_r: (n // nt, n % nt, 0)
    grid_spec = pltpu.PrefetchScalarGridSpec(
        num_scalar_prefetch=2, grid=(B * nt,),
        in_specs=[
            pl.BlockSpec((1, TOK_BLOCK, D), tile),
            pl.BlockSpec((1, 1, D), lambda n, off_r, tot_r: (n // nt, 0, 0)),
            pl.BlockSpec((1, TOK_BLOCK, E), tile),
            pl.BlockSpec(memory_space=pl.ANY),
        ],
        out_specs=pl.BlockSpec((1, TOK_BLOCK, D), tile),
        scratch_shapes=[pltpu.VMEM((2, E, W, D), F32), pltpu.VMEM((W, D), F32),
                        pltpu.VMEM((TOK_BLOCK, E * W), F32), pltpu.VMEM((TOK_BLOCK, D), F32),
                        pltpu.SemaphoreType.DMA((2,)), pltpu.SemaphoreType.DMA((1,))],
    )
    return pl.pallas_call(
        functools.partial(_combine_kernel, nt=nt, cap=cap), grid_spec=grid_spec,
        out_shape=jax.ShapeDtypeStruct((B, T, D), F32),
        compiler_params=_params("arbitrary"), name="combine",
    )(off.reshape(-1), tot.reshape(-1), x, gt, pos_t, ye)


FF_CHUNKS = ((0, 512), (512, 512), (1024, 384))


def _expert_kernel(xe_ref, g_ref, wg_ref, wu_ref, wd_ref, o_ref):
    xe = xe_ref[0, 0]
    acc = None
    for lo, n in FF_CHUNKS:
        a = _dot(xe, wg_ref[0, :, lo:lo + n].astype(BF16))
        u = _dot(xe, wu_ref[0, :, lo:lo + n].astype(BF16))
        hid = (a * jax.nn.sigmoid(a) * u).astype(BF16)
        part = _dot(hid, wd_ref[0, lo:lo + n, :].astype(BF16))
        acc = part if acc is None else acc + part
    o_ref[0, 0] = acc * g_ref[0, 0]


def _experts(xe, g, wg, wu, wd, tm):
    B, E, cap, D = xe.shape
    F = wg.shape[-1]
    assert F == EXPERT_FF
    tile = lambda e, b, i: (b, e, i, 0)
    return pl.pallas_call(
        _expert_kernel, grid=(E, B, cap // tm),
        in_specs=[
            pl.BlockSpec((1, 1, tm, D), tile), pl.BlockSpec((1, 1, tm, 1), tile),
            pl.BlockSpec((1, D, F), lambda e, b, i: (e, 0, 0)),
            pl.BlockSpec((1, D, F), lambda e, b, i: (e, 0, 0)),
            pl.BlockSpec((1, F, D), lambda e, b, i: (e, 0, 0)),
        ],
        out_specs=pl.BlockSpec((1, 1, tm, D), tile),
        out_shape=jax.ShapeDtypeStruct((B, E, cap, D), F32),
        compiler_params=_params("parallel", "parallel", "parallel"), name="experts",
    )(xe, g, wg, wu, wd)


def _ec_ffn(x, gt, g, sh, sc, w_router, wg, wu, wd, tm_router, tm_expert):
    B, N, D = x.shape
    cap = max(1, EC_FACTOR * N // N_EXPERTS)
    h, aff = _router(x, g, sh, sc, w_router, tm_router)
    bidx = jnp.arange(B)[:, None, None]
    if (N // TOK_BLOCK) % 8 == 0 and cap % COMBINE_WIN == 0:
        idx, gsel, pos, off, tot = _select(aff, cap)
        ye = _experts(h[bidx, idx], gsel[..., None], wg, wu, wd, min(tm_expert, cap))
        return _combine(x, gt, pos.transpose(0, 2, 1), off, tot, ye)
    gsel, idx = lax.top_k(aff, cap)
    ye = _experts(h[bidx, idx], gsel[..., None], wg, wu, wd, min(tm_expert, cap))
    return x + gt * jnp.zeros((B, N, D), F32).at[bidx, idx].add(ye)


DFT_RADIX = 128


def _dot3(a_hi, a_lo, b_hi, b_lo):
    return _dot(a_hi, b_hi) + _dot(a_lo, b_hi) + _dot(a_hi, b_lo)


def _cos_sin(n_rows, n_cols, period):
    k = (np.arange(n_rows)[:, None] * np.arange(n_cols)[None, :]) % period
    ang = 2.0 * np.pi * k.astype(np.float64) / period
    return np.cos(ang), np.sin(ang)


def _const_split(m):
    return _split_bf16(jnp.asarray(m, F32))


def _chan_dft_kernel(u_ref, mh_ref, ml_ref, xr_ref, xi_ref):
    uh, ul = _split_bf16(u_ref[0])
    y = _dot3(uh, ul, mh_ref[...], ml_ref[...])
    c = xr_ref.shape[-1]
    xr_ref[0] = y[:, :c]
    xi_ref[0] = y[:, c:]


def _chan_dft(u, tm):
    B, N, C = u.shape
    cc, sc = _cos_sin(C, C, C)
    mh, ml = _const_split(np.concatenate([cc, -sc], axis=1))
    row = lambda b, i: (b, i, 0)
    const2 = lambda b, i: (0, 0)
    return pl.pallas_call(
        _chan_dft_kernel, grid=(B, N // tm),
        in_specs=[pl.BlockSpec((1, tm, C), row), pl.BlockSpec((C, 2 * C), const2), pl.BlockSpec((C, 2 * C), const2)],
        out_specs=[pl.BlockSpec((1, tm, C), row)] * 2,
        out_shape=[jax.ShapeDtypeStruct((B, N, C), F32)] * 2,
        compiler_params=_params("parallel", "parallel"), name="chan_dft",
    )(u, mh, ml)


def _dft_left_kernel(xr_ref, xi_ref, mh_ref, ml_ref, o_ref, *, scale):
    xh, xl = _split_bf16(jnp.concatenate([xr_ref[0], xi_ref[0]], axis=0))
    o_ref[0] = _dot3(mh_ref[...], ml_ref[...], xh, xl) * scale


def _dft_left(xr, xi, m, scale, tn):
    B, K, cols = xr.shape
    R = m.shape[0]
    mh, ml = _const_split(m)
    col = lambda b, j: (b, 0, j)
    const2 = lambda b, j: (0, 0)
    return pl.pallas_call(
        functools.partial(_dft_left_kernel, scale=scale), grid=(B, cols // tn),
        in_specs=[pl.BlockSpec((1, K, tn), col), pl.BlockSpec((1, K, tn), col),
                  pl.BlockSpec((R, 2 * K), const2), pl.BlockSpec((R, 2 * K), const2)],
        out_specs=pl.BlockSpec((1, R, tn), col),
        out_shape=jax.ShapeDtypeStruct((B, R, cols), F32),
        compiler_params=_params("parallel", "parallel"), name="dft_left",
    )(xr, xi, mh, ml)


DFT_K1_PER_STEP = 8


def _dft_twiddle_kernel(y_ref, tc_ref, ts_ref, dh_ref, dl_ref, o_ref, *, scale):
    tc = tc_ref[0]
    ts = ts_ref[0]
    for kk in range(DFT_K1_PER_STEP):
        yr = y_ref[0, 0, kk]
        yi = y_ref[0, 1, kk]
        c, s = tc[:, kk:kk + 1], ts[:, kk:kk + 1]
        zh, zl = _split_bf16(jnp.concatenate([yr * c + yi * s, yi * c - yr * s], axis=0))
        o_ref[0, kk] = _dot3(dh_ref[...], dl_ref[...], zh, zl) * scale


def _fourier_real(u):
    B, N, C = u.shape
    scale = 1.0 / math.sqrt(N * C)
    xr, xi = _chan_dft(u, min(N, 512))
    if N % (DFT_RADIX * DFT_K1_PER_STEP) != 0:
        assert N <= 1024
        cn, sn = _cos_sin(N, N, N)
        return _dft_left(xr, xi, np.concatenate([cn, sn], axis=1), scale, C)
    n1, n2 = N // DFT_RADIX, DFT_RADIX
    c1, s1 = _cos_sin(n1, n1, n1)
    m1 = np.block([[c1, s1], [-s1, c1]])
    cols = n2 * C
    yy = _dft_left(xr.reshape(B, n1, cols), xi.reshape(B, n1, cols), m1, 1.0, min(cols, 2048))
    yy = yy.reshape(B, 2, n1, n2, C)
    tcos, tsin = _cos_sin(n1, n2, N)
    grp = lambda t: jnp.asarray(t.reshape(n1 // DFT_K1_PER_STEP, DFT_K1_PER_STEP, n2).transpose(0, 2, 1), F32)
    c2, s2 = _cos_sin(n2, n2, n2)
    dh, dl = _const_split(np.concatenate([c2, s2], axis=1))
    fp = pl.pallas_call(
        functools.partial(_dft_twiddle_kernel, scale=scale), grid=(B, n1 // DFT_K1_PER_STEP),
        in_specs=[
            pl.BlockSpec((1, 2, DFT_K1_PER_STEP, n2, C), lambda b, g: (b, 0, g, 0, 0)),
            pl.BlockSpec((1, n2, DFT_K1_PER_STEP), lambda b, g: (g, 0, 0)),
            pl.BlockSpec((1, n2, DFT_K1_PER_STEP), lambda b, g: (g, 0, 0)),
            pl.BlockSpec((n2, 2 * n2), lambda b, g: (0, 0)), pl.BlockSpec((n2, 2 * n2), lambda b, g: (0, 0)),
        ],
        out_specs=pl.BlockSpec((1, DFT_K1_PER_STEP, n2, C), lambda b, g: (b, g, 0, 0)),
        out_shape=jax.ShapeDtypeStruct((B, n1, n2, C), F32),
        compiler_params=_params("parallel", "parallel"), name="dft_twiddle",
    )(yy, grp(tcos), grp(tsin), dh, dl)
    return fp.transpose(0, 2, 1, 3).reshape(B, N, C)


def _ctx_attn_kernel(q_ref, kv_ref, o_ref):
    q = q_ref[0]
    kv = kv_ref[0]
    for h in range(NA_HEADS):
        ks = slice(h * NA_HEAD_DIM, (h + 1) * NA_HEAD_DIM)
        vs = slice(NA_W + h * NA_HEAD_DIM, NA_W + (h + 1) * NA_HEAD_DIM)
        s = _dot_nt(q[:, ks], kv[:, ks])
        p = jnp.exp(s - jnp.max(s, axis=-1, keepdims=True))
        l = jnp.sum(p, axis=-1, keepdims=True)
        o_ref[0, :, ks] = _dot(p.astype(BF16), kv[:, vs]) / l


def _ctx_dense_attn(q, kv):
    B, Q, _ = q.shape
    return pl.pallas_call(
        _ctx_attn_kernel, grid=(B,),
        in_specs=[pl.BlockSpec((1, Q, NA_W), lambda b: (b, 0, 0)),
                  pl.BlockSpec((1, kv.shape[1], 2 * NA_W), lambda b: (b, 0, 0))],
        out_specs=pl.BlockSpec((1, Q, NA_W), lambda b: (b, 0, 0)),
        out_shape=jax.ShapeDtypeStruct((B, Q, NA_W), F32),
        compiler_params=_params("parallel"), name="ctx_attn",
    )(q, kv)


def _tile(g, n):
    return jnp.tile(g.astype(F32), n)[None, :]


def kernel(x, c, ctx, c_ctx, w_ada, b_ada, g_mix, g_ffn, w_in, na_q_g, na_k_g, na_rpb, df_q_g, df_k_g,
           df_lambda, df_subln_g, pool_w, pool_scale, fnet_w, w_branch, w_out, w_router, w_gate_e, w_up_e,
           w_down_e):
    B, T, D = x.shape
    ctx_len = ctx.shape[1]
    rope = _rope_tables(T)
    s_c = jax.nn.silu(c)
    s_cc = jax.nn.silu(c_ctx)
    tm = 256
    tq, tk = 1024, 1280
    for l in range(DEPTH):
        last = l == DEPTH - 1
        lam_init = 0.8 - 0.6 * math.exp(-0.3 * l)
        mod = jnp.dot(s_c, w_ada[l], precision=lax.Precision.HIGHEST) + b_ada[l]
        sh1, sc1, gt1, sh2, sc2, gt2 = [m[:, None, :] for m in jnp.split(mod, 6, axis=-1)]
        cmod = jnp.dot(s_cc, w_ada[l], precision=lax.Precision.HIGHEST) + b_ada[l]
        bc = lambda m: jnp.broadcast_to(m[None, None, :], (B, 1, D))
        csh1, csc1, cgt1, csh2, csc2, cgt2 = [bc(m) for m in jnp.split(cmod, 6, axis=-1)]

        w_bf = w_in[l].astype(BF16)
        gq, gk = _tile(na_q_g[l], NA_HEADS), _tile(na_k_g[l], NA_HEADS)
        gdq, gdk = _tile(df_q_g[l], 2 * DIFF_HEADS), _tile(df_k_g[l], 2 * DIFF_HEADS)
        gmix = g_mix[l][None, :]
        gffn = g_ffn[l][None, :]
        subg = df_subln_g[l][:, None].astype(F32)
        wbr = w_branch[l].astype(BF16)
        wf = fnet_w[l].astype(BF16)
        wpool = jax.scipy.linalg.block_diag(*[pool_w[l, gi] for gi in range(len(POOL_WINDOWS))]).astype(BF16)
        pscale = pool_scale[l][None, :].astype(F32)
        wo = w_out[l].astype(BF16)
        wg, wu, wd = w_gate_e[l], w_up_e[l], w_down_e[l]

        naq, dfq, pool_in, fnet_in, gate, nakv, dfk, dfv = _inproj(
            x, gmix, sh1, sc1, w_bf, gq, gk, gdq, gdk, rope, tm)
        (cnaq, cdfq, cpool_in, cfnet_in, cgate, cnakv, cdfk, cdfv) = _inproj(
            ctx, gmix, csh1, csc1, w_bf, gq, gk, gdq, gdk, None, ctx_len)

        y_na = _na_latent(naq, nakv, cnakv, na_rpb[l])
        lamp = df_lambda[l].astype(F32)
        y_df = _diff_attn(dfq, jnp.concatenate([dfk, cdfk], axis=1), jnp.concatenate([dfv, cdfv], axis=2),
                          df_q_g[l], df_k_g[l], lamp, subg, lam_init, tq, tk)
        f_real = _fourier_real(fnet_in)
        x_new = _merge(x, gt1, y_na, y_df, pool_in, f_real, gate, wbr, wpool, pscale, wf, wo, tm)

        if not last:
            yc_na = _ctx_dense_attn(cnaq, cnakv)
            yc_df = _diff_attn(cdfq, cdfk, cdfv, df_q_g[l], df_k_g[l], lamp, subg, lam_init, ctx_len, ctx_len)
            fc_real = _fourier_real(cfnet_in)
            ctx_new = _merge(ctx, cgt1, yc_na, yc_df, cpool_in, fc_real, cgate, wbr, wpool, pscale, wf, wo,
                             ctx_len)

        x = x_new
        x = _ec_ffn(x, gt2, gffn, sh2, sc2, w_router[l], wg, wu, wd, tm, 512)
        if not last:
            ctx = ctx_new
            ctx = _ec_ffn(ctx, cgt2, gffn, csh2, csc2, w_router[l], wg, wu, wd, ctx_len, 512)
    return x
```

```python
import functools
import math

import jax
import jax.numpy as jnp
import numpy as np
from jax import lax
from jax.experimental import pallas as pl
from jax.experimental.pallas import tpu as pltpu

F32 = jnp.float32
BF16 = jnp.bfloat16

D_MODEL = 1024
DEPTH = 2
GRID_W = 64
EPS = 1e-6
ROPE_BASE = 10000.0

NA_HEADS = 4
NA_HEAD_DIM = 64
NA_WIN_R = 8
NA_WIN_C = 16
NA_W = NA_HEADS * NA_HEAD_DIM

DIFF_HEADS = 4
DIFF_HEAD_DIM = 32
DIFF_QK_W = DIFF_HEADS * 2 * DIFF_HEAD_DIM
DIFF_V_W = DIFF_HEADS * 2 * DIFF_HEAD_DIM

POOL_WINDOWS = (2, 4, 8, 16)
POOL_GROUP_W = 64
POOL_W = len(POOL_WINDOWS) * POOL_GROUP_W
FNET_W = 256
N_BRANCH = 4
BRANCH_W = 256

OFF_NA_Q = 0
OFF_DF_Q = OFF_NA_Q + NA_W
OFF_POOL = OFF_DF_Q + DIFF_QK_W
OFF_FNET = OFF_POOL + POOL_W
OFF_GATE = OFF_FNET + FNET_W
OFF_KV = OFF_GATE + N_BRANCH * D_MODEL
KV_W = 2 * NA_W + DIFF_QK_W + DIFF_V_W
IN_COLS = OFF_KV + KV_W

N_EXPERTS = 16
EC_FACTOR = 2
EXPERT_FF = 1408

VMEM_LIMIT_BYTES = 56 * 1024 * 1024
LANES = 128
NEG_BIG = -1e30


def _params(*sem):
    return pltpu.CompilerParams(dimension_semantics=sem, vmem_limit_bytes=VMEM_LIMIT_BYTES)


def _split_bf16(a):
    hi = a.astype(BF16)
    lo = (a - hi.astype(F32)).astype(BF16)
    return hi, lo


def _dot(a, b):
    return jnp.dot(a, b, preferred_element_type=F32)


def _dot_nt(a, b):
    return lax.dot_general(a, b, (((1,), (1,)), ((), ())), preferred_element_type=F32)


def _group_rmsnorm(p, bd_ref, g):
    hi, lo = _split_bf16(p * p)
    ms = _dot(hi, bd_ref[...]) + _dot(lo, bd_ref[...])
    return p * lax.rsqrt(ms + EPS) * g


def _rope256(y, cos, s_next, s_prev):
    outs = []
    for half in range(2):
        z = y[:, half * LANES:(half + 1) * LANES]
        outs.append(z * cos + pltpu.roll(z, LANES - 8, 1) * s_next + pltpu.roll(z, 8, 1) * s_prev)
    return jnp.concatenate(outs, axis=1)


def _inproj_kernel(*refs, use_rope):
    if use_rope:
        (x_ref, g_ref, sh_ref, sc_ref, w_ref, gq_ref, gk_ref, gdq_ref, gdk_ref, bd64_ref, bd32_ref,
         cos_ref, sn_ref, sp_ref,
         naq_ref, dfq_ref, pool_ref, fnet_ref, gate_ref, nakv_ref, dfk_ref, dfv_ref) = refs
    else:
        (x_ref, g_ref, sh_ref, sc_ref, w_ref, gq_ref, gk_ref, gdq_ref, gdk_ref, bd64_ref, bd32_ref,
         naq_ref, dfq_ref, pool_ref, fnet_ref, gate_ref, nakv_ref, dfk_ref, dfv_ref) = refs
    x = x_ref[0]
    ms = jnp.mean(x * x, axis=-1, keepdims=True)
    y = x * lax.rsqrt(ms + EPS) * g_ref[...]
    h = (y * (1.0 + sc_ref[0]) + sh_ref[0]).astype(BF16)

    def seg(lo, n):
        return _dot(h, w_ref[:, lo:lo + n])

    def rope(v):
        if not use_rope:
            return v
        return _rope256(v, cos_ref[...], sn_ref[...], sp_ref[...])

    naq = _group_rmsnorm(seg(OFF_NA_Q, NA_W), bd64_ref, gq_ref[...])
    naq_ref[0] = (naq * (NA_HEAD_DIM ** -0.5)).astype(BF16)
    dfq = rope(_group_rmsnorm(seg(OFF_DF_Q, DIFF_QK_W), bd32_ref, gdq_ref[...]))
    dfq_ref[0] = (dfq * (math.log2(math.e) * DIFF_HEAD_DIM ** -0.5)).T.astype(BF16)
    pool_ref[0] = seg(OFF_POOL, POOL_W)
    fnet_ref[0] = seg(OFF_FNET, FNET_W)
    for j in range(0, N_BRANCH * D_MODEL, 512):
        gate_ref[0, :, j:j + 512] = jax.nn.sigmoid(seg(OFF_GATE + j, 512)).astype(BF16)
    nak = _group_rmsnorm(seg(OFF_KV, NA_W), bd64_ref, gk_ref[...])
    nakv_ref[0, :, 0:NA_W] = nak.astype(BF16)
    nakv_ref[0, :, NA_W:2 * NA_W] = seg(OFF_KV + NA_W, NA_W).astype(BF16)
    dfk = rope(_group_rmsnorm(seg(OFF_KV + 2 * NA_W, DIFF_QK_W), bd32_ref, gdk_ref[...]))
    dfk_ref[0] = dfk.astype(BF16)
    dfv_ref[0] = seg(OFF_KV + 2 * NA_W + DIFF_QK_W, DIFF_V_W).T.astype(BF16)


def _block_diag_mean(width, group):
    i = jnp.arange(width)
    return jnp.where((i[:, None] // group) == (i[None, :] // group), 1.0 / group, 0.0).astype(BF16)


def _rope_tables(T):
    t = jnp.arange(T)
    j = jnp.arange(LANES)
    jj = j % DIFF_HEAD_DIM
    quarter = DIFF_HEAD_DIM // 4
    use_row = jj < DIFF_HEAD_DIM // 2
    first = (jj % (DIFF_HEAD_DIM // 2)) < quarter
    inv = ROPE_BASE ** (-(jj % quarter).astype(F32) / quarter)
    pos = jnp.where(use_row[None, :], (t // GRID_W)[:, None], (t % GRID_W)[:, None]).astype(F32)
    ang = pos * inv[None, :]
    cos, sin = jnp.cos(ang), jnp.sin(ang)
    s_next = jnp.where(first[None, :], -sin, 0.0)
    s_prev = jnp.where(first[None, :], 0.0, sin)
    return cos, s_next, s_prev


def _inproj(x, g, sh, sc, w_bf16, gq, gk, gdq, gdk, rope, tm):
    B, T, D = x.shape
    use_rope = rope is not None
    row = lambda b, i: (b, i, 0)
    const2 = lambda b, i: (0, 0)
    perb = lambda b, i: (b, 0, 0)
    in_specs = [
        pl.BlockSpec((1, tm, D), row),
        pl.BlockSpec((1, D), const2),
        pl.BlockSpec((1, 1, D), perb),
        pl.BlockSpec((1, 1, D), perb),
        pl.BlockSpec((D, IN_COLS), const2),
        pl.BlockSpec((1, NA_W), const2), pl.BlockSpec((1, NA_W), const2),
        pl.BlockSpec((1, DIFF_QK_W), const2), pl.BlockSpec((1, DIFF_QK_W), const2),
        pl.BlockSpec((NA_W, NA_W), const2), pl.BlockSpec((DIFF_QK_W, DIFF_QK_W), const2),
    ]
    args = [x, g, sh, sc, w_bf16, gq, gk, gdq, gdk,
            _block_diag_mean(NA_W, NA_HEAD_DIM), _block_diag_mean(DIFF_QK_W, DIFF_HEAD_DIM)]
    if use_rope:
        in_specs += [pl.BlockSpec((tm, LANES), lambda b, i: (i, 0))] * 3
        args += list(rope)
    outs = [(NA_W, BF16, False), (DIFF_QK_W, BF16, True), (POOL_W, F32, False), (FNET_W, F32, False),
            (N_BRANCH * D_MODEL, BF16, False), (2 * NA_W, BF16, False), (DIFF_QK_W, BF16, False),
            (DIFF_V_W, BF16, True)]
    col = lambda b, i: (b, 0, i)
    out_shape = [jax.ShapeDtypeStruct((B, w, T) if tr else (B, T, w), dt) for w, dt, tr in outs]
    out_specs = [pl.BlockSpec((1, w, tm), col) if tr else pl.BlockSpec((1, tm, w), row) for w, _, tr in outs]
    return pl.pallas_call(
        functools.partial(_inproj_kernel, use_rope=use_rope),
        grid=(B, T // tm), in_specs=in_specs, out_specs=out_specs, out_shape=out_shape,
        compiler_params=_params("parallel", "parallel"), name="inproj",
    )(*args)


NA_ROWS_PER_STEP = 4


def _na_kernel(q_ref, kv0_ref, kv1_ref, kv2_ref, ckv_ref, bias_ref, o_ref):
    q = q_ref[0]
    kv = jnp.concatenate([kv0_ref[0], kv1_ref[0], kv2_ref[0]], axis=0)
    ckv = ckv_ref[0]
    for h in range(NA_HEADS):
        ks = slice(h * NA_HEAD_DIM, (h + 1) * NA_HEAD_DIM)
        vs = slice(NA_W + h * NA_HEAD_DIM, NA_W + (h + 1) * NA_HEAD_DIM)
        qh = q[:, ks]
        s = _dot_nt(qh, kv[:, ks]) + bias_ref[0, h]
        sc = _dot_nt(qh, ckv[:, ks])
        m = jnp.maximum(jnp.max(s, axis=-1, keepdims=True), jnp.max(sc, axis=-1, keepdims=True))
        p = jnp.exp(s - m)
        pc = jnp.exp(sc - m)
        l = jnp.sum(p, axis=-1, keepdims=True) + jnp.sum(pc, axis=-1, keepdims=True)
        o = _dot(p.astype(BF16), kv[:, vs]) + _dot(pc.astype(BF16), ckv[:, vs])
        o_ref[0, :, ks] = o / l


def _na_bias_table(rpb, rows):
    R = NA_ROWS_PER_STEP
    nb = rows // R
    col = jnp.arange(GRID_W)
    cs = jnp.clip(col - NA_WIN_C // 2, 0, GRID_W - NA_WIN_C)
    col_ok = (col[None, :] >= cs[:, None]) & (col[None, :] < cs[:, None] + NA_WIN_C)
    ci = jnp.clip(col[None, :] - col[:, None] + (NA_WIN_C - 1), 0, 2 * NA_WIN_C - 2)
    j = jnp.array([0, 1, nb - 1])[:, None, None]
    r = R * j + jnp.arange(R)[None, :, None]
    krow = R * (j - 1) + jnp.arange(3 * R)[None, None, :]
    rs = jnp.clip(r - NA_WIN_R // 2, 0, rows - NA_WIN_R)
    row_ok = (krow >= rs) & (krow < rs + NA_WIN_R) & (krow >= 0) & (krow < rows)
    ri = jnp.clip(krow - r + (NA_WIN_R - 1), 0, 2 * NA_WIN_R - 2)
    tab = rpb.astype(F32)[:, ri][:, :, :, :, ci]
    ok = row_ok[None, :, :, :, None, None] & col_ok[None, None, None, None]
    tab = jnp.where(ok, tab, NEG_BIG)
    tab = tab.transpose(1, 0, 2, 4, 3, 5)
    return tab.reshape(3, NA_HEADS, R * GRID_W, 3 * R * GRID_W)


def _na_latent(naq, nakv, cnakv, rpb):
    B, T, _ = naq.shape
    rows = T // GRID_W
    R = NA_ROWS_PER_STEP
    assert rows % R == 0 and rows // R >= 3 and rows >= NA_WIN_R and R + NA_WIN_R <= 3 * R
    nb = rows // R
    tb = R * GRID_W
    ctx_len = cnakv.shape[1]
    pattern = lambda j: (j > 0).astype(jnp.int32) + (j == nb - 1).astype(jnp.int32)
    kv_spec = lambda d: pl.BlockSpec((1, tb, 2 * NA_W), lambda b, j: (b, jnp.clip(j + d, 0, nb - 1), 0))
    return pl.pallas_call(
        _na_kernel, grid=(B, nb),
        in_specs=[
            pl.BlockSpec((1, tb, NA_W), lambda b, j: (b, j, 0)),
            kv_spec(-1), kv_spec(0), kv_spec(1),
            pl.BlockSpec((1, ctx_len, 2 * NA_W), lambda b, j: (b, 0, 0)),
            pl.BlockSpec((1, NA_HEADS, tb, 3 * tb), lambda b, j: (pattern(j), 0, 0, 0)),
        ],
        out_specs=pl.BlockSpec((1, tb, NA_W), lambda b, j: (b, j, 0)),
        out_shape=jax.ShapeDtypeStruct((B, T, NA_W), F32),
        compiler_params=_params("parallel", "parallel"), name="na_latent",
    )(naq, nakv, nakv, nakv, cnakv, _na_bias_table(rpb, rows))


DIFF_FIXED_SHIFT_MAX = 60.0


def _diff_kernel(shift_ref, lam_ref, g_ref, qt_ref, k_ref, vt_ref, o_ref, qx_sc, m_sc, l_sc, acc_sc, *,
                 lam_init, tk, nk, online):
    h = pl.program_id(1)
    qt = qt_ref[0]
    tq = qt.shape[1]
    grp = lax.broadcasted_iota(jnp.int32, qt.shape, 0) // DIFF_HEAD_DIM
    for mi in range(2):
        qx_sc[mi] = jnp.where(grp == 2 * h + mi, qt, jnp.zeros_like(qt))
    m_sc[...] = jnp.full(m_sc.shape, -jnp.inf, F32)
    l_sc[...] = jnp.zeros(l_sc.shape, F32)
    acc_sc[...] = jnp.zeros(acc_sc.shape, F32)

    def body(k, carry):
        off = pl.multiple_of(k * tk, tk)
        kb = k_ref[0, pl.ds(off, tk), :]
        vtb = vt_ref[0, :, pl.ds(off, tk)]
        for mi in range(2):
            s = _dot(kb, qx_sc[mi])
            if online:
                m_prev = m_sc[mi]
                m_new = jnp.maximum(m_prev, jnp.max(s, axis=0, keepdims=True))
                alpha = jnp.exp2(m_prev - m_new)
                p = jnp.exp2(s - m_new)
                l_sc[mi] = alpha * l_sc[mi] + p.reshape(tk // 8, 8, tq).sum(axis=0)
                acc_sc[mi] = alpha * acc_sc[mi] + _dot(vtb, p.astype(BF16))
                m_sc[mi] = m_new
            else:
                p = jnp.exp2(s - shift_ref[0])
                l_sc[mi] += p.reshape(tk // 8, 8, tq).sum(axis=0)
                acc_sc[mi] += _dot(vtb, p.astype(BF16))
        return carry

    lax.fori_loop(0, nk, body, 0)

    lv = lam_ref[...]
    lam = (jnp.exp(jnp.sum(lv[0:1] * lv[1:2], axis=-1, keepdims=True))
           - jnp.exp(jnp.sum(lv[2:3] * lv[3:4], axis=-1, keepdims=True)) + lam_init)
    l0 = jnp.sum(l_sc[0], axis=0, keepdims=True)
    l1 = jnp.sum(l_sc[1], axis=0, keepdims=True)
    o = acc_sc[0] / l0 - lam * (acc_sc[1] / l1)
    ms = jnp.mean(o * o, axis=0, keepdims=True)
    o_ref[0] = o * lax.rsqrt(ms + EPS) * g_ref[...] * (1.0 - lam_init)


def _diff_call(shift, qt, k, vt, lam_params, subln_g, lam_init, tq, tk, online):
    B, W, Tq = qt.shape
    Tk = k.shape[1]
    dv = W // DIFF_HEADS
    return pl.pallas_call(
        functools.partial(_diff_kernel, lam_init=lam_init, tk=tk, nk=Tk // tk, online=online),
        grid=(B, DIFF_HEADS, Tq // tq),
        in_specs=[
            pl.BlockSpec(memory_space=pltpu.SMEM),
            pl.BlockSpec((4, DIFF_HEAD_DIM), lambda b, h, i: (0, 0)),
            pl.BlockSpec((dv, 1), lambda b, h, i: (0, 0)),
            pl.BlockSpec((1, W, tq), lambda b, h, i: (b, 0, i)),
            pl.BlockSpec((1, Tk, W), lambda b, h, i: (b, 0, 0)),
            pl.BlockSpec((1, dv, Tk), lambda b, h, i: (b, h, 0)),
        ],
        out_specs=pl.BlockSpec((1, dv, tq), lambda b, h, i: (b, h, i)),
        out_shape=jax.ShapeDtypeStruct((B, W, Tq), F32),
        scratch_shapes=[pltpu.VMEM((2, W, tq), BF16), pltpu.VMEM((2, 1, tq), F32),
                        pltpu.VMEM((2, 8, tq), F32), pltpu.VMEM((2, dv, tq), F32)],
        compiler_params=_params("parallel", "parallel", "parallel"),
        name="diff_attn_online" if online else "diff_attn_fixed",
    )(shift, lam_params, subln_g, qt, k, vt)


def _diff_attn(qt, dfk, vt, gq, gk, lam_params, subln_g, lam_init, tq, tk):
    bound = (1.02 * math.log2(math.e) * DIFF_HEAD_DIM ** 0.5) * jnp.max(jnp.abs(gq)) * jnp.max(jnp.abs(gk))
    shift = bound.astype(F32).reshape(1)
    args = (shift, qt, dfk, vt, lam_params, subln_g, lam_init, tq, tk)
    return lax.cond(bound <= DIFF_FIXED_SHIFT_MAX,
                    lambda: _diff_call(*args, online=False), lambda: _diff_call(*args, online=True))


POOL_HALO = 8


def _pool_mixer(u, prev, nxt, wp_ref, scale, n_total):
    tm = u.shape[0]
    i = pl.program_id(1)
    prev = jnp.where(i > 0, prev, 0.0)
    nxt = jnp.where(i < pl.num_programs(1) - 1, nxt, 0.0)
    ext = jnp.concatenate([prev, u, nxt], axis=0)
    sh = lambda d: ext[POOL_HALO + d:POOL_HALO + d + tm]
    t = i * tm + lax.broadcasted_iota(jnp.int32, (tm, 1), 0)
    lane_grp = lax.broadcasted_iota(jnp.int32, (1, POOL_W), 1) // POOL_GROUP_W
    win = None
    mean = None
    prev_half = 0
    for gi, w in enumerate(POOL_WINDOWS):
        half = w // 2
        for d in range(prev_half, half):
            add = sh(d) + sh(-d - 1)
            win = add if win is None else win + add
        prev_half = half
        cnt = (jnp.minimum(t + half, n_total) - jnp.maximum(t - half, 0)).astype(F32)
        m_w = win / cnt
        mean = m_w if mean is None else jnp.where(lane_grp >= gi, m_w, mean)
    return _dot((mean - u).astype(BF16), wp_ref[...]) * scale


def _merge_kernel(x_ref, gt_ref, yna_ref, ydf_ref, u_ref, up_ref, un_ref, f_ref, gate_ref, wbr_ref, wp_ref,
                  ps_ref, wf_ref, wo_ref, o_ref, *, n_total):
    yf = _dot(f_ref[0].astype(BF16), wf_ref[...])
    ypool = _pool_mixer(u_ref[0], up_ref[0], un_ref[0], wp_ref, ps_ref[...], n_total)
    ys = (yna_ref[0], ydf_ref[0].T, ypool, yf)
    acc = None
    for i, y in enumerate(ys):
        g = gate_ref[0, :, i * D_MODEL:(i + 1) * D_MODEL].astype(F32)
        term = g * _dot(y.astype(BF16), wbr_ref[i])
        acc = term if acc is None else acc + term
    o_ref[0] = x_ref[0] + gt_ref[0] * _dot(acc.astype(BF16), wo_ref[...])


def _merge(x, gt, y_na, y_df, pool_in, f_real, gate, w_br, w_pool_bd, pool_scale, w_f, w_o, tm):
    B, T, D = x.shape
    assert tm % POOL_HALO == 0 and POOL_HALO >= max(POOL_WINDOWS) // 2
    hb = tm // POOL_HALO
    row = lambda b, i: (b, i, 0)
    const2 = lambda b, i: (0, 0)
    return pl.pallas_call(
        functools.partial(_merge_kernel, n_total=T), grid=(B, T // tm),
        in_specs=[
            pl.BlockSpec((1, tm, D), row),
            pl.BlockSpec((1, 1, D), lambda b, i: (b, 0, 0)),
            pl.BlockSpec((1, tm, BRANCH_W), row), pl.BlockSpec((1, BRANCH_W, tm), lambda b, i: (b, 0, i)),
            pl.BlockSpec((1, tm, POOL_W), row),
            pl.BlockSpec((1, POOL_HALO, POOL_W), lambda b, i: (b, jnp.maximum(i * hb - 1, 0), 0)),
            pl.BlockSpec((1, POOL_HALO, POOL_W), lambda b, i: (b, jnp.minimum((i + 1) * hb, T // POOL_HALO - 1), 0)),
            pl.BlockSpec((1, tm, FNET_W), row),
            pl.BlockSpec((1, tm, N_BRANCH * D), row),
            pl.BlockSpec((N_BRANCH, BRANCH_W, D), lambda b, i: (0, 0, 0)),
            pl.BlockSpec((POOL_W, POOL_W), const2), pl.BlockSpec((1, POOL_W), const2),
            pl.BlockSpec((FNET_W, FNET_W), const2),
            pl.BlockSpec((D, D), const2),
        ],
        out_specs=pl.BlockSpec((1, tm, D), row),
        out_shape=jax.ShapeDtypeStruct((B, T, D), F32),
        compiler_params=_params("parallel", "parallel"), name="merge",
    )(x, gt, y_na, y_df, pool_in, pool_in, pool_in, f_real, gate, w_br, w_pool_bd, pool_scale, w_f, w_o)


TOK_BLOCK = LANES
SEL_GROUP = 8
COMBINE_WIN = 32


def _router_kernel(x_ref, g_ref, sh_ref, sc_ref, wrh_ref, wrl_ref, h_ref, aff_ref):
    x = x_ref[0]
    ms = jnp.mean(x * x, axis=-1, keepdims=True)
    h = x * lax.rsqrt(ms + EPS) * g_ref[...] * (1.0 + sc_ref[0]) + sh_ref[0]
    h_ref[0] = h.astype(BF16)
    hi, lo = _split_bf16(h)
    logits = _dot_nt(wrh_ref[...], hi) + _dot_nt(wrh_ref[...], lo) + _dot_nt(wrl_ref[...], hi)
    logits = logits - jnp.max(logits, axis=0, keepdims=True)
    e = jnp.exp(logits)
    aff_ref[0] = e / jnp.sum(e, axis=0, keepdims=True)


def _router(x, g, sh, sc, w_router, tm):
    B, T, D = x.shape
    E = w_router.shape[1]
    wrh, wrl = _split_bf16(w_router.T)
    row = lambda b, i: (b, i, 0)
    return pl.pallas_call(
        _router_kernel, grid=(B, T // tm),
        in_specs=[
            pl.BlockSpec((1, tm, D), row), pl.BlockSpec((1, D), lambda b, i: (0, 0)),
            pl.BlockSpec((1, 1, D), lambda b, i: (b, 0, 0)), pl.BlockSpec((1, 1, D), lambda b, i: (b, 0, 0)),
            pl.BlockSpec((E, D), lambda b, i: (0, 0)), pl.BlockSpec((E, D), lambda b, i: (0, 0)),
        ],
        out_specs=[pl.BlockSpec((1, tm, D), row), pl.BlockSpec((1, E, tm), lambda b, i: (b, 0, i))],
        out_shape=[jax.ShapeDtypeStruct((B, T, D), BF16), jax.ShapeDtypeStruct((B, E, T), F32)],
        compiler_params=_params("parallel", "parallel"), name="router",
    )(x, g, sh, sc, wrh, wrl)


def _select_kernel(aff_ref, u_ref, ls_ref, idx_ref, gate_ref, pos_ref, off_ref, tot_ref, *, cap):
    G, nb, _ = aff_ref.shape
    aff = aff_ref[...]
    bits = pltpu.bitcast(aff, jnp.int32)
    kf = float(cap)
    ones_f = lambda m: jnp.where(m, 1.0, 0.0)
    count = lambda m: jnp.sum(ones_f(m), axis=(1, 2), keepdims=True)

    def search(it, lo):
        cand = lo | jnp.left_shift(jnp.int32(1), 30 - it)
        return jnp.where(count(bits >= cand) >= kf, cand, lo)

    thr = lax.fori_loop(0, 31, search, jnp.zeros((G, 1, 1), jnp.int32))
    need = kf - count(bits > thr)

    u = u_ref[...]
    ls = ls_ref[...]
    ones_m = jnp.ones((LANES, LANES), BF16)

    def prefix(mf):
        mb = mf.astype(BF16)
        tot = _dot(mb, ones_m)
        off = _dot(ls, tot.astype(BF16))
        return _dot(mb, u) + off, off, tot

    blk_i = lax.broadcasted_iota(jnp.int32, (nb, LANES), 0)
    lane_i = lax.broadcasted_iota(jnp.int32, (nb, LANES), 1)
    s_row = lax.broadcasted_iota(jnp.int32, (1, cap), 1).astype(F32)
    col_nb = lax.broadcasted_iota(jnp.int32, (nb, 1), 0).astype(F32)
    col_l = lax.broadcasted_iota(jnp.int32, (LANES, 1), 0).astype(F32)
    for g in range(G):
        gt = bits[g] > thr[g]
        eq = bits[g] == thr[g]
        eqf = ones_f(eq)
        tie_before = prefix(eqf)[0] - eqf
        sel = gt | (eq & (tie_before < need[g]))
        pin, off, tot = prefix(ones_f(sel))
        pos_ref[g] = jnp.where(sel, pin - 1.0, -1.0).astype(jnp.int32)
        diag = blk_i == lane_i
        off_ref[g] = jnp.sum(jnp.where(diag, off, 0.0), axis=0, keepdims=True).astype(jnp.int32)
        tot_ref[g] = jnp.sum(jnp.where(diag, tot, 0.0), axis=0, keepdims=True).astype(jnp.int32)
        blk = jnp.sum(ones_f((off + tot)[:, 0:1] <= s_row), axis=0, keepdims=True)
        oh = jnp.where(col_nb == blk, 1.0, 0.0).astype(BF16)
        pin_t = pin.T
        p_hi = jnp.floor(pin_t * (1.0 / LANES))
        p_lo = pin_t - p_hi * LANES
        rows = _dot(p_hi.astype(BF16), oh) * LANES + _dot(p_lo.astype(BF16), oh)
        lane = jnp.sum(ones_f(rows <= s_row), axis=0, keepdims=True)
        idx_ref[g] = (blk * LANES + lane).astype(jnp.int32)
        a_t = aff[g].T
        a1 = a_t.astype(BF16)
        r1 = a_t - a1.astype(F32)
        a2 = r1.astype(BF16)
        a3 = (r1 - a2.astype(F32)).astype(BF16)
        arow = _dot(a1, oh) + _dot(a2, oh) + _dot(a3, oh)
        gate_ref[g] = jnp.sum(jnp.where(col_l == lane, arow, 0.0), axis=0, keepdims=True)


def _select(aff, cap):
    B, E, T = aff.shape
    nb = T // TOK_BLOCK
    R = B * E
    G = SEL_GROUP
    i = np.arange(LANES)
    u = jnp.asarray(i[:, None] <= i[None, :], BF16)
    j = np.arange(nb)
    ls = jnp.asarray(j[None, :] < j[:, None], BF16)
    grp = lambda r: (r, 0, 0)
    outs = pl.pallas_call(
        functools.partial(_select_kernel, cap=cap), grid=(R // G,),
        in_specs=[pl.BlockSpec((G, nb, LANES), grp), pl.BlockSpec((LANES, LANES), lambda r: (0, 0)),
                  pl.BlockSpec((nb, nb), lambda r: (0, 0))],
        out_specs=[pl.BlockSpec((G, 1, cap), grp), pl.BlockSpec((G, 1, cap), grp),
                   pl.BlockSpec((G, nb, LANES), grp), pl.BlockSpec((G, 1, LANES), grp),
                   pl.BlockSpec((G, 1, LANES), grp)],
        out_shape=[jax.ShapeDtypeStruct((R, 1, cap), jnp.int32), jax.ShapeDtypeStruct((R, 1, cap), F32),
                   jax.ShapeDtypeStruct((R, nb, LANES), jnp.int32), jax.ShapeDtypeStruct((R, 1, LANES), jnp.int32),
                   jax.ShapeDtypeStruct((R, 1, LANES), jnp.int32)],
        compiler_params=_params("parallel"), name="select",
    )(aff.reshape(R, nb, LANES), u, ls)
    idx, gate, pos, off, tot = outs
    return (idx.reshape(B, E, cap), gate.reshape(B, E, cap), pos.reshape(B, E, T),
            off[:, 0, :nb].reshape(B, E, nb), tot[:, 0, :nb].reshape(B, E, nb))


def _combine_kernel(off_ref, tot_ref, x_ref, gt_ref, pos_ref, ye_hbm, o_ref, buf, xbuf, oh_sc, acc_sc, sem,
                    xsem, *, nt, cap):
    n = pl.program_id(0)
    E, W = N_EXPERTS, COMBINE_WIN

    def window(step, e):
        b = step // nt
        o = off_ref[(b * E + e) * nt + step % nt]
        return b, o, jnp.minimum((o // 8) * 8, cap - W)

    def fetch(step, slot):
        for e in range(E):
            b, _, st = window(step, e)
            pltpu.make_async_copy(ye_hbm.at[b, e, pl.ds(pl.multiple_of(st, 8), W)], buf.at[slot, e],
                                  sem.at[slot]).start()

    @pl.when(n == 0)
    def _():
        fetch(0, 0)

    @pl.when(n + 1 < pl.num_programs(0))
    def _():
        fetch(n + 1, (n + 1) % 2)

    slot = n % 2
    for e in range(E):
        pltpu.make_async_copy(ye_hbm.at[0, 0, pl.ds(0, W)], buf.at[slot, e], sem.at[slot]).wait()

    pos = pos_ref[0]
    w_i = lax.broadcasted_iota(jnp.int32, (1, W), 1)
    for e in range(E):
        _, _, st = window(n, e)
        oh_sc[:, e * W:(e + 1) * W] = jnp.where(pos[:, e:e + 1] == st + w_i, 1.0, 0.0)
    oh_all = oh_sc[...].astype(BF16)
    rh, rl = _split_bf16(buf[slot].reshape(E * W, buf.shape[-1]))
    acc_sc[...] = _dot(oh_all, rh) + _dot(oh_all, rl)

    for e in range(E):
        b, o, st = window(n, e)
        t = tot_ref[(b * E + e) * nt + n % nt]
        n_extra = jnp.maximum(o + t - (st + W) + (W - 1), 0) // W

        def extra(j, carry, e=e, b=b, st=st):
            base = st + W * (j + 1)
            src = jnp.minimum(base, cap - W)
            cp = pltpu.make_async_copy(ye_hbm.at[b, e, pl.ds(pl.multiple_of(src, 8), W)], xbuf, xsem.at[0])
            cp.start()
            cp.wait()
            slot_i = src + w_i
            oh = jnp.where((pos[:, e:e + 1] == slot_i) & (slot_i >= base), 1.0, 0.0).astype(BF16)
            xh, xl = _split_bf16(xbuf[...])
            acc_sc[...] += _dot(oh, xh) + _dot(oh, xl)
            return carry

        lax.fori_loop(0, n_extra, extra, 0)

    o_ref[0] = x_ref[0] + gt_ref[0] * acc_sc[...]


def _combine(x, gt, pos_t, off, tot, ye):
    B, T, D = x.shape
    E, cap = ye.shape[1], ye.shape[2]
    nt = T // TOK_BLOCK
    W = COMBINE_WIN
    assert cap % W == 0 and W % 8 == 0
    tile = lambda n, off_r, tot_r: (n // nt, n % nt, 0)
    grid_spec = pltpu.PrefetchScalarGridSpec(
        num_scalar_prefetch=2, grid=(B * nt,),
        in_specs=[
            pl.BlockSpec((1, TOK_BLOCK, D), tile),
            pl.BlockSpec((1, 1, D), lambda n, off_r, tot_r: (n // nt, 0, 0)),
            pl.BlockSpec((1, TOK_BLOCK, E), tile),
            pl.BlockSpec(memory_space=pl.ANY),
        ],
        out_specs=pl.BlockSpec((1, TOK_BLOCK, D), tile),
        scratch_shapes=[pltpu.VMEM((2, E, W, D), F32), pltpu.VMEM((W, D), F32),
                        pltpu.VMEM((TOK_BLOCK, E * W), F32), pltpu.VMEM((TOK_BLOCK, D), F32),
                        pltpu.SemaphoreType.DMA((2,)), pltpu.SemaphoreType.DMA((1,))],
    )
    return pl.pallas_call(
        functools.partial(_combine_kernel, nt=nt, cap=cap), grid_spec=grid_spec,
        out_shape=jax.ShapeDtypeStruct((B, T, D), F32),
        compiler_params=_params("arbitrary"), name="combine",
    )(off.reshape(-1), tot.reshape(-1), x, gt, pos_t, ye)


FF_CHUNKS = ((0, 512), (512, 512), (1024, 384))


def _expert_kernel(xe_ref, g_ref, wg_ref, wu_ref, wd_ref, o_ref):
    xe = xe_ref[0, 0]
    acc = None
    for lo, n in FF_CHUNKS:
        a = _dot(xe, wg_ref[0, 0, :, lo:lo + n].astype(BF16))
        u = _dot(xe, wu_ref[0, 0, :, lo:lo + n].astype(BF16))
        hid = (a * jax.nn.sigmoid(a) * u).astype(BF16)
        part = _dot(hid, wd_ref[0, 0, lo:lo + n, :].astype(BF16))
        acc = part if acc is None else acc + part
    o_ref[0, 0] = acc * g_ref[0, 0]


def _experts(xe, g, wg, wu, wd, layer, tm):
    B, E, cap, D = xe.shape
    F = wg.shape[-1]
    assert F == EXPERT_FF
    tile = lambda e, b, i: (b, e, i, 0)
    wsel = lambda e, b, i: (layer, e, 0, 0)
    return pl.pallas_call(
        _expert_kernel, grid=(E, B, cap // tm),
        in_specs=[
            pl.BlockSpec((1, 1, tm, D), tile), pl.BlockSpec((1, 1, tm, 1), tile),
            pl.BlockSpec((1, 1, D, F), wsel), pl.BlockSpec((1, 1, D, F), wsel), pl.BlockSpec((1, 1, F, D), wsel),
        ],
        out_specs=pl.BlockSpec((1, 1, tm, D), tile),
        out_shape=jax.ShapeDtypeStruct((B, E, cap, D), F32),
        compiler_params=_params("parallel", "parallel", "parallel"), name="experts",
    )(xe, g, wg, wu, wd)


SEL_TOKENS = TOK_BLOCK * LANES


def _ec_ffn(x, gt, g, sh, sc, w_router, wg, wu, wd, layer, tm_router, tm_expert):
    B, N, D = x.shape
    cap = max(1, EC_FACTOR * N // N_EXPERTS)
    assert N <= SEL_TOKENS and N % TOK_BLOCK == 0 and cap % COMBINE_WIN == 0
    h, aff = _router(x, g, sh, sc, w_router, tm_router)
    aff = jnp.pad(aff, ((0, 0), (0, 0), (0, SEL_TOKENS - N)))
    idx, gsel, pos, off, tot = _select(aff, cap)
    nt = N // TOK_BLOCK
    xe = h[jnp.arange(B)[:, None, None], idx]
    ye = _experts(xe, gsel[..., None], wg, wu, wd, layer, min(tm_expert, cap))
    return _combine(x, gt, pos[:, :, :N].transpose(0, 2, 1), off[:, :, :nt], tot[:, :, :nt], ye)


DFT_RADIX = 128


def _dot3(a_hi, a_lo, b_hi, b_lo):
    return _dot(a_hi, b_hi) + _dot(a_lo, b_hi) + _dot(a_hi, b_lo)


def _cos_sin(n_rows, n_cols, period):
    k = (np.arange(n_rows)[:, None] * np.arange(n_cols)[None, :]) % period
    ang = 2.0 * np.pi * k.astype(np.float64) / period
    return np.cos(ang), np.sin(ang)


def _const_split(m):
    return _split_bf16(jnp.asarray(m, F32))


def _chan_dft_kernel(u_ref, mh_ref, ml_ref, xr_ref, xi_ref):
    uh, ul = _split_bf16(u_ref[0])
    y = _dot3(uh, ul, mh_ref[...], ml_ref[...])
    c = xr_ref.shape[-1]
    xr_ref[0] = y[:, :c]
    xi_ref[0] = y[:, c:]


def _chan_dft(u, tm):
    B, N, C = u.shape
    cc, sc = _cos_sin(C, C, C)
    mh, ml = _const_split(np.concatenate([cc, -sc], axis=1))
    row = lambda b, i: (b, i, 0)
    const2 = lambda b, i: (0, 0)
    return pl.pallas_call(
        _chan_dft_kernel, grid=(B, N // tm),
        in_specs=[pl.BlockSpec((1, tm, C), row), pl.BlockSpec((C, 2 * C), const2), pl.BlockSpec((C, 2 * C), const2)],
        out_specs=[pl.BlockSpec((1, tm, C), row)] * 2,
        out_shape=[jax.ShapeDtypeStruct((B, N, C), F32)] * 2,
        compiler_params=_params("parallel", "parallel"), name="chan_dft",
    )(u, mh, ml)


def _dft_left_kernel(xr_ref, xi_ref, mh_ref, ml_ref, o_ref, *, scale):
    xh, xl = _split_bf16(jnp.concatenate([xr_ref[0], xi_ref[0]], axis=0))
    o_ref[0] = _dot3(mh_ref[...], ml_ref[...], xh, xl) * scale


def _dft_left(xr, xi, m, scale, tn):
    B, K, cols = xr.shape
    R = m.shape[0]
    mh, ml = _const_split(m)
    col = lambda b, j: (b, 0, j)
    const2 = lambda b, j: (0, 0)
    return pl.pallas_call(
        functools.partial(_dft_left_kernel, scale=scale), grid=(B, cols // tn),
        in_specs=[pl.BlockSpec((1, K, tn), col), pl.BlockSpec((1, K, tn), col),
                  pl.BlockSpec((R, 2 * K), const2), pl.BlockSpec((R, 2 * K), const2)],
        out_specs=pl.BlockSpec((1, R, tn), col),
        out_shape=jax.ShapeDtypeStruct((B, R, cols), F32),
        compiler_params=_params("parallel", "parallel"), name="dft_left",
    )(xr, xi, mh, ml)


DFT_K1_PER_STEP = 8


def _dft_twiddle_kernel(y_ref, tc_ref, ts_ref, dh_ref, dl_ref, o_ref, *, scale):
    tc = tc_ref[0]
    ts = ts_ref[0]
    for kk in range(DFT_K1_PER_STEP):
        yr = y_ref[0, 0, kk]
        yi = y_ref[0, 1, kk]
        c, s = tc[:, kk:kk + 1], ts[:, kk:kk + 1]
        zh, zl = _split_bf16(jnp.concatenate([yr * c + yi * s, yi * c - yr * s], axis=0))
        o_ref[0, kk] = _dot3(dh_ref[...], dl_ref[...], zh, zl) * scale


def _fourier_real(u):
    B, N, C = u.shape
    scale = 1.0 / math.sqrt(N * C)
    xr, xi = _chan_dft(u, min(N, 512))
    if N % (DFT_RADIX * DFT_K1_PER_STEP) != 0:
        assert N <= 1024
        cn, sn = _cos_sin(N, N, N)
        return _dft_left(xr, xi, np.concatenate([cn, sn], axis=1), scale, C)
    n1, n2 = N // DFT_RADIX, DFT_RADIX
    c1, s1 = _cos_sin(n1, n1, n1)
    m1 = np.block([[c1, s1], [-s1, c1]])
    cols = n2 * C
    yy = _dft_left(xr.reshape(B, n1, cols), xi.reshape(B, n1, cols), m1, 1.0, min(cols, 2048))
    yy = yy.reshape(B, 2, n1, n2, C)
    tcos, tsin = _cos_sin(n1, n2, N)
    grp = lambda t: jnp.asarray(t.reshape(n1 // DFT_K1_PER_STEP, DFT_K1_PER_STEP, n2).transpose(0, 2, 1), F32)
    c2, s2 = _cos_sin(n2, n2, n2)
    dh, dl = _const_split(np.concatenate([c2, s2], axis=1))
    fp = pl.pallas_call(
        functools.partial(_dft_twiddle_kernel, scale=scale), grid=(B, n1 // DFT_K1_PER_STEP),
        in_specs=[
            pl.BlockSpec((1, 2, DFT_K1_PER_STEP, n2, C), lambda b, g: (b, 0, g, 0, 0)),
            pl.BlockSpec((1, n2, DFT_K1_PER_STEP), lambda b, g: (g, 0, 0)),
            pl.BlockSpec((1, n2, DFT_K1_PER_STEP), lambda b, g: (g, 0, 0)),
            pl.BlockSpec((n2, 2 * n2), lambda b, g: (0, 0)), pl.BlockSpec((n2, 2 * n2), lambda b, g: (0, 0)),
        ],
        out_specs=pl.BlockSpec((1, DFT_K1_PER_STEP, n2, C), lambda b, g: (b, g, 0, 0)),
        out_shape=jax.ShapeDtypeStruct((B, n1, n2, C), F32),
        compiler_params=_params("parallel", "parallel"), name="dft_twiddle",
    )(yy, grp(tcos), grp(tsin), dh, dl)
    return fp.transpose(0, 2, 1, 3).reshape(B, N, C)


def _ctx_attn_kernel(q_ref, kv_ref, o_ref):
    q = q_ref[0]
    kv = kv_ref[0]
    for h in range(NA_HEADS):
        ks = slice(h * NA_HEAD_DIM, (h + 1) * NA_HEAD_DIM)
        vs = slice(NA_W + h * NA_HEAD_DIM, NA_W + (h + 1) * NA_HEAD_DIM)
        s = _dot_nt(q[:, ks], kv[:, ks])
        p = jnp.exp(s - jnp.max(s, axis=-1, keepdims=True))
        l = jnp.sum(p, axis=-1, keepdims=True)
        o_ref[0, :, ks] = _dot(p.astype(BF16), kv[:, vs]) / l


def _ctx_dense_attn(q, kv):
    B, Q, _ = q.shape
    return pl.pallas_call(
        _ctx_attn_kernel, grid=(B,),
        in_specs=[pl.BlockSpec((1, Q, NA_W), lambda b: (b, 0, 0)),
                  pl.BlockSpec((1, kv.shape[1], 2 * NA_W), lambda b: (b, 0, 0))],
        out_specs=pl.BlockSpec((1, Q, NA_W), lambda b: (b, 0, 0)),
        out_shape=jax.ShapeDtypeStruct((B, Q, NA_W), F32),
        compiler_params=_params("parallel"), name="ctx_attn",
    )(q, kv)


ADA_ROWS = 8
ADA_TN = 512


def _ada_kernel(c_ref, w_ref, b_ref, o_ref):
    c = c_ref[...]
    sh, sl = _split_bf16(c * jax.nn.sigmoid(c))
    wh, wl = _split_bf16(w_ref[0])
    o_ref[...] = _dot3(sh, sl, wh, wl) + b_ref[0]


def _ada_mod(c_rows, w_ada, b_ada, layer):
    R, D = c_rows.shape
    N = w_ada.shape[-1]
    return pl.pallas_call(
        _ada_kernel, grid=(N // ADA_TN,),
        in_specs=[pl.BlockSpec((R, D), lambda j: (0, 0)),
                  pl.BlockSpec((1, D, ADA_TN), lambda j: (layer, 0, j)),
                  pl.BlockSpec((1, 1, ADA_TN), lambda j: (layer, 0, j))],
        out_specs=pl.BlockSpec((R, ADA_TN), lambda j: (0, j)),
        out_shape=jax.ShapeDtypeStruct((R, N), F32),
        compiler_params=_params("parallel"), name="ada_mod",
    )(c_rows, w_ada, b_ada[:, None, :])


def _tile(g, n):
    return jnp.tile(g.astype(F32), n)[None, :]


def kernel(x, c, ctx, c_ctx, w_ada, b_ada, g_mix, g_ffn, w_in, na_q_g, na_k_g, na_rpb, df_q_g, df_k_g,
           df_lambda, df_subln_g, pool_w, pool_scale, fnet_w, w_branch, w_out, w_router, w_gate_e, w_up_e,
           w_down_e):
    B, T, D = x.shape
    ctx_len = ctx.shape[1]
    rope = _rope_tables(T)
    assert B + 1 <= ADA_ROWS
    c_rows = jnp.concatenate([c, c_ctx[None, :], jnp.zeros((ADA_ROWS - B - 1, D), F32)], axis=0)
    tm = 256
    tq, tk = 1024, 1280
    for l in range(DEPTH):
        last = l == DEPTH - 1
        lam_init = 0.8 - 0.6 * math.exp(-0.3 * l)
        mods = _ada_mod(c_rows, w_ada, b_ada, l)
        sh1, sc1, gt1, sh2, sc2, gt2 = [m[:, None, :] for m in jnp.split(mods[:B], 6, axis=-1)]
        bc = lambda m: jnp.broadcast_to(m[None, None, :], (B, 1, D))
        csh1, csc1, cgt1, csh2, csc2, cgt2 = [bc(m) for m in jnp.split(mods[B], 6, axis=-1)]

        w_bf = w_in[l].astype(BF16)
        gq, gk = _tile(na_q_g[l], NA_HEADS), _tile(na_k_g[l], NA_HEADS)
        gdq, gdk = _tile(df_q_g[l], 2 * DIFF_HEADS), _tile(df_k_g[l], 2 * DIFF_HEADS)
        gmix = g_mix[l][None, :]
        gffn = g_ffn[l][None, :]
        subg = df_subln_g[l][:, None].astype(F32)
        wbr = w_branch[l].astype(BF16)
        wf = fnet_w[l].astype(BF16)
        wpool = jax.scipy.linalg.block_diag(*[pool_w[l, gi] for gi in range(len(POOL_WINDOWS))]).astype(BF16)
        pscale = pool_scale[l][None, :].astype(F32)
        wo = w_out[l].astype(BF16)

        naq, dfq, pool_in, fnet_in, gate, nakv, dfk, dfv = _inproj(
            x, gmix, sh1, sc1, w_bf, gq, gk, gdq, gdk, rope, 2 * tm)
        (cnaq, cdfq, cpool_in, cfnet_in, cgate, cnakv, cdfk, cdfv) = _inproj(
            ctx, gmix, csh1, csc1, w_bf, gq, gk, gdq, gdk, None, ctx_len)

        y_na = _na_latent(naq, nakv, cnakv, na_rpb[l])
        lamp = df_lambda[l].astype(F32)
        y_df = _diff_attn(dfq, jnp.concatenate([dfk, cdfk], axis=1), jnp.concatenate([dfv, cdfv], axis=2),
                          df_q_g[l], df_k_g[l], lamp, subg, lam_init, tq, tk)
        f_real = _fourier_real(fnet_in)
        x_new = _merge(x, gt1, y_na, y_df, pool_in, f_real, gate, wbr, wpool, pscale, wf, wo, tm)

        if not last:
            yc_na = _ctx_dense_attn(cnaq, cnakv)
            yc_df = _diff_attn(cdfq, cdfk, cdfv, df_q_g[l], df_k_g[l], lamp, subg, lam_init, ctx_len, ctx_len)
            fc_real = _fourier_real(cfnet_in)
            ctx_new = _merge(ctx, cgt1, yc_na, yc_df, cpool_in, fc_real, cgate, wbr, wpool, pscale, wf, wo,
                             ctx_len)

        x = x_new
        x = _ec_ffn(x, gt2, gffn, sh2, sc2, w_router[l], w_gate_e, w_up_e, w_down_e, l, tm, 512)
        if not last:
            ctx = ctx_new
            ctx = _ec_ffn(ctx, cgt2, gffn, csh2, csc2, w_router[l], w_gate_e, w_up_e, w_down_e, l, ctx_len, 512)
    return x
```

```python
import functools
import math

import jax
import jax.numpy as jnp
import numpy as np
from jax import lax
from jax.experimental import pallas as pl
from jax.experimental.pallas import tpu as pltpu

F32 = jnp.float32
BF16 = jnp.bfloat16

D_MODEL = 1024
DEPTH = 2
GRID_W = 64
EPS = 1e-6
ROPE_BASE = 10000.0

NA_HEADS = 4
NA_HEAD_DIM = 64
NA_WIN_R = 8
NA_WIN_C = 16
NA_W = NA_HEADS * NA_HEAD_DIM

DIFF_HEADS = 4
DIFF_HEAD_DIM = 32
DIFF_QK_W = DIFF_HEADS * 2 * DIFF_HEAD_DIM
DIFF_V_W = DIFF_HEADS * 2 * DIFF_HEAD_DIM

POOL_WINDOWS = (2, 4, 8, 16)
POOL_GROUP_W = 64
POOL_W = len(POOL_WINDOWS) * POOL_GROUP_W
FNET_W = 256
N_BRANCH = 4
BRANCH_W = 256

OFF_NA_Q = 0
OFF_DF_Q = OFF_NA_Q + NA_W
OFF_POOL = OFF_DF_Q + DIFF_QK_W
OFF_FNET = OFF_POOL + POOL_W
OFF_GATE = OFF_FNET + FNET_W
OFF_KV = OFF_GATE + N_BRANCH * D_MODEL
KV_W = 2 * NA_W + DIFF_QK_W + DIFF_V_W
IN_COLS = OFF_KV + KV_W

N_EXPERTS = 16
EC_FACTOR = 2
EXPERT_FF = 1408

VMEM_LIMIT_BYTES = 56 * 1024 * 1024
LANES = 128
NEG_BIG = -1e30


def _params(*sem):
    return pltpu.CompilerParams(dimension_semantics=sem, vmem_limit_bytes=VMEM_LIMIT_BYTES)


def _split_bf16(a):
    hi = a.astype(BF16)
    lo = (a - hi.astype(F32)).astype(BF16)
    return hi, lo


def _dot(a, b):
    return jnp.dot(a, b, preferred_element_type=F32)


def _dot_nt(a, b):
    return lax.dot_general(a, b, (((1,), (1,)), ((), ())), preferred_element_type=F32)


def _group_rmsnorm(p, bd_ref, g):
    hi, lo = _split_bf16(p * p)
    ms = _dot(hi, bd_ref[...]) + _dot(lo, bd_ref[...])
    return p * lax.rsqrt(ms + EPS) * g


def _rope256(y, cos, s_next, s_prev):
    outs = []
    for half in range(2):
        z = y[:, half * LANES:(half + 1) * LANES]
        outs.append(z * cos + pltpu.roll(z, LANES - 8, 1) * s_next + pltpu.roll(z, 8, 1) * s_prev)
    return jnp.concatenate(outs, axis=1)


def _inproj_kernel(*refs, use_rope):
    if use_rope:
        (x_ref, g_ref, sh_ref, sc_ref, w_ref, gq_ref, gk_ref, gdq_ref, gdk_ref, bd64_ref, bd32_ref,
         cos_ref, sn_ref, sp_ref,
         naq_ref, dfq_ref, pool_ref, fnet_ref, gate_ref, nakv_ref, dfk_ref, dfv_ref) = refs
    else:
        (x_ref, g_ref, sh_ref, sc_ref, w_ref, gq_ref, gk_ref, gdq_ref, gdk_ref, bd64_ref, bd32_ref,
         naq_ref, dfq_ref, pool_ref, fnet_ref, gate_ref, nakv_ref, dfk_ref, dfv_ref) = refs
    x = x_ref[0]
    ms = jnp.mean(x * x, axis=-1, keepdims=True)
    y = x * lax.rsqrt(ms + EPS) * g_ref[...]
    h = (y * (1.0 + sc_ref[0]) + sh_ref[0]).astype(BF16)

    def seg(lo, n):
        return _dot(h, w_ref[:, lo:lo + n])

    def rope(v):
        if not use_rope:
            return v
        return _rope256(v, cos_ref[...], sn_ref[...], sp_ref[...])

    naq = _group_rmsnorm(seg(OFF_NA_Q, NA_W), bd64_ref, gq_ref[...])
    naq_ref[0] = (naq * (NA_HEAD_DIM ** -0.5)).astype(BF16)
    dfq = rope(_group_rmsnorm(seg(OFF_DF_Q, DIFF_QK_W), bd32_ref, gdq_ref[...]))
    dfq_ref[0] = (dfq * (math.log2(math.e) * DIFF_HEAD_DIM ** -0.5)).T.astype(BF16)
    pool_ref[0] = seg(OFF_POOL, POOL_W)
    fnet_ref[0] = seg(OFF_FNET, FNET_W)
    for j in range(0, N_BRANCH * D_MODEL, 512):
        gate_ref[0, :, j:j + 512] = jax.nn.sigmoid(seg(OFF_GATE + j, 512)).astype(BF16)
    nak = _group_rmsnorm(seg(OFF_KV, NA_W), bd64_ref, gk_ref[...])
    nakv_ref[0, :, 0:NA_W] = nak.astype(BF16)
    nakv_ref[0, :, NA_W:2 * NA_W] = seg(OFF_KV + NA_W, NA_W).astype(BF16)
    dfk = rope(_group_rmsnorm(seg(OFF_KV + 2 * NA_W, DIFF_QK_W), bd32_ref, gdk_ref[...]))
    dfk_ref[0] = dfk.astype(BF16)
    dfv_ref[0] = seg(OFF_KV + 2 * NA_W + DIFF_QK_W, DIFF_V_W).T.astype(BF16)


def _block_diag_mean(width, group):
    i = jnp.arange(width)
    return jnp.where((i[:, None] // group) == (i[None, :] // group), 1.0 / group, 0.0).astype(BF16)


def _rope_tables(T):
    t = jnp.arange(T)
    j = jnp.arange(LANES)
    jj = j % DIFF_HEAD_DIM
    quarter = DIFF_HEAD_DIM // 4
    use_row = jj < DIFF_HEAD_DIM // 2
    first = (jj % (DIFF_HEAD_DIM // 2)) < quarter
    inv = ROPE_BASE ** (-(jj % quarter).astype(F32) / quarter)
    pos = jnp.where(use_row[None, :], (t // GRID_W)[:, None], (t % GRID_W)[:, None]).astype(F32)
    ang = pos * inv[None, :]
    cos, sin = jnp.cos(ang), jnp.sin(ang)
    s_next = jnp.where(first[None, :], -sin, 0.0)
    s_prev = jnp.where(first[None, :], 0.0, sin)
    return cos, s_next, s_prev


def _inproj(x, g, sh, sc, w_bf16, gq, gk, gdq, gdk, rope, tm):
    B, T, D = x.shape
    use_rope = rope is not None
    row = lambda b, i: (b, i, 0)
    const2 = lambda b, i: (0, 0)
    perb = lambda b, i: (b, 0, 0)
    in_specs = [
        pl.BlockSpec((1, tm, D), row),
        pl.BlockSpec((1, D), const2),
        pl.BlockSpec((1, 1, D), perb),
        pl.BlockSpec((1, 1, D), perb),
        pl.BlockSpec((D, IN_COLS), const2),
        pl.BlockSpec((1, NA_W), const2), pl.BlockSpec((1, NA_W), const2),
        pl.BlockSpec((1, DIFF_QK_W), const2), pl.BlockSpec((1, DIFF_QK_W), const2),
        pl.BlockSpec((NA_W, NA_W), const2), pl.BlockSpec((DIFF_QK_W, DIFF_QK_W), const2),
    ]
    args = [x, g, sh, sc, w_bf16, gq, gk, gdq, gdk,
            _block_diag_mean(NA_W, NA_HEAD_DIM), _block_diag_mean(DIFF_QK_W, DIFF_HEAD_DIM)]
    if use_rope:
        in_specs += [pl.BlockSpec((tm, LANES), lambda b, i: (i, 0))] * 3
        args += list(rope)
    outs = [(NA_W, BF16, False), (DIFF_QK_W, BF16, True), (POOL_W, F32, False), (FNET_W, F32, False),
            (N_BRANCH * D_MODEL, BF16, False), (2 * NA_W, BF16, False), (DIFF_QK_W, BF16, False),
            (DIFF_V_W, BF16, True)]
    col = lambda b, i: (b, 0, i)
    out_shape = [jax.ShapeDtypeStruct((B, w, T) if tr else (B, T, w), dt) for w, dt, tr in outs]
    out_specs = [pl.BlockSpec((1, w, tm), col) if tr else pl.BlockSpec((1, tm, w), row) for w, _, tr in outs]
    return pl.pallas_call(
        functools.partial(_inproj_kernel, use_rope=use_rope),
        grid=(B, T // tm), in_specs=in_specs, out_specs=out_specs, out_shape=out_shape,
        compiler_params=_params("parallel", "parallel"), name="inproj",
    )(*args)


NA_ROWS_PER_STEP = 4


def _na_kernel(q_ref, kv0_ref, kv1_ref, kv2_ref, ckv_ref, bias_ref, o_ref):
    q = q_ref[0]
    kv = jnp.concatenate([kv0_ref[0], kv1_ref[0], kv2_ref[0]], axis=0)
    ckv = ckv_ref[0]
    for h in range(NA_HEADS):
        ks = slice(h * NA_HEAD_DIM, (h + 1) * NA_HEAD_DIM)
        vs = slice(NA_W + h * NA_HEAD_DIM, NA_W + (h + 1) * NA_HEAD_DIM)
        qh = q[:, ks]
        s = _dot_nt(qh, kv[:, ks]) + bias_ref[0, h]
        sc = _dot_nt(qh, ckv[:, ks])
        m = jnp.maximum(jnp.max(s, axis=-1, keepdims=True), jnp.max(sc, axis=-1, keepdims=True))
        p = jnp.exp(s - m)
        pc = jnp.exp(sc - m)
        l = jnp.sum(p, axis=-1, keepdims=True) + jnp.sum(pc, axis=-1, keepdims=True)
        o = _dot(p.astype(BF16), kv[:, vs]) + _dot(pc.astype(BF16), ckv[:, vs])
        o_ref[0, :, ks] = o / l


def _na_bias_table(rpb, rows):
    R = NA_ROWS_PER_STEP
    nb = rows // R
    col = jnp.arange(GRID_W)
    cs = jnp.clip(col - NA_WIN_C // 2, 0, GRID_W - NA_WIN_C)
    col_ok = (col[None, :] >= cs[:, None]) & (col[None, :] < cs[:, None] + NA_WIN_C)
    ci = jnp.clip(col[None, :] - col[:, None] + (NA_WIN_C - 1), 0, 2 * NA_WIN_C - 2)
    j = jnp.array([0, 1, nb - 1])[:, None, None]
    r = R * j + jnp.arange(R)[None, :, None]
    krow = R * (j - 1) + jnp.arange(3 * R)[None, None, :]
    rs = jnp.clip(r - NA_WIN_R // 2, 0, rows - NA_WIN_R)
    row_ok = (krow >= rs) & (krow < rs + NA_WIN_R) & (krow >= 0) & (krow < rows)
    ri = jnp.clip(krow - r + (NA_WIN_R - 1), 0, 2 * NA_WIN_R - 2)
    tab = rpb.astype(F32)[:, ri][:, :, :, :, ci]
    ok = row_ok[None, :, :, :, None, None] & col_ok[None, None, None, None]
    tab = jnp.where(ok, tab, NEG_BIG)
    tab = tab.transpose(1, 0, 2, 4, 3, 5)
    return tab.reshape(3, NA_HEADS, R * GRID_W, 3 * R * GRID_W)


def _na_latent(naq, nakv, cnakv, rpb):
    B, T, _ = naq.shape
    rows = T // GRID_W
    R = NA_ROWS_PER_STEP
    assert rows % R == 0 and rows // R >= 3 and rows >= NA_WIN_R and R + NA_WIN_R <= 3 * R
    nb = rows // R
    tb = R * GRID_W
    ctx_len = cnakv.shape[1]
    pattern = lambda j: (j > 0).astype(jnp.int32) + (j == nb - 1).astype(jnp.int32)
    kv_spec = lambda d: pl.BlockSpec((1, tb, 2 * NA_W), lambda b, j: (b, jnp.clip(j + d, 0, nb - 1), 0))
    return pl.pallas_call(
        _na_kernel, grid=(B, nb),
        in_specs=[
            pl.BlockSpec((1, tb, NA_W), lambda b, j: (b, j, 0)),
            kv_spec(-1), kv_spec(0), kv_spec(1),
            pl.BlockSpec((1, ctx_len, 2 * NA_W), lambda b, j: (b, 0, 0)),
            pl.BlockSpec((1, NA_HEADS, tb, 3 * tb), lambda b, j: (pattern(j), 0, 0, 0)),
        ],
        out_specs=pl.BlockSpec((1, tb, NA_W), lambda b, j: (b, j, 0)),
        out_shape=jax.ShapeDtypeStruct((B, T, NA_W), F32),
        compiler_params=_params("parallel", "parallel"), name="na_latent",
    )(naq, nakv, nakv, nakv, cnakv, _na_bias_table(rpb, rows))


DIFF_FIXED_SHIFT_MAX = 60.0


def _diff_kernel(shift_ref, lam_ref, g_ref, qt_ref, k_ref, vt_ref, o_ref, qx_sc, m_sc, l_sc, acc_sc, *,
                 lam_init, tk, nk, online):
    h = pl.program_id(1)
    qt = qt_ref[0]
    tq = qt.shape[1]
    grp = lax.broadcasted_iota(jnp.int32, qt.shape, 0) // DIFF_HEAD_DIM
    for mi in range(2):
        qx_sc[mi] = jnp.where(grp == 2 * h + mi, qt, jnp.zeros_like(qt))
    m_sc[...] = jnp.full(m_sc.shape, -jnp.inf, F32)
    l_sc[...] = jnp.zeros(l_sc.shape, F32)
    acc_sc[...] = jnp.zeros(acc_sc.shape, F32)

    def body(k, carry):
        off = pl.multiple_of(k * tk, tk)
        kb = k_ref[0, pl.ds(off, tk), :]
        vtb = vt_ref[0, :, pl.ds(off, tk)]
        for mi in range(2):
            s = _dot(kb, qx_sc[mi])
            if online:
                m_prev = m_sc[mi]
                m_new = jnp.maximum(m_prev, jnp.max(s, axis=0, keepdims=True))
                alpha = jnp.exp2(m_prev - m_new)
                p = jnp.exp2(s - m_new)
                l_sc[mi] = alpha * l_sc[mi] + p.reshape(tk // 8, 8, tq).sum(axis=0)
                acc_sc[mi] = alpha * acc_sc[mi] + _dot(vtb, p.astype(BF16))
                m_sc[mi] = m_new
            else:
                p = jnp.exp2(s - shift_ref[0])
                l_sc[mi] += p.reshape(tk // 8, 8, tq).sum(axis=0)
                acc_sc[mi] += _dot(vtb, p.astype(BF16))
        return carry

    lax.fori_loop(0, nk, body, 0)

    lv = lam_ref[...]
    lam = (jnp.exp(jnp.sum(lv[0:1] * lv[1:2], axis=-1, keepdims=True))
           - jnp.exp(jnp.sum(lv[2:3] * lv[3:4], axis=-1, keepdims=True)) + lam_init)
    l0 = jnp.sum(l_sc[0], axis=0, keepdims=True)
    l1 = jnp.sum(l_sc[1], axis=0, keepdims=True)
    o = acc_sc[0] / l0 - lam * (acc_sc[1] / l1)
    ms = jnp.mean(o * o, axis=0, keepdims=True)
    o_ref[0] = o * lax.rsqrt(ms + EPS) * g_ref[...] * (1.0 - lam_init)


def _diff_call(shift, qt, k, vt, lam_params, subln_g, lam_init, tq, tk, online):
    B, W, Tq = qt.shape
    Tk = k.shape[1]
    dv = W // DIFF_HEADS
    return pl.pallas_call(
        functools.partial(_diff_kernel, lam_init=lam_init, tk=tk, nk=Tk // tk, online=online),
        grid=(B, DIFF_HEADS, Tq // tq),
        in_specs=[
            pl.BlockSpec(memory_space=pltpu.SMEM),
            pl.BlockSpec((4, DIFF_HEAD_DIM), lambda b, h, i: (0, 0)),
            pl.BlockSpec((dv, 1), lambda b, h, i: (0, 0)),
            pl.BlockSpec((1, W, tq), lambda b, h, i: (b, 0, i)),
            pl.BlockSpec((1, Tk, W), lambda b, h, i: (b, 0, 0)),
            pl.BlockSpec((1, dv, Tk), lambda b, h, i: (b, h, 0)),
        ],
        out_specs=pl.BlockSpec((1, dv, tq), lambda b, h, i: (b, h, i)),
        out_shape=jax.ShapeDtypeStruct((B, W, Tq), F32),
        scratch_shapes=[pltpu.VMEM((2, W, tq), BF16), pltpu.VMEM((2, 1, tq), F32),
                        pltpu.VMEM((2, 8, tq), F32), pltpu.VMEM((2, dv, tq), F32)],
        compiler_params=_params("parallel", "parallel", "parallel"),
        name="diff_attn_online" if online else "diff_attn_fixed",
    )(shift, lam_params, subln_g, qt, k, vt)


def _diff_attn(qt, dfk, vt, gq, gk, lam_params, subln_g, lam_init, tq, tk):
    bound = (1.02 * math.log2(math.e) * DIFF_HEAD_DIM ** 0.5) * jnp.max(jnp.abs(gq)) * jnp.max(jnp.abs(gk))
    shift = bound.astype(F32).reshape(1)
    args = (shift, qt, dfk, vt, lam_params, subln_g, lam_init, tq, tk)
    return lax.cond(bound <= DIFF_FIXED_SHIFT_MAX,
                    lambda: _diff_call(*args, online=False), lambda: _diff_call(*args, online=True))


POOL_HALO = 8


def _pool_mixer(u, prev, nxt, wp_ref, scale, n_total):
    tm = u.shape[0]
    i = pl.program_id(1)
    prev = jnp.where(i > 0, prev, 0.0)
    nxt = jnp.where(i < pl.num_programs(1) - 1, nxt, 0.0)
    ext = jnp.concatenate([prev, u, nxt], axis=0)
    sh = lambda d: ext[POOL_HALO + d:POOL_HALO + d + tm]
    t = i * tm + lax.broadcasted_iota(jnp.int32, (tm, 1), 0)
    lane_grp = lax.broadcasted_iota(jnp.int32, (1, POOL_W), 1) // POOL_GROUP_W
    win = None
    mean = None
    prev_half = 0
    for gi, w in enumerate(POOL_WINDOWS):
        half = w // 2
        for d in range(prev_half, half):
            add = sh(d) + sh(-d - 1)
            win = add if win is None else win + add
        prev_half = half
        cnt = (jnp.minimum(t + half, n_total) - jnp.maximum(t - half, 0)).astype(F32)
        m_w = win / cnt
        mean = m_w if mean is None else jnp.where(lane_grp >= gi, m_w, mean)
    return _dot((mean - u).astype(BF16), wp_ref[...]) * scale


def _merge_kernel(x_ref, gt_ref, yna_ref, ydf_ref, u_ref, up_ref, un_ref, f_ref, gate_ref, wbr_ref, wp_ref,
                  ps_ref, wf_ref, wo_ref, o_ref, *, n_total):
    yf = _dot(f_ref[0].astype(BF16), wf_ref[...])
    ypool = _pool_mixer(u_ref[0], up_ref[0], un_ref[0], wp_ref, ps_ref[...], n_total)
    ys = (yna_ref[0], ydf_ref[0].T, ypool, yf)
    acc = None
    for i, y in enumerate(ys):
        g = gate_ref[0, :, i * D_MODEL:(i + 1) * D_MODEL].astype(F32)
        term = g * _dot(y.astype(BF16), wbr_ref[i])
        acc = term if acc is None else acc + term
    o_ref[0] = x_ref[0] + gt_ref[0] * _dot(acc.astype(BF16), wo_ref[...])


def _merge(x, gt, y_na, y_df, pool_in, f_real, gate, w_br, w_pool_bd, pool_scale, w_f, w_o, tm):
    B, T, D = x.shape
    assert tm % POOL_HALO == 0 and POOL_HALO >= max(POOL_WINDOWS) // 2
    hb = tm // POOL_HALO
    row = lambda b, i: (b, i, 0)
    const2 = lambda b, i: (0, 0)
    return pl.pallas_call(
        functools.partial(_merge_kernel, n_total=T), grid=(B, T // tm),
        in_specs=[
            pl.BlockSpec((1, tm, D), row),
            pl.BlockSpec((1, 1, D), lambda b, i: (b, 0, 0)),
            pl.BlockSpec((1, tm, BRANCH_W), row), pl.BlockSpec((1, BRANCH_W, tm), lambda b, i: (b, 0, i)),
            pl.BlockSpec((1, tm, POOL_W), row),
            pl.BlockSpec((1, POOL_HALO, POOL_W), lambda b, i: (b, jnp.maximum(i * hb - 1, 0), 0)),
            pl.BlockSpec((1, POOL_HALO, POOL_W), lambda b, i: (b, jnp.minimum((i + 1) * hb, T // POOL_HALO - 1), 0)),
            pl.BlockSpec((1, tm, FNET_W), row),
            pl.BlockSpec((1, tm, N_BRANCH * D), row),
            pl.BlockSpec((N_BRANCH, BRANCH_W, D), lambda b, i: (0, 0, 0)),
            pl.BlockSpec((POOL_W, POOL_W), const2), pl.BlockSpec((1, POOL_W), const2),
            pl.BlockSpec((FNET_W, FNET_W), const2),
            pl.BlockSpec((D, D), const2),
        ],
        out_specs=pl.BlockSpec((1, tm, D), row),
        out_shape=jax.ShapeDtypeStruct((B, T, D), F32),
        compiler_params=_params("parallel", "parallel"), name="merge",
    )(x, gt, y_na, y_df, pool_in, pool_in, pool_in, f_real, gate, w_br, w_pool_bd, pool_scale, w_f, w_o)


TOK_BLOCK = LANES
SEL_GROUP = 8
COMBINE_WIN = 48
BF16_ROWS = 16


def _router_kernel(x_ref, g_ref, sh_ref, sc_ref, wrh_ref, wrl_ref, h_ref, aff_ref):
    x = x_ref[0]
    ms = jnp.mean(x * x, axis=-1, keepdims=True)
    h = x * lax.rsqrt(ms + EPS) * g_ref[...] * (1.0 + sc_ref[0]) + sh_ref[0]
    h_ref[0] = h.astype(BF16)
    hi, lo = _split_bf16(h)
    logits = _dot_nt(wrh_ref[...], hi) + _dot_nt(wrh_ref[...], lo) + _dot_nt(wrl_ref[...], hi)
    logits = logits - jnp.max(logits, axis=0, keepdims=True)
    e = jnp.exp(logits)
    aff_ref[0] = e / jnp.sum(e, axis=0, keepdims=True)


def _router(x, g, sh, sc, w_router, tm):
    B, T, D = x.shape
    E = w_router.shape[1]
    wrh, wrl = _split_bf16(w_router.T)
    row = lambda b, i: (b, i, 0)
    return pl.pallas_call(
        _router_kernel, grid=(B, T // tm),
        in_specs=[
            pl.BlockSpec((1, tm, D), row), pl.BlockSpec((1, D), lambda b, i: (0, 0)),
            pl.BlockSpec((1, 1, D), lambda b, i: (b, 0, 0)), pl.BlockSpec((1, 1, D), lambda b, i: (b, 0, 0)),
            pl.BlockSpec((E, D), lambda b, i: (0, 0)), pl.BlockSpec((E, D), lambda b, i: (0, 0)),
        ],
        out_specs=[pl.BlockSpec((1, tm, D), row), pl.BlockSpec((1, E, tm), lambda b, i: (b, 0, i))],
        out_shape=[jax.ShapeDtypeStruct((B, T, D), BF16), jax.ShapeDtypeStruct((B, E, T), F32)],
        compiler_params=_params("parallel", "parallel"), name="router",
    )(x, g, sh, sc, wrh, wrl)


def _select_kernel(aff_ref, u_ref, ls_ref, idx_ref, gate_ref, pos_ref, off_ref, tot_ref, *, cap):
    G, nb, _ = aff_ref.shape
    aff = aff_ref[...]
    bits = pltpu.bitcast(aff, jnp.int32)
    kf = float(cap)
    ones_f = lambda m: jnp.where(m, 1.0, 0.0)
    count = lambda m: jnp.sum(ones_f(m), axis=(1, 2), keepdims=True)

    def search(it, lo):
        cand = lo | jnp.left_shift(jnp.int32(1), 30 - it)
        return jnp.where(count(bits >= cand) >= kf, cand, lo)

    thr = lax.fori_loop(0, 31, search, jnp.zeros((G, 1, 1), jnp.int32))
    need = kf - count(bits > thr)

    u = u_ref[...]
    ls = ls_ref[...]
    ones_m = jnp.ones((LANES, LANES), BF16)

    def prefix(mf):
        mb = mf.astype(BF16)
        tot = _dot(mb, ones_m)
        off = _dot(ls, tot.astype(BF16))
        return _dot(mb, u) + off, off, tot

    blk_i = lax.broadcasted_iota(jnp.int32, (nb, LANES), 0)
    lane_i = lax.broadcasted_iota(jnp.int32, (nb, LANES), 1)
    s_row = lax.broadcasted_iota(jnp.int32, (1, cap), 1).astype(F32)
    col_nb = lax.broadcasted_iota(jnp.int32, (nb, 1), 0).astype(F32)
    col_l = lax.broadcasted_iota(jnp.int32, (LANES, 1), 0).astype(F32)
    for g in range(G):
        gt = bits[g] > thr[g]
        eq = bits[g] == thr[g]
        eqf = ones_f(eq)
        tie_before = prefix(eqf)[0] - eqf
        sel = gt | (eq & (tie_before < need[g]))
        pin, off, tot = prefix(ones_f(sel))
        pos_ref[g] = jnp.where(sel, pin - 1.0, -1.0).astype(jnp.int32)
        diag = blk_i == lane_i
        off_ref[g] = jnp.sum(jnp.where(diag, off, 0.0), axis=0, keepdims=True).astype(jnp.int32)
        tot_ref[g] = jnp.sum(jnp.where(diag, tot, 0.0), axis=0, keepdims=True).astype(jnp.int32)
        blk = jnp.sum(ones_f((off + tot)[:, 0:1] <= s_row), axis=0, keepdims=True)
        oh = jnp.where(col_nb == blk, 1.0, 0.0).astype(BF16)
        pin_t = pin.T
        p_hi = jnp.floor(pin_t * (1.0 / LANES))
        p_lo = pin_t - p_hi * LANES
        rows = _dot(p_hi.astype(BF16), oh) * LANES + _dot(p_lo.astype(BF16), oh)
        lane = jnp.sum(ones_f(rows <= s_row), axis=0, keepdims=True)
        idx_ref[g] = (blk * LANES + lane).astype(jnp.int32)
        a_t = aff[g].T
        a1 = a_t.astype(BF16)
        r1 = a_t - a1.astype(F32)
        a2 = r1.astype(BF16)
        a3 = (r1 - a2.astype(F32)).astype(BF16)
        arow = _dot(a1, oh) + _dot(a2, oh) + _dot(a3, oh)
        gate_ref[g] = jnp.sum(jnp.where(col_l == lane, arow, 0.0), axis=0, keepdims=True)


def _select(aff, cap):
    B, E, T = aff.shape
    nb = T // TOK_BLOCK
    R = B * E
    G = SEL_GROUP
    i = np.arange(LANES)
    u = jnp.asarray(i[:, None] <= i[None, :], BF16)
    j = np.arange(nb)
    ls = jnp.asarray(j[None, :] < j[:, None], BF16)
    grp = lambda r: (r, 0, 0)
    outs = pl.pallas_call(
        functools.partial(_select_kernel, cap=cap), grid=(R // G,),
        in_specs=[pl.BlockSpec((G, nb, LANES), grp), pl.BlockSpec((LANES, LANES), lambda r: (0, 0)),
                  pl.BlockSpec((nb, nb), lambda r: (0, 0))],
        out_specs=[pl.BlockSpec((G, 1, cap), grp), pl.BlockSpec((G, 1, cap), grp),
                   pl.BlockSpec((G, nb, LANES), grp), pl.BlockSpec((G, 1, LANES), grp),
                   pl.BlockSpec((G, 1, LANES), grp)],
        out_shape=[jax.ShapeDtypeStruct((R, 1, cap), jnp.int32), jax.ShapeDtypeStruct((R, 1, cap), F32),
                   jax.ShapeDtypeStruct((R, nb, LANES), jnp.int32), jax.ShapeDtypeStruct((R, 1, LANES), jnp.int32),
                   jax.ShapeDtypeStruct((R, 1, LANES), jnp.int32)],
        compiler_params=_params("parallel"), name="select",
    )(aff.reshape(R, nb, LANES), u, ls)
    idx, gate, pos, off, tot = outs
    return (idx.reshape(B, E, cap), gate.reshape(B, E, cap), pos.reshape(B, E, T),
            off[:, 0, :nb].reshape(B, E, nb), tot[:, 0, :nb].reshape(B, E, nb))


def _combine_kernel(off_ref, tot_ref, x_ref, gt_ref, pos_ref, ex_ref, ye_hbm, o_ref, buf, xbuf, acc_sc, sem,
                    xsem, *, nt, cap, W):
    n = pl.program_id(0)
    E = N_EXPERTS

    def window(step, e):
        b = step // nt
        o = off_ref[(b * E + e) * nt + step % nt]
        return b, o, jnp.minimum((o // BF16_ROWS) * BF16_ROWS, cap - W)

    def fetch(step, slot):
        for e in range(E):
            b, _, st = window(step, e)
            pltpu.make_async_copy(ye_hbm.at[b, e, pl.ds(pl.multiple_of(st, BF16_ROWS), W)], buf.at[slot, e],
                                  sem.at[slot]).start()

    @pl.when(n == 0)
    def _():
        fetch(0, 0)

    @pl.when(n + 1 < pl.num_programs(0))
    def _():
        fetch(n + 1, (n + 1) % 2)

    slot = n % 2
    for e in range(E):
        pltpu.make_async_copy(ye_hbm.at[0, 0, pl.ds(0, W)], buf.at[slot, e], sem.at[slot]).wait()

    pos = pos_ref[0].astype(F32).T
    p_hi = jnp.floor(pos * (1.0 / 64.0))
    p_lo = pos - 64.0 * p_hi
    ex = ex_ref[...]
    posx = 64.0 * _dot(p_hi.astype(BF16), ex) + _dot(p_lo.astype(BF16), ex)
    col = lax.broadcasted_iota(jnp.int32, (1, E * W), 1)
    tgt = jnp.zeros((1, E * W), jnp.int32)
    for e in range(E):
        _, _, st = window(n, e)
        tgt = jnp.where(col // W == e, st + col % W, tgt)
    onehot = jnp.where(posx == tgt.astype(F32), 1.0, 0.0).astype(BF16)
    acc_sc[...] = _dot(onehot, buf[slot].reshape(E * W, buf.shape[-1]))

    w_i = lax.broadcasted_iota(jnp.int32, (1, W), 1)
    for e in range(E):
        b, o, st = window(n, e)
        t = tot_ref[(b * E + e) * nt + n % nt]
        n_extra = jnp.maximum(o + t - (st + W) + (W - 1), 0) // W

        def extra(j, carry, e=e, b=b, st=st):
            base = st + W * (j + 1)
            src = jnp.minimum(base, cap - W)
            cp = pltpu.make_async_copy(ye_hbm.at[b, e, pl.ds(pl.multiple_of(src, BF16_ROWS), W)], xbuf,
                                       xsem.at[0])
            cp.start()
            cp.wait()
            slot_i = (src + w_i).astype(F32)
            oh = jnp.where((pos[:, e:e + 1] == slot_i) & (slot_i >= base.astype(F32)), 1.0, 0.0).astype(BF16)
            acc_sc[...] += _dot(oh, xbuf[...])
            return carry

        lax.fori_loop(0, n_extra, extra, 0)

    o_ref[0] = x_ref[0] + gt_ref[0] * acc_sc[...]


def _combine(x, gt, pos, off, tot, ye):
    B, T, D = x.shape
    E, cap = ye.shape[1], ye.shape[2]
    nt = T // TOK_BLOCK
    W = min(COMBINE_WIN, cap)
    assert W % BF16_ROWS == 0 and (cap - W) % BF16_ROWS == 0 and (E * W) % LANES == 0
    expand = jnp.asarray(np.arange(E)[:, None] == (np.arange(E * W)[None, :] // W), BF16)
    tile = lambda n, off_r, tot_r: (n // nt, n % nt, 0)
    grid_spec = pltpu.PrefetchScalarGridSpec(
        num_scalar_prefetch=2, grid=(B * nt,),
        in_specs=[
            pl.BlockSpec((1, TOK_BLOCK, D), tile),
            pl.BlockSpec((1, 1, D), lambda n, off_r, tot_r: (n // nt, 0, 0)),
            pl.BlockSpec((1, E, TOK_BLOCK), lambda n, off_r, tot_r: (n // nt, 0, n % nt)),
            pl.BlockSpec((E, E * W), lambda n, off_r, tot_r: (0, 0)),
            pl.BlockSpec(memory_space=pl.ANY),
        ],
        out_specs=pl.BlockSpec((1, TOK_BLOCK, D), tile),
        scratch_shapes=[pltpu.VMEM((2, E, W, D), BF16), pltpu.VMEM((W, D), BF16),
                        pltpu.VMEM((TOK_BLOCK, D), F32),
                        pltpu.SemaphoreType.DMA((2,)), pltpu.SemaphoreType.DMA((1,))],
    )
    return pl.pallas_call(
        functools.partial(_combine_kernel, nt=nt, cap=cap, W=W), grid_spec=grid_spec,
        out_shape=jax.ShapeDtypeStruct((B, T, D), F32),
        compiler_params=_params("arbitrary"), name="combine",
    )(off.reshape(-1), tot.reshape(-1), x, gt, pos, expand, ye)


FF_CHUNKS = ((0, 512), (512, 512), (1024, 384))


def _expert_kernel(xe_ref, g_ref, wg_ref, wu_ref, wd_ref, o_ref):
    xe = xe_ref[0, 0]
    acc = None
    for lo, n in FF_CHUNKS:
        a = _dot(xe, wg_ref[0, 0, :, lo:lo + n].astype(BF16))
        u = _dot(xe, wu_ref[0, 0, :, lo:lo + n].astype(BF16))
        hid = (a * jax.nn.sigmoid(a) * u).astype(BF16)
        part = _dot(hid, wd_ref[0, 0, lo:lo + n, :].astype(BF16))
        acc = part if acc is None else acc + part
    o_ref[0, 0] = (acc * g_ref[0, 0]).astype(o_ref.dtype)


def _experts(xe, g, wg, wu, wd, layer, tm):
    B, E, cap, D = xe.shape
    F = wg.shape[-1]
    assert F == EXPERT_FF
    tile = lambda e, b, i: (b, e, i, 0)
    wsel = lambda e, b, i: (layer, e, 0, 0)
    return pl.pallas_call(
        _expert_kernel, grid=(E, B, cap // tm),
        in_specs=[
            pl.BlockSpec((1, 1, tm, D), tile), pl.BlockSpec((1, 1, tm, 1), tile),
            pl.BlockSpec((1, 1, D, F), wsel), pl.BlockSpec((1, 1, D, F), wsel), pl.BlockSpec((1, 1, F, D), wsel),
        ],
        out_specs=pl.BlockSpec((1, 1, tm, D), tile),
        out_shape=jax.ShapeDtypeStruct((B, E, cap, D), BF16),
        compiler_params=_params("parallel", "parallel", "parallel"), name="experts",
    )(xe, g, wg, wu, wd)


SEL_TOKENS = TOK_BLOCK * LANES


def _ec_ffn(x, gt, g, sh, sc, w_router, wg, wu, wd, layer, tm_router, tm_expert):
    B, N, D = x.shape
    cap = max(1, EC_FACTOR * N // N_EXPERTS)
    assert N <= SEL_TOKENS and N % TOK_BLOCK == 0
    h, aff = _router(x, g, sh, sc, w_router, tm_router)
    aff = jnp.pad(aff, ((0, 0), (0, 0), (0, SEL_TOKENS - N)))
    idx, gsel, pos, off, tot = _select(aff, cap)
    nt = N // TOK_BLOCK
    xe = h[jnp.arange(B)[:, None, None], idx]
    ye = _experts(xe, gsel[..., None], wg, wu, wd, layer, min(tm_expert, cap))
    return _combine(x, gt, pos[:, :, :N], off[:, :, :nt], tot[:, :, :nt], ye)


DFT_RADIX = 128


def _dot3(a_hi, a_lo, b_hi, b_lo):
    return _dot(a_hi, b_hi) + _dot(a_lo, b_hi) + _dot(a_hi, b_lo)


def _cos_sin(n_rows, n_cols, period):
    k = (np.arange(n_rows)[:, None] * np.arange(n_cols)[None, :]) % period
    ang = 2.0 * np.pi * k.astype(np.float64) / period
    return np.cos(ang), np.sin(ang)


def _const_split(m):
    return _split_bf16(jnp.asarray(m, F32))


def _chan_dft_kernel(u_ref, mh_ref, ml_ref, xr_ref, xi_ref):
    uh, ul = _split_bf16(u_ref[0])
    y = _dot3(uh, ul, mh_ref[...], ml_ref[...])
    c = xr_ref.shape[-1]
    xr_ref[0] = y[:, :c]
    xi_ref[0] = y[:, c:]


def _chan_dft(u, tm):
    B, N, C = u.shape
    cc, sc = _cos_sin(C, C, C)
    mh, ml = _const_split(np.concatenate([cc, -sc], axis=1))
    row = lambda b, i: (b, i, 0)
    const2 = lambda b, i: (0, 0)
    return pl.pallas_call(
        _chan_dft_kernel, grid=(B, N // tm),
        in_specs=[pl.BlockSpec((1, tm, C), row), pl.BlockSpec((C, 2 * C), const2), pl.BlockSpec((C, 2 * C), const2)],
        out_specs=[pl.BlockSpec((1, tm, C), row)] * 2,
        out_shape=[jax.ShapeDtypeStruct((B, N, C), F32)] * 2,
        compiler_params=_params("parallel", "parallel"), name="chan_dft",
    )(u, mh, ml)


def _dft_left_kernel(xr_ref, xi_ref, mh_ref, ml_ref, o_ref, *, scale):
    xh, xl = _split_bf16(jnp.concatenate([xr_ref[0], xi_ref[0]], axis=0))
    o_ref[0] = _dot3(mh_ref[...], ml_ref[...], xh, xl) * scale


def _dft_left(xr, xi, m, scale, tn):
    B, K, cols = xr.shape
    R = m.shape[0]
    mh, ml = _const_split(m)
    col = lambda b, j: (b, 0, j)
    const2 = lambda b, j: (0, 0)
    return pl.pallas_call(
        functools.partial(_dft_left_kernel, scale=scale), grid=(B, cols // tn),
        in_specs=[pl.BlockSpec((1, K, tn), col), pl.BlockSpec((1, K, tn), col),
                  pl.BlockSpec((R, 2 * K), const2), pl.BlockSpec((R, 2 * K), const2)],
        out_specs=pl.BlockSpec((1, R, tn), col),
        out_shape=jax.ShapeDtypeStruct((B, R, cols), F32),
        compiler_params=_params("parallel", "parallel"), name="dft_left",
    )(xr, xi, mh, ml)


DFT_K1_PER_STEP = 8


def _dft_twiddle_kernel(y_ref, tc_ref, ts_ref, dh_ref, dl_ref, o_ref, *, scale):
    tc = tc_ref[0]
    ts = ts_ref[0]
    for kk in range(DFT_K1_PER_STEP):
        yr = y_ref[0, 0, kk]
        yi = y_ref[0, 1, kk]
        c, s = tc[:, kk:kk + 1], ts[:, kk:kk + 1]
        zh, zl = _split_bf16(jnp.concatenate([yr * c + yi * s, yi * c - yr * s], axis=0))
        o_ref[0, :, kk, :] = _dot3(dh_ref[...], dl_ref[...], zh, zl) * scale


def _fourier_real(u):
    B, N, C = u.shape
    scale = 1.0 / math.sqrt(N * C)
    xr, xi = _chan_dft(u, min(N, 512))
    if N % (DFT_RADIX * DFT_K1_PER_STEP) != 0:
        assert N <= 1024
        cn, sn = _cos_sin(N, N, N)
        return _dft_left(xr, xi, np.concatenate([cn, sn], axis=1), scale, C)
    n1, n2 = N // DFT_RADIX, DFT_RADIX
    c1, s1 = _cos_sin(n1, n1, n1)
    m1 = np.block([[c1, s1], [-s1, c1]])
    cols = n2 * C
    yy = _dft_left(xr.reshape(B, n1, cols), xi.reshape(B, n1, cols), m1, 1.0, min(cols, 2048))
    yy = yy.reshape(B, 2, n1, n2, C)
    tcos, tsin = _cos_sin(n1, n2, N)
    grp = lambda t: jnp.asarray(t.reshape(n1 // DFT_K1_PER_STEP, DFT_K1_PER_STEP, n2).transpose(0, 2, 1), F32)
    c2, s2 = _cos_sin(n2, n2, n2)
    dh, dl = _const_split(np.concatenate([c2, s2], axis=1))
    fp = pl.pallas_call(
        functools.partial(_dft_twiddle_kernel, scale=scale), grid=(B, n1 // DFT_K1_PER_STEP),
        in_specs=[
            pl.BlockSpec((1, 2, DFT_K1_PER_STEP, n2, C), lambda b, g: (b, 0, g, 0, 0)),
            pl.BlockSpec((1, n2, DFT_K1_PER_STEP), lambda b, g: (g, 0, 0)),
            pl.BlockSpec((1, n2, DFT_K1_PER_STEP), lambda b, g: (g, 0, 0)),
            pl.BlockSpec((n2, 2 * n2), lambda b, g: (0, 0)), pl.BlockSpec((n2, 2 * n2), lambda b, g: (0, 0)),
        ],
        out_specs=pl.BlockSpec((1, n2, DFT_K1_PER_STEP, C), lambda b, g: (b, 0, g, 0)),
        out_shape=jax.ShapeDtypeStruct((B, n2, n1, C), F32),
        compiler_params=_params("parallel", "parallel"), name="dft_twiddle",
    )(yy, grp(tcos), grp(tsin), dh, dl)
    return fp.reshape(B, N, C)


def _ctx_attn_kernel(q_ref, kv_ref, o_ref):
    q = q_ref[0]
    kv = kv_ref[0]
    for h in range(NA_HEADS):
        ks = slice(h * NA_HEAD_DIM, (h + 1) * NA_HEAD_DIM)
        vs = slice(NA_W + h * NA_HEAD_DIM, NA_W + (h + 1) * NA_HEAD_DIM)
        s = _dot_nt(q[:, ks], kv[:, ks])
        p = jnp.exp(s - jnp.max(s, axis=-1, keepdims=True))
        l = jnp.sum(p, axis=-1, keepdims=True)
        o_ref[0, :, ks] = _dot(p.astype(BF16), kv[:, vs]) / l


def _ctx_dense_attn(q, kv):
    B, Q, _ = q.shape
    return pl.pallas_call(
        _ctx_attn_kernel, grid=(B,),
        in_specs=[pl.BlockSpec((1, Q, NA_W), lambda b: (b, 0, 0)),
                  pl.BlockSpec((1, kv.shape[1], 2 * NA_W), lambda b: (b, 0, 0))],
        out_specs=pl.BlockSpec((1, Q, NA_W), lambda b: (b, 0, 0)),
        out_shape=jax.ShapeDtypeStruct((B, Q, NA_W), F32),
        compiler_params=_params("parallel"), name="ctx_attn",
    )(q, kv)


ADA_ROWS = 8
ADA_TN = 512


def _ada_kernel(c_ref, w_ref, b_ref, o_ref):
    c = c_ref[...]
    sh, sl = _split_bf16(c * jax.nn.sigmoid(c))
    wh, wl = _split_bf16(w_ref[0])
    o_ref[...] = _dot3(sh, sl, wh, wl) + b_ref[0]


def _ada_mod(c_rows, w_ada, b_ada, layer):
    R, D = c_rows.shape
    N = w_ada.shape[-1]
    return pl.pallas_call(
        _ada_kernel, grid=(N // ADA_TN,),
        in_specs=[pl.BlockSpec((R, D), lambda j: (0, 0)),
                  pl.BlockSpec((1, D, ADA_TN), lambda j: (layer, 0, j)),
                  pl.BlockSpec((1, 1, ADA_TN), lambda j: (layer, 0, j))],
        out_specs=pl.BlockSpec((R, ADA_TN), lambda j: (0, j)),
        out_shape=jax.ShapeDtypeStruct((R, N), F32),
        compiler_params=_params("parallel"), name="ada_mod",
    )(c_rows, w_ada, b_ada[:, None, :])


def _tile(g, n):
    return jnp.tile(g.astype(F32), n)[None, :]


def kernel(x, c, ctx, c_ctx, w_ada, b_ada, g_mix, g_ffn, w_in, na_q_g, na_k_g, na_rpb, df_q_g, df_k_g,
           df_lambda, df_subln_g, pool_w, pool_scale, fnet_w, w_branch, w_out, w_router, w_gate_e, w_up_e,
           w_down_e):
    B, T, D = x.shape
    ctx_len = ctx.shape[1]
    rope = _rope_tables(T)
    assert B + 1 <= ADA_ROWS
    c_rows = jnp.concatenate([c, c_ctx[None, :], jnp.zeros((ADA_ROWS - B - 1, D), F32)], axis=0)
    tm = 256
    tq, tk = min(2048, T), 1280
    for l in range(DEPTH):
        last = l == DEPTH - 1
        lam_init = 0.8 - 0.6 * math.exp(-0.3 * l)
        mods = _ada_mod(c_rows, w_ada, b_ada, l)
        sh1, sc1, gt1, sh2, sc2, gt2 = [m[:, None, :] for m in jnp.split(mods[:B], 6, axis=-1)]
        bc = lambda m: jnp.broadcast_to(m[None, None, :], (B, 1, D))
        csh1, csc1, cgt1, csh2, csc2, cgt2 = [bc(m) for m in jnp.split(mods[B], 6, axis=-1)]

        w_bf = w_in[l].astype(BF16)
        gq, gk = _tile(na_q_g[l], NA_HEADS), _tile(na_k_g[l], NA_HEADS)
        gdq, gdk = _tile(df_q_g[l], 2 * DIFF_HEADS), _tile(df_k_g[l], 2 * DIFF_HEADS)
        gmix = g_mix[l][None, :]
        gffn = g_ffn[l][None, :]
        subg = df_subln_g[l][:, None].astype(F32)
        wbr = w_branch[l].astype(BF16)
        wf = fnet_w[l].astype(BF16)
        wpool = jax.scipy.linalg.block_diag(*[pool_w[l, gi] for gi in range(len(POOL_WINDOWS))]).astype(BF16)
        pscale = pool_scale[l][None, :].astype(F32)
        wo = w_out[l].astype(BF16)

        naq, dfq, pool_in, fnet_in, gate, nakv, dfk, dfv = _inproj(
            x, gmix, sh1, sc1, w_bf, gq, gk, gdq, gdk, rope, 2 * tm)
        (cnaq, cdfq, cpool_in, cfnet_in, cgate, cnakv, cdfk, cdfv) = _inproj(
            ctx, gmix, csh1, csc1, w_bf, gq, gk, gdq, gdk, None, ctx_len)

        y_na = _na_latent(naq, nakv, cnakv, na_rpb[l])
        lamp = df_lambda[l].astype(F32)
        y_df = _diff_attn(dfq, jnp.concatenate([dfk, cdfk], axis=1), jnp.concatenate([dfv, cdfv], axis=2),
                          df_q_g[l], df_k_g[l], lamp, subg, lam_init, tq, tk)
        f_real = _fourier_real(fnet_in)
        x_new = _merge(x, gt1, y_na, y_df, pool_in, f_real, gate, wbr, wpool, pscale, wf, wo, tm)

        if not last:
            yc_na = _ctx_dense_attn(cnaq, cnakv)
            yc_df = _diff_attn(cdfq, cdfk, cdfv, df_q_g[l], df_k_g[l], lamp, subg, lam_init, ctx_len, ctx_len)
            fc_real = _fourier_real(cfnet_in)
            ctx_new = _merge(ctx, cgt1, yc_na, yc_df, cpool_in, fc_real, cgate, wbr, wpool, pscale, wf, wo,
                             ctx_len)

        x = x_new
        x = _ec_ffn(x, gt2, gffn, sh2, sc2, w_router[l], w_gate_e, w_up_e, w_down_e, l, tm, 512)
        if not last:
            ctx = ctx_new
            ctx = _ec_ffn(ctx, cgt2, gffn, csh2, csc2, w_router[l], w_gate_e, w_up_e, w_down_e, l, ctx_len, 512)
    return x
```

```python
import functools
import math

import jax
import jax.numpy as jnp
import numpy as np
from jax import lax
from jax.experimental import pallas as pl
from jax.experimental.pallas import tpu as pltpu

F32 = jnp.float32
BF16 = jnp.bfloat16

D_MODEL = 1024
DEPTH = 2
GRID_W = 64
EPS = 1e-6
ROPE_BASE = 10000.0

NA_HEADS = 4
NA_HEAD_DIM = 64
NA_WIN_R = 8
NA_WIN_C = 16
NA_W = NA_HEADS * NA_HEAD_DIM

DIFF_HEADS = 4
DIFF_HEAD_DIM = 32
DIFF_QK_W = DIFF_HEADS * 2 * DIFF_HEAD_DIM
DIFF_V_W = DIFF_HEADS * 2 * DIFF_HEAD_DIM

POOL_WINDOWS = (2, 4, 8, 16)
POOL_GROUP_W = 64
POOL_W = len(POOL_WINDOWS) * POOL_GROUP_W
FNET_W = 256
N_BRANCH = 4
BRANCH_W = 256

OFF_NA_Q = 0
OFF_DF_Q = OFF_NA_Q + NA_W
OFF_POOL = OFF_DF_Q + DIFF_QK_W
OFF_FNET = OFF_POOL + POOL_W
OFF_GATE = OFF_FNET + FNET_W
OFF_KV = OFF_GATE + N_BRANCH * D_MODEL
KV_W = 2 * NA_W + DIFF_QK_W + DIFF_V_W
IN_COLS = OFF_KV + KV_W

N_EXPERTS = 16
EC_FACTOR = 2
EXPERT_FF = 1408

VMEM_LIMIT_BYTES = 56 * 1024 * 1024
LANES = 128
NEG_BIG = -1e30


def _params(*sem):
    return pltpu.CompilerParams(dimension_semantics=sem, vmem_limit_bytes=VMEM_LIMIT_BYTES)


def _split_bf16(a):
    hi = a.astype(BF16)
    lo = (a - hi.astype(F32)).astype(BF16)
    return hi, lo


def _dot(a, b):
    return jnp.dot(a, b, preferred_element_type=F32)


def _dot_nt(a, b):
    return lax.dot_general(a, b, (((1,), (1,)), ((), ())), preferred_element_type=F32)


def _group_rmsnorm(p, bd_ref, g):
    hi, lo = _split_bf16(p * p)
    ms = _dot(hi, bd_ref[...]) + _dot(lo, bd_ref[...])
    return p * lax.rsqrt(ms + EPS) * g


def _rope256(y, cos, s_next, s_prev):
    outs = []
    for half in range(2):
        z = y[:, half * LANES:(half + 1) * LANES]
        outs.append(z * cos + pltpu.roll(z, LANES - 8, 1) * s_next + pltpu.roll(z, 8, 1) * s_prev)
    return jnp.concatenate(outs, axis=1)


def _inproj_kernel(*refs, use_rope):
    if use_rope:
        (x_ref, g_ref, sh_ref, sc_ref, w_ref, gq_ref, gk_ref, gdq_ref, gdk_ref, bd64_ref, bd32_ref,
         cos_ref, sn_ref, sp_ref,
         naq_ref, dfq_ref, pool_ref, fnet_ref, gate_ref, nakv_ref, dfk_ref, dfv_ref) = refs
    else:
        (x_ref, g_ref, sh_ref, sc_ref, w_ref, gq_ref, gk_ref, gdq_ref, gdk_ref, bd64_ref, bd32_ref,
         naq_ref, dfq_ref, pool_ref, fnet_ref, gate_ref, nakv_ref, dfk_ref, dfv_ref) = refs
    x = x_ref[0]
    ms = jnp.mean(x * x, axis=-1, keepdims=True)
    y = x * lax.rsqrt(ms + EPS) * g_ref[...]
    h = (y * (1.0 + sc_ref[0]) + sh_ref[0]).astype(BF16)

    def seg(lo, n):
        return _dot(h, w_ref[:, lo:lo + n])

    def rope(v):
        if not use_rope:
            return v
        return _rope256(v, cos_ref[...], sn_ref[...], sp_ref[...])

    naq = _group_rmsnorm(seg(OFF_NA_Q, NA_W), bd64_ref, gq_ref[...])
    naq_ref[0] = (naq * (NA_HEAD_DIM ** -0.5)).astype(BF16)
    dfq = rope(_group_rmsnorm(seg(OFF_DF_Q, DIFF_QK_W), bd32_ref, gdq_ref[...]))
    dfq_ref[0] = (dfq * (math.log2(math.e) * DIFF_HEAD_DIM ** -0.5)).T.astype(BF16)
    pool_ref[0] = seg(OFF_POOL, POOL_W)
    fnet_ref[0] = seg(OFF_FNET, FNET_W)
    for j in range(0, N_BRANCH * D_MODEL, 512):
        gate_ref[0, :, j:j + 512] = jax.nn.sigmoid(seg(OFF_GATE + j, 512)).astype(BF16)
    nak = _group_rmsnorm(seg(OFF_KV, NA_W), bd64_ref, gk_ref[...])
    nakv_ref[0, :, 0:NA_W] = nak.astype(BF16)
    nakv_ref[0, :, NA_W:2 * NA_W] = seg(OFF_KV + NA_W, NA_W).astype(BF16)
    dfk = rope(_group_rmsnorm(seg(OFF_KV + 2 * NA_W, DIFF_QK_W), bd32_ref, gdk_ref[...]))
    dfk_ref[0] = dfk.astype(BF16)
    dfv_ref[0] = seg(OFF_KV + 2 * NA_W + DIFF_QK_W, DIFF_V_W).T.astype(BF16)


def _block_diag_mean(width, group):
    i = jnp.arange(width)
    return jnp.where((i[:, None] // group) == (i[None, :] // group), 1.0 / group, 0.0).astype(BF16)


def _rope_tables(T):
    t = jnp.arange(T)
    j = jnp.arange(LANES)
    jj = j % DIFF_HEAD_DIM
    quarter = DIFF_HEAD_DIM // 4
    use_row = jj < DIFF_HEAD_DIM // 2
    first = (jj % (DIFF_HEAD_DIM // 2)) < quarter
    inv = ROPE_BASE ** (-(jj % quarter).astype(F32) / quarter)
    pos = jnp.where(use_row[None, :], (t // GRID_W)[:, None], (t % GRID_W)[:, None]).astype(F32)
    ang = pos * inv[None, :]
    cos, sin = jnp.cos(ang), jnp.sin(ang)
    s_next = jnp.where(first[None, :], -sin, 0.0)
    s_prev = jnp.where(first[None, :], 0.0, sin)
    return cos, s_next, s_prev


def _inproj(x, g, sh, sc, w_bf16, gq, gk, gdq, gdk, rope, tm):
    B, T, D = x.shape
    use_rope = rope is not None
    row = lambda b, i: (b, i, 0)
    const2 = lambda b, i: (0, 0)
    perb = lambda b, i: (b, 0, 0)
    in_specs = [
        pl.BlockSpec((1, tm, D), row),
        pl.BlockSpec((1, D), const2),
        pl.BlockSpec((1, 1, D), perb),
        pl.BlockSpec((1, 1, D), perb),
        pl.BlockSpec((D, IN_COLS), const2),
        pl.BlockSpec((1, NA_W), const2), pl.BlockSpec((1, NA_W), const2),
        pl.BlockSpec((1, DIFF_QK_W), const2), pl.BlockSpec((1, DIFF_QK_W), const2),
        pl.BlockSpec((NA_W, NA_W), const2), pl.BlockSpec((DIFF_QK_W, DIFF_QK_W), const2),
    ]
    args = [x, g, sh, sc, w_bf16, gq, gk, gdq, gdk,
            _block_diag_mean(NA_W, NA_HEAD_DIM), _block_diag_mean(DIFF_QK_W, DIFF_HEAD_DIM)]
    if use_rope:
        in_specs += [pl.BlockSpec((tm, LANES), lambda b, i: (i, 0))] * 3
        args += list(rope)
    outs = [(NA_W, BF16, False), (DIFF_QK_W, BF16, True), (POOL_W, F32, False), (FNET_W, F32, False),
            (N_BRANCH * D_MODEL, BF16, False), (2 * NA_W, BF16, False), (DIFF_QK_W, BF16, False),
            (DIFF_V_W, BF16, True)]
    col = lambda b, i: (b, 0, i)
    out_shape = [jax.ShapeDtypeStruct((B, w, T) if tr else (B, T, w), dt) for w, dt, tr in outs]
    out_specs = [pl.BlockSpec((1, w, tm), col) if tr else pl.BlockSpec((1, tm, w), row) for w, _, tr in outs]
    return pl.pallas_call(
        functools.partial(_inproj_kernel, use_rope=use_rope),
        grid=(B, T // tm), in_specs=in_specs, out_specs=out_specs, out_shape=out_shape,
        compiler_params=_params("parallel", "parallel"), name="inproj",
    )(*args)


NA_ROWS_PER_STEP = 4


def _na_kernel(q_ref, kv0_ref, kv1_ref, kv2_ref, ckv_ref, bias_ref, o_ref):
    q = q_ref[0]
    kv = jnp.concatenate([kv0_ref[0], kv1_ref[0], kv2_ref[0]], axis=0)
    ckv = ckv_ref[0]
    for h in range(NA_HEADS):
        ks = slice(h * NA_HEAD_DIM, (h + 1) * NA_HEAD_DIM)
        vs = slice(NA_W + h * NA_HEAD_DIM, NA_W + (h + 1) * NA_HEAD_DIM)
        qh = q[:, ks]
        s = _dot_nt(qh, kv[:, ks]) + bias_ref[0, h]
        sc = _dot_nt(qh, ckv[:, ks])
        m = jnp.maximum(jnp.max(s, axis=-1, keepdims=True), jnp.max(sc, axis=-1, keepdims=True))
        p = jnp.exp(s - m)
        pc = jnp.exp(sc - m)
        l = jnp.sum(p, axis=-1, keepdims=True) + jnp.sum(pc, axis=-1, keepdims=True)
        o = _dot(p.astype(BF16), kv[:, vs]) + _dot(pc.astype(BF16), ckv[:, vs])
        o_ref[0, :, ks] = o / l


def _na_bias_table(rpb, rows):
    R = NA_ROWS_PER_STEP
    nb = rows // R
    col = jnp.arange(GRID_W)
    cs = jnp.clip(col - NA_WIN_C // 2, 0, GRID_W - NA_WIN_C)
    col_ok = (col[None, :] >= cs[:, None]) & (col[None, :] < cs[:, None] + NA_WIN_C)
    ci = jnp.clip(col[None, :] - col[:, None] + (NA_WIN_C - 1), 0, 2 * NA_WIN_C - 2)
    j = jnp.array([0, 1, nb - 1])[:, None, None]
    r = R * j + jnp.arange(R)[None, :, None]
    krow = R * (j - 1) + jnp.arange(3 * R)[None, None, :]
    rs = jnp.clip(r - NA_WIN_R // 2, 0, rows - NA_WIN_R)
    row_ok = (krow >= rs) & (krow < rs + NA_WIN_R) & (krow >= 0) & (krow < rows)
    ri = jnp.clip(krow - r + (NA_WIN_R - 1), 0, 2 * NA_WIN_R - 2)
    tab = rpb.astype(F32)[:, ri][:, :, :, :, ci]
    ok = row_ok[None, :, :, :, None, None] & col_ok[None, None, None, None]
    tab = jnp.where(ok, tab, NEG_BIG)
    tab = tab.transpose(1, 0, 2, 4, 3, 5)
    return tab.reshape(3, NA_HEADS, R * GRID_W, 3 * R * GRID_W)


def _na_latent(naq, nakv, cnakv, rpb):
    B, T, _ = naq.shape
    rows = T // GRID_W
    R = NA_ROWS_PER_STEP
    assert rows % R == 0 and rows // R >= 3 and rows >= NA_WIN_R and R + NA_WIN_R <= 3 * R
    nb = rows // R
    tb = R * GRID_W
    ctx_len = cnakv.shape[1]
    pattern = lambda j: (j > 0).astype(jnp.int32) + (j == nb - 1).astype(jnp.int32)
    kv_spec = lambda d: pl.BlockSpec((1, tb, 2 * NA_W), lambda b, j: (b, jnp.clip(j + d, 0, nb - 1), 0))
    return pl.pallas_call(
        _na_kernel, grid=(B, nb),
        in_specs=[
            pl.BlockSpec((1, tb, NA_W), lambda b, j: (b, j, 0)),
            kv_spec(-1), kv_spec(0), kv_spec(1),
            pl.BlockSpec((1, ctx_len, 2 * NA_W), lambda b, j: (b, 0, 0)),
            pl.BlockSpec((1, NA_HEADS, tb, 3 * tb), lambda b, j: (pattern(j), 0, 0, 0)),
        ],
        out_specs=pl.BlockSpec((1, tb, NA_W), lambda b, j: (b, j, 0)),
        out_shape=jax.ShapeDtypeStruct((B, T, NA_W), F32),
        compiler_params=_params("parallel", "parallel"), name="na_latent",
    )(naq, nakv, nakv, nakv, cnakv, _na_bias_table(rpb, rows))


DIFF_FIXED_SHIFT_MAX = 60.0


def _diff_kernel(shift_ref, lam_ref, g_ref, qt_ref, k_ref, vt_ref, o_ref, qx_sc, m_sc, l_sc, acc_sc, *,
                 lam_init, tk, nk, online):
    h = pl.program_id(1)
    qt = qt_ref[0]
    tq = qt.shape[1]
    grp = lax.broadcasted_iota(jnp.int32, qt.shape, 0) // DIFF_HEAD_DIM
    for mi in range(2):
        qx_sc[mi] = jnp.where(grp == 2 * h + mi, qt, jnp.zeros_like(qt))
    m_sc[...] = jnp.full(m_sc.shape, -jnp.inf, F32)
    l_sc[...] = jnp.zeros(l_sc.shape, F32)
    acc_sc[...] = jnp.zeros(acc_sc.shape, F32)

    def body(k, carry):
        off = pl.multiple_of(k * tk, tk)
        kb = k_ref[0, pl.ds(off, tk), :]
        vtb = vt_ref[0, :, pl.ds(off, tk)]
        for mi in range(2):
            s = _dot(kb, qx_sc[mi])
            if online:
                m_prev = m_sc[mi]
                m_new = jnp.maximum(m_prev, jnp.max(s, axis=0, keepdims=True))
                alpha = jnp.exp2(m_prev - m_new)
                p = jnp.exp2(s - m_new)
                l_sc[mi] = alpha * l_sc[mi] + p.reshape(tk // 8, 8, tq).sum(axis=0)
                acc_sc[mi] = alpha * acc_sc[mi] + _dot(vtb, p.astype(BF16))
                m_sc[mi] = m_new
            else:
                p = jnp.exp2(s - shift_ref[0])
                l_sc[mi] += p.reshape(tk // 8, 8, tq).sum(axis=0)
                acc_sc[mi] += _dot(vtb, p.astype(BF16))
        return carry

    lax.fori_loop(0, nk, body, 0)

    lv = lam_ref[...]
    lam = (jnp.exp(jnp.sum(lv[0:1] * lv[1:2], axis=-1, keepdims=True))
           - jnp.exp(jnp.sum(lv[2:3] * lv[3:4], axis=-1, keepdims=True)) + lam_init)
    l0 = jnp.sum(l_sc[0], axis=0, keepdims=True)
    l1 = jnp.sum(l_sc[1], axis=0, keepdims=True)
    o = acc_sc[0] / l0 - lam * (acc_sc[1] / l1)
    ms = jnp.mean(o * o, axis=0, keepdims=True)
    o_ref[0] = o * lax.rsqrt(ms + EPS) * g_ref[...] * (1.0 - lam_init)


def _diff_call(shift, qt, k, vt, lam_params, subln_g, lam_init, tq, tk, online):
    B, W, Tq = qt.shape
    Tk = k.shape[1]
    dv = W // DIFF_HEADS
    return pl.pallas_call(
        functools.partial(_diff_kernel, lam_init=lam_init, tk=tk, nk=Tk // tk, online=online),
        grid=(B, DIFF_HEADS, Tq // tq),
        in_specs=[
            pl.BlockSpec(memory_space=pltpu.SMEM),
            pl.BlockSpec((4, DIFF_HEAD_DIM), lambda b, h, i: (0, 0)),
            pl.BlockSpec((dv, 1), lambda b, h, i: (0, 0)),
            pl.BlockSpec((1, W, tq), lambda b, h, i: (b, 0, i)),
            pl.BlockSpec((1, Tk, W), lambda b, h, i: (b, 0, 0)),
            pl.BlockSpec((1, dv, Tk), lambda b, h, i: (b, h, 0)),
        ],
        out_specs=pl.BlockSpec((1, dv, tq), lambda b, h, i: (b, h, i)),
        out_shape=jax.ShapeDtypeStruct((B, W, Tq), F32),
        scratch_shapes=[pltpu.VMEM((2, W, tq), BF16), pltpu.VMEM((2, 1, tq), F32),
                        pltpu.VMEM((2, 8, tq), F32), pltpu.VMEM((2, dv, tq), F32)],
        compiler_params=_params("parallel", "parallel", "parallel"),
        name="diff_attn_online" if online else "diff_attn_fixed",
    )(shift, lam_params, subln_g, qt, k, vt)


def _diff_attn(qt, dfk, vt, gq, gk, lam_params, subln_g, lam_init, tq, tk):
    bound = (1.02 * math.log2(math.e) * DIFF_HEAD_DIM ** 0.5) * jnp.max(jnp.abs(gq)) * jnp.max(jnp.abs(gk))
    shift = bound.astype(F32).reshape(1)
    args = (shift, qt, dfk, vt, lam_params, subln_g, lam_init, tq, tk)
    return lax.cond(bound <= DIFF_FIXED_SHIFT_MAX,
                    lambda: _diff_call(*args, online=False), lambda: _diff_call(*args, online=True))


POOL_HALO = 8


def _pool_mixer(u, prev, nxt, wp_ref, scale, n_total):
    tm = u.shape[0]
    i = pl.program_id(1)
    prev = jnp.where(i > 0, prev, 0.0)
    nxt = jnp.where(i < pl.num_programs(1) - 1, nxt, 0.0)
    ext = jnp.concatenate([prev, u, nxt], axis=0)
    sh = lambda d: ext[POOL_HALO + d:POOL_HALO + d + tm]
    t = i * tm + lax.broadcasted_iota(jnp.int32, (tm, 1), 0)
    lane_grp = lax.broadcasted_iota(jnp.int32, (1, POOL_W), 1) // POOL_GROUP_W
    win = None
    mean = None
    prev_half = 0
    for gi, w in enumerate(POOL_WINDOWS):
        half = w // 2
        for d in range(prev_half, half):
            add = sh(d) + sh(-d - 1)
            win = add if win is None else win + add
        prev_half = half
        cnt = (jnp.minimum(t + half, n_total) - jnp.maximum(t - half, 0)).astype(F32)
        m_w = win / cnt
        mean = m_w if mean is None else jnp.where(lane_grp >= gi, m_w, mean)
    return _dot((mean - u).astype(BF16), wp_ref[...]) * scale


def _merge_kernel(x_ref, gt_ref, yna_ref, ydf_ref, u_ref, up_ref, un_ref, f_ref, gate_ref, wbr_ref, wp_ref,
                  ps_ref, wf_ref, wo_ref, o_ref, *, n_total):
    yf = _dot(f_ref[0].astype(BF16), wf_ref[...])
    ypool = _pool_mixer(u_ref[0], up_ref[0], un_ref[0], wp_ref, ps_ref[...], n_total)
    ys = (yna_ref[0], ydf_ref[0].T, ypool, yf)
    acc = None
    for i, y in enumerate(ys):
        g = gate_ref[0, :, i * D_MODEL:(i + 1) * D_MODEL].astype(F32)
        term = g * _dot(y.astype(BF16), wbr_ref[i])
        acc = term if acc is None else acc + term
    o_ref[0] = x_ref[0] + gt_ref[0] * _dot(acc.astype(BF16), wo_ref[...])


def _merge(x, gt, y_na, y_df, pool_in, f_real, gate, w_br, w_pool_bd, pool_scale, w_f, w_o, tm):
    B, T, D = x.shape
    assert tm % POOL_HALO == 0 and POOL_HALO >= max(POOL_WINDOWS) // 2
    hb = tm // POOL_HALO
    row = lambda b, i: (b, i, 0)
    const2 = lambda b, i: (0, 0)
    return pl.pallas_call(
        functools.partial(_merge_kernel, n_total=T), grid=(B, T // tm),
        in_specs=[
            pl.BlockSpec((1, tm, D), row),
            pl.BlockSpec((1, 1, D), lambda b, i: (b, 0, 0)),
            pl.BlockSpec((1, tm, BRANCH_W), row), pl.BlockSpec((1, BRANCH_W, tm), lambda b, i: (b, 0, i)),
            pl.BlockSpec((1, tm, POOL_W), row),
            pl.BlockSpec((1, POOL_HALO, POOL_W), lambda b, i: (b, jnp.maximum(i * hb - 1, 0), 0)),
            pl.BlockSpec((1, POOL_HALO, POOL_W), lambda b, i: (b, jnp.minimum((i + 1) * hb, T // POOL_HALO - 1), 0)),
            pl.BlockSpec((1, tm, FNET_W), row),
            pl.BlockSpec((1, tm, N_BRANCH * D), row),
            pl.BlockSpec((N_BRANCH, BRANCH_W, D), lambda b, i: (0, 0, 0)),
            pl.BlockSpec((POOL_W, POOL_W), const2), pl.BlockSpec((1, POOL_W), const2),
            pl.BlockSpec((FNET_W, FNET_W), const2),
            pl.BlockSpec((D, D), const2),
        ],
        out_specs=pl.BlockSpec((1, tm, D), row),
        out_shape=jax.ShapeDtypeStruct((B, T, D), F32),
        compiler_params=_params("parallel", "parallel"), name="merge",
    )(x, gt, y_na, y_df, pool_in, pool_in, pool_in, f_real, gate, w_br, w_pool_bd, pool_scale, w_f, w_o)


TOK_BLOCK = LANES
SEL_GROUP = 8
COMBINE_WIN = 48
BF16_ROWS = 16


def _router_kernel(x_ref, g_ref, sh_ref, sc_ref, wrh_ref, wrl_ref, h_ref, aff_ref):
    x = x_ref[0]
    ms = jnp.mean(x * x, axis=-1, keepdims=True)
    h = x * lax.rsqrt(ms + EPS) * g_ref[...] * (1.0 + sc_ref[0]) + sh_ref[0]
    h_ref[0] = h.astype(BF16)
    hi, lo = _split_bf16(h)
    logits = _dot_nt(wrh_ref[...], hi) + _dot_nt(wrh_ref[...], lo) + _dot_nt(wrl_ref[...], hi)
    logits = logits - jnp.max(logits, axis=0, keepdims=True)
    e = jnp.exp(logits)
    aff = e / jnp.sum(e, axis=0, keepdims=True)
    for j in range(aff.shape[1] // TOK_BLOCK):
        aff_ref[0, :, j, :] = aff[:, j * TOK_BLOCK:(j + 1) * TOK_BLOCK]


def _router(x, g, sh, sc, w_router, tm):
    B, T, D = x.shape
    E = w_router.shape[1]
    wrh, wrl = _split_bf16(w_router.T)
    row = lambda b, i: (b, i, 0)
    return pl.pallas_call(
        _router_kernel, grid=(B, T // tm),
        in_specs=[
            pl.BlockSpec((1, tm, D), row), pl.BlockSpec((1, D), lambda b, i: (0, 0)),
            pl.BlockSpec((1, 1, D), lambda b, i: (b, 0, 0)), pl.BlockSpec((1, 1, D), lambda b, i: (b, 0, 0)),
            pl.BlockSpec((E, D), lambda b, i: (0, 0)), pl.BlockSpec((E, D), lambda b, i: (0, 0)),
        ],
        out_specs=[pl.BlockSpec((1, tm, D), row),
                   pl.BlockSpec((1, E, tm // TOK_BLOCK, TOK_BLOCK), lambda b, i: (b, 0, i, 0))],
        out_shape=[jax.ShapeDtypeStruct((B, T, D), BF16),
                   jax.ShapeDtypeStruct((B, E, T // TOK_BLOCK, TOK_BLOCK), F32)],
        compiler_params=_params("parallel", "parallel"), name="router",
    )(x, g, sh, sc, wrh, wrl)


def _select_kernel(aff_ref, u_ref, ls_ref, idx_ref, gate_ref, pos_ref, off_ref, tot_ref, *, cap):
    G, nb, _ = aff_ref.shape
    aff = aff_ref[...]
    bits = pltpu.bitcast(aff, jnp.int32)
    kf = float(cap)
    ones_f = lambda m: jnp.where(m, 1.0, 0.0)
    count = lambda m: jnp.sum(ones_f(m), axis=(1, 2), keepdims=True)

    def search(it, lo):
        cand = lo | jnp.left_shift(jnp.int32(1), 30 - it)
        return jnp.where(count(bits >= cand) >= kf, cand, lo)

    thr = lax.fori_loop(0, 31, search, jnp.zeros((G, 1, 1), jnp.int32))
    need = kf - count(bits > thr)

    u = u_ref[...]
    ls = ls_ref[...]
    ones_m = jnp.ones((LANES, LANES), BF16)

    def prefix(mf):
        mb = mf.astype(BF16)
        tot = _dot(mb, ones_m)
        off = _dot(ls, tot.astype(BF16))
        return _dot(mb, u) + off, off, tot

    blk_i = lax.broadcasted_iota(jnp.int32, (nb, LANES), 0)
    lane_i = lax.broadcasted_iota(jnp.int32, (nb, LANES), 1)
    s_row = lax.broadcasted_iota(jnp.int32, (1, cap), 1).astype(F32)
    col_nb = lax.broadcasted_iota(jnp.int32, (nb, 1), 0).astype(F32)
    col_l = lax.broadcasted_iota(jnp.int32, (LANES, 1), 0).astype(F32)
    for g in range(G):
        gt = bits[g] > thr[g]
        eq = bits[g] == thr[g]
        eqf = ones_f(eq)
        tie_before = prefix(eqf)[0] - eqf
        sel = gt | (eq & (tie_before < need[g]))
        pin, off, tot = prefix(ones_f(sel))
        pos_ref[g] = jnp.where(sel, pin - 1.0, -1.0).astype(jnp.int32)
        diag = blk_i == lane_i
        off_ref[g] = jnp.sum(jnp.where(diag, off, 0.0), axis=0, keepdims=True).astype(jnp.int32)
        tot_ref[g] = jnp.sum(jnp.where(diag, tot, 0.0), axis=0, keepdims=True).astype(jnp.int32)
        blk = jnp.sum(ones_f((off + tot)[:, 0:1] <= s_row), axis=0, keepdims=True)
        oh = jnp.where(col_nb == blk, 1.0, 0.0).astype(BF16)
        pin_t = pin.T
        p_hi = jnp.floor(pin_t * (1.0 / LANES))
        p_lo = pin_t - p_hi * LANES
        rows = _dot(p_hi.astype(BF16), oh) * LANES + _dot(p_lo.astype(BF16), oh)
        lane = jnp.sum(ones_f(rows <= s_row), axis=0, keepdims=True)
        idx_ref[g] = (blk * LANES + lane).astype(jnp.int32)
        a_t = aff[g].T
        a1 = a_t.astype(BF16)
        r1 = a_t - a1.astype(F32)
        a2 = r1.astype(BF16)
        a3 = (r1 - a2.astype(F32)).astype(BF16)
        arow = _dot(a1, oh) + _dot(a2, oh) + _dot(a3, oh)
        gate_ref[g] = jnp.sum(jnp.where(col_l == lane, arow, 0.0), axis=0, keepdims=True)


def _select(aff, cap):
    B, E, nb, _ = aff.shape
    R = B * E
    G = SEL_GROUP
    i = np.arange(LANES)
    u = jnp.asarray(i[:, None] <= i[None, :], BF16)
    j = np.arange(nb)
    ls = jnp.asarray(j[None, :] < j[:, None], BF16)
    grp = lambda r: (r, 0, 0)
    outs = pl.pallas_call(
        functools.partial(_select_kernel, cap=cap), grid=(R // G,),
        in_specs=[pl.BlockSpec((G, nb, LANES), grp), pl.BlockSpec((LANES, LANES), lambda r: (0, 0)),
                  pl.BlockSpec((nb, nb), lambda r: (0, 0))],
        out_specs=[pl.BlockSpec((G, 1, cap), grp), pl.BlockSpec((G, 1, cap), grp),
                   pl.BlockSpec((G, nb, LANES), grp), pl.BlockSpec((G, 1, LANES), grp),
                   pl.BlockSpec((G, 1, LANES), grp)],
        out_shape=[jax.ShapeDtypeStruct((R, 1, cap), jnp.int32), jax.ShapeDtypeStruct((R, 1, cap), F32),
                   jax.ShapeDtypeStruct((R, nb, LANES), jnp.int32), jax.ShapeDtypeStruct((R, 1, LANES), jnp.int32),
                   jax.ShapeDtypeStruct((R, 1, LANES), jnp.int32)],
        compiler_params=_params("parallel"), name="select",
    )(aff.reshape(R, nb, LANES), u, ls)
    idx, gate, pos, off, tot = outs
    return (idx.reshape(B, E, cap), gate.reshape(B, E, cap), pos.reshape(B, E, nb, LANES),
            off[:, 0, :nb].reshape(B, E, nb), tot[:, 0, :nb].reshape(B, E, nb))


def _combine_kernel(off_ref, tot_ref, x_ref, gt_ref, pos_ref, ex_ref, ye_hbm, o_ref, buf, xbuf, acc_sc, sem,
                    xsem, *, nt, cap, W):
    n = pl.program_id(0)
    E = N_EXPERTS

    def window(step, e):
        b = step // nt
        o = off_ref[(b * E + e) * nt + step % nt]
        return b, o, jnp.minimum((o // BF16_ROWS) * BF16_ROWS, cap - W)

    def fetch(step, slot):
        for e in range(E):
            b, _, st = window(step, e)
            pltpu.make_async_copy(ye_hbm.at[b, e, pl.ds(pl.multiple_of(st, BF16_ROWS), W)], buf.at[slot, e],
                                  sem.at[slot]).start()

    @pl.when(n == 0)
    def _():
        fetch(0, 0)

    @pl.when(n + 1 < pl.num_programs(0))
    def _():
        fetch(n + 1, (n + 1) % 2)

    slot = n % 2
    for e in range(E):
        pltpu.make_async_copy(ye_hbm.at[0, 0, pl.ds(0, W)], buf.at[slot, e], sem.at[slot]).wait()

    pos = pos_ref[0, :, (n % nt) % pos_ref.shape[2], :].astype(F32).T
    p_hi = jnp.floor(pos * (1.0 / 64.0))
    p_lo = pos - 64.0 * p_hi
    ex = ex_ref[...]
    posx = 64.0 * _dot(p_hi.astype(BF16), ex) + _dot(p_lo.astype(BF16), ex)
    col = lax.broadcasted_iota(jnp.int32, (1, E * W), 1)
    tgt = jnp.zeros((1, E * W), jnp.int32)
    for e in range(E):
        _, _, st = window(n, e)
        tgt = jnp.where(col // W == e, st + col % W, tgt)
    onehot = jnp.where(posx == tgt.astype(F32), 1.0, 0.0).astype(BF16)
    acc_sc[...] = _dot(onehot, buf[slot].reshape(E * W, buf.shape[-1]))

    w_i = lax.broadcasted_iota(jnp.int32, (1, W), 1)
    for e in range(E):
        b, o, st = window(n, e)
        t = tot_ref[(b * E + e) * nt + n % nt]
        n_extra = jnp.maximum(o + t - (st + W) + (W - 1), 0) // W

        def extra(j, carry, e=e, b=b, st=st):
            base = st + W * (j + 1)
            src = jnp.minimum(base, cap - W)
            cp = pltpu.make_async_copy(ye_hbm.at[b, e, pl.ds(pl.multiple_of(src, BF16_ROWS), W)], xbuf,
                                       xsem.at[0])
            cp.start()
            cp.wait()
            slot_i = (src + w_i).astype(F32)
            oh = jnp.where((pos[:, e:e + 1] == slot_i) & (slot_i >= base.astype(F32)), 1.0, 0.0).astype(BF16)
            acc_sc[...] += _dot(oh, xbuf[...])
            return carry

        lax.fori_loop(0, n_extra, extra, 0)

    o_ref[0] = x_ref[0] + gt_ref[0] * acc_sc[...]


def _combine(x, gt, pos, off, tot, ye):
    B, T, D = x.shape
    E, cap = ye.shape[1], ye.shape[2]
    nt = T // TOK_BLOCK
    W = min(COMBINE_WIN, cap)
    pr = min(8, nt)
    assert nt % pr == 0
    assert W % BF16_ROWS == 0 and (cap - W) % BF16_ROWS == 0 and (E * W) % LANES == 0
    expand = jnp.asarray(np.arange(E)[:, None] == (np.arange(E * W)[None, :] // W), BF16)
    tile = lambda n, off_r, tot_r: (n // nt, n % nt, 0)
    grid_spec = pltpu.PrefetchScalarGridSpec(
        num_scalar_prefetch=2, grid=(B * nt,),
        in_specs=[
            pl.BlockSpec((1, TOK_BLOCK, D), tile),
            pl.BlockSpec((1, 1, D), lambda n, off_r, tot_r: (n // nt, 0, 0)),
            pl.BlockSpec((1, E, pr, TOK_BLOCK), lambda n, off_r, tot_r: (n // nt, 0, (n % nt) // pr, 0)),
            pl.BlockSpec((E, E * W), lambda n, off_r, tot_r: (0, 0)),
            pl.BlockSpec(memory_space=pl.ANY),
        ],
        out_specs=pl.BlockSpec((1, TOK_BLOCK, D), tile),
        scratch_shapes=[pltpu.VMEM((2, E, W, D), BF16), pltpu.VMEM((W, D), BF16),
                        pltpu.VMEM((TOK_BLOCK, D), F32),
                        pltpu.SemaphoreType.DMA((2,)), pltpu.SemaphoreType.DMA((1,))],
    )
    return pl.pallas_call(
        functools.partial(_combine_kernel, nt=nt, cap=cap, W=W), grid_spec=grid_spec,
        out_shape=jax.ShapeDtypeStruct((B, T, D), F32),
        compiler_params=_params("arbitrary"), name="combine",
    )(off.reshape(-1), tot.reshape(-1), x, gt, pos, expand, ye)


FF_CHUNKS = ((0, 512), (512, 512), (1024, 384))


def _expert_kernel(xe_ref, g_ref, wg_ref, wu_ref, wd_ref, o_ref):
    xe = xe_ref[0, 0]
    acc = None
    for lo, n in FF_CHUNKS:
        a = _dot(xe, wg_ref[0, 0, :, lo:lo + n].astype(BF16))
        u = _dot(xe, wu_ref[0, 0, :, lo:lo + n].astype(BF16))
        hid = (a * jax.nn.sigmoid(a) * u).astype(BF16)
        part = _dot(hid, wd_ref[0, 0, lo:lo + n, :].astype(BF16))
        acc = part if acc is None else acc + part
    o_ref[0, 0] = (acc * g_ref[0, 0]).astype(o_ref.dtype)


def _experts(xe, g, wg, wu, wd, layer, tm):
    B, E, cap, D = xe.shape
    F = wg.shape[-1]
    assert F == EXPERT_FF
    tile = lambda e, b, i: (b, e, i, 0)
    wsel = lambda e, b, i: (layer, e, 0, 0)
    return pl.pallas_call(
        _expert_kernel, grid=(E, B, cap // tm),
        in_specs=[
            pl.BlockSpec((1, 1, tm, D), tile), pl.BlockSpec((1, 1, tm, 1), tile),
            pl.BlockSpec((1, 1, D, F), wsel), pl.BlockSpec((1, 1, D, F), wsel), pl.BlockSpec((1, 1, F, D), wsel),
        ],
        out_specs=pl.BlockSpec((1, 1, tm, D), tile),
        out_shape=jax.ShapeDtypeStruct((B, E, cap, D), BF16),
        compiler_params=_params("parallel", "parallel", "parallel"), name="experts",
    )(xe, g, wg, wu, wd)


def _ec_ffn(x, gt, g, sh, sc, w_router, wg, wu, wd, layer, tm_router, tm_expert):
    B, N, D = x.shape
    cap = max(1, EC_FACTOR * N // N_EXPERTS)
    assert N <= TOK_BLOCK * LANES and N % TOK_BLOCK == 0
    nt = N // TOK_BLOCK
    h, aff = _router(x, g, sh, sc, w_router, tm_router)
    aff = jnp.pad(aff, ((0, 0), (0, 0), (0, LANES - nt), (0, 0)))
    idx, gsel, pos, off, tot = _select(aff, cap)
    xe = h[jnp.arange(B)[:, None, None], idx]
    ye = _experts(xe, gsel[..., None], wg, wu, wd, layer, min(tm_expert, cap))
    return _combine(x, gt, pos[:, :, :nt], off[:, :, :nt], tot[:, :, :nt], ye)


DFT_RADIX = 128


def _dot3(a_hi, a_lo, b_hi, b_lo):
    return _dot(a_hi, b_hi) + _dot(a_lo, b_hi) + _dot(a_hi, b_lo)


def _cos_sin(n_rows, n_cols, period):
    k = (np.arange(n_rows)[:, None] * np.arange(n_cols)[None, :]) % period
    ang = 2.0 * np.pi * k.astype(np.float64) / period
    return np.cos(ang), np.sin(ang)


def _const_split(m):
    return _split_bf16(jnp.asarray(m, F32))


def _chan_dft_kernel(u_ref, mh_ref, ml_ref, xr_ref, xi_ref):
    uh, ul = _split_bf16(u_ref[0])
    y = _dot3(uh, ul, mh_ref[...], ml_ref[...])
    c = xr_ref.shape[-1]
    xr_ref[0] = y[:, :c]
    xi_ref[0] = y[:, c:]


def _chan_dft(u, tm):
    B, N, C = u.shape
    cc, sc = _cos_sin(C, C, C)
    mh, ml = _const_split(np.concatenate([cc, -sc], axis=1))
    row = lambda b, i: (b, i, 0)
    const2 = lambda b, i: (0, 0)
    return pl.pallas_call(
        _chan_dft_kernel, grid=(B, N // tm),
        in_specs=[pl.BlockSpec((1, tm, C), row), pl.BlockSpec((C, 2 * C), const2), pl.BlockSpec((C, 2 * C), const2)],
        out_specs=[pl.BlockSpec((1, tm, C), row)] * 2,
        out_shape=[jax.ShapeDtypeStruct((B, N, C), F32)] * 2,
        compiler_params=_params("parallel", "parallel"), name="chan_dft",
    )(u, mh, ml)


def _dft_left_kernel(xr_ref, xi_ref, mh_ref, ml_ref, o_ref, *, scale):
    xh, xl = _split_bf16(jnp.concatenate([xr_ref[0], xi_ref[0]], axis=0))
    o_ref[0] = _dot3(mh_ref[...], ml_ref[...], xh, xl) * scale


def _dft_left(xr, xi, m, scale, tn):
    B, K, cols = xr.shape
    R = m.shape[0]
    mh, ml = _const_split(m)
    col = lambda b, j: (b, 0, j)
    const2 = lambda b, j: (0, 0)
    return pl.pallas_call(
        functools.partial(_dft_left_kernel, scale=scale), grid=(B, cols // tn),
        in_specs=[pl.BlockSpec((1, K, tn), col), pl.BlockSpec((1, K, tn), col),
                  pl.BlockSpec((R, 2 * K), const2), pl.BlockSpec((R, 2 * K), const2)],
        out_specs=pl.BlockSpec((1, R, tn), col),
        out_shape=jax.ShapeDtypeStruct((B, R, cols), F32),
        compiler_params=_params("parallel", "parallel"), name="dft_left",
    )(xr, xi, mh, ml)


DFT_N2_PER_STEP = 8


def _dft_outer_kernel(xr_ref, xi_ref, mh_ref, ml_ref, o_ref):
    n1 = xr_ref.shape[1]
    for s in range(DFT_N2_PER_STEP):
        xh, xl = _split_bf16(jnp.concatenate([xr_ref[0, :, s, :], xi_ref[0, :, s, :]], axis=0))
        y = _dot3(mh_ref[...], ml_ref[...], xh, xl)
        o_ref[0, 0, :, s, :] = y[:n1]
        o_ref[0, 1, :, s, :] = y[n1:]


def _dft_outer(xr, xi, m):
    B, n1, n2, C = xr.shape
    mh, ml = _const_split(m)
    blk = lambda b, j: (b, 0, j, 0)
    const2 = lambda b, j: (0, 0)
    return pl.pallas_call(
        _dft_outer_kernel, grid=(B, n2 // DFT_N2_PER_STEP),
        in_specs=[pl.BlockSpec((1, n1, DFT_N2_PER_STEP, C), blk), pl.BlockSpec((1, n1, DFT_N2_PER_STEP, C), blk),
                  pl.BlockSpec((2 * n1, 2 * n1), const2), pl.BlockSpec((2 * n1, 2 * n1), const2)],
        out_specs=pl.BlockSpec((1, 2, n1, DFT_N2_PER_STEP, C), lambda b, j: (b, 0, 0, j, 0)),
        out_shape=jax.ShapeDtypeStruct((B, 2, n1, n2, C), F32),
        compiler_params=_params("parallel", "parallel"), name="dft_outer",
    )(xr, xi, mh, ml)


DFT_K1_PER_STEP = 8


def _dft_twiddle_kernel(y_ref, tc_ref, ts_ref, dh_ref, dl_ref, o_ref, *, scale):
    tc = tc_ref[0]
    ts = ts_ref[0]
    for kk in range(DFT_K1_PER_STEP):
        yr = y_ref[0, 0, kk]
        yi = y_ref[0, 1, kk]
        c, s = tc[:, kk:kk + 1], ts[:, kk:kk + 1]
        zh, zl = _split_bf16(jnp.concatenate([yr * c + yi * s, yi * c - yr * s], axis=0))
        o_ref[0, :, kk, :] = _dot3(dh_ref[...], dl_ref[...], zh, zl) * scale


def _fourier_real(u):
    B, N, C = u.shape
    scale = 1.0 / math.sqrt(N * C)
    xr, xi = _chan_dft(u, min(N, 512))
    if N % (DFT_RADIX * DFT_K1_PER_STEP) != 0:
        assert N <= 1024
        cn, sn = _cos_sin(N, N, N)
        return _dft_left(xr, xi, np.concatenate([cn, sn], axis=1), scale, C)
    n1, n2 = N // DFT_RADIX, DFT_RADIX
    c1, s1 = _cos_sin(n1, n1, n1)
    m1 = np.block([[c1, s1], [-s1, c1]])
    yy = _dft_outer(xr.reshape(B, n1, n2, C), xi.reshape(B, n1, n2, C), m1)
    tcos, tsin = _cos_sin(n1, n2, N)
    grp = lambda t: jnp.asarray(t.reshape(n1 // DFT_K1_PER_STEP, DFT_K1_PER_STEP, n2).transpose(0, 2, 1), F32)
    c2, s2 = _cos_sin(n2, n2, n2)
    dh, dl = _const_split(np.concatenate([c2, s2], axis=1))
    fp = pl.pallas_call(
        functools.partial(_dft_twiddle_kernel, scale=scale), grid=(B, n1 // DFT_K1_PER_STEP),
        in_specs=[
            pl.BlockSpec((1, 2, DFT_K1_PER_STEP, n2, C), lambda b, g: (b, 0, g, 0, 0)),
            pl.BlockSpec((1, n2, DFT_K1_PER_STEP), lambda b, g: (g, 0, 0)),
            pl.BlockSpec((1, n2, DFT_K1_PER_STEP), lambda b, g: (g, 0, 0)),
            pl.BlockSpec((n2, 2 * n2), lambda b, g: (0, 0)), pl.BlockSpec((n2, 2 * n2), lambda b, g: (0, 0)),
        ],
        out_specs=pl.BlockSpec((1, n2, DFT_K1_PER_STEP, C), lambda b, g: (b, 0, g, 0)),
        out_shape=jax.ShapeDtypeStruct((B, n2, n1, C), F32),
        compiler_params=_params("parallel", "parallel"), name="dft_twiddle",
    )(yy, grp(tcos), grp(tsin), dh, dl)
    return fp.reshape(B, N, C)


def _ctx_attn_kernel(q_ref, kv_ref, o_ref):
    q = q_ref[0]
    kv = kv_ref[0]
    for h in range(NA_HEADS):
        ks = slice(h * NA_HEAD_DIM, (h + 1) * NA_HEAD_DIM)
        vs = slice(NA_W + h * NA_HEAD_DIM, NA_W + (h + 1) * NA_HEAD_DIM)
        s = _dot_nt(q[:, ks], kv[:, ks])
        p = jnp.exp(s - jnp.max(s, axis=-1, keepdims=True))
        l = jnp.sum(p, axis=-1, keepdims=True)
        o_ref[0, :, ks] = _dot(p.astype(BF16), kv[:, vs]) / l


def _ctx_dense_attn(q, kv):
    B, Q, _ = q.shape
    return pl.pallas_call(
        _ctx_attn_kernel, grid=(B,),
        in_specs=[pl.BlockSpec((1, Q, NA_W), lambda b: (b, 0, 0)),
                  pl.BlockSpec((1, kv.shape[1], 2 * NA_W), lambda b: (b, 0, 0))],
        out_specs=pl.BlockSpec((1, Q, NA_W), lambda b: (b, 0, 0)),
        out_shape=jax.ShapeDtypeStruct((B, Q, NA_W), F32),
        compiler_params=_params("parallel"), name="ctx_attn",
    )(q, kv)


ADA_ROWS = 8
ADA_TN = 512


def _ada_kernel(c_ref, w_ref, b_ref, o_ref):
    c = c_ref[...]
    sh, sl = _split_bf16(c * jax.nn.sigmoid(c))
    wh, wl = _split_bf16(w_ref[0])
    o_ref[...] = _dot3(sh, sl, wh, wl) + b_ref[0]


def _ada_mod(c_rows, w_ada, b_ada, layer):
    R, D = c_rows.shape
    N = w_ada.shape[-1]
    return pl.pallas_call(
        _ada_kernel, grid=(N // ADA_TN,),
        in_specs=[pl.BlockSpec((R, D), lambda j: (0, 0)),
                  pl.BlockSpec((1, D, ADA_TN), lambda j: (layer, 0, j)),
                  pl.BlockSpec((1, 1, ADA_TN), lambda j: (layer, 0, j))],
        out_specs=pl.BlockSpec((R, ADA_TN), lambda j: (0, j)),
        out_shape=jax.ShapeDtypeStruct((R, N), F32),
        compiler_params=_params("parallel"), name="ada_mod",
    )(c_rows, w_ada, b_ada[:, None, :])


def _tile(g, n):
    return jnp.tile(g.astype(F32), n)[None, :]


def kernel(x, c, ctx, c_ctx, w_ada, b_ada, g_mix, g_ffn, w_in, na_q_g, na_k_g, na_rpb, df_q_g, df_k_g,
           df_lambda, df_subln_g, pool_w, pool_scale, fnet_w, w_branch, w_out, w_router, w_gate_e, w_up_e,
           w_down_e):
    B, T, D = x.shape
    ctx_len = ctx.shape[1]
    rope = _rope_tables(T)
    assert B + 1 <= ADA_ROWS
    c_rows = jnp.concatenate([c, c_ctx[None, :], jnp.zeros((ADA_ROWS - B - 1, D), F32)], axis=0)
    tm = 256
    tq, tk = min(2048, T), 1280
    for l in range(DEPTH):
        last = l == DEPTH - 1
        lam_init = 0.8 - 0.6 * math.exp(-0.3 * l)
        mods = _ada_mod(c_rows, w_ada, b_ada, l)
        sh1, sc1, gt1, sh2, sc2, gt2 = [m[:, None, :] for m in jnp.split(mods[:B], 6, axis=-1)]
        bc = lambda m: jnp.broadcast_to(m[None, None, :], (B, 1, D))
        csh1, csc1, cgt1, csh2, csc2, cgt2 = [bc(m) for m in jnp.split(mods[B], 6, axis=-1)]

        w_bf = w_in[l].astype(BF16)
        gq, gk = _tile(na_q_g[l], NA_HEADS), _tile(na_k_g[l], NA_HEADS)
        gdq, gdk = _tile(df_q_g[l], 2 * DIFF_HEADS), _tile(df_k_g[l], 2 * DIFF_HEADS)
        gmix = g_mix[l][None, :]
        gffn = g_ffn[l][None, :]
        subg = df_subln_g[l][:, None].astype(F32)
        wbr = w_branch[l].astype(BF16)
        wf = fnet_w[l].astype(BF16)
        wpool = jax.scipy.linalg.block_diag(*[pool_w[l, gi] for gi in range(len(POOL_WINDOWS))]).astype(BF16)
        pscale = pool_scale[l][None, :].astype(F32)
        wo = w_out[l].astype(BF16)

        naq, dfq, pool_in, fnet_in, gate, nakv, dfk, dfv = _inproj(
            x, gmix, sh1, sc1, w_bf, gq, gk, gdq, gdk, rope, 2 * tm)
        (cnaq, cdfq, cpool_in, cfnet_in, cgate, cnakv, cdfk, cdfv) = _inproj(
            ctx, gmix, csh1, csc1, w_bf, gq, gk, gdq, gdk, None, ctx_len)

        y_na = _na_latent(naq, nakv, cnakv, na_rpb[l])
        lamp = df_lambda[l].astype(F32)
        y_df = _diff_attn(dfq, jnp.concatenate([dfk, cdfk], axis=1), jnp.concatenate([dfv, cdfv], axis=2),
                          df_q_g[l], df_k_g[l], lamp, subg, lam_init, tq, tk)
        f_real = _fourier_real(fnet_in)
        x_new = _merge(x, gt1, y_na, y_df, pool_in, f_real, gate, wbr, wpool, pscale, wf, wo, tm)

        if not last:
            yc_na = _ctx_dense_attn(cnaq, cnakv)
            yc_df = _diff_attn(cdfq, cdfk, cdfv, df_q_g[l], df_k_g[l], lamp, subg, lam_init, ctx_len, ctx_len)
            fc_real = _fourier_real(cfnet_in)
            ctx_new = _merge(ctx, cgt1, yc_na, yc_df, cpool_in, fc_real, cgate, wbr, wpool, pscale, wf, wo,
                             ctx_len)

        x = x_new
        x = _ec_ffn(x, gt2, gffn, sh2, sc2, w_router[l], w_gate_e, w_up_e, w_down_e, l, 4 * tm, 512)
        if not last:
            ctx = ctx_new
            ctx = _ec_ffn(ctx, cgt2, gffn, csh2, csc2, w_router[l], w_gate_e, w_up_e, w_down_e, l, ctx_len, 512)
    return x
```

```python
import functools
import math

import jax
import jax.numpy as jnp
import numpy as np
from jax import lax
from jax.experimental import pallas as pl
from jax.experimental.pallas import tpu as pltpu

F32 = jnp.float32
BF16 = jnp.bfloat16

D_MODEL = 1024
DEPTH = 2
GRID_W = 64
EPS = 1e-6
ROPE_BASE = 10000.0

NA_HEADS = 4
NA_HEAD_DIM = 64
NA_WIN_R = 8
NA_WIN_C = 16
NA_W = NA_HEADS * NA_HEAD_DIM

DIFF_HEADS = 4
DIFF_HEAD_DIM = 32
DIFF_QK_W = DIFF_HEADS * 2 * DIFF_HEAD_DIM
DIFF_V_W = DIFF_HEADS * 2 * DIFF_HEAD_DIM

POOL_WINDOWS = (2, 4, 8, 16)
POOL_GROUP_W = 64
POOL_W = len(POOL_WINDOWS) * POOL_GROUP_W
FNET_W = 256
N_BRANCH = 4
BRANCH_W = 256

OFF_NA_Q = 0
OFF_DF_Q = OFF_NA_Q + NA_W
OFF_POOL = OFF_DF_Q + DIFF_QK_W
OFF_FNET = OFF_POOL + POOL_W
OFF_GATE = OFF_FNET + FNET_W
OFF_KV = OFF_GATE + N_BRANCH * D_MODEL
KV_W = 2 * NA_W + DIFF_QK_W + DIFF_V_W
IN_COLS = OFF_KV + KV_W

N_EXPERTS = 16
EC_FACTOR = 2
EXPERT_FF = 1408

VMEM_LIMIT_BYTES = 56 * 1024 * 1024
LANES = 128
NEG_BIG = -1e30

INPROJ_ROWS = 512
MERGE_ROWS = 512
ROUTER_ROWS = 1024
EXPERT_ROWS = 512
DIFF_TQ, DIFF_TK = 2048, 1280


def _params(*sem):
    return pltpu.CompilerParams(dimension_semantics=sem, vmem_limit_bytes=VMEM_LIMIT_BYTES)


def _split_bf16(a):
    hi = a.astype(BF16)
    lo = (a - hi.astype(F32)).astype(BF16)
    return hi, lo


def _dot(a, b):
    return jnp.dot(a, b, preferred_element_type=F32)


def _dot_nt(a, b):
    return lax.dot_general(a, b, (((1,), (1,)), ((), ())), preferred_element_type=F32)


def _group_rmsnorm(p, bd_ref, g):
    hi, lo = _split_bf16(p * p)
    ms = _dot(hi, bd_ref[...]) + _dot(lo, bd_ref[...])
    return p * lax.rsqrt(ms + EPS) * g


def _rope256(y, cos, s_next, s_prev):
    outs = []
    for half in range(2):
        z = y[:, half * LANES:(half + 1) * LANES]
        outs.append(z * cos + pltpu.roll(z, LANES - 8, 1) * s_next + pltpu.roll(z, 8, 1) * s_prev)
    return jnp.concatenate(outs, axis=1)


def _inproj_kernel(*refs, use_rope):
    if use_rope:
        (x_ref, g_ref, sh_ref, sc_ref, w_ref, gq_ref, gk_ref, gdq_ref, gdk_ref, bd64_ref, bd32_ref,
         cos_ref, sn_ref, sp_ref,
         naq_ref, dfq_ref, pool_ref, fnet_ref, gate_ref, nakv_ref, dfk_ref, dfv_ref) = refs
    else:
        (x_ref, g_ref, sh_ref, sc_ref, w_ref, gq_ref, gk_ref, gdq_ref, gdk_ref, bd64_ref, bd32_ref,
         naq_ref, dfq_ref, pool_ref, fnet_ref, gate_ref, nakv_ref, dfk_ref, dfv_ref) = refs
    x = x_ref[0]
    ms = jnp.mean(x * x, axis=-1, keepdims=True)
    y = x * lax.rsqrt(ms + EPS) * g_ref[...]
    h = (y * (1.0 + sc_ref[0]) + sh_ref[0]).astype(BF16)

    def seg(lo, n):
        return _dot(h, w_ref[:, lo:lo + n])

    def rope(v):
        if not use_rope:
            return v
        return _rope256(v, cos_ref[...], sn_ref[...], sp_ref[...])

    naq = _group_rmsnorm(seg(OFF_NA_Q, NA_W), bd64_ref, gq_ref[...])
    naq_ref[0] = (naq * (NA_HEAD_DIM ** -0.5)).astype(BF16)
    dfq = rope(_group_rmsnorm(seg(OFF_DF_Q, DIFF_QK_W), bd32_ref, gdq_ref[...]))
    dfq_ref[0] = (dfq * (math.log2(math.e) * DIFF_HEAD_DIM ** -0.5)).T.astype(BF16)
    pool_ref[0] = seg(OFF_POOL, POOL_W)
    fnet_ref[0] = seg(OFF_FNET, FNET_W)
    for j in range(0, N_BRANCH * D_MODEL, 512):
        gate_ref[0, :, j:j + 512] = jax.nn.sigmoid(seg(OFF_GATE + j, 512)).astype(BF16)
    nak = _group_rmsnorm(seg(OFF_KV, NA_W), bd64_ref, gk_ref[...])
    nakv_ref[0, :, 0:NA_W] = nak.astype(BF16)
    nakv_ref[0, :, NA_W:2 * NA_W] = seg(OFF_KV + NA_W, NA_W).astype(BF16)
    dfk = rope(_group_rmsnorm(seg(OFF_KV + 2 * NA_W, DIFF_QK_W), bd32_ref, gdk_ref[...]))
    dfk_ref[0] = dfk.astype(BF16)
    dfv_ref[0] = seg(OFF_KV + 2 * NA_W + DIFF_QK_W, DIFF_V_W).T.astype(BF16)


def _block_diag_mean(width, group):
    i = jnp.arange(width)
    return jnp.where((i[:, None] // group) == (i[None, :] // group), 1.0 / group, 0.0).astype(BF16)


def _rope_tables(T):
    t = jnp.arange(T)
    j = jnp.arange(LANES)
    jj = j % DIFF_HEAD_DIM
    quarter = DIFF_HEAD_DIM // 4
    use_row = jj < DIFF_HEAD_DIM // 2
    first = (jj % (DIFF_HEAD_DIM // 2)) < quarter
    inv = ROPE_BASE ** (-(jj % quarter).astype(F32) / quarter)
    pos = jnp.where(use_row[None, :], (t // GRID_W)[:, None], (t % GRID_W)[:, None]).astype(F32)
    ang = pos * inv[None, :]
    cos, sin = jnp.cos(ang), jnp.sin(ang)
    s_next = jnp.where(first[None, :], -sin, 0.0)
    s_prev = jnp.where(first[None, :], 0.0, sin)
    return cos, s_next, s_prev


def _inproj(x, g, sh, sc, w_bf16, gq, gk, gdq, gdk, rope, tm):
    B, T, D = x.shape
    use_rope = rope is not None
    row = lambda b, i: (b, i, 0)
    const2 = lambda b, i: (0, 0)
    perb = lambda b, i: (b, 0, 0)
    in_specs = [
        pl.BlockSpec((1, tm, D), row),
        pl.BlockSpec((1, D), const2),
        pl.BlockSpec((1, 1, D), perb),
        pl.BlockSpec((1, 1, D), perb),
        pl.BlockSpec((D, IN_COLS), const2),
        pl.BlockSpec((1, NA_W), const2), pl.BlockSpec((1, NA_W), const2),
        pl.BlockSpec((1, DIFF_QK_W), const2), pl.BlockSpec((1, DIFF_QK_W), const2),
        pl.BlockSpec((NA_W, NA_W), const2), pl.BlockSpec((DIFF_QK_W, DIFF_QK_W), const2),
    ]
    args = [x, g, sh, sc, w_bf16, gq, gk, gdq, gdk,
            _block_diag_mean(NA_W, NA_HEAD_DIM), _block_diag_mean(DIFF_QK_W, DIFF_HEAD_DIM)]
    if use_rope:
        in_specs += [pl.BlockSpec((tm, LANES), lambda b, i: (i, 0))] * 3
        args += list(rope)
    outs = [(NA_W, BF16, False), (DIFF_QK_W, BF16, True), (POOL_W, F32, False), (FNET_W, F32, False),
            (N_BRANCH * D_MODEL, BF16, False), (2 * NA_W, BF16, False), (DIFF_QK_W, BF16, False),
            (DIFF_V_W, BF16, True)]
    col = lambda b, i: (b, 0, i)
    out_shape = [jax.ShapeDtypeStruct((B, w, T) if tr else (B, T, w), dt) for w, dt, tr in outs]
    out_specs = [pl.BlockSpec((1, w, tm), col) if tr else pl.BlockSpec((1, tm, w), row) for w, _, tr in outs]
    return pl.pallas_call(
        functools.partial(_inproj_kernel, use_rope=use_rope),
        grid=(B, T // tm), in_specs=in_specs, out_specs=out_specs, out_shape=out_shape,
        compiler_params=_params("parallel", "parallel"), name="inproj",
    )(*args)


NA_ROWS_PER_STEP = 4


def _na_kernel(q_ref, kv0_ref, kv1_ref, kv2_ref, ckv_ref, bias_ref, o_ref):
    q = q_ref[0]
    kv = jnp.concatenate([kv0_ref[0], kv1_ref[0], kv2_ref[0]], axis=0)
    ckv = ckv_ref[0]
    for h in range(NA_HEADS):
        ks = slice(h * NA_HEAD_DIM, (h + 1) * NA_HEAD_DIM)
        vs = slice(NA_W + h * NA_HEAD_DIM, NA_W + (h + 1) * NA_HEAD_DIM)
        qh = q[:, ks]
        s = _dot_nt(qh, kv[:, ks]) + bias_ref[0, h]
        sc = _dot_nt(qh, ckv[:, ks])
        m = jnp.maximum(jnp.max(s, axis=-1, keepdims=True), jnp.max(sc, axis=-1, keepdims=True))
        p = jnp.exp(s - m)
        pc = jnp.exp(sc - m)
        l = jnp.sum(p, axis=-1, keepdims=True) + jnp.sum(pc, axis=-1, keepdims=True)
        o = _dot(p.astype(BF16), kv[:, vs]) + _dot(pc.astype(BF16), ckv[:, vs])
        o_ref[0, :, ks] = o / l


def _na_bias_table(rpb, rows):
    R = NA_ROWS_PER_STEP
    nb = rows // R
    col = jnp.arange(GRID_W)
    cs = jnp.clip(col - NA_WIN_C // 2, 0, GRID_W - NA_WIN_C)
    col_ok = (col[None, :] >= cs[:, None]) & (col[None, :] < cs[:, None] + NA_WIN_C)
    ci = jnp.clip(col[None, :] - col[:, None] + (NA_WIN_C - 1), 0, 2 * NA_WIN_C - 2)
    j = jnp.array([0, 1, nb - 1])[:, None, None]
    r = R * j + jnp.arange(R)[None, :, None]
    krow = R * (j - 1) + jnp.arange(3 * R)[None, None, :]
    rs = jnp.clip(r - NA_WIN_R // 2, 0, rows - NA_WIN_R)
    row_ok = (krow >= rs) & (krow < rs + NA_WIN_R) & (krow >= 0) & (krow < rows)
    ri = jnp.clip(krow - r + (NA_WIN_R - 1), 0, 2 * NA_WIN_R - 2)
    tab = rpb.astype(F32)[:, ri][:, :, :, :, ci]
    ok = row_ok[None, :, :, :, None, None] & col_ok[None, None, None, None]
    tab = jnp.where(ok, tab, NEG_BIG)
    tab = tab.transpose(1, 0, 2, 4, 3, 5)
    return tab.reshape(3, NA_HEADS, R * GRID_W, 3 * R * GRID_W)


def _na_latent(naq, nakv, cnakv, rpb):
    B, T, _ = naq.shape
    rows = T // GRID_W
    R = NA_ROWS_PER_STEP
    assert rows % R == 0 and rows // R >= 3 and rows >= NA_WIN_R and R + NA_WIN_R <= 3 * R
    nb = rows // R
    tb = R * GRID_W
    ctx_len = cnakv.shape[1]
    pattern = lambda j: (j > 0).astype(jnp.int32) + (j == nb - 1).astype(jnp.int32)
    kv_spec = lambda d: pl.BlockSpec((1, tb, 2 * NA_W), lambda b, j: (b, jnp.clip(j + d, 0, nb - 1), 0))
    return pl.pallas_call(
        _na_kernel, grid=(B, nb),
        in_specs=[
            pl.BlockSpec((1, tb, NA_W), lambda b, j: (b, j, 0)),
            kv_spec(-1), kv_spec(0), kv_spec(1),
            pl.BlockSpec((1, ctx_len, 2 * NA_W), lambda b, j: (b, 0, 0)),
            pl.BlockSpec((1, NA_HEADS, tb, 3 * tb), lambda b, j: (pattern(j), 0, 0, 0)),
        ],
        out_specs=pl.BlockSpec((1, tb, NA_W), lambda b, j: (b, j, 0)),
        out_shape=jax.ShapeDtypeStruct((B, T, NA_W), F32),
        compiler_params=_params("parallel", "parallel"), name="na_latent",
    )(naq, nakv, nakv, nakv, cnakv, _na_bias_table(rpb, rows))


DIFF_FIXED_SHIFT_MAX = 60.0


def _diff_kernel(shift_ref, lam_ref, g_ref, qt_ref, k_ref, vt_ref, o_ref, qx_sc, m_sc, l_sc, acc_sc, *,
                 lam_init, tk, nk, online):
    h = pl.program_id(1)
    qt = qt_ref[0]
    tq = qt.shape[1]
    grp = lax.broadcasted_iota(jnp.int32, qt.shape, 0) // DIFF_HEAD_DIM
    for mi in range(2):
        qx_sc[mi] = jnp.where(grp == 2 * h + mi, qt, jnp.zeros_like(qt))
    m_sc[...] = jnp.full(m_sc.shape, -jnp.inf, F32)
    l_sc[...] = jnp.zeros(l_sc.shape, F32)
    acc_sc[...] = jnp.zeros(acc_sc.shape, F32)

    def body(k, carry):
        off = pl.multiple_of(k * tk, tk)
        kb = k_ref[0, pl.ds(off, tk), :]
        vtb = vt_ref[0, :, pl.ds(off, tk)]
        for mi in range(2):
            s = _dot(kb, qx_sc[mi])
            if online:
                m_prev = m_sc[mi]
                m_new = jnp.maximum(m_prev, jnp.max(s, axis=0, keepdims=True))
                alpha = jnp.exp2(m_prev - m_new)
                p = jnp.exp2(s - m_new)
                l_sc[mi] = alpha * l_sc[mi] + p.reshape(tk // 8, 8, tq).sum(axis=0)
                acc_sc[mi] = alpha * acc_sc[mi] + _dot(vtb, p.astype(BF16))
                m_sc[mi] = m_new
            else:
                p = jnp.exp2(s - shift_ref[0])
                l_sc[mi] += p.reshape(tk // 8, 8, tq).sum(axis=0)
                acc_sc[mi] += _dot(vtb, p.astype(BF16))
        return carry

    lax.fori_loop(0, nk, body, 0)

    lv = lam_ref[...]
    lam = (jnp.exp(jnp.sum(lv[0:1] * lv[1:2], axis=-1, keepdims=True))
           - jnp.exp(jnp.sum(lv[2:3] * lv[3:4], axis=-1, keepdims=True)) + lam_init)
    l0 = jnp.sum(l_sc[0], axis=0, keepdims=True)
    l1 = jnp.sum(l_sc[1], axis=0, keepdims=True)
    o = acc_sc[0] / l0 - lam * (acc_sc[1] / l1)
    ms = jnp.mean(o * o, axis=0, keepdims=True)
    o_ref[0] = o * lax.rsqrt(ms + EPS) * g_ref[...] * (1.0 - lam_init)


def _diff_call(shift, qt, k, vt, lam_params, subln_g, lam_init, tq, tk, online):
    B, W, Tq = qt.shape
    Tk = k.shape[1]
    dv = W // DIFF_HEADS
    return pl.pallas_call(
        functools.partial(_diff_kernel, lam_init=lam_init, tk=tk, nk=Tk // tk, online=online),
        grid=(B, DIFF_HEADS, Tq // tq),
        in_specs=[
            pl.BlockSpec(memory_space=pltpu.SMEM),
            pl.BlockSpec((4, DIFF_HEAD_DIM), lambda b, h, i: (0, 0)),
            pl.BlockSpec((dv, 1), lambda b, h, i: (0, 0)),
            pl.BlockSpec((1, W, tq), lambda b, h, i: (b, 0, i)),
            pl.BlockSpec((1, Tk, W), lambda b, h, i: (b, 0, 0)),
            pl.BlockSpec((1, dv, Tk), lambda b, h, i: (b, h, 0)),
        ],
        out_specs=pl.BlockSpec((1, dv, tq), lambda b, h, i: (b, h, i)),
        out_shape=jax.ShapeDtypeStruct((B, W, Tq), F32),
        scratch_shapes=[pltpu.VMEM((2, W, tq), BF16), pltpu.VMEM((2, 1, tq), F32),
                        pltpu.VMEM((2, 8, tq), F32), pltpu.VMEM((2, dv, tq), F32)],
        compiler_params=_params("parallel", "parallel", "parallel"),
        name="diff_attn_online" if online else "diff_attn_fixed",
    )(shift, lam_params, subln_g, qt, k, vt)


def _diff_attn(qt, dfk, vt, gq, gk, lam_params, subln_g, lam_init, tq, tk):
    bound = (1.02 * math.log2(math.e) * DIFF_HEAD_DIM ** 0.5) * jnp.max(jnp.abs(gq)) * jnp.max(jnp.abs(gk))
    shift = bound.astype(F32).reshape(1)
    args = (shift, qt, dfk, vt, lam_params, subln_g, lam_init, tq, tk)
    return lax.cond(bound <= DIFF_FIXED_SHIFT_MAX,
                    lambda: _diff_call(*args, online=False), lambda: _diff_call(*args, online=True))


POOL_HALO = 8


def _pool_mixer(u, prev, nxt, wp_ref, scale, n_total):
    tm = u.shape[0]
    i = pl.program_id(1)
    prev = jnp.where(i > 0, prev, 0.0)
    nxt = jnp.where(i < pl.num_programs(1) - 1, nxt, 0.0)
    ext = jnp.concatenate([prev, u, nxt], axis=0)
    sh = lambda d: ext[POOL_HALO + d:POOL_HALO + d + tm]
    t = i * tm + lax.broadcasted_iota(jnp.int32, (tm, 1), 0)
    lane_grp = lax.broadcasted_iota(jnp.int32, (1, POOL_W), 1) // POOL_GROUP_W
    win = None
    mean = None
    prev_half = 0
    for gi, w in enumerate(POOL_WINDOWS):
        half = w // 2
        for d in range(prev_half, half):
            add = sh(d) + sh(-d - 1)
            win = add if win is None else win + add
        prev_half = half
        cnt = (jnp.minimum(t + half, n_total) - jnp.maximum(t - half, 0)).astype(F32)
        m_w = win / cnt
        mean = m_w if mean is None else jnp.where(lane_grp >= gi, m_w, mean)
    return _dot((mean - u).astype(BF16), wp_ref[...]) * scale


def _merge_kernel(x_ref, gt_ref, yna_ref, ydf_ref, u_ref, up_ref, un_ref, f_ref, gate_ref, wbr_ref, wp_ref,
                  ps_ref, wf_ref, wo_ref, o_ref, *, n_total):
    yf = _dot(f_ref[0].astype(BF16), wf_ref[...])
    ypool = _pool_mixer(u_ref[0], up_ref[0], un_ref[0], wp_ref, ps_ref[...], n_total)
    ys = (yna_ref[0], ydf_ref[0].T, ypool, yf)
    acc = None
    for i, y in enumerate(ys):
        g = gate_ref[0, :, i * D_MODEL:(i + 1) * D_MODEL].astype(F32)
        term = g * _dot(y.astype(BF16), wbr_ref[i])
        acc = term if acc is None else acc + term
    o_ref[0] = x_ref[0] + gt_ref[0] * _dot(acc.astype(BF16), wo_ref[...])


def _merge(x, gt, y_na, y_df, pool_in, f_real, gate, w_br, w_pool_bd, pool_scale, w_f, w_o, tm):
    B, T, D = x.shape
    assert tm % POOL_HALO == 0 and POOL_HALO >= max(POOL_WINDOWS) // 2
    hb = tm // POOL_HALO
    row = lambda b, i: (b, i, 0)
    const2 = lambda b, i: (0, 0)
    return pl.pallas_call(
        functools.partial(_merge_kernel, n_total=T), grid=(B, T // tm),
        in_specs=[
            pl.BlockSpec((1, tm, D), row),
            pl.BlockSpec((1, 1, D), lambda b, i: (b, 0, 0)),
            pl.BlockSpec((1, tm, BRANCH_W), row), pl.BlockSpec((1, BRANCH_W, tm), lambda b, i: (b, 0, i)),
            pl.BlockSpec((1, tm, POOL_W), row),
            pl.BlockSpec((1, POOL_HALO, POOL_W), lambda b, i: (b, jnp.maximum(i * hb - 1, 0), 0)),
            pl.BlockSpec((1, POOL_HALO, POOL_W), lambda b, i: (b, jnp.minimum((i + 1) * hb, T // POOL_HALO - 1), 0)),
            pl.BlockSpec((1, tm, FNET_W), row),
            pl.BlockSpec((1, tm, N_BRANCH * D), row),
            pl.BlockSpec((N_BRANCH, BRANCH_W, D), lambda b, i: (0, 0, 0)),
            pl.BlockSpec((POOL_W, POOL_W), const2), pl.BlockSpec((1, POOL_W), const2),
            pl.BlockSpec((FNET_W, FNET_W), const2),
            pl.BlockSpec((D, D), const2),
        ],
        out_specs=pl.BlockSpec((1, tm, D), row),
        out_shape=jax.ShapeDtypeStruct((B, T, D), F32),
        compiler_params=_params("parallel", "parallel"), name="merge",
    )(x, gt, y_na, y_df, pool_in, pool_in, pool_in, f_real, gate, w_br, w_pool_bd, pool_scale, w_f, w_o)


TOK_BLOCK = LANES
SEL_GROUP = 8
COMBINE_WIN = 48
BF16_ROWS = 16
SLOT_SPLIT = 64.0


def _router_kernel(x_ref, g_ref, sh_ref, sc_ref, wrh_ref, wrl_ref, h_ref, aff_ref):
    x = x_ref[0]
    ms = jnp.mean(x * x, axis=-1, keepdims=True)
    h = x * lax.rsqrt(ms + EPS) * g_ref[...] * (1.0 + sc_ref[0]) + sh_ref[0]
    h_ref[0] = h.astype(BF16)
    hi, lo = _split_bf16(h)
    logits = _dot_nt(wrh_ref[...], hi) + _dot_nt(wrh_ref[...], lo) + _dot_nt(wrl_ref[...], hi)
    logits = logits - jnp.max(logits, axis=0, keepdims=True)
    e = jnp.exp(logits)
    aff = e / jnp.sum(e, axis=0, keepdims=True)
    for j in range(aff.shape[1] // TOK_BLOCK):
        aff_ref[0, :, j, :] = aff[:, j * TOK_BLOCK:(j + 1) * TOK_BLOCK]


def _router(x, g, sh, sc, w_router, tm):
    B, T, D = x.shape
    E = w_router.shape[1]
    wrh, wrl = _split_bf16(w_router.T)
    row = lambda b, i: (b, i, 0)
    return pl.pallas_call(
        _router_kernel, grid=(B, T // tm),
        in_specs=[
            pl.BlockSpec((1, tm, D), row), pl.BlockSpec((1, D), lambda b, i: (0, 0)),
            pl.BlockSpec((1, 1, D), lambda b, i: (b, 0, 0)), pl.BlockSpec((1, 1, D), lambda b, i: (b, 0, 0)),
            pl.BlockSpec((E, D), lambda b, i: (0, 0)), pl.BlockSpec((E, D), lambda b, i: (0, 0)),
        ],
        out_specs=[pl.BlockSpec((1, tm, D), row),
                   pl.BlockSpec((1, E, tm // TOK_BLOCK, TOK_BLOCK), lambda b, i: (b, 0, i, 0))],
        out_shape=[jax.ShapeDtypeStruct((B, T, D), BF16),
                   jax.ShapeDtypeStruct((B, E, T // TOK_BLOCK, TOK_BLOCK), F32)],
        compiler_params=_params("parallel", "parallel"), name="router",
    )(x, g, sh, sc, wrh, wrl)


def _select_kernel(aff_ref, u_ref, ls_ref, idx_ref, gate_ref, pos_ref, off_ref, tot_ref, *, cap):
    G, nb, _ = aff_ref.shape
    aff = aff_ref[...]
    bits = pltpu.bitcast(aff, jnp.int32)
    kf = float(cap)
    ones_f = lambda m: jnp.where(m, 1.0, 0.0)
    count = lambda m: jnp.sum(ones_f(m), axis=(1, 2), keepdims=True)

    def search(it, lo):
        cand = lo | jnp.left_shift(jnp.int32(1), 30 - it)
        return jnp.where(count(bits >= cand) >= kf, cand, lo)

    thr = lax.fori_loop(0, 31, search, jnp.zeros((G, 1, 1), jnp.int32))
    need = kf - count(bits > thr)

    u = u_ref[...]
    ls = ls_ref[...]
    ones_m = jnp.ones((LANES, LANES), BF16)

    def prefix(mf):
        mb = mf.astype(BF16)
        tot = _dot(mb, ones_m)
        off = _dot(ls, tot.astype(BF16))
        return _dot(mb, u) + off, off, tot

    blk_i = lax.broadcasted_iota(jnp.int32, (nb, LANES), 0)
    lane_i = lax.broadcasted_iota(jnp.int32, (nb, LANES), 1)
    s_row = lax.broadcasted_iota(jnp.int32, (1, cap), 1).astype(F32)
    col_nb = lax.broadcasted_iota(jnp.int32, (nb, 1), 0).astype(F32)
    col_l = lax.broadcasted_iota(jnp.int32, (LANES, 1), 0).astype(F32)
    for g in range(G):
        gt = bits[g] > thr[g]
        eq = bits[g] == thr[g]
        eqf = ones_f(eq)
        tie_before = prefix(eqf)[0] - eqf
        sel = gt | (eq & (tie_before < need[g]))
        pin, off, tot = prefix(ones_f(sel))
        pos_ref[g] = jnp.where(sel, pin - 1.0, -1.0).astype(jnp.int32)
        diag = blk_i == lane_i
        off_ref[g] = jnp.sum(jnp.where(diag, off, 0.0), axis=0, keepdims=True).astype(jnp.int32)
        tot_ref[g] = jnp.sum(jnp.where(diag, tot, 0.0), axis=0, keepdims=True).astype(jnp.int32)
        blk = jnp.sum(ones_f((off + tot)[:, 0:1] <= s_row), axis=0, keepdims=True)
        oh = jnp.where(col_nb == blk, 1.0, 0.0).astype(BF16)
        pin_t = pin.T
        p_hi = jnp.floor(pin_t * (1.0 / LANES))
        p_lo = pin_t - p_hi * LANES
        rows = _dot(p_hi.astype(BF16), oh) * LANES + _dot(p_lo.astype(BF16), oh)
        lane = jnp.sum(ones_f(rows <= s_row), axis=0, keepdims=True)
        idx_ref[g] = (blk * LANES + lane).astype(jnp.int32)
        a_t = aff[g].T
        a1 = a_t.astype(BF16)
        r1 = a_t - a1.astype(F32)
        a2 = r1.astype(BF16)
        a3 = (r1 - a2.astype(F32)).astype(BF16)
        arow = _dot(a1, oh) + _dot(a2, oh) + _dot(a3, oh)
        gate_ref[g] = jnp.sum(jnp.where(col_l == lane, arow, 0.0), axis=0, keepdims=True)


def _select(aff, cap):
    B, E, nb, _ = aff.shape
    R = B * E
    G = SEL_GROUP
    i = np.arange(LANES)
    u = jnp.asarray(i[:, None] <= i[None, :], BF16)
    j = np.arange(nb)
    ls = jnp.asarray(j[None, :] < j[:, None], BF16)
    grp = lambda r: (r, 0, 0)
    outs = pl.pallas_call(
        functools.partial(_select_kernel, cap=cap), grid=(R // G,),
        in_specs=[pl.BlockSpec((G, nb, LANES), grp), pl.BlockSpec((LANES, LANES), lambda r: (0, 0)),
                  pl.BlockSpec((nb, nb), lambda r: (0, 0))],
        out_specs=[pl.BlockSpec((G, 1, cap), grp), pl.BlockSpec((G, 1, cap), grp),
                   pl.BlockSpec((G, nb, LANES), grp), pl.BlockSpec((G, 1, LANES), grp),
                   pl.BlockSpec((G, 1, LANES), grp)],
        out_shape=[jax.ShapeDtypeStruct((R, 1, cap), jnp.int32), jax.ShapeDtypeStruct((R, 1, cap), F32),
                   jax.ShapeDtypeStruct((R, nb, LANES), jnp.int32), jax.ShapeDtypeStruct((R, 1, LANES), jnp.int32),
                   jax.ShapeDtypeStruct((R, 1, LANES), jnp.int32)],
        compiler_params=_params("parallel"), name="select",
    )(aff.reshape(R, nb, LANES), u, ls)
    idx, gate, pos, off, tot = outs
    return (idx.reshape(B, E, cap), gate.reshape(B, E, cap), pos.reshape(B, E, nb, LANES),
            off[:, 0, :nb].reshape(B, E, nb), tot[:, 0, :nb].reshape(B, E, nb))


def _combine_kernel(off_ref, tot_ref, x_ref, gt_ref, pos_ref, ex_ref, ye_hbm, o_ref, buf, xbuf, acc_sc, sem,
                    xsem, *, nt, cap, W):
    n = pl.program_id(0)
    E = N_EXPERTS

    def window(step, e):
        b = step // nt
        o = off_ref[(b * E + e) * nt + step % nt]
        return b, o, jnp.minimum((o // BF16_ROWS) * BF16_ROWS, cap - W)

    def fetch(step, slot):
        for e in range(E):
            b, _, st = window(step, e)
            pltpu.make_async_copy(ye_hbm.at[b, e, pl.ds(pl.multiple_of(st, BF16_ROWS), W)], buf.at[slot, e],
                                  sem.at[slot]).start()

    @pl.when(n == 0)
    def _():
        fetch(0, 0)

    @pl.when(n + 1 < pl.num_programs(0))
    def _():
        fetch(n + 1, (n + 1) % 2)

    slot = n % 2
    for e in range(E):
        pltpu.make_async_copy(ye_hbm.at[0, 0, pl.ds(0, W)], buf.at[slot, e], sem.at[slot]).wait()

    pos = pos_ref[0, :, (n % nt) % pos_ref.shape[2], :].astype(F32).T
    p_hi = jnp.floor(pos * (1.0 / SLOT_SPLIT))
    p_lo = pos - SLOT_SPLIT * p_hi
    ex = ex_ref[...]
    posx = SLOT_SPLIT * _dot(p_hi.astype(BF16), ex) + _dot(p_lo.astype(BF16), ex)
    col = lax.broadcasted_iota(jnp.int32, (1, E * W), 1)
    tgt = jnp.zeros((1, E * W), jnp.int32)
    for e in range(E):
        _, _, st = window(n, e)
        tgt = jnp.where(col // W == e, st + col % W, tgt)
    onehot = jnp.where(posx == tgt.astype(F32), 1.0, 0.0).astype(BF16)
    acc_sc[...] = _dot(onehot, buf[slot].reshape(E * W, buf.shape[-1]))

    w_i = lax.broadcasted_iota(jnp.int32, (1, W), 1)
    for e in range(E):
        b, o, st = window(n, e)
        t = tot_ref[(b * E + e) * nt + n % nt]
        n_extra = jnp.maximum(o + t - (st + W) + (W - 1), 0) // W

        def extra(j, carry, e=e, b=b, st=st):
            base = st + W * (j + 1)
            src = jnp.minimum(base, cap - W)
            cp = pltpu.make_async_copy(ye_hbm.at[b, e, pl.ds(pl.multiple_of(src, BF16_ROWS), W)], xbuf,
                                       xsem.at[0])
            cp.start()
            cp.wait()
            slot_i = (src + w_i).astype(F32)
            oh = jnp.where((pos[:, e:e + 1] == slot_i) & (slot_i >= base.astype(F32)), 1.0, 0.0).astype(BF16)
            acc_sc[...] += _dot(oh, xbuf[...])
            return carry

        lax.fori_loop(0, n_extra, extra, 0)

    o_ref[0] = x_ref[0] + gt_ref[0] * acc_sc[...]


def _combine(x, gt, pos, off, tot, ye):
    B, T, D = x.shape
    E, cap = ye.shape[1], ye.shape[2]
    nt = T // TOK_BLOCK
    W = min(COMBINE_WIN, cap)
    pr = min(8, nt)
    assert nt % pr == 0
    assert W % BF16_ROWS == 0 and (cap - W) % BF16_ROWS == 0 and (E * W) % LANES == 0
    expand = jnp.asarray(np.arange(E)[:, None] == (np.arange(E * W)[None, :] // W), BF16)
    tile = lambda n, off_r, tot_r: (n // nt, n % nt, 0)
    grid_spec = pltpu.PrefetchScalarGridSpec(
        num_scalar_prefetch=2, grid=(B * nt,),
        in_specs=[
            pl.BlockSpec((1, TOK_BLOCK, D), tile),
            pl.BlockSpec((1, 1, D), lambda n, off_r, tot_r: (n // nt, 0, 0)),
            pl.BlockSpec((1, E, pr, TOK_BLOCK), lambda n, off_r, tot_r: (n // nt, 0, (n % nt) // pr, 0)),
            pl.BlockSpec((E, E * W), lambda n, off_r, tot_r: (0, 0)),
            pl.BlockSpec(memory_space=pl.ANY),
        ],
        out_specs=pl.BlockSpec((1, TOK_BLOCK, D), tile),
        scratch_shapes=[pltpu.VMEM((2, E, W, D), BF16), pltpu.VMEM((W, D), BF16),
                        pltpu.VMEM((TOK_BLOCK, D), F32),
                        pltpu.SemaphoreType.DMA((2,)), pltpu.SemaphoreType.DMA((1,))],
    )
    return pl.pallas_call(
        functools.partial(_combine_kernel, nt=nt, cap=cap, W=W), grid_spec=grid_spec,
        out_shape=jax.ShapeDtypeStruct((B, T, D), F32),
        compiler_params=_params("arbitrary"), name="combine",
    )(off.reshape(-1), tot.reshape(-1), x, gt, pos, expand, ye)


FF_CHUNKS = ((0, 512), (512, 512), (1024, 384))


def _expert_kernel(xe_ref, g_ref, wg_ref, wu_ref, wd_ref, o_ref):
    xe = xe_ref[0, 0]
    acc = None
    for lo, n in FF_CHUNKS:
        a = _dot(xe, wg_ref[0, 0, :, lo:lo + n].astype(BF16))
        u = _dot(xe, wu_ref[0, 0, :, lo:lo + n].astype(BF16))
        hid = (a * jax.nn.sigmoid(a) * u).astype(BF16)
        part = _dot(hid, wd_ref[0, 0, lo:lo + n, :].astype(BF16))
        acc = part if acc is None else acc + part
    o_ref[0, 0] = (acc * g_ref[0, 0]).astype(o_ref.dtype)


def _experts(xe, g, wg, wu, wd, layer, tm):
    B, E, cap, D = xe.shape
    F = wg.shape[-1]
    assert F == EXPERT_FF
    tile = lambda e, b, i: (b, e, i, 0)
    wsel = lambda e, b, i: (layer, e, 0, 0)
    return pl.pallas_call(
        _expert_kernel, grid=(E, B, cap // tm),
        in_specs=[
            pl.BlockSpec((1, 1, tm, D), tile), pl.BlockSpec((1, 1, tm, 1), tile),
            pl.BlockSpec((1, 1, D, F), wsel), pl.BlockSpec((1, 1, D, F), wsel), pl.BlockSpec((1, 1, F, D), wsel),
        ],
        out_specs=pl.BlockSpec((1, 1, tm, D), tile),
        out_shape=jax.ShapeDtypeStruct((B, E, cap, D), BF16),
        compiler_params=_params("parallel", "parallel", "parallel"), name="experts",
    )(xe, g, wg, wu, wd)


def _ec_ffn(x, gt, g, sh, sc, w_router, wg, wu, wd, layer, tm_router, tm_expert):
    B, N, D = x.shape
    cap = max(1, EC_FACTOR * N // N_EXPERTS)
    assert N <= TOK_BLOCK * LANES and N % TOK_BLOCK == 0
    nt = N // TOK_BLOCK
    h, aff = _router(x, g, sh, sc, w_router, tm_router)
    aff = jnp.pad(aff, ((0, 0), (0, 0), (0, LANES - nt), (0, 0)))
    idx, gsel, pos, off, tot = _select(aff, cap)
    xe = h[jnp.arange(B)[:, None, None], idx]
    ye = _experts(xe, gsel[..., None], wg, wu, wd, layer, min(tm_expert, cap))
    return _combine(x, gt, pos[:, :, :nt], off[:, :, :nt], tot[:, :, :nt], ye)


DFT_RADIX = 128


def _dot3(a_hi, a_lo, b_hi, b_lo):
    return _dot(a_hi, b_hi) + _dot(a_lo, b_hi) + _dot(a_hi, b_lo)


def _cos_sin(n_rows, n_cols, period):
    k = (np.arange(n_rows)[:, None] * np.arange(n_cols)[None, :]) % period
    ang = 2.0 * np.pi * k.astype(np.float64) / period
    return np.cos(ang), np.sin(ang)


def _const_split(m):
    return _split_bf16(jnp.asarray(m, F32))


def _chan_dft_kernel(u_ref, mh_ref, ml_ref, xr_ref, xi_ref):
    uh, ul = _split_bf16(u_ref[0])
    y = _dot3(uh, ul, mh_ref[...], ml_ref[...])
    c = xr_ref.shape[-1]
    xr_ref[0] = y[:, :c]
    xi_ref[0] = y[:, c:]


def _chan_dft(u, tm):
    B, N, C = u.shape
    cc, sc = _cos_sin(C, C, C)
    mh, ml = _const_split(np.concatenate([cc, -sc], axis=1))
    row = lambda b, i: (b, i, 0)
    const2 = lambda b, i: (0, 0)
    return pl.pallas_call(
        _chan_dft_kernel, grid=(B, N // tm),
        in_specs=[pl.BlockSpec((1, tm, C), row), pl.BlockSpec((C, 2 * C), const2), pl.BlockSpec((C, 2 * C), const2)],
        out_specs=[pl.BlockSpec((1, tm, C), row)] * 2,
        out_shape=[jax.ShapeDtypeStruct((B, N, C), F32)] * 2,
        compiler_params=_params("parallel", "parallel"), name="chan_dft",
    )(u, mh, ml)


def _dft_left_kernel(xr_ref, xi_ref, mh_ref, ml_ref, o_ref, *, scale):
    xh, xl = _split_bf16(jnp.concatenate([xr_ref[0], xi_ref[0]], axis=0))
    o_ref[0] = _dot3(mh_ref[...], ml_ref[...], xh, xl) * scale


def _dft_left(xr, xi, m, scale, tn):
    B, K, cols = xr.shape
    R = m.shape[0]
    mh, ml = _const_split(m)
    col = lambda b, j: (b, 0, j)
    const2 = lambda b, j: (0, 0)
    return pl.pallas_call(
        functools.partial(_dft_left_kernel, scale=scale), grid=(B, cols // tn),
        in_specs=[pl.BlockSpec((1, K, tn), col), pl.BlockSpec((1, K, tn), col),
                  pl.BlockSpec((R, 2 * K), const2), pl.BlockSpec((R, 2 * K), const2)],
        out_specs=pl.BlockSpec((1, R, tn), col),
        out_shape=jax.ShapeDtypeStruct((B, R, cols), F32),
        compiler_params=_params("parallel", "parallel"), name="dft_left",
    )(xr, xi, mh, ml)


DFT_N2_PER_STEP = 8


def _dft_outer_kernel(xr_ref, xi_ref, mh_ref, ml_ref, o_ref):
    n1 = xr_ref.shape[1]
    for s in range(DFT_N2_PER_STEP):
        xh, xl = _split_bf16(jnp.concatenate([xr_ref[0, :, s, :], xi_ref[0, :, s, :]], axis=0))
        y = _dot3(mh_ref[...], ml_ref[...], xh, xl)
        o_ref[0, 0, :, s, :] = y[:n1]
        o_ref[0, 1, :, s, :] = y[n1:]


def _dft_outer(xr, xi, m):
    B, n1, n2, C = xr.shape
    mh, ml = _const_split(m)
    blk = lambda b, j: (b, 0, j, 0)
    const2 = lambda b, j: (0, 0)
    return pl.pallas_call(
        _dft_outer_kernel, grid=(B, n2 // DFT_N2_PER_STEP),
        in_specs=[pl.BlockSpec((1, n1, DFT_N2_PER_STEP, C), blk), pl.BlockSpec((1, n1, DFT_N2_PER_STEP, C), blk),
                  pl.BlockSpec((2 * n1, 2 * n1), const2), pl.BlockSpec((2 * n1, 2 * n1), const2)],
        out_specs=pl.BlockSpec((1, 2, n1, DFT_N2_PER_STEP, C), lambda b, j: (b, 0, 0, j, 0)),
        out_shape=jax.ShapeDtypeStruct((B, 2, n1, n2, C), F32),
        compiler_params=_params("parallel", "parallel"), name="dft_outer",
    )(xr, xi, mh, ml)


DFT_K1_PER_STEP = 8


def _dft_twiddle_kernel(y_ref, tc_ref, ts_ref, dh_ref, dl_ref, o_ref, *, scale):
    tc = tc_ref[0]
    ts = ts_ref[0]
    for kk in range(DFT_K1_PER_STEP):
        yr = y_ref[0, 0, kk]
        yi = y_ref[0, 1, kk]
        c, s = tc[:, kk:kk + 1], ts[:, kk:kk + 1]
        zh, zl = _split_bf16(jnp.concatenate([yr * c + yi * s, yi * c - yr * s], axis=0))
        o_ref[0, :, kk, :] = _dot3(dh_ref[...], dl_ref[...], zh, zl) * scale


def _fourier_real(u):
    B, N, C = u.shape
    scale = 1.0 / math.sqrt(N * C)
    xr, xi = _chan_dft(u, min(N, 512))
    if N % (DFT_RADIX * DFT_K1_PER_STEP) != 0:
        assert N <= 1024
        cn, sn = _cos_sin(N, N, N)
        return _dft_left(xr, xi, np.concatenate([cn, sn], axis=1), scale, C)
    n1, n2 = N // DFT_RADIX, DFT_RADIX
    c1, s1 = _cos_sin(n1, n1, n1)
    m1 = np.block([[c1, s1], [-s1, c1]])
    yy = _dft_outer(xr.reshape(B, n1, n2, C), xi.reshape(B, n1, n2, C), m1)
    tcos, tsin = _cos_sin(n1, n2, N)
    grp = lambda t: jnp.asarray(t.reshape(n1 // DFT_K1_PER_STEP, DFT_K1_PER_STEP, n2).transpose(0, 2, 1), F32)
    c2, s2 = _cos_sin(n2, n2, n2)
    dh, dl = _const_split(np.concatenate([c2, s2], axis=1))
    fp = pl.pallas_call(
        functools.partial(_dft_twiddle_kernel, scale=scale), grid=(B, n1 // DFT_K1_PER_STEP),
        in_specs=[
            pl.BlockSpec((1, 2, DFT_K1_PER_STEP, n2, C), lambda b, g: (b, 0, g, 0, 0)),
            pl.BlockSpec((1, n2, DFT_K1_PER_STEP), lambda b, g: (g, 0, 0)),
            pl.BlockSpec((1, n2, DFT_K1_PER_STEP), lambda b, g: (g, 0, 0)),
            pl.BlockSpec((n2, 2 * n2), lambda b, g: (0, 0)), pl.BlockSpec((n2, 2 * n2), lambda b, g: (0, 0)),
        ],
        out_specs=pl.BlockSpec((1, n2, DFT_K1_PER_STEP, C), lambda b, g: (b, 0, g, 0)),
        out_shape=jax.ShapeDtypeStruct((B, n2, n1, C), F32),
        compiler_params=_params("parallel", "parallel"), name="dft_twiddle",
    )(yy, grp(tcos), grp(tsin), dh, dl)
    return fp.reshape(B, N, C)


def _ctx_attn_kernel(q_ref, kv_ref, o_ref):
    q = q_ref[0]
    kv = kv_ref[0]
    for h in range(NA_HEADS):
        ks = slice(h * NA_HEAD_DIM, (h + 1) * NA_HEAD_DIM)
        vs = slice(NA_W + h * NA_HEAD_DIM, NA_W + (h + 1) * NA_HEAD_DIM)
        s = _dot_nt(q[:, ks], kv[:, ks])
        p = jnp.exp(s - jnp.max(s, axis=-1, keepdims=True))
        l = jnp.sum(p, axis=-1, keepdims=True)
        o_ref[0, :, ks] = _dot(p.astype(BF16), kv[:, vs]) / l


def _ctx_dense_attn(q, kv):
    B, Q, _ = q.shape
    return pl.pallas_call(
        _ctx_attn_kernel, grid=(B,),
        in_specs=[pl.BlockSpec((1, Q, NA_W), lambda b: (b, 0, 0)),
                  pl.BlockSpec((1, kv.shape[1], 2 * NA_W), lambda b: (b, 0, 0))],
        out_specs=pl.BlockSpec((1, Q, NA_W), lambda b: (b, 0, 0)),
        out_shape=jax.ShapeDtypeStruct((B, Q, NA_W), F32),
        compiler_params=_params("parallel"), name="ctx_attn",
    )(q, kv)


ADA_ROWS = 8
ADA_TN = 512


def _ada_kernel(c_ref, w_ref, b_ref, o_ref):
    c = c_ref[...]
    sh, sl = _split_bf16(c * jax.nn.sigmoid(c))
    wh, wl = _split_bf16(w_ref[0])
    o_ref[...] = _dot3(sh, sl, wh, wl) + b_ref[0]


def _ada_mod(c_rows, w_ada, b_ada, layer):
    R, D = c_rows.shape
    N = w_ada.shape[-1]
    return pl.pallas_call(
        _ada_kernel, grid=(N // ADA_TN,),
        in_specs=[pl.BlockSpec((R, D), lambda j: (0, 0)),
                  pl.BlockSpec((1, D, ADA_TN), lambda j: (layer, 0, j)),
                  pl.BlockSpec((1, 1, ADA_TN), lambda j: (layer, 0, j))],
        out_specs=pl.BlockSpec((R, ADA_TN), lambda j: (0, j)),
        out_shape=jax.ShapeDtypeStruct((R, N), F32),
        compiler_params=_params("parallel"), name="ada_mod",
    )(c_rows, w_ada, b_ada[:, None, :])


def _tile(g, n):
    return jnp.tile(g.astype(F32), n)[None, :]


def kernel(x, c, ctx, c_ctx, w_ada, b_ada, g_mix, g_ffn, w_in, na_q_g, na_k_g, na_rpb, df_q_g, df_k_g,
           df_lambda, df_subln_g, pool_w, pool_scale, fnet_w, w_branch, w_out, w_router, w_gate_e, w_up_e,
           w_down_e):
    B, T, D = x.shape
    ctx_len = ctx.shape[1]
    rope = _rope_tables(T)
    assert B + 1 <= ADA_ROWS
    c_rows = jnp.concatenate([c, c_ctx[None, :], jnp.zeros((ADA_ROWS - B - 1, D), F32)], axis=0)
    tq = min(DIFF_TQ, T)
    for l in range(DEPTH):
        last = l == DEPTH - 1
        lam_init = 0.8 - 0.6 * math.exp(-0.3 * l)
        mods = _ada_mod(c_rows, w_ada, b_ada, l)
        sh1, sc1, gt1, sh2, sc2, gt2 = [m[:, None, :] for m in jnp.split(mods[:B], 6, axis=-1)]
        bc = lambda m: jnp.broadcast_to(m[None, None, :], (B, 1, D))
        csh1, csc1, cgt1, csh2, csc2, cgt2 = [bc(m) for m in jnp.split(mods[B], 6, axis=-1)]

        w_bf = w_in[l].astype(BF16)
        gq, gk = _tile(na_q_g[l], NA_HEADS), _tile(na_k_g[l], NA_HEADS)
        gdq, gdk = _tile(df_q_g[l], 2 * DIFF_HEADS), _tile(df_k_g[l], 2 * DIFF_HEADS)
        gmix = g_mix[l][None, :]
        gffn = g_ffn[l][None, :]
        subg = df_subln_g[l][:, None].astype(F32)
        wbr = w_branch[l].astype(BF16)
        wf = fnet_w[l].astype(BF16)
        wpool = jax.scipy.linalg.block_diag(*[pool_w[l, gi] for gi in range(len(POOL_WINDOWS))]).astype(BF16)
        pscale = pool_scale[l][None, :].astype(F32)
        wo = w_out[l].astype(BF16)

        naq, dfq, pool_in, fnet_in, gate, nakv, dfk, dfv = _inproj(
            x, gmix, sh1, sc1, w_bf, gq, gk, gdq, gdk, rope, min(INPROJ_ROWS, T))
        (cnaq, cdfq, cpool_in, cfnet_in, cgate, cnakv, cdfk, cdfv) = _inproj(
            ctx, gmix, csh1, csc1, w_bf, gq, gk, gdq, gdk, None, ctx_len)

        y_na = _na_latent(naq, nakv, cnakv, na_rpb[l])
        lamp = df_lambda[l].astype(F32)
        y_df = _diff_attn(dfq, jnp.concatenate([dfk, cdfk], axis=1), jnp.concatenate([dfv, cdfv], axis=2),
                          df_q_g[l], df_k_g[l], lamp, subg, lam_init, tq, DIFF_TK)
        f_real = _fourier_real(fnet_in)
        x_new = _merge(x, gt1, y_na, y_df, pool_in, f_real, gate, wbr, wpool, pscale, wf, wo, MERGE_ROWS)

        if not last:
            yc_na = _ctx_dense_attn(cnaq, cnakv)
            yc_df = _diff_attn(cdfq, cdfk, cdfv, df_q_g[l], df_k_g[l], lamp, subg, lam_init, ctx_len, ctx_len)
            fc_real = _fourier_real(cfnet_in)
            ctx_new = _merge(ctx, cgt1, yc_na, yc_df, cpool_in, fc_real, cgate, wbr, wpool, pscale, wf, wo,
                             ctx_len)

        x = x_new
        x = _ec_ffn(x, gt2, gffn, sh2, sc2, w_router[l], w_gate_e, w_up_e, w_down_e, l, min(ROUTER_ROWS, T),
                    EXPERT_ROWS)
        if not last:
            ctx = ctx_new
            ctx = _ec_ffn(ctx, cgt2, gffn, csh2, csc2, w_router[l], w_gate_e, w_up_e, w_down_e, l, ctx_len,
                          EXPERT_ROWS)
    return x
```

```python
import functools
import math

import jax
import jax.numpy as jnp
import numpy as np
from jax import lax
from jax.experimental import pallas as pl
from jax.experimental.pallas import tpu as pltpu

F32 = jnp.float32
BF16 = jnp.bfloat16

D_MODEL = 1024
DEPTH = 2
GRID_W = 64
EPS = 1e-6
ROPE_BASE = 10000.0

NA_HEADS = 4
NA_HEAD_DIM = 64
NA_WIN_R = 8
NA_WIN_C = 16
NA_W = NA_HEADS * NA_HEAD_DIM

DIFF_HEADS = 4
DIFF_HEAD_DIM = 32
DIFF_QK_W = DIFF_HEADS * 2 * DIFF_HEAD_DIM
DIFF_V_W = DIFF_HEADS * 2 * DIFF_HEAD_DIM

POOL_WINDOWS = (2, 4, 8, 16)
POOL_GROUP_W = 64
POOL_W = len(POOL_WINDOWS) * POOL_GROUP_W
FNET_W = 256
N_BRANCH = 4
BRANCH_W = 256

OFF_NA_Q = 0
OFF_DF_Q = OFF_NA_Q + NA_W
OFF_POOL = OFF_DF_Q + DIFF_QK_W
OFF_FNET = OFF_POOL + POOL_W
OFF_GATE = OFF_FNET + FNET_W
OFF_KV = OFF_GATE + N_BRANCH * D_MODEL
KV_W = 2 * NA_W + DIFF_QK_W + DIFF_V_W
IN_COLS = OFF_KV + KV_W

N_EXPERTS = 16
EC_FACTOR = 2
EXPERT_FF = 1408

VMEM_LIMIT_BYTES = 56 * 1024 * 1024
LANES = 128
NEG_BIG = -1e30

INPROJ_ROWS = 512
MERGE_ROWS = 512
ROUTER_ROWS = 1024
EXPERT_ROWS = 512
DIFF_TQ, DIFF_TK = 2048, 1280


def _params(*sem):
    return pltpu.CompilerParams(dimension_semantics=sem, vmem_limit_bytes=VMEM_LIMIT_BYTES)


def _split_bf16(a):
    hi = a.astype(BF16)
    lo = (a - hi.astype(F32)).astype(BF16)
    return hi, lo


def _dot(a, b):
    return jnp.dot(a, b, preferred_element_type=F32)


def _dot_nt(a, b):
    return lax.dot_general(a, b, (((1,), (1,)), ((), ())), preferred_element_type=F32)


def _group_rmsnorm(p, bd_ref, g):
    hi, lo = _split_bf16(p * p)
    ms = _dot(hi, bd_ref[...]) + _dot(lo, bd_ref[...])
    return p * lax.rsqrt(ms + EPS) * g


def _rope256(y, cos, s_next, s_prev):
    outs = []
    for half in range(2):
        z = y[:, half * LANES:(half + 1) * LANES]
        outs.append(z * cos + pltpu.roll(z, LANES - 8, 1) * s_next + pltpu.roll(z, 8, 1) * s_prev)
    return jnp.concatenate(outs, axis=1)


def _inproj_kernel(*refs, use_rope):
    if use_rope:
        (x_ref, g_ref, sh_ref, sc_ref, w_ref, gq_ref, gk_ref, gdq_ref, gdk_ref, bd64_ref, bd32_ref,
         cos_ref, sn_ref, sp_ref,
         naq_ref, dfq_ref, pool_ref, fnet_ref, gate_ref, nakv_ref, dfk_ref, dfv_ref) = refs
    else:
        (x_ref, g_ref, sh_ref, sc_ref, w_ref, gq_ref, gk_ref, gdq_ref, gdk_ref, bd64_ref, bd32_ref,
         naq_ref, dfq_ref, pool_ref, fnet_ref, gate_ref, nakv_ref, dfk_ref, dfv_ref) = refs
    x = x_ref[0]
    ms = jnp.mean(x * x, axis=-1, keepdims=True)
    y = x * lax.rsqrt(ms + EPS) * g_ref[...]
    h = (y * (1.0 + sc_ref[0]) + sh_ref[0]).astype(BF16)

    def seg(lo, n):
        return _dot(h, w_ref[:, lo:lo + n])

    def rope(v):
        if not use_rope:
            return v
        return _rope256(v, cos_ref[...], sn_ref[...], sp_ref[...])

    naq = _group_rmsnorm(seg(OFF_NA_Q, NA_W), bd64_ref, gq_ref[...])
    naq_ref[0] = (naq * (NA_HEAD_DIM ** -0.5)).astype(BF16)
    dfq = rope(_group_rmsnorm(seg(OFF_DF_Q, DIFF_QK_W), bd32_ref, gdq_ref[...]))
    dfq_ref[0] = (dfq * (math.log2(math.e) * DIFF_HEAD_DIM ** -0.5)).T.astype(BF16)
    pool_ref[0] = seg(OFF_POOL, POOL_W)
    fnet_ref[0] = seg(OFF_FNET, FNET_W)
    for j in range(0, N_BRANCH * D_MODEL, 512):
        gate_ref[0, :, j:j + 512] = jax.nn.sigmoid(seg(OFF_GATE + j, 512)).astype(BF16)
    nak = _group_rmsnorm(seg(OFF_KV, NA_W), bd64_ref, gk_ref[...])
    nakv_ref[0, :, 0:NA_W] = nak.astype(BF16)
    nakv_ref[0, :, NA_W:2 * NA_W] = seg(OFF_KV + NA_W, NA_W).astype(BF16)
    dfk = rope(_group_rmsnorm(seg(OFF_KV + 2 * NA_W, DIFF_QK_W), bd32_ref, gdk_ref[...]))
    dfk_ref[0] = dfk.astype(BF16)
    dfv_ref[0] = seg(OFF_KV + 2 * NA_W + DIFF_QK_W, DIFF_V_W).T.astype(BF16)


def _block_diag_mean(width, group):
    i = jnp.arange(width)
    return jnp.where((i[:, None] // group) == (i[None, :] // group), 1.0 / group, 0.0).astype(BF16)


def _rope_tables(T):
    t = jnp.arange(T)
    j = jnp.arange(LANES)
    jj = j % DIFF_HEAD_DIM
    quarter = DIFF_HEAD_DIM // 4
    use_row = jj < DIFF_HEAD_DIM // 2
    first = (jj % (DIFF_HEAD_DIM // 2)) < quarter
    inv = ROPE_BASE ** (-(jj % quarter).astype(F32) / quarter)
    pos = jnp.where(use_row[None, :], (t // GRID_W)[:, None], (t % GRID_W)[:, None]).astype(F32)
    ang = pos * inv[None, :]
    cos, sin = jnp.cos(ang), jnp.sin(ang)
    s_next = jnp.where(first[None, :], -sin, 0.0)
    s_prev = jnp.where(first[None, :], 0.0, sin)
    return cos, s_next, s_prev


def _inproj(x, g, sh, sc, w_bf16, gq, gk, gdq, gdk, rope, tm):
    B, T, D = x.shape
    use_rope = rope is not None
    row = lambda b, i: (b, i, 0)
    const2 = lambda b, i: (0, 0)
    perb = lambda b, i: (b, 0, 0)
    in_specs = [
        pl.BlockSpec((1, tm, D), row),
        pl.BlockSpec((1, D), const2),
        pl.BlockSpec((1, 1, D), perb),
        pl.BlockSpec((1, 1, D), perb),
        pl.BlockSpec((D, IN_COLS), const2),
        pl.BlockSpec((1, NA_W), const2), pl.BlockSpec((1, NA_W), const2),
        pl.BlockSpec((1, DIFF_QK_W), const2), pl.BlockSpec((1, DIFF_QK_W), const2),
        pl.BlockSpec((NA_W, NA_W), const2), pl.BlockSpec((DIFF_QK_W, DIFF_QK_W), const2),
    ]
    args = [x, g, sh, sc, w_bf16, gq, gk, gdq, gdk,
            _block_diag_mean(NA_W, NA_HEAD_DIM), _block_diag_mean(DIFF_QK_W, DIFF_HEAD_DIM)]
    if use_rope:
        in_specs += [pl.BlockSpec((tm, LANES), lambda b, i: (i, 0))] * 3
        args += list(rope)
    outs = [(NA_W, BF16, False), (DIFF_QK_W, BF16, True), (POOL_W, F32, False), (FNET_W, F32, False),
            (N_BRANCH * D_MODEL, BF16, False), (2 * NA_W, BF16, False), (DIFF_QK_W, BF16, False),
            (DIFF_V_W, BF16, True)]
    col = lambda b, i: (b, 0, i)
    out_shape = [jax.ShapeDtypeStruct((B, w, T) if tr else (B, T, w), dt) for w, dt, tr in outs]
    out_specs = [pl.BlockSpec((1, w, tm), col) if tr else pl.BlockSpec((1, tm, w), row) for w, _, tr in outs]
    return pl.pallas_call(
        functools.partial(_inproj_kernel, use_rope=use_rope),
        grid=(B, T // tm), in_specs=in_specs, out_specs=out_specs, out_shape=out_shape,
        compiler_params=_params("parallel", "parallel"), name="inproj",
    )(*args)


NA_ROWS_PER_STEP = 4


def _na_kernel(q_ref, kv0_ref, kv1_ref, kv2_ref, ckv_ref, bias_ref, o_ref):
    q = q_ref[0]
    kv = jnp.concatenate([kv0_ref[0], kv1_ref[0], kv2_ref[0]], axis=0)
    ckv = ckv_ref[0]
    for h in range(NA_HEADS):
        ks = slice(h * NA_HEAD_DIM, (h + 1) * NA_HEAD_DIM)
        vs = slice(NA_W + h * NA_HEAD_DIM, NA_W + (h + 1) * NA_HEAD_DIM)
        qh = q[:, ks]
        s = _dot_nt(qh, kv[:, ks]) + bias_ref[0, h]
        sc = _dot_nt(qh, ckv[:, ks])
        m = jnp.maximum(jnp.max(s, axis=-1, keepdims=True), jnp.max(sc, axis=-1, keepdims=True))
        p = jnp.exp(s - m)
        pc = jnp.exp(sc - m)
        l = jnp.sum(p, axis=-1, keepdims=True) + jnp.sum(pc, axis=-1, keepdims=True)
        o = _dot(p.astype(BF16), kv[:, vs]) + _dot(pc.astype(BF16), ckv[:, vs])
        o_ref[0, :, ks] = o / l


def _na_bias_table(rpb, rows):
    R = NA_ROWS_PER_STEP
    nb = rows // R
    col = jnp.arange(GRID_W)
    cs = jnp.clip(col - NA_WIN_C // 2, 0, GRID_W - NA_WIN_C)
    col_ok = (col[None, :] >= cs[:, None]) & (col[None, :] < cs[:, None] + NA_WIN_C)
    ci = jnp.clip(col[None, :] - col[:, None] + (NA_WIN_C - 1), 0, 2 * NA_WIN_C - 2)
    j = jnp.array([0, 1, nb - 1])[:, None, None]
    r = R * j + jnp.arange(R)[None, :, None]
    krow = R * (j - 1) + jnp.arange(3 * R)[None, None, :]
    rs = jnp.clip(r - NA_WIN_R // 2, 0, rows - NA_WIN_R)
    row_ok = (krow >= rs) & (krow < rs + NA_WIN_R) & (krow >= 0) & (krow < rows)
    ri = jnp.clip(krow - r + (NA_WIN_R - 1), 0, 2 * NA_WIN_R - 2)
    tab = rpb.astype(F32)[:, ri][:, :, :, :, ci]
    ok = row_ok[None, :, :, :, None, None] & col_ok[None, None, None, None]
    tab = jnp.where(ok, tab, NEG_BIG)
    tab = tab.transpose(1, 0, 2, 4, 3, 5)
    return tab.reshape(3, NA_HEADS, R * GRID_W, 3 * R * GRID_W)


def _na_latent(naq, nakv, cnakv, rpb):
    B, T, _ = naq.shape
    rows = T // GRID_W
    R = NA_ROWS_PER_STEP
    assert rows % R == 0 and rows // R >= 3 and rows >= NA_WIN_R and R + NA_WIN_R <= 3 * R
    nb = rows // R
    tb = R * GRID_W
    ctx_len = cnakv.shape[1]
    pattern = lambda j: (j > 0).astype(jnp.int32) + (j == nb - 1).astype(jnp.int32)
    kv_spec = lambda d: pl.BlockSpec((1, tb, 2 * NA_W), lambda b, j: (b, jnp.clip(j + d, 0, nb - 1), 0))
    return pl.pallas_call(
        _na_kernel, grid=(B, nb),
        in_specs=[
            pl.BlockSpec((1, tb, NA_W), lambda b, j: (b, j, 0)),
            kv_spec(-1), kv_spec(0), kv_spec(1),
            pl.BlockSpec((1, ctx_len, 2 * NA_W), lambda b, j: (b, 0, 0)),
            pl.BlockSpec((1, NA_HEADS, tb, 3 * tb), lambda b, j: (pattern(j), 0, 0, 0)),
        ],
        out_specs=pl.BlockSpec((1, tb, NA_W), lambda b, j: (b, j, 0)),
        out_shape=jax.ShapeDtypeStruct((B, T, NA_W), F32),
        compiler_params=_params("parallel", "parallel"), name="na_latent",
    )(naq, nakv, nakv, nakv, cnakv, _na_bias_table(rpb, rows))


DIFF_FIXED_SHIFT_MAX = 60.0


def _diff_kernel(shift_ref, lam_ref, g_ref, qt_ref, k_ref, vt_ref, o_ref, qx_sc, m_sc, l_sc, acc_sc, *,
                 lam_init, tk, nk, online):
    h = pl.program_id(1)
    qt = qt_ref[0]
    tq = qt.shape[1]
    grp = lax.broadcasted_iota(jnp.int32, qt.shape, 0) // DIFF_HEAD_DIM
    for mi in range(2):
        qx_sc[mi] = jnp.where(grp == 2 * h + mi, qt, jnp.zeros_like(qt))
    m_sc[...] = jnp.full(m_sc.shape, -jnp.inf, F32)
    l_sc[...] = jnp.zeros(l_sc.shape, F32)
    acc_sc[...] = jnp.zeros(acc_sc.shape, F32)

    def body(k, carry):
        off = pl.multiple_of(k * tk, tk)
        kb = k_ref[0, pl.ds(off, tk), :]
        vtb = vt_ref[0, :, pl.ds(off, tk)]
        for mi in range(2):
            s = _dot(kb, qx_sc[mi])
            if online:
                m_prev = m_sc[mi]
                m_new = jnp.maximum(m_prev, jnp.max(s, axis=0, keepdims=True))
                alpha = jnp.exp2(m_prev - m_new)
                p = jnp.exp2(s - m_new)
                l_sc[mi] = alpha * l_sc[mi] + p.reshape(tk // 8, 8, tq).sum(axis=0)
                acc_sc[mi] = alpha * acc_sc[mi] + _dot(vtb, p.astype(BF16))
                m_sc[mi] = m_new
            else:
                p = jnp.exp2(s - shift_ref[0])
                l_sc[mi] += p.reshape(tk // 8, 8, tq).sum(axis=0)
                acc_sc[mi] += _dot(vtb, p.astype(BF16))
        return carry

    lax.fori_loop(0, nk, body, 0)

    lv = lam_ref[...]
    lam = (jnp.exp(jnp.sum(lv[0:1] * lv[1:2], axis=-1, keepdims=True))
           - jnp.exp(jnp.sum(lv[2:3] * lv[3:4], axis=-1, keepdims=True)) + lam_init)
    l0 = jnp.sum(l_sc[0], axis=0, keepdims=True)
    l1 = jnp.sum(l_sc[1], axis=0, keepdims=True)
    o = acc_sc[0] / l0 - lam * (acc_sc[1] / l1)
    ms = jnp.mean(o * o, axis=0, keepdims=True)
    o_ref[0] = o * lax.rsqrt(ms + EPS) * g_ref[...] * (1.0 - lam_init)


def _diff_call(shift, qt, k, vt, lam_params, subln_g, lam_init, tq, tk, online):
    B, W, Tq = qt.shape
    Tk = k.shape[1]
    dv = W // DIFF_HEADS
    return pl.pallas_call(
        functools.partial(_diff_kernel, lam_init=lam_init, tk=tk, nk=Tk // tk, online=online),
        grid=(B, DIFF_HEADS, Tq // tq),
        in_specs=[
            pl.BlockSpec(memory_space=pltpu.SMEM),
            pl.BlockSpec((4, DIFF_HEAD_DIM), lambda b, h, i: (0, 0)),
            pl.BlockSpec((dv, 1), lambda b, h, i: (0, 0)),
            pl.BlockSpec((1, W, tq), lambda b, h, i: (b, 0, i)),
            pl.BlockSpec((1, Tk, W), lambda b, h, i: (b, 0, 0)),
            pl.BlockSpec((1, dv, Tk), lambda b, h, i: (b, h, 0)),
        ],
        out_specs=pl.BlockSpec((1, dv, tq), lambda b, h, i: (b, h, i)),
        out_shape=jax.ShapeDtypeStruct((B, W, Tq), F32),
        scratch_shapes=[pltpu.VMEM((2, W, tq), BF16), pltpu.VMEM((2, 1, tq), F32),
                        pltpu.VMEM((2, 8, tq), F32), pltpu.VMEM((2, dv, tq), F32)],
        compiler_params=_params("parallel", "parallel", "parallel"),
        name="diff_attn_online" if online else "diff_attn_fixed",
    )(shift, lam_params, subln_g, qt, k, vt)


def _diff_attn(qt, dfk, vt, gq, gk, lam_params, subln_g, lam_init, tq, tk):
    bound = (1.02 * math.log2(math.e) * DIFF_HEAD_DIM ** 0.5) * jnp.max(jnp.abs(gq)) * jnp.max(jnp.abs(gk))
    shift = bound.astype(F32).reshape(1)
    args = (shift, qt, dfk, vt, lam_params, subln_g, lam_init, tq, tk)
    return lax.cond(bound <= DIFF_FIXED_SHIFT_MAX,
                    lambda: _diff_call(*args, online=False), lambda: _diff_call(*args, online=True))


POOL_HALO = 8


def _pool_mixer(u, prev, nxt, wp_ref, scale, n_total):
    tm = u.shape[0]
    i = pl.program_id(1)
    prev = jnp.where(i > 0, prev, 0.0)
    nxt = jnp.where(i < pl.num_programs(1) - 1, nxt, 0.0)
    ext = jnp.concatenate([prev, u, nxt], axis=0)
    sh = lambda d: ext[POOL_HALO + d:POOL_HALO + d + tm]
    t = i * tm + lax.broadcasted_iota(jnp.int32, (tm, 1), 0)
    lane_grp = lax.broadcasted_iota(jnp.int32, (1, POOL_W), 1) // POOL_GROUP_W
    win = None
    mean = None
    prev_half = 0
    for gi, w in enumerate(POOL_WINDOWS):
        half = w // 2
        for d in range(prev_half, half):
            add = sh(d) + sh(-d - 1)
            win = add if win is None else win + add
        prev_half = half
        cnt = (jnp.minimum(t + half, n_total) - jnp.maximum(t - half, 0)).astype(F32)
        m_w = win / cnt
        mean = m_w if mean is None else jnp.where(lane_grp >= gi, m_w, mean)
    return _dot((mean - u).astype(BF16), wp_ref[...]) * scale


def _merge_kernel(x_ref, gt_ref, yna_ref, ydf_ref, u_ref, up_ref, un_ref, f_ref, gate_ref, wbr_ref, wp_ref,
                  ps_ref, wf_ref, wo_ref, o_ref, *, n_total):
    yf = _dot(f_ref[0].astype(BF16), wf_ref[...])
    ypool = _pool_mixer(u_ref[0], up_ref[0], un_ref[0], wp_ref, ps_ref[...], n_total)
    ys = (yna_ref[0], ydf_ref[0].T, ypool, yf)
    acc = None
    for i, y in enumerate(ys):
        g = gate_ref[0, :, i * D_MODEL:(i + 1) * D_MODEL].astype(F32)
        term = g * _dot(y.astype(BF16), wbr_ref[i])
        acc = term if acc is None else acc + term
    o_ref[0] = x_ref[0] + gt_ref[0] * _dot(acc.astype(BF16), wo_ref[...])


def _merge(x, gt, y_na, y_df, pool_in, f_real, gate, w_br, w_pool_bd, pool_scale, w_f, w_o, tm):
    B, T, D = x.shape
    assert tm % POOL_HALO == 0 and POOL_HALO >= max(POOL_WINDOWS) // 2
    hb = tm // POOL_HALO
    row = lambda b, i: (b, i, 0)
    const2 = lambda b, i: (0, 0)
    return pl.pallas_call(
        functools.partial(_merge_kernel, n_total=T), grid=(B, T // tm),
        in_specs=[
            pl.BlockSpec((1, tm, D), row),
            pl.BlockSpec((1, 1, D), lambda b, i: (b, 0, 0)),
            pl.BlockSpec((1, tm, BRANCH_W), row), pl.BlockSpec((1, BRANCH_W, tm), lambda b, i: (b, 0, i)),
            pl.BlockSpec((1, tm, POOL_W), row),
            pl.BlockSpec((1, POOL_HALO, POOL_W), lambda b, i: (b, jnp.maximum(i * hb - 1, 0), 0)),
            pl.BlockSpec((1, POOL_HALO, POOL_W), lambda b, i: (b, jnp.minimum((i + 1) * hb, T // POOL_HALO - 1), 0)),
            pl.BlockSpec((1, tm, FNET_W), row),
            pl.BlockSpec((1, tm, N_BRANCH * D), row),
            pl.BlockSpec((N_BRANCH, BRANCH_W, D), lambda b, i: (0, 0, 0)),
            pl.BlockSpec((POOL_W, POOL_W), const2), pl.BlockSpec((1, POOL_W), const2),
            pl.BlockSpec((FNET_W, FNET_W), const2),
            pl.BlockSpec((D, D), const2),
        ],
        out_specs=pl.BlockSpec((1, tm, D), row),
        out_shape=jax.ShapeDtypeStruct((B, T, D), F32),
        compiler_params=_params("parallel", "parallel"), name="merge",
    )(x, gt, y_na, y_df, pool_in, pool_in, pool_in, f_real, gate, w_br, w_pool_bd, pool_scale, w_f, w_o)


TOK_BLOCK = LANES
SEL_GROUP = 8
COMBINE_WIN = 48
BF16_ROWS = 16
SLOT_SPLIT = 64.0


def _router_kernel(x_ref, g_ref, sh_ref, sc_ref, wrh_ref, wrl_ref, h_ref, aff_ref):
    x = x_ref[0]
    ms = jnp.mean(x * x, axis=-1, keepdims=True)
    h = x * lax.rsqrt(ms + EPS) * g_ref[...] * (1.0 + sc_ref[0]) + sh_ref[0]
    h_ref[0] = h.astype(BF16)
    hi, lo = _split_bf16(h)
    logits = _dot_nt(wrh_ref[...], hi) + _dot_nt(wrh_ref[...], lo) + _dot_nt(wrl_ref[...], hi)
    logits = logits - jnp.max(logits, axis=0, keepdims=True)
    e = jnp.exp(logits)
    aff = e / jnp.sum(e, axis=0, keepdims=True)
    for j in range(aff.shape[1] // TOK_BLOCK):
        aff_ref[0, :, j, :] = aff[:, j * TOK_BLOCK:(j + 1) * TOK_BLOCK]


def _router(x, g, sh, sc, w_router, tm):
    B, T, D = x.shape
    E = w_router.shape[1]
    wrh, wrl = _split_bf16(w_router.T)
    row = lambda b, i: (b, i, 0)
    return pl.pallas_call(
        _router_kernel, grid=(B, T // tm),
        in_specs=[
            pl.BlockSpec((1, tm, D), row), pl.BlockSpec((1, D), lambda b, i: (0, 0)),
            pl.BlockSpec((1, 1, D), lambda b, i: (b, 0, 0)), pl.BlockSpec((1, 1, D), lambda b, i: (b, 0, 0)),
            pl.BlockSpec((E, D), lambda b, i: (0, 0)), pl.BlockSpec((E, D), lambda b, i: (0, 0)),
        ],
        out_specs=[pl.BlockSpec((1, tm, D), row),
                   pl.BlockSpec((1, E, tm // TOK_BLOCK, TOK_BLOCK), lambda b, i: (b, 0, i, 0))],
        out_shape=[jax.ShapeDtypeStruct((B, T, D), BF16),
                   jax.ShapeDtypeStruct((B, E, T // TOK_BLOCK, TOK_BLOCK), F32)],
        compiler_params=_params("parallel", "parallel"), name="router",
    )(x, g, sh, sc, wrh, wrl)


def _select_kernel(aff_ref, u_ref, ls_ref, idx_ref, gate_ref, pos_ref, off_ref, tot_ref, *, cap):
    G, nb, _ = aff_ref.shape
    aff = aff_ref[...]
    bits = pltpu.bitcast(aff, jnp.int32)
    kf = float(cap)
    ones_f = lambda m: jnp.where(m, 1.0, 0.0)
    count = lambda m: jnp.sum(ones_f(m), axis=(1, 2), keepdims=True)

    def search(it, lo):
        cand = lo | jnp.left_shift(jnp.int32(1), 30 - it)
        return jnp.where(count(bits >= cand) >= kf, cand, lo)

    thr = lax.fori_loop(0, 31, search, jnp.zeros((G, 1, 1), jnp.int32))
    need = kf - count(bits > thr)

    u = u_ref[...]
    ls = ls_ref[...]
    ones_m = jnp.ones((LANES, LANES), BF16)

    def prefix(mf):
        mb = mf.astype(BF16)
        tot = _dot(mb, ones_m)
        off = _dot(ls, tot.astype(BF16))
        return _dot(mb, u) + off, off, tot

    blk_i = lax.broadcasted_iota(jnp.int32, (nb, LANES), 0)
    lane_i = lax.broadcasted_iota(jnp.int32, (nb, LANES), 1)
    s_row = lax.broadcasted_iota(jnp.int32, (1, cap), 1).astype(F32)
    col_nb = lax.broadcasted_iota(jnp.int32, (nb, 1), 0).astype(F32)
    col_l = lax.broadcasted_iota(jnp.int32, (LANES, 1), 0).astype(F32)
    for g in range(G):
        gt = bits[g] > thr[g]
        eq = bits[g] == thr[g]
        eqf = ones_f(eq)
        tie_before = prefix(eqf)[0] - eqf
        sel = gt | (eq & (tie_before < need[g]))
        pin, off, tot = prefix(ones_f(sel))
        pos_ref[g] = jnp.where(sel, pin - 1.0, -1.0).astype(jnp.int32)
        diag = blk_i == lane_i
        off_ref[g] = jnp.sum(jnp.where(diag, off, 0.0), axis=0, keepdims=True).astype(jnp.int32)
        tot_ref[g] = jnp.sum(jnp.where(diag, tot, 0.0), axis=0, keepdims=True).astype(jnp.int32)
        blk = jnp.sum(ones_f((off + tot)[:, 0:1] <= s_row), axis=0, keepdims=True)
        oh = jnp.where(col_nb == blk, 1.0, 0.0).astype(BF16)
        pin_t = pin.T
        p_hi = jnp.floor(pin_t * (1.0 / LANES))
        p_lo = pin_t - p_hi * LANES
        rows = _dot(p_hi.astype(BF16), oh) * LANES + _dot(p_lo.astype(BF16), oh)
        lane = jnp.sum(ones_f(rows <= s_row), axis=0, keepdims=True)
        idx_ref[g] = (blk * LANES + lane).astype(jnp.int32)
        a_t = aff[g].T
        a1 = a_t.astype(BF16)
        r1 = a_t - a1.astype(F32)
        a2 = r1.astype(BF16)
        a3 = (r1 - a2.astype(F32)).astype(BF16)
        arow = _dot(a1, oh) + _dot(a2, oh) + _dot(a3, oh)
        gate_ref[g] = jnp.sum(jnp.where(col_l == lane, arow, 0.0), axis=0, keepdims=True)


def _select(aff, cap):
    B, E, nb, _ = aff.shape
    R = B * E
    G = SEL_GROUP
    i = np.arange(LANES)
    u = jnp.asarray(i[:, None] <= i[None, :], BF16)
    j = np.arange(nb)
    ls = jnp.asarray(j[None, :] < j[:, None], BF16)
    grp = lambda r: (r, 0, 0)
    outs = pl.pallas_call(
        functools.partial(_select_kernel, cap=cap), grid=(R // G,),
        in_specs=[pl.BlockSpec((G, nb, LANES), grp), pl.BlockSpec((LANES, LANES), lambda r: (0, 0)),
                  pl.BlockSpec((nb, nb), lambda r: (0, 0))],
        out_specs=[pl.BlockSpec((G, 1, cap), grp), pl.BlockSpec((G, 1, cap), grp),
                   pl.BlockSpec((G, nb, LANES), grp), pl.BlockSpec((G, 1, LANES), grp),
                   pl.BlockSpec((G, 1, LANES), grp)],
        out_shape=[jax.ShapeDtypeStruct((R, 1, cap), jnp.int32), jax.ShapeDtypeStruct((R, 1, cap), F32),
                   jax.ShapeDtypeStruct((R, nb, LANES), jnp.int32), jax.ShapeDtypeStruct((R, 1, LANES), jnp.int32),
                   jax.ShapeDtypeStruct((R, 1, LANES), jnp.int32)],
        compiler_params=_params("parallel"), name="select",
    )(aff.reshape(R, nb, LANES), u, ls)
    idx, gate, pos, off, tot = outs
    return (idx.reshape(B, E, cap), gate.reshape(B, E, cap), pos.reshape(B, E, nb, LANES),
            off[:, 0, :nb].reshape(B, E, nb), tot[:, 0, :nb].reshape(B, E, nb))


def _combine_kernel(off_ref, tot_ref, x_ref, gt_ref, pos_ref, ex_ref, ye_hbm, o_ref, buf, xbuf, acc_sc, sem,
                    xsem, *, nt, cap, W):
    n = pl.program_id(0)
    E = N_EXPERTS

    def window(step, e):
        b = step // nt
        o = off_ref[(b * E + e) * nt + step % nt]
        return b, o, jnp.minimum((o // BF16_ROWS) * BF16_ROWS, cap - W)

    def fetch(step, slot):
        for e in range(E):
            b, _, st = window(step, e)
            pltpu.make_async_copy(ye_hbm.at[b, e, pl.ds(pl.multiple_of(st, BF16_ROWS), W)], buf.at[slot, e],
                                  sem.at[slot]).start()

    @pl.when(n == 0)
    def _():
        fetch(0, 0)

    @pl.when(n + 1 < pl.num_programs(0))
    def _():
        fetch(n + 1, (n + 1) % 2)

    slot = n % 2
    for e in range(E):
        pltpu.make_async_copy(ye_hbm.at[0, 0, pl.ds(0, W)], buf.at[slot, e], sem.at[slot]).wait()

    pos = pos_ref[0, :, (n % nt) % pos_ref.shape[2], :].astype(F32).T
    p_hi = jnp.floor(pos * (1.0 / SLOT_SPLIT))
    p_lo = pos - SLOT_SPLIT * p_hi
    ex = ex_ref[...]
    posx = SLOT_SPLIT * _dot(p_hi.astype(BF16), ex) + _dot(p_lo.astype(BF16), ex)
    col = lax.broadcasted_iota(jnp.int32, (1, E * W), 1)
    tgt = jnp.zeros((1, E * W), jnp.int32)
    for e in range(E):
        _, _, st = window(n, e)
        tgt = jnp.where(col // W == e, st + col % W, tgt)
    onehot = jnp.where(posx == tgt.astype(F32), 1.0, 0.0).astype(BF16)
    acc_sc[...] = _dot(onehot, buf[slot].reshape(E * W, buf.shape[-1]))

    w_i = lax.broadcasted_iota(jnp.int32, (1, W), 1)
    for e in range(E):
        b, o, st = window(n, e)
        t = tot_ref[(b * E + e) * nt + n % nt]
        n_extra = jnp.maximum(o + t - (st + W) + (W - 1), 0) // W

        def extra(j, carry, e=e, b=b, st=st):
            base = st + W * (j + 1)
            src = jnp.minimum(base, cap - W)
            cp = pltpu.make_async_copy(ye_hbm.at[b, e, pl.ds(pl.multiple_of(src, BF16_ROWS), W)], xbuf,
                                       xsem.at[0])
            cp.start()
            cp.wait()
            slot_i = (src + w_i).astype(F32)
            oh = jnp.where((pos[:, e:e + 1] == slot_i) & (slot_i >= base.astype(F32)), 1.0, 0.0).astype(BF16)
            acc_sc[...] += _dot(oh, xbuf[...])
            return carry

        lax.fori_loop(0, n_extra, extra, 0)

    o_ref[0] = x_ref[0] + gt_ref[0] * acc_sc[...]


def _combine(x, gt, pos, off, tot, ye):
    B, T, D = x.shape
    E, cap = ye.shape[1], ye.shape[2]
    nt = T // TOK_BLOCK
    W = min(COMBINE_WIN, cap)
    pr = min(8, nt)
    assert nt % pr == 0
    assert W % BF16_ROWS == 0 and (cap - W) % BF16_ROWS == 0 and (E * W) % LANES == 0
    expand = jnp.asarray(np.arange(E)[:, None] == (np.arange(E * W)[None, :] // W), BF16)
    tile = lambda n, off_r, tot_r: (n // nt, n % nt, 0)
    grid_spec = pltpu.PrefetchScalarGridSpec(
        num_scalar_prefetch=2, grid=(B * nt,),
        in_specs=[
            pl.BlockSpec((1, TOK_BLOCK, D), tile),
            pl.BlockSpec((1, 1, D), lambda n, off_r, tot_r: (n // nt, 0, 0)),
            pl.BlockSpec((1, E, pr, TOK_BLOCK), lambda n, off_r, tot_r: (n // nt, 0, (n % nt) // pr, 0)),
            pl.BlockSpec((E, E * W), lambda n, off_r, tot_r: (0, 0)),
            pl.BlockSpec(memory_space=pl.ANY),
        ],
        out_specs=pl.BlockSpec((1, TOK_BLOCK, D), tile),
        scratch_shapes=[pltpu.VMEM((2, E, W, D), BF16), pltpu.VMEM((W, D), BF16),
                        pltpu.VMEM((TOK_BLOCK, D), F32),
                        pltpu.SemaphoreType.DMA((2,)), pltpu.SemaphoreType.DMA((1,))],
    )
    return pl.pallas_call(
        functools.partial(_combine_kernel, nt=nt, cap=cap, W=W), grid_spec=grid_spec,
        out_shape=jax.ShapeDtypeStruct((B, T, D), F32),
        compiler_params=_params("arbitrary"), name="combine",
    )(off.reshape(-1), tot.reshape(-1), x, gt, pos, expand, ye)


FF_CHUNKS = ((0, 512), (512, 512), (1024, 384))


def _expert_kernel(xe_ref, g_ref, wg_ref, wu_ref, wd_ref, o_ref):
    xe = xe_ref[0, 0]
    acc = None
    for lo, n in FF_CHUNKS:
        a = _dot(xe, wg_ref[0, 0, :, lo:lo + n].astype(BF16))
        u = _dot(xe, wu_ref[0, 0, :, lo:lo + n].astype(BF16))
        hid = (a * jax.nn.sigmoid(a) * u).astype(BF16)
        part = _dot(hid, wd_ref[0, 0, lo:lo + n, :].astype(BF16))
        acc = part if acc is None else acc + part
    g_col = jnp.broadcast_to(g_ref[0, 0], (8, acc.shape[0])).T[:, 0:1]
    o_ref[0, 0] = (acc * g_col).astype(o_ref.dtype)


def _experts(xe, g, wg, wu, wd, layer, tm):
    B, E, cap, D = xe.shape
    F = wg.shape[-1]
    assert F == EXPERT_FF
    tile = lambda e, b, i: (b, e, i, 0)
    wsel = lambda e, b, i: (layer, e, 0, 0)
    return pl.pallas_call(
        _expert_kernel, grid=(E, B, cap // tm),
        in_specs=[
            pl.BlockSpec((1, 1, tm, D), tile), pl.BlockSpec((1, 1, 1, tm), lambda e, b, i: (b, e, 0, i)),
            pl.BlockSpec((1, 1, D, F), wsel), pl.BlockSpec((1, 1, D, F), wsel), pl.BlockSpec((1, 1, F, D), wsel),
        ],
        out_specs=pl.BlockSpec((1, 1, tm, D), tile),
        out_shape=jax.ShapeDtypeStruct((B, E, cap, D), BF16),
        compiler_params=_params("parallel", "parallel", "parallel"), name="experts",
    )(xe, g, wg, wu, wd)


def _ec_ffn(x, gt, g, sh, sc, w_router, wg, wu, wd, layer, tm_router, tm_expert):
    B, N, D = x.shape
    cap = max(1, EC_FACTOR * N // N_EXPERTS)
    assert N <= TOK_BLOCK * LANES and N % TOK_BLOCK == 0
    nt = N // TOK_BLOCK
    h, aff = _router(x, g, sh, sc, w_router, tm_router)
    aff = jnp.pad(aff, ((0, 0), (0, 0), (0, LANES - nt), (0, 0)))
    idx, gsel, pos, off, tot = _select(aff, cap)
    flat = (idx + (jnp.arange(B) * N)[:, None, None]).reshape(-1)
    xe = jnp.take(h.reshape(B * N, D), flat, axis=0, mode="clip").reshape(B, N_EXPERTS, cap, D)
    ye = _experts(xe, gsel[:, :, None, :], wg, wu, wd, layer, min(tm_expert, cap))
    return _combine(x, gt, pos[:, :, :nt], off[:, :, :nt], tot[:, :, :nt], ye)


DFT_RADIX = 128


def _dot3(a_hi, a_lo, b_hi, b_lo):
    return _dot(a_hi, b_hi) + _dot(a_lo, b_hi) + _dot(a_hi, b_lo)


def _cos_sin(n_rows, n_cols, period):
    k = (np.arange(n_rows)[:, None] * np.arange(n_cols)[None, :]) % period
    ang = 2.0 * np.pi * k.astype(np.float64) / period
    return np.cos(ang), np.sin(ang)


def _const_split(m):
    return _split_bf16(jnp.asarray(m, F32))


def _chan_dft_kernel(u_ref, mh_ref, ml_ref, xr_ref, xi_ref):
    uh, ul = _split_bf16(u_ref[0])
    y = _dot3(uh, ul, mh_ref[...], ml_ref[...])
    c = xr_ref.shape[-1]
    xr_ref[0] = y[:, :c]
    xi_ref[0] = y[:, c:]


def _chan_dft(u, tm):
    B, N, C = u.shape
    cc, sc = _cos_sin(C, C, C)
    mh, ml = _const_split(np.concatenate([cc, -sc], axis=1))
    row = lambda b, i: (b, i, 0)
    const2 = lambda b, i: (0, 0)
    return pl.pallas_call(
        _chan_dft_kernel, grid=(B, N // tm),
        in_specs=[pl.BlockSpec((1, tm, C), row), pl.BlockSpec((C, 2 * C), const2), pl.BlockSpec((C, 2 * C), const2)],
        out_specs=[pl.BlockSpec((1, tm, C), row)] * 2,
        out_shape=[jax.ShapeDtypeStruct((B, N, C), F32)] * 2,
        compiler_params=_params("parallel", "parallel"), name="chan_dft",
    )(u, mh, ml)


def _dft_left_kernel(xr_ref, xi_ref, mh_ref, ml_ref, o_ref, *, scale):
    xh, xl = _split_bf16(jnp.concatenate([xr_ref[0], xi_ref[0]], axis=0))
    o_ref[0] = _dot3(mh_ref[...], ml_ref[...], xh, xl) * scale


def _dft_left(xr, xi, m, scale, tn):
    B, K, cols = xr.shape
    R = m.shape[0]
    mh, ml = _const_split(m)
    col = lambda b, j: (b, 0, j)
    const2 = lambda b, j: (0, 0)
    return pl.pallas_call(
        functools.partial(_dft_left_kernel, scale=scale), grid=(B, cols // tn),
        in_specs=[pl.BlockSpec((1, K, tn), col), pl.BlockSpec((1, K, tn), col),
                  pl.BlockSpec((R, 2 * K), const2), pl.BlockSpec((R, 2 * K), const2)],
        out_specs=pl.BlockSpec((1, R, tn), col),
        out_shape=jax.ShapeDtypeStruct((B, R, cols), F32),
        compiler_params=_params("parallel", "parallel"), name="dft_left",
    )(xr, xi, mh, ml)


DFT_N2_PER_STEP = 8


def _dft_outer_kernel(xr_ref, xi_ref, mh_ref, ml_ref, o_ref):
    n1 = xr_ref.shape[1]
    for s in range(DFT_N2_PER_STEP):
        xh, xl = _split_bf16(jnp.concatenate([xr_ref[0, :, s, :], xi_ref[0, :, s, :]], axis=0))
        y = _dot3(mh_ref[...], ml_ref[...], xh, xl)
        o_ref[0, 0, :, s, :] = y[:n1]
        o_ref[0, 1, :, s, :] = y[n1:]


def _dft_outer(xr, xi, m):
    B, n1, n2, C = xr.shape
    mh, ml = _const_split(m)
    blk = lambda b, j: (b, 0, j, 0)
    const2 = lambda b, j: (0, 0)
    return pl.pallas_call(
        _dft_outer_kernel, grid=(B, n2 // DFT_N2_PER_STEP),
        in_specs=[pl.BlockSpec((1, n1, DFT_N2_PER_STEP, C), blk), pl.BlockSpec((1, n1, DFT_N2_PER_STEP, C), blk),
                  pl.BlockSpec((2 * n1, 2 * n1), const2), pl.BlockSpec((2 * n1, 2 * n1), const2)],
        out_specs=pl.BlockSpec((1, 2, n1, DFT_N2_PER_STEP, C), lambda b, j: (b, 0, 0, j, 0)),
        out_shape=jax.ShapeDtypeStruct((B, 2, n1, n2, C), F32),
        compiler_params=_params("parallel", "parallel"), name="dft_outer",
    )(xr, xi, mh, ml)


DFT_K1_PER_STEP = 8


def _dft_twiddle_kernel(y_ref, tc_ref, ts_ref, dh_ref, dl_ref, o_ref, *, scale):
    tc = tc_ref[0]
    ts = ts_ref[0]
    for kk in range(DFT_K1_PER_STEP):
        yr = y_ref[0, 0, kk]
        yi = y_ref[0, 1, kk]
        c, s = tc[:, kk:kk + 1], ts[:, kk:kk + 1]
        zh, zl = _split_bf16(jnp.concatenate([yr * c + yi * s, yi * c - yr * s], axis=0))
        o_ref[0, :, kk, :] = _dot3(dh_ref[...], dl_ref[...], zh, zl) * scale


def _fourier_real(u):
    B, N, C = u.shape
    scale = 1.0 / math.sqrt(N * C)
    xr, xi = _chan_dft(u, min(N, 512))
    if N % (DFT_RADIX * DFT_K1_PER_STEP) != 0:
        assert N <= 1024
        cn, sn = _cos_sin(N, N, N)
        return _dft_left(xr, xi, np.concatenate([cn, sn], axis=1), scale, C)
    n1, n2 = N // DFT_RADIX, DFT_RADIX
    c1, s1 = _cos_sin(n1, n1, n1)
    m1 = np.block([[c1, s1], [-s1, c1]])
    yy = _dft_outer(xr.reshape(B, n1, n2, C), xi.reshape(B, n1, n2, C), m1)
    tcos, tsin = _cos_sin(n1, n2, N)
    grp = lambda t: jnp.asarray(t.reshape(n1 // DFT_K1_PER_STEP, DFT_K1_PER_STEP, n2).transpose(0, 2, 1), F32)
    c2, s2 = _cos_sin(n2, n2, n2)
    dh, dl = _const_split(np.concatenate([c2, s2], axis=1))
    fp = pl.pallas_call(
        functools.partial(_dft_twiddle_kernel, scale=scale), grid=(B, n1 // DFT_K1_PER_STEP),
        in_specs=[
            pl.BlockSpec((1, 2, DFT_K1_PER_STEP, n2, C), lambda b, g: (b, 0, g, 0, 0)),
            pl.BlockSpec((1, n2, DFT_K1_PER_STEP), lambda b, g: (g, 0, 0)),
            pl.BlockSpec((1, n2, DFT_K1_PER_STEP), lambda b, g: (g, 0, 0)),
            pl.BlockSpec((n2, 2 * n2), lambda b, g: (0, 0)), pl.BlockSpec((n2, 2 * n2), lambda b, g: (0, 0)),
        ],
        out_specs=pl.BlockSpec((1, n2, DFT_K1_PER_STEP, C), lambda b, g: (b, 0, g, 0)),
        out_shape=jax.ShapeDtypeStruct((B, n2, n1, C), F32),
        compiler_params=_params("parallel", "parallel"), name="dft_twiddle",
    )(yy, grp(tcos), grp(tsin), dh, dl)
    return fp.reshape(B, N, C)


def _ctx_attn_kernel(q_ref, kv_ref, o_ref):
    q = q_ref[0]
    kv = kv_ref[0]
    for h in range(NA_HEADS):
        ks = slice(h * NA_HEAD_DIM, (h + 1) * NA_HEAD_DIM)
        vs = slice(NA_W + h * NA_HEAD_DIM, NA_W + (h + 1) * NA_HEAD_DIM)
        s = _dot_nt(q[:, ks], kv[:, ks])
        p = jnp.exp(s - jnp.max(s, axis=-1, keepdims=True))
        l = jnp.sum(p, axis=-1, keepdims=True)
        o_ref[0, :, ks] = _dot(p.astype(BF16), kv[:, vs]) / l


def _ctx_dense_attn(q, kv):
    B, Q, _ = q.shape
    return pl.pallas_call(
        _ctx_attn_kernel, grid=(B,),
        in_specs=[pl.BlockSpec((1, Q, NA_W), lambda b: (b, 0, 0)),
                  pl.BlockSpec((1, kv.shape[1], 2 * NA_W), lambda b: (b, 0, 0))],
        out_specs=pl.BlockSpec((1, Q, NA_W), lambda b: (b, 0, 0)),
        out_shape=jax.ShapeDtypeStruct((B, Q, NA_W), F32),
        compiler_params=_params("parallel"), name="ctx_attn",
    )(q, kv)


ADA_ROWS = 8
ADA_TN = 512


def _ada_kernel(c_ref, w_ref, b_ref, o_ref):
    c = c_ref[...]
    sh, sl = _split_bf16(c * jax.nn.sigmoid(c))
    wh, wl = _split_bf16(w_ref[0])
    o_ref[...] = _dot3(sh, sl, wh, wl) + b_ref[0]


def _ada_mod(c_rows, w_ada, b_ada, layer):
    R, D = c_rows.shape
    N = w_ada.shape[-1]
    return pl.pallas_call(
        _ada_kernel, grid=(N // ADA_TN,),
        in_specs=[pl.BlockSpec((R, D), lambda j: (0, 0)),
                  pl.BlockSpec((1, D, ADA_TN), lambda j: (layer, 0, j)),
                  pl.BlockSpec((1, 1, ADA_TN), lambda j: (layer, 0, j))],
        out_specs=pl.BlockSpec((R, ADA_TN), lambda j: (0, j)),
        out_shape=jax.ShapeDtypeStruct((R, N), F32),
        compiler_params=_params("parallel"), name="ada_mod",
    )(c_rows, w_ada, b_ada[:, None, :])


def _tile(g, n):
    return jnp.tile(g.astype(F32), n)[None, :]


def kernel(x, c, ctx, c_ctx, w_ada, b_ada, g_mix, g_ffn, w_in, na_q_g, na_k_g, na_rpb, df_q_g, df_k_g,
           df_lambda, df_subln_g, pool_w, pool_scale, fnet_w, w_branch, w_out, w_router, w_gate_e, w_up_e,
           w_down_e):
    B, T, D = x.shape
    ctx_len = ctx.shape[1]
    rope = _rope_tables(T)
    assert B + 1 <= ADA_ROWS
    c_rows = jnp.concatenate([c, c_ctx[None, :], jnp.zeros((ADA_ROWS - B - 1, D), F32)], axis=0)
    tq = min(DIFF_TQ, T)
    for l in range(DEPTH):
        last = l == DEPTH - 1
        lam_init = 0.8 - 0.6 * math.exp(-0.3 * l)
        mods = _ada_mod(c_rows, w_ada, b_ada, l)
        sh1, sc1, gt1, sh2, sc2, gt2 = [m[:, None, :] for m in jnp.split(mods[:B], 6, axis=-1)]
        bc = lambda m: jnp.broadcast_to(m[None, None, :], (B, 1, D))
        csh1, csc1, cgt1, csh2, csc2, cgt2 = [bc(m) for m in jnp.split(mods[B], 6, axis=-1)]

        w_bf = w_in[l].astype(BF16)
        gq, gk = _tile(na_q_g[l], NA_HEADS), _tile(na_k_g[l], NA_HEADS)
        gdq, gdk = _tile(df_q_g[l], 2 * DIFF_HEADS), _tile(df_k_g[l], 2 * DIFF_HEADS)
        gmix = g_mix[l][None, :]
        gffn = g_ffn[l][None, :]
        subg = df_subln_g[l][:, None].astype(F32)
        wbr = w_branch[l].astype(BF16)
        wf = fnet_w[l].astype(BF16)
        wpool = jax.scipy.linalg.block_diag(*[pool_w[l, gi] for gi in range(len(POOL_WINDOWS))]).astype(BF16)
        pscale = pool_scale[l][None, :].astype(F32)
        wo = w_out[l].astype(BF16)

        naq, dfq, pool_in, fnet_in, gate, nakv, dfk, dfv = _inproj(
            x, gmix, sh1, sc1, w_bf, gq, gk, gdq, gdk, rope, min(INPROJ_ROWS, T))
        (cnaq, cdfq, cpool_in, cfnet_in, cgate, cnakv, cdfk, cdfv) = _inproj(
            ctx, gmix, csh1, csc1, w_bf, gq, gk, gdq, gdk, None, ctx_len)

        y_na = _na_latent(naq, nakv, cnakv, na_rpb[l])
        lamp = df_lambda[l].astype(F32)
        y_df = _diff_attn(dfq, jnp.concatenate([dfk, cdfk], axis=1), jnp.concatenate([dfv, cdfv], axis=2),
                          df_q_g[l], df_k_g[l], lamp, subg, lam_init, tq, DIFF_TK)
        f_real = _fourier_real(fnet_in)
        x_new = _merge(x, gt1, y_na, y_df, pool_in, f_real, gate, wbr, wpool, pscale, wf, wo, MERGE_ROWS)

        if not last:
            yc_na = _ctx_dense_attn(cnaq, cnakv)
            yc_df = _diff_attn(cdfq, cdfk, cdfv, df_q_g[l], df_k_g[l], lamp, subg, lam_init, ctx_len, ctx_len)
            fc_real = _fourier_real(cfnet_in)
            ctx_new = _merge(ctx, cgt1, yc_na, yc_df, cpool_in, fc_real, cgate, wbr, wpool, pscale, wf, wo,
                             ctx_len)

        x = x_new
        x = _ec_ffn(x, gt2, gffn, sh2, sc2, w_router[l], w_gate_e, w_up_e, w_down_e, l, min(ROUTER_ROWS, T),
                    EXPERT_ROWS)
        if not last:
            ctx = ctx_new
            ctx = _ec_ffn(ctx, cgt2, gffn, csh2, csc2, w_router[l], w_gate_e, w_up_e, w_down_e, l, ctx_len,
                          EXPERT_ROWS)
    return x
```

```python
import functools
import math

import jax
import jax.numpy as jnp
import numpy as np
from jax import lax
from jax.experimental import pallas as pl
from jax.experimental.pallas import tpu as pltpu

F32 = jnp.float32
BF16 = jnp.bfloat16

D_MODEL = 1024
DEPTH = 2
GRID_W = 64
EPS = 1e-6
ROPE_BASE = 10000.0

NA_HEADS = 4
NA_HEAD_DIM = 64
NA_WIN_R = 8
NA_WIN_C = 16
NA_W = NA_HEADS * NA_HEAD_DIM

DIFF_HEADS = 4
DIFF_HEAD_DIM = 32
DIFF_QK_W = DIFF_HEADS * 2 * DIFF_HEAD_DIM
DIFF_V_W = DIFF_HEADS * 2 * DIFF_HEAD_DIM

POOL_WINDOWS = (2, 4, 8, 16)
POOL_GROUP_W = 64
POOL_W = len(POOL_WINDOWS) * POOL_GROUP_W
FNET_W = 256
N_BRANCH = 4
BRANCH_W = 256

OFF_NA_Q = 0
OFF_DF_Q = OFF_NA_Q + NA_W
OFF_POOL = OFF_DF_Q + DIFF_QK_W
OFF_FNET = OFF_POOL + POOL_W
OFF_GATE = OFF_FNET + FNET_W
OFF_KV = OFF_GATE + N_BRANCH * D_MODEL
KV_W = 2 * NA_W + DIFF_QK_W + DIFF_V_W
IN_COLS = OFF_KV + KV_W

N_EXPERTS = 16
EC_FACTOR = 2
EXPERT_FF = 1408

VMEM_LIMIT_BYTES = 56 * 1024 * 1024
LANES = 128
NEG_BIG = -1e30

INPROJ_ROWS = 512
MERGE_ROWS = 512
ROUTER_ROWS = 1024
EXPERT_ROWS = 512
DIFF_TQ, DIFF_TK = 2048, 1280


def _params(*sem):
    return pltpu.CompilerParams(dimension_semantics=sem, vmem_limit_bytes=VMEM_LIMIT_BYTES)


def _split_bf16(a):
    hi = a.astype(BF16)
    lo = (a - hi.astype(F32)).astype(BF16)
    return hi, lo


def _dot(a, b):
    return jnp.dot(a, b, preferred_element_type=F32)


def _dot_nt(a, b):
    return lax.dot_general(a, b, (((1,), (1,)), ((), ())), preferred_element_type=F32)


def _group_rmsnorm(p, bd_ref, g):
    hi, lo = _split_bf16(p * p)
    ms = _dot(hi, bd_ref[...]) + _dot(lo, bd_ref[...])
    return p * lax.rsqrt(ms + EPS) * g


def _rope256(y, cos, s_next, s_prev):
    outs = []
    for half in range(2):
        z = y[:, half * LANES:(half + 1) * LANES]
        outs.append(z * cos + pltpu.roll(z, LANES - 8, 1) * s_next + pltpu.roll(z, 8, 1) * s_prev)
    return jnp.concatenate(outs, axis=1)


def _inproj_kernel(*refs, use_rope):
    if use_rope:
        (x_ref, g_ref, sh_ref, sc_ref, w_ref, gq_ref, gk_ref, gdq_ref, gdk_ref, bd64_ref, bd32_ref,
         cos_ref, sn_ref, sp_ref,
         naq_ref, dfq_ref, pool_ref, fnet_ref, gate_ref, nakv_ref, dfk_ref, dfv_ref) = refs
    else:
        (x_ref, g_ref, sh_ref, sc_ref, w_ref, gq_ref, gk_ref, gdq_ref, gdk_ref, bd64_ref, bd32_ref,
         naq_ref, dfq_ref, pool_ref, fnet_ref, gate_ref, nakv_ref, dfk_ref, dfv_ref) = refs
    x = x_ref[0]
    ms = jnp.mean(x * x, axis=-1, keepdims=True)
    y = x * lax.rsqrt(ms + EPS) * g_ref[...]
    h = (y * (1.0 + sc_ref[0]) + sh_ref[0]).astype(BF16)

    def seg(lo, n):
        return _dot(h, w_ref[:, lo:lo + n])

    def rope(v):
        if not use_rope:
            return v
        return _rope256(v, cos_ref[...], sn_ref[...], sp_ref[...])

    naq = _group_rmsnorm(seg(OFF_NA_Q, NA_W), bd64_ref, gq_ref[...])
    naq_ref[0] = (naq * (NA_HEAD_DIM ** -0.5)).astype(BF16)
    dfq = rope(_group_rmsnorm(seg(OFF_DF_Q, DIFF_QK_W), bd32_ref, gdq_ref[...]))
    dfq_ref[0] = (dfq * (math.log2(math.e) * DIFF_HEAD_DIM ** -0.5)).T.astype(BF16)
    pool_ref[0] = seg(OFF_POOL, POOL_W)
    fnet_ref[0] = seg(OFF_FNET, FNET_W)
    for j in range(0, N_BRANCH * D_MODEL, 512):
        gate_ref[0, :, j:j + 512] = jax.nn.sigmoid(seg(OFF_GATE + j, 512)).astype(BF16)
    nak = _group_rmsnorm(seg(OFF_KV, NA_W), bd64_ref, gk_ref[...])
    nakv_ref[0, :, 0:NA_W] = nak.astype(BF16)
    nakv_ref[0, :, NA_W:2 * NA_W] = seg(OFF_KV + NA_W, NA_W).astype(BF16)
    dfk = rope(_group_rmsnorm(seg(OFF_KV + 2 * NA_W, DIFF_QK_W), bd32_ref, gdk_ref[...]))
    dfk_ref[0] = dfk.astype(BF16)
    dfv_ref[0] = seg(OFF_KV + 2 * NA_W + DIFF_QK_W, DIFF_V_W).T.astype(BF16)


def _block_diag_mean(width, group):
    i = jnp.arange(width)
    return jnp.where((i[:, None] // group) == (i[None, :] // group), 1.0 / group, 0.0).astype(BF16)


def _rope_tables(T):
    t = jnp.arange(T)
    j = jnp.arange(LANES)
    jj = j % DIFF_HEAD_DIM
    quarter = DIFF_HEAD_DIM // 4
    use_row = jj < DIFF_HEAD_DIM // 2
    first = (jj % (DIFF_HEAD_DIM // 2)) < quarter
    inv = ROPE_BASE ** (-(jj % quarter).astype(F32) / quarter)
    pos = jnp.where(use_row[None, :], (t // GRID_W)[:, None], (t % GRID_W)[:, None]).astype(F32)
    ang = pos * inv[None, :]
    cos, sin = jnp.cos(ang), jnp.sin(ang)
    s_next = jnp.where(first[None, :], -sin, 0.0)
    s_prev = jnp.where(first[None, :], 0.0, sin)
    return cos, s_next, s_prev


def _inproj(x, g, sh, sc, w_bf16, gq, gk, gdq, gdk, rope, tm):
    B, T, D = x.shape
    use_rope = rope is not None
    row = lambda b, i: (b, i, 0)
    const2 = lambda b, i: (0, 0)
    perb = lambda b, i: (b, 0, 0)
    in_specs = [
        pl.BlockSpec((1, tm, D), row),
        pl.BlockSpec((1, D), const2),
        pl.BlockSpec((1, 1, D), perb),
        pl.BlockSpec((1, 1, D), perb),
        pl.BlockSpec((D, IN_COLS), const2),
        pl.BlockSpec((1, NA_W), const2), pl.BlockSpec((1, NA_W), const2),
        pl.BlockSpec((1, DIFF_QK_W), const2), pl.BlockSpec((1, DIFF_QK_W), const2),
        pl.BlockSpec((NA_W, NA_W), const2), pl.BlockSpec((DIFF_QK_W, DIFF_QK_W), const2),
    ]
    args = [x, g, sh, sc, w_bf16, gq, gk, gdq, gdk,
            _block_diag_mean(NA_W, NA_HEAD_DIM), _block_diag_mean(DIFF_QK_W, DIFF_HEAD_DIM)]
    if use_rope:
        in_specs += [pl.BlockSpec((tm, LANES), lambda b, i: (i, 0))] * 3
        args += list(rope)
    outs = [(NA_W, BF16, False), (DIFF_QK_W, BF16, True), (POOL_W, F32, False), (FNET_W, F32, False),
            (N_BRANCH * D_MODEL, BF16, False), (2 * NA_W, BF16, False), (DIFF_QK_W, BF16, False),
            (DIFF_V_W, BF16, True)]
    col = lambda b, i: (b, 0, i)
    out_shape = [jax.ShapeDtypeStruct((B, w, T) if tr else (B, T, w), dt) for w, dt, tr in outs]
    out_specs = [pl.BlockSpec((1, w, tm), col) if tr else pl.BlockSpec((1, tm, w), row) for w, _, tr in outs]
    return pl.pallas_call(
        functools.partial(_inproj_kernel, use_rope=use_rope),
        grid=(B, T // tm), in_specs=in_specs, out_specs=out_specs, out_shape=out_shape,
        compiler_params=_params("parallel", "parallel"), name="inproj",
    )(*args)


NA_ROWS_PER_STEP = 4


def _na_kernel(q_ref, kv0_ref, kv1_ref, kv2_ref, ckv_ref, bias_ref, o_ref):
    q = q_ref[0]
    kv = jnp.concatenate([kv0_ref[0], kv1_ref[0], kv2_ref[0]], axis=0)
    ckv = ckv_ref[0]
    for h in range(NA_HEADS):
        ks = slice(h * NA_HEAD_DIM, (h + 1) * NA_HEAD_DIM)
        vs = slice(NA_W + h * NA_HEAD_DIM, NA_W + (h + 1) * NA_HEAD_DIM)
        qh = q[:, ks]
        s = _dot_nt(qh, kv[:, ks]) + bias_ref[0, h]
        sc = _dot_nt(qh, ckv[:, ks])
        m = jnp.maximum(jnp.max(s, axis=-1, keepdims=True), jnp.max(sc, axis=-1, keepdims=True))
        p = jnp.exp(s - m)
        pc = jnp.exp(sc - m)
        l = jnp.sum(p, axis=-1, keepdims=True) + jnp.sum(pc, axis=-1, keepdims=True)
        o = _dot(p.astype(BF16), kv[:, vs]) + _dot(pc.astype(BF16), ckv[:, vs])
        o_ref[0, :, ks] = o / l


def _na_bias_table(rpb, rows):
    R = NA_ROWS_PER_STEP
    nb = rows // R
    col = jnp.arange(GRID_W)
    cs = jnp.clip(col - NA_WIN_C // 2, 0, GRID_W - NA_WIN_C)
    col_ok = (col[None, :] >= cs[:, None]) & (col[None, :] < cs[:, None] + NA_WIN_C)
    ci = jnp.clip(col[None, :] - col[:, None] + (NA_WIN_C - 1), 0, 2 * NA_WIN_C - 2)
    j = jnp.array([0, 1, nb - 1])[:, None, None]
    r = R * j + jnp.arange(R)[None, :, None]
    krow = R * (j - 1) + jnp.arange(3 * R)[None, None, :]
    rs = jnp.clip(r - NA_WIN_R // 2, 0, rows - NA_WIN_R)
    row_ok = (krow >= rs) & (krow < rs + NA_WIN_R) & (krow >= 0) & (krow < rows)
    ri = jnp.clip(krow - r + (NA_WIN_R - 1), 0, 2 * NA_WIN_R - 2)
    tab = rpb.astype(F32)[:, ri][:, :, :, :, ci]
    ok = row_ok[None, :, :, :, None, None] & col_ok[None, None, None, None]
    tab = jnp.where(ok, tab, NEG_BIG)
    tab = tab.transpose(1, 0, 2, 4, 3, 5)
    return tab.reshape(3, NA_HEADS, R * GRID_W, 3 * R * GRID_W)


def _na_latent(naq, nakv, cnakv, rpb):
    B, T, _ = naq.shape
    rows = T // GRID_W
    R = NA_ROWS_PER_STEP
    assert rows % R == 0 and rows // R >= 3 and rows >= NA_WIN_R and R + NA_WIN_R <= 3 * R
    nb = rows // R
    tb = R * GRID_W
    ctx_len = cnakv.shape[1]
    pattern = lambda j: (j > 0).astype(jnp.int32) + (j == nb - 1).astype(jnp.int32)
    kv_spec = lambda d: pl.BlockSpec((1, tb, 2 * NA_W), lambda b, j: (b, jnp.clip(j + d, 0, nb - 1), 0))
    return pl.pallas_call(
        _na_kernel, grid=(B, nb),
        in_specs=[
            pl.BlockSpec((1, tb, NA_W), lambda b, j: (b, j, 0)),
            kv_spec(-1), kv_spec(0), kv_spec(1),
            pl.BlockSpec((1, ctx_len, 2 * NA_W), lambda b, j: (b, 0, 0)),
            pl.BlockSpec((1, NA_HEADS, tb, 3 * tb), lambda b, j: (pattern(j), 0, 0, 0)),
        ],
        out_specs=pl.BlockSpec((1, tb, NA_W), lambda b, j: (b, j, 0)),
        out_shape=jax.ShapeDtypeStruct((B, T, NA_W), F32),
        compiler_params=_params("parallel", "parallel"), name="na_latent",
    )(naq, nakv, nakv, nakv, cnakv, _na_bias_table(rpb, rows))


DIFF_FIXED_SHIFT_MAX = -1.0


def _diff_kernel(shift_ref, lam_ref, g_ref, qt_ref, k_ref, vt_ref, o_ref, qx_sc, m_sc, l_sc, acc_sc, *,
                 lam_init, tk, nk, online):
    h = pl.program_id(1)
    qt = qt_ref[0]
    tq = qt.shape[1]
    grp = lax.broadcasted_iota(jnp.int32, qt.shape, 0) // DIFF_HEAD_DIM
    for mi in range(2):
        qx_sc[mi] = jnp.where(grp == 2 * h + mi, qt, jnp.zeros_like(qt))
    m_sc[...] = jnp.full(m_sc.shape, -jnp.inf, F32)
    l_sc[...] = jnp.zeros(l_sc.shape, F32)
    acc_sc[...] = jnp.zeros(acc_sc.shape, F32)

    def body(k, carry):
        off = pl.multiple_of(k * tk, tk)
        kb = k_ref[0, pl.ds(off, tk), :]
        vtb = vt_ref[0, :, pl.ds(off, tk)]
        for mi in range(2):
            s = _dot(kb, qx_sc[mi])
            if online:
                m_prev = m_sc[mi]
                m_new = jnp.maximum(m_prev, jnp.max(s, axis=0, keepdims=True))
                alpha = jnp.exp2(m_prev - m_new)
                p = jnp.exp2(s - m_new)
                l_sc[mi] = alpha * l_sc[mi] + p.reshape(tk // 8, 8, tq).sum(axis=0)
                acc_sc[mi] = alpha * acc_sc[mi] + _dot(vtb, p.astype(BF16))
                m_sc[mi] = m_new
            else:
                p = jnp.exp2(s - shift_ref[0])
                l_sc[mi] += p.reshape(tk // 8, 8, tq).sum(axis=0)
                acc_sc[mi] += _dot(vtb, p.astype(BF16))
        return carry

    lax.fori_loop(0, nk, body, 0)

    lv = lam_ref[...]
    lam = (jnp.exp(jnp.sum(lv[0:1] * lv[1:2], axis=-1, keepdims=True))
           - jnp.exp(jnp.sum(lv[2:3] * lv[3:4], axis=-1, keepdims=True)) + lam_init)
    l0 = jnp.sum(l_sc[0], axis=0, keepdims=True)
    l1 = jnp.sum(l_sc[1], axis=0, keepdims=True)
    o = acc_sc[0] / l0 - lam * (acc_sc[1] / l1)
    ms = jnp.mean(o * o, axis=0, keepdims=True)
    o_ref[0] = o * lax.rsqrt(ms + EPS) * g_ref[...] * (1.0 - lam_init)


def _diff_call(shift, qt, k, vt, lam_params, subln_g, lam_init, tq, tk, online):
    B, W, Tq = qt.shape
    Tk = k.shape[1]
    dv = W // DIFF_HEADS
    return pl.pallas_call(
        functools.partial(_diff_kernel, lam_init=lam_init, tk=tk, nk=Tk // tk, online=online),
        grid=(B, DIFF_HEADS, Tq // tq),
        in_specs=[
            pl.BlockSpec(memory_space=pltpu.SMEM),
            pl.BlockSpec((4, DIFF_HEAD_DIM), lambda b, h, i: (0, 0)),
            pl.BlockSpec((dv, 1), lambda b, h, i: (0, 0)),
            pl.BlockSpec((1, W, tq), lambda b, h, i: (b, 0, i)),
            pl.BlockSpec((1, Tk, W), lambda b, h, i: (b, 0, 0)),
            pl.BlockSpec((1, dv, Tk), lambda b, h, i: (b, h, 0)),
        ],
        out_specs=pl.BlockSpec((1, dv, tq), lambda b, h, i: (b, h, i)),
        out_shape=jax.ShapeDtypeStruct((B, W, Tq), F32),
        scratch_shapes=[pltpu.VMEM((2, W, tq), BF16), pltpu.VMEM((2, 1, tq), F32),
                        pltpu.VMEM((2, 8, tq), F32), pltpu.VMEM((2, dv, tq), F32)],
        compiler_params=_params("parallel", "parallel", "parallel"),
        name="diff_attn_online" if online else "diff_attn_fixed",
    )(shift, lam_params, subln_g, qt, k, vt)


def _diff_attn(qt, dfk, vt, gq, gk, lam_params, subln_g, lam_init, tq, tk):
    bound = (1.02 * math.log2(math.e) * DIFF_HEAD_DIM ** 0.5) * jnp.max(jnp.abs(gq)) * jnp.max(jnp.abs(gk))
    shift = bound.astype(F32).reshape(1)
    args = (shift, qt, dfk, vt, lam_params, subln_g, lam_init, tq, tk)
    return lax.cond(bound <= DIFF_FIXED_SHIFT_MAX,
                    lambda: _diff_call(*args, online=False), lambda: _diff_call(*args, online=True))


POOL_HALO = 8


def _pool_mixer(u, prev, nxt, wp_ref, scale, n_total):
    tm = u.shape[0]
    i = pl.program_id(1)
    prev = jnp.where(i > 0, prev, 0.0)
    nxt = jnp.where(i < pl.num_programs(1) - 1, nxt, 0.0)
    ext = jnp.concatenate([prev, u, nxt], axis=0)
    sh = lambda d: ext[POOL_HALO + d:POOL_HALO + d + tm]
    t = i * tm + lax.broadcasted_iota(jnp.int32, (tm, 1), 0)
    lane_grp = lax.broadcasted_iota(jnp.int32, (1, POOL_W), 1) // POOL_GROUP_W
    win = None
    mean = None
    prev_half = 0
    for gi, w in enumerate(POOL_WINDOWS):
        half = w // 2
        for d in range(prev_half, half):
            add = sh(d) + sh(-d - 1)
            win = add if win is None else win + add
        prev_half = half
        cnt = (jnp.minimum(t + half, n_total) - jnp.maximum(t - half, 0)).astype(F32)
        m_w = win / cnt
        mean = m_w if mean is None else jnp.where(lane_grp >= gi, m_w, mean)
    return _dot((mean - u).astype(BF16), wp_ref[...]) * scale


def _merge_kernel(x_ref, gt_ref, yna_ref, ydf_ref, u_ref, up_ref, un_ref, f_ref, gate_ref, wbr_ref, wp_ref,
                  ps_ref, wf_ref, wo_ref, o_ref, *, n_total):
    yf = _dot(f_ref[0].astype(BF16), wf_ref[...])
    ypool = _pool_mixer(u_ref[0], up_ref[0], un_ref[0], wp_ref, ps_ref[...], n_total)
    ys = (yna_ref[0], ydf_ref[0].T, ypool, yf)
    acc = None
    for i, y in enumerate(ys):
        g = gate_ref[0, :, i * D_MODEL:(i + 1) * D_MODEL].astype(F32)
        term = g * _dot(y.astype(BF16), wbr_ref[i])
        acc = term if acc is None else acc + term
    o_ref[0] = x_ref[0] + gt_ref[0] * _dot(acc.astype(BF16), wo_ref[...])


def _merge(x, gt, y_na, y_df, pool_in, f_real, gate, w_br, w_pool_bd, pool_scale, w_f, w_o, tm):
    B, T, D = x.shape
    assert tm % POOL_HALO == 0 and POOL_HALO >= max(POOL_WINDOWS) // 2
    hb = tm // POOL_HALO
    row = lambda b, i: (b, i, 0)
    const2 = lambda b, i: (0, 0)
    return pl.pallas_call(
        functools.partial(_merge_kernel, n_total=T), grid=(B, T // tm),
        in_specs=[
            pl.BlockSpec((1, tm, D), row),
            pl.BlockSpec((1, 1, D), lambda b, i: (b, 0, 0)),
            pl.BlockSpec((1, tm, BRANCH_W), row), pl.BlockSpec((1, BRANCH_W, tm), lambda b, i: (b, 0, i)),
            pl.BlockSpec((1, tm, POOL_W), row),
            pl.BlockSpec((1, POOL_HALO, POOL_W), lambda b, i: (b, jnp.maximum(i * hb - 1, 0), 0)),
            pl.BlockSpec((1, POOL_HALO, POOL_W), lambda b, i: (b, jnp.minimum((i + 1) * hb, T // POOL_HALO - 1), 0)),
            pl.BlockSpec((1, tm, FNET_W), row),
            pl.BlockSpec((1, tm, N_BRANCH * D), row),
            pl.BlockSpec((N_BRANCH, BRANCH_W, D), lambda b, i: (0, 0, 0)),
            pl.BlockSpec((POOL_W, POOL_W), const2), pl.BlockSpec((1, POOL_W), const2),
            pl.BlockSpec((FNET_W, FNET_W), const2),
            pl.BlockSpec((D, D), const2),
        ],
        out_specs=pl.BlockSpec((1, tm, D), row),
        out_shape=jax.ShapeDtypeStruct((B, T, D), F32),
        compiler_params=_params("parallel", "parallel"), name="merge",
    )(x, gt, y_na, y_df, pool_in, pool_in, pool_in, f_real, gate, w_br, w_pool_bd, pool_scale, w_f, w_o)


TOK_BLOCK = LANES
SEL_GROUP = 8
COMBINE_WIN = 48
BF16_ROWS = 16
SLOT_SPLIT = 64.0


def _router_kernel(x_ref, g_ref, sh_ref, sc_ref, wrh_ref, wrl_ref, h_ref, aff_ref):
    x = x_ref[0]
    ms = jnp.mean(x * x, axis=-1, keepdims=True)
    h = x * lax.rsqrt(ms + EPS) * g_ref[...] * (1.0 + sc_ref[0]) + sh_ref[0]
    h_ref[0] = h.astype(BF16)
    hi, lo = _split_bf16(h)
    logits = _dot_nt(wrh_ref[...], hi) + _dot_nt(wrh_ref[...], lo) + _dot_nt(wrl_ref[...], hi)
    logits = logits - jnp.max(logits, axis=0, keepdims=True)
    e = jnp.exp(logits)
    aff = e / jnp.sum(e, axis=0, keepdims=True)
    for j in range(aff.shape[1] // TOK_BLOCK):
        aff_ref[0, :, j, :] = aff[:, j * TOK_BLOCK:(j + 1) * TOK_BLOCK]


def _router(x, g, sh, sc, w_router, tm):
    B, T, D = x.shape
    E = w_router.shape[1]
    wrh, wrl = _split_bf16(w_router.T)
    row = lambda b, i: (b, i, 0)
    return pl.pallas_call(
        _router_kernel, grid=(B, T // tm),
        in_specs=[
            pl.BlockSpec((1, tm, D), row), pl.BlockSpec((1, D), lambda b, i: (0, 0)),
            pl.BlockSpec((1, 1, D), lambda b, i: (b, 0, 0)), pl.BlockSpec((1, 1, D), lambda b, i: (b, 0, 0)),
            pl.BlockSpec((E, D), lambda b, i: (0, 0)), pl.BlockSpec((E, D), lambda b, i: (0, 0)),
        ],
        out_specs=[pl.BlockSpec((1, tm, D), row),
                   pl.BlockSpec((1, E, tm // TOK_BLOCK, TOK_BLOCK), lambda b, i: (b, 0, i, 0))],
        out_shape=[jax.ShapeDtypeStruct((B, T, D), BF16),
                   jax.ShapeDtypeStruct((B, E, T // TOK_BLOCK, TOK_BLOCK), F32)],
        compiler_params=_params("parallel", "parallel"), name="router",
    )(x, g, sh, sc, wrh, wrl)


def _select_kernel(aff_ref, u_ref, ls_ref, idx_ref, gate_ref, pos_ref, off_ref, tot_ref, *, cap):
    G, nb, _ = aff_ref.shape
    aff = aff_ref[...]
    bits = pltpu.bitcast(aff, jnp.int32)
    kf = float(cap)
    ones_f = lambda m: jnp.where(m, 1.0, 0.0)
    count = lambda m: jnp.sum(ones_f(m), axis=(1, 2), keepdims=True)

    def search(it, lo):
        cand = lo | jnp.left_shift(jnp.int32(1), 30 - it)
        return jnp.where(count(bits >= cand) >= kf, cand, lo)

    thr = lax.fori_loop(0, 31, search, jnp.zeros((G, 1, 1), jnp.int32))
    need = kf - count(bits > thr)

    u = u_ref[...]
    ls = ls_ref[...]
    ones_m = jnp.ones((LANES, LANES), BF16)

    def prefix(mf):
        mb = mf.astype(BF16)
        tot = _dot(mb, ones_m)
        off = _dot(ls, tot.astype(BF16))
        return _dot(mb, u) + off, off, tot

    blk_i = lax.broadcasted_iota(jnp.int32, (nb, LANES), 0)
    lane_i = lax.broadcasted_iota(jnp.int32, (nb, LANES), 1)
    s_row = lax.broadcasted_iota(jnp.int32, (1, cap), 1).astype(F32)
    col_nb = lax.broadcasted_iota(jnp.int32, (nb, 1), 0).astype(F32)
    col_l = lax.broadcasted_iota(jnp.int32, (LANES, 1), 0).astype(F32)
    for g in range(G):
        gt = bits[g] > thr[g]
        eq = bits[g] == thr[g]
        eqf = ones_f(eq)
        tie_before = prefix(eqf)[0] - eqf
        sel = gt | (eq & (tie_before < need[g]))
        pin, off, tot = prefix(ones_f(sel))
        pos_ref[g] = jnp.where(sel, pin - 1.0, -1.0).astype(jnp.int32)
        diag = blk_i == lane_i
        off_ref[g] = jnp.sum(jnp.where(diag, off, 0.0), axis=0, keepdims=True).astype(jnp.int32)
        tot_ref[g] = jnp.sum(jnp.where(diag, tot, 0.0), axis=0, keepdims=True).astype(jnp.int32)
        blk = jnp.sum(ones_f((off + tot)[:, 0:1] <= s_row), axis=0, keepdims=True)
        oh = jnp.where(col_nb == blk, 1.0, 0.0).astype(BF16)
        pin_t = pin.T
        p_hi = jnp.floor(pin_t * (1.0 / LANES))
        p_lo = pin_t - p_hi * LANES
        rows = _dot(p_hi.astype(BF16), oh) * LANES + _dot(p_lo.astype(BF16), oh)
        lane = jnp.sum(ones_f(rows <= s_row), axis=0, keepdims=True)
        idx_ref[g] = (blk * LANES + lane).astype(jnp.int32)
        a_t = aff[g].T
        a1 = a_t.astype(BF16)
        r1 = a_t - a1.astype(F32)
        a2 = r1.astype(BF16)
        a3 = (r1 - a2.astype(F32)).astype(BF16)
        arow = _dot(a1, oh) + _dot(a2, oh) + _dot(a3, oh)
        gate_ref[g] = jnp.sum(jnp.where(col_l == lane, arow, 0.0), axis=0, keepdims=True)


def _select(aff, cap):
    B, E, nb, _ = aff.shape
    R = B * E
    G = SEL_GROUP
    i = np.arange(LANES)
    u = jnp.asarray(i[:, None] <= i[None, :], BF16)
    j = np.arange(nb)
    ls = jnp.asarray(j[None, :] < j[:, None], BF16)
    grp = lambda r: (r, 0, 0)
    outs = pl.pallas_call(
        functools.partial(_select_kernel, cap=cap), grid=(R // G,),
        in_specs=[pl.BlockSpec((G, nb, LANES), grp), pl.BlockSpec((LANES, LANES), lambda r: (0, 0)),
                  pl.BlockSpec((nb, nb), lambda r: (0, 0))],
        out_specs=[pl.BlockSpec((G, 1, cap), grp), pl.BlockSpec((G, 1, cap), grp),
                   pl.BlockSpec((G, nb, LANES), grp), pl.BlockSpec((G, 1, LANES), grp),
                   pl.BlockSpec((G, 1, LANES), grp)],
        out_shape=[jax.ShapeDtypeStruct((R, 1, cap), jnp.int32), jax.ShapeDtypeStruct((R, 1, cap), F32),
                   jax.ShapeDtypeStruct((R, nb, LANES), jnp.int32), jax.ShapeDtypeStruct((R, 1, LANES), jnp.int32),
                   jax.ShapeDtypeStruct((R, 1, LANES), jnp.int32)],
        compiler_params=_params("parallel"), name="select",
    )(aff.reshape(R, nb, LANES), u, ls)
    idx, gate, pos, off, tot = outs
    return (idx.reshape(B, E, cap), gate.reshape(B, E, cap), pos.reshape(B, E, nb, LANES),
            off[:, 0, :nb].reshape(B, E, nb), tot[:, 0, :nb].reshape(B, E, nb))


def _combine_kernel(off_ref, tot_ref, x_ref, gt_ref, pos_ref, ex_ref, ye_hbm, o_ref, buf, xbuf, acc_sc, sem,
                    xsem, *, nt, cap, W):
    n = pl.program_id(0)
    E = N_EXPERTS

    def window(step, e):
        b = step // nt
        o = off_ref[(b * E + e) * nt + step % nt]
        return b, o, jnp.minimum((o // BF16_ROWS) * BF16_ROWS, cap - W)

    def fetch(step, slot):
        for e in range(E):
            b, _, st = window(step, e)
            pltpu.make_async_copy(ye_hbm.at[b, e, pl.ds(pl.multiple_of(st, BF16_ROWS), W)], buf.at[slot, e],
                                  sem.at[slot]).start()

    @pl.when(n == 0)
    def _():
        fetch(0, 0)

    @pl.when(n + 1 < pl.num_programs(0))
    def _():
        fetch(n + 1, (n + 1) % 2)

    slot = n % 2
    for e in range(E):
        pltpu.make_async_copy(ye_hbm.at[0, 0, pl.ds(0, W)], buf.at[slot, e], sem.at[slot]).wait()

    pos = pos_ref[0, :, (n % nt) % pos_ref.shape[2], :].astype(F32).T
    p_hi = jnp.floor(pos * (1.0 / SLOT_SPLIT))
    p_lo = pos - SLOT_SPLIT * p_hi
    ex = ex_ref[...]
    posx = SLOT_SPLIT * _dot(p_hi.astype(BF16), ex) + _dot(p_lo.astype(BF16), ex)
    col = lax.broadcasted_iota(jnp.int32, (1, E * W), 1)
    tgt = jnp.zeros((1, E * W), jnp.int32)
    for e in range(E):
        _, _, st = window(n, e)
        tgt = jnp.where(col // W == e, st + col % W, tgt)
    onehot = jnp.where(posx == tgt.astype(F32), 1.0, 0.0).astype(BF16)
    acc_sc[...] = _dot(onehot, buf[slot].reshape(E * W, buf.shape[-1]))

    w_i = lax.broadcasted_iota(jnp.int32, (1, W), 1)
    for e in range(E):
        b, o, st = window(n, e)
        t = tot_ref[(b * E + e) * nt + n % nt]
        n_extra = jnp.maximum(o + t - (st + W) + (W - 1), 0) // W

        def extra(j, carry, e=e, b=b, st=st):
            base = st + W * (j + 1)
            src = jnp.minimum(base, cap - W)
            cp = pltpu.make_async_copy(ye_hbm.at[b, e, pl.ds(pl.multiple_of(src, BF16_ROWS), W)], xbuf,
                                       xsem.at[0])
            cp.start()
            cp.wait()
            slot_i = (src + w_i).astype(F32)
            oh = jnp.where((pos[:, e:e + 1] == slot_i) & (slot_i >= base.astype(F32)), 1.0, 0.0).astype(BF16)
            acc_sc[...] += _dot(oh, xbuf[...])
            return carry

        lax.fori_loop(0, n_extra, extra, 0)

    o_ref[0] = x_ref[0] + gt_ref[0] * acc_sc[...]


def _combine(x, gt, pos, off, tot, ye):
    B, T, D = x.shape
    E, cap = ye.shape[1], ye.shape[2]
    nt = T // TOK_BLOCK
    W = min(COMBINE_WIN, cap)
    pr = min(8, nt)
    assert nt % pr == 0
    assert W % BF16_ROWS == 0 and (cap - W) % BF16_ROWS == 0 and (E * W) % LANES == 0
    expand = jnp.asarray(np.arange(E)[:, None] == (np.arange(E * W)[None, :] // W), BF16)
    tile = lambda n, off_r, tot_r: (n // nt, n % nt, 0)
    grid_spec = pltpu.PrefetchScalarGridSpec(
        num_scalar_prefetch=2, grid=(B * nt,),
        in_specs=[
            pl.BlockSpec((1, TOK_BLOCK, D), tile),
            pl.BlockSpec((1, 1, D), lambda n, off_r, tot_r: (n // nt, 0, 0)),
            pl.BlockSpec((1, E, pr, TOK_BLOCK), lambda n, off_r, tot_r: (n // nt, 0, (n % nt) // pr, 0)),
            pl.BlockSpec((E, E * W), lambda n, off_r, tot_r: (0, 0)),
            pl.BlockSpec(memory_space=pl.ANY),
        ],
        out_specs=pl.BlockSpec((1, TOK_BLOCK, D), tile),
        scratch_shapes=[pltpu.VMEM((2, E, W, D), BF16), pltpu.VMEM((W, D), BF16),
                        pltpu.VMEM((TOK_BLOCK, D), F32),
                        pltpu.SemaphoreType.DMA((2,)), pltpu.SemaphoreType.DMA((1,))],
    )
    return pl.pallas_call(
        functools.partial(_combine_kernel, nt=nt, cap=cap, W=W), grid_spec=grid_spec,
        out_shape=jax.ShapeDtypeStruct((B, T, D), F32),
        compiler_params=_params("arbitrary"), name="combine",
    )(off.reshape(-1), tot.reshape(-1), x, gt, pos, expand, ye)


FF_CHUNKS = ((0, 512), (512, 512), (1024, 384))


def _expert_kernel(xe_ref, g_ref, wg_ref, wu_ref, wd_ref, o_ref):
    xe = xe_ref[0, 0]
    acc = None
    for lo, n in FF_CHUNKS:
        a = _dot(xe, wg_ref[0, 0, :, lo:lo + n].astype(BF16))
        u = _dot(xe, wu_ref[0, 0, :, lo:lo + n].astype(BF16))
        hid = (a * jax.nn.sigmoid(a) * u).astype(BF16)
        part = _dot(hid, wd_ref[0, 0, lo:lo + n, :].astype(BF16))
        acc = part if acc is None else acc + part
    g_col = jnp.broadcast_to(g_ref[0, 0], (8, acc.shape[0])).T[:, 0:1]
    o_ref[0, 0] = (acc * g_col).astype(o_ref.dtype)


def _experts(xe, g, wg, wu, wd, layer, tm):
    B, E, cap, D = xe.shape
    F = wg.shape[-1]
    assert F == EXPERT_FF
    tile = lambda e, b, i: (b, e, i, 0)
    wsel = lambda e, b, i: (layer, e, 0, 0)
    return pl.pallas_call(
        _expert_kernel, grid=(E, B, cap // tm),
        in_specs=[
            pl.BlockSpec((1, 1, tm, D), tile), pl.BlockSpec((1, 1, 1, tm), lambda e, b, i: (b, e, 0, i)),
            pl.BlockSpec((1, 1, D, F), wsel), pl.BlockSpec((1, 1, D, F), wsel), pl.BlockSpec((1, 1, F, D), wsel),
        ],
        out_specs=pl.BlockSpec((1, 1, tm, D), tile),
        out_shape=jax.ShapeDtypeStruct((B, E, cap, D), BF16),
        compiler_params=_params("parallel", "parallel", "parallel"), name="experts",
    )(xe, g, wg, wu, wd)


def _ec_ffn(x, gt, g, sh, sc, w_router, wg, wu, wd, layer, tm_router, tm_expert):
    B, N, D = x.shape
    cap = max(1, EC_FACTOR * N // N_EXPERTS)
    assert N <= TOK_BLOCK * LANES and N % TOK_BLOCK == 0
    nt = N // TOK_BLOCK
    h, aff = _router(x, g, sh, sc, w_router, tm_router)
    aff = jnp.pad(aff, ((0, 0), (0, 0), (0, LANES - nt), (0, 0)))
    idx, gsel, pos, off, tot = _select(aff, cap)
    flat = (idx + (jnp.arange(B) * N)[:, None, None]).reshape(-1)
    xe = jnp.take(h.reshape(B * N, D), flat, axis=0, mode="clip").reshape(B, N_EXPERTS, cap, D)
    ye = _experts(xe, gsel[:, :, None, :], wg, wu, wd, layer, min(tm_expert, cap))
    return _combine(x, gt, pos[:, :, :nt], off[:, :, :nt], tot[:, :, :nt], ye)


DFT_RADIX = 128


def _dot3(a_hi, a_lo, b_hi, b_lo):
    return _dot(a_hi, b_hi) + _dot(a_lo, b_hi) + _dot(a_hi, b_lo)


def _cos_sin(n_rows, n_cols, period):
    k = (np.arange(n_rows)[:, None] * np.arange(n_cols)[None, :]) % period
    ang = 2.0 * np.pi * k.astype(np.float64) / period
    return np.cos(ang), np.sin(ang)


def _const_split(m):
    return _split_bf16(jnp.asarray(m, F32))


def _chan_dft_kernel(u_ref, mh_ref, ml_ref, xr_ref, xi_ref):
    uh, ul = _split_bf16(u_ref[0])
    y = _dot3(uh, ul, mh_ref[...], ml_ref[...])
    c = xr_ref.shape[-1]
    xr_ref[0] = y[:, :c]
    xi_ref[0] = y[:, c:]


def _chan_dft(u, tm):
    B, N, C = u.shape
    cc, sc = _cos_sin(C, C, C)
    mh, ml = _const_split(np.concatenate([cc, -sc], axis=1))
    row = lambda b, i: (b, i, 0)
    const2 = lambda b, i: (0, 0)
    return pl.pallas_call(
        _chan_dft_kernel, grid=(B, N // tm),
        in_specs=[pl.BlockSpec((1, tm, C), row), pl.BlockSpec((C, 2 * C), const2), pl.BlockSpec((C, 2 * C), const2)],
        out_specs=[pl.BlockSpec((1, tm, C), row)] * 2,
        out_shape=[jax.ShapeDtypeStruct((B, N, C), F32)] * 2,
        compiler_params=_params("parallel", "parallel"), name="chan_dft",
    )(u, mh, ml)


def _dft_left_kernel(xr_ref, xi_ref, mh_ref, ml_ref, o_ref, *, scale):
    xh, xl = _split_bf16(jnp.concatenate([xr_ref[0], xi_ref[0]], axis=0))
    o_ref[0] = _dot3(mh_ref[...], ml_ref[...], xh, xl) * scale


def _dft_left(xr, xi, m, scale, tn):
    B, K, cols = xr.shape
    R = m.shape[0]
    mh, ml = _const_split(m)
    col = lambda b, j: (b, 0, j)
    const2 = lambda b, j: (0, 0)
    return pl.pallas_call(
        functools.partial(_dft_left_kernel, scale=scale), grid=(B, cols // tn),
        in_specs=[pl.BlockSpec((1, K, tn), col), pl.BlockSpec((1, K, tn), col),
                  pl.BlockSpec((R, 2 * K), const2), pl.BlockSpec((R, 2 * K), const2)],
        out_specs=pl.BlockSpec((1, R, tn), col),
        out_shape=jax.ShapeDtypeStruct((B, R, cols), F32),
        compiler_params=_params("parallel", "parallel"), name="dft_left",
    )(xr, xi, mh, ml)


DFT_N2_PER_STEP = 8


def _dft_outer_kernel(xr_ref, xi_ref, mh_ref, ml_ref, o_ref):
    n1 = xr_ref.shape[1]
    for s in range(DFT_N2_PER_STEP):
        xh, xl = _split_bf16(jnp.concatenate([xr_ref[0, :, s, :], xi_ref[0, :, s, :]], axis=0))
        y = _dot3(mh_ref[...], ml_ref[...], xh, xl)
        o_ref[0, 0, :, s, :] = y[:n1]
        o_ref[0, 1, :, s, :] = y[n1:]


def _dft_outer(xr, xi, m):
    B, n1, n2, C = xr.shape
    mh, ml = _const_split(m)
    blk = lambda b, j: (b, 0, j, 0)
    const2 = lambda b, j: (0, 0)
    return pl.pallas_call(
        _dft_outer_kernel, grid=(B, n2 // DFT_N2_PER_STEP),
        in_specs=[pl.BlockSpec((1, n1, DFT_N2_PER_STEP, C), blk), pl.BlockSpec((1, n1, DFT_N2_PER_STEP, C), blk),
                  pl.BlockSpec((2 * n1, 2 * n1), const2), pl.BlockSpec((2 * n1, 2 * n1), const2)],
        out_specs=pl.BlockSpec((1, 2, n1, DFT_N2_PER_STEP, C), lambda b, j: (b, 0, 0, j, 0)),
        out_shape=jax.ShapeDtypeStruct((B, 2, n1, n2, C), F32),
        compiler_params=_params("parallel", "parallel"), name="dft_outer",
    )(xr, xi, mh, ml)


DFT_K1_PER_STEP = 8


def _dft_twiddle_kernel(y_ref, tc_ref, ts_ref, dh_ref, dl_ref, o_ref, *, scale):
    tc = tc_ref[0]
    ts = ts_ref[0]
    for kk in range(DFT_K1_PER_STEP):
        yr = y_ref[0, 0, kk]
        yi = y_ref[0, 1, kk]
        c, s = tc[:, kk:kk + 1], ts[:, kk:kk + 1]
        zh, zl = _split_bf16(jnp.concatenate([yr * c + yi * s, yi * c - yr * s], axis=0))
        o_ref[0, :, kk, :] = _dot3(dh_ref[...], dl_ref[...], zh, zl) * scale


def _fourier_real(u):
    B, N, C = u.shape
    scale = 1.0 / math.sqrt(N * C)
    xr, xi = _chan_dft(u, min(N, 512))
    if N % (DFT_RADIX * DFT_K1_PER_STEP) != 0:
        assert N <= 1024
        cn, sn = _cos_sin(N, N, N)
        return _dft_left(xr, xi, np.concatenate([cn, sn], axis=1), scale, C)
    n1, n2 = N // DFT_RADIX, DFT_RADIX
    c1, s1 = _cos_sin(n1, n1, n1)
    m1 = np.block([[c1, s1], [-s1, c1]])
    yy = _dft_outer(xr.reshape(B, n1, n2, C), xi.reshape(B, n1, n2, C), m1)
    tcos, tsin = _cos_sin(n1, n2, N)
    grp = lambda t: jnp.asarray(t.reshape(n1 // DFT_K1_PER_STEP, DFT_K1_PER_STEP, n2).transpose(0, 2, 1), F32)
    c2, s2 = _cos_sin(n2, n2, n2)
    dh, dl = _const_split(np.concatenate([c2, s2], axis=1))
    fp = pl.pallas_call(
        functools.partial(_dft_twiddle_kernel, scale=scale), grid=(B, n1 // DFT_K1_PER_STEP),
        in_specs=[
            pl.BlockSpec((1, 2, DFT_K1_PER_STEP, n2, C), lambda b, g: (b, 0, g, 0, 0)),
            pl.BlockSpec((1, n2, DFT_K1_PER_STEP), lambda b, g: (g, 0, 0)),
            pl.BlockSpec((1, n2, DFT_K1_PER_STEP), lambda b, g: (g, 0, 0)),
            pl.BlockSpec((n2, 2 * n2), lambda b, g: (0, 0)), pl.BlockSpec((n2, 2 * n2), lambda b, g: (0, 0)),
        ],
        out_specs=pl.BlockSpec((1, n2, DFT_K1_PER_STEP, C), lambda b, g: (b, 0, g, 0)),
        out_shape=jax.ShapeDtypeStruct((B, n2, n1, C), F32),
        compiler_params=_params("parallel", "parallel"), name="dft_twiddle",
    )(yy, grp(tcos), grp(tsin), dh, dl)
    return fp.reshape(B, N, C)


def _ctx_attn_kernel(q_ref, kv_ref, o_ref):
    q = q_ref[0]
    kv = kv_ref[0]
    for h in range(NA_HEADS):
        ks = slice(h * NA_HEAD_DIM, (h + 1) * NA_HEAD_DIM)
        vs = slice(NA_W + h * NA_HEAD_DIM, NA_W + (h + 1) * NA_HEAD_DIM)
        s = _dot_nt(q[:, ks], kv[:, ks])
        p = jnp.exp(s - jnp.max(s, axis=-1, keepdims=True))
        l = jnp.sum(p, axis=-1, keepdims=True)
        o_ref[0, :, ks] = _dot(p.astype(BF16), kv[:, vs]) / l


def _ctx_dense_attn(q, kv):
    B, Q, _ = q.shape
    return pl.pallas_call(
        _ctx_attn_kernel, grid=(B,),
        in_specs=[pl.BlockSpec((1, Q, NA_W), lambda b: (b, 0, 0)),
                  pl.BlockSpec((1, kv.shape[1], 2 * NA_W), lambda b: (b, 0, 0))],
        out_specs=pl.BlockSpec((1, Q, NA_W), lambda b: (b, 0, 0)),
        out_shape=jax.ShapeDtypeStruct((B, Q, NA_W), F32),
        compiler_params=_params("parallel"), name="ctx_attn",
    )(q, kv)


ADA_ROWS = 8
ADA_TN = 512


def _ada_kernel(c_ref, w_ref, b_ref, o_ref):
    c = c_ref[...]
    sh, sl = _split_bf16(c * jax.nn.sigmoid(c))
    wh, wl = _split_bf16(w_ref[0])
    o_ref[...] = _dot3(sh, sl, wh, wl) + b_ref[0]


def _ada_mod(c_rows, w_ada, b_ada, layer):
    R, D = c_rows.shape
    N = w_ada.shape[-1]
    return pl.pallas_call(
        _ada_kernel, grid=(N // ADA_TN,),
        in_specs=[pl.BlockSpec((R, D), lambda j: (0, 0)),
                  pl.BlockSpec((1, D, ADA_TN), lambda j: (layer, 0, j)),
                  pl.BlockSpec((1, 1, ADA_TN), lambda j: (layer, 0, j))],
        out_specs=pl.BlockSpec((R, ADA_TN), lambda j: (0, j)),
        out_shape=jax.ShapeDtypeStruct((R, N), F32),
        compiler_params=_params("parallel"), name="ada_mod",
    )(c_rows, w_ada, b_ada[:, None, :])


def _tile(g, n):
    return jnp.tile(g.astype(F32), n)[None, :]


def kernel(x, c, ctx, c_ctx, w_ada, b_ada, g_mix, g_ffn, w_in, na_q_g, na_k_g, na_rpb, df_q_g, df_k_g,
           df_lambda, df_subln_g, pool_w, pool_scale, fnet_w, w_branch, w_out, w_router, w_gate_e, w_up_e,
           w_down_e):
    B, T, D = x.shape
    ctx_len = ctx.shape[1]
    rope = _rope_tables(T)
    assert B + 1 <= ADA_ROWS
    c_rows = jnp.concatenate([c, c_ctx[None, :], jnp.zeros((ADA_ROWS - B - 1, D), F32)], axis=0)
    tq = min(DIFF_TQ, T)
    for l in range(DEPTH):
        last = l == DEPTH - 1
        lam_init = 0.8 - 0.6 * math.exp(-0.3 * l)
        mods = _ada_mod(c_rows, w_ada, b_ada, l)
        sh1, sc1, gt1, sh2, sc2, gt2 = [m[:, None, :] for m in jnp.split(mods[:B], 6, axis=-1)]
        bc = lambda m: jnp.broadcast_to(m[None, None, :], (B, 1, D))
        csh1, csc1, cgt1, csh2, csc2, cgt2 = [bc(m) for m in jnp.split(mods[B], 6, axis=-1)]

        w_bf = w_in[l].astype(BF16)
        gq, gk = _tile(na_q_g[l], NA_HEADS), _tile(na_k_g[l], NA_HEADS)
        gdq, gdk = _tile(df_q_g[l], 2 * DIFF_HEADS), _tile(df_k_g[l], 2 * DIFF_HEADS)
        gmix = g_mix[l][None, :]
        gffn = g_ffn[l][None, :]
        subg = df_subln_g[l][:, None].astype(F32)
        wbr = w_branch[l].astype(BF16)
        wf = fnet_w[l].astype(BF16)
        wpool = jax.scipy.linalg.block_diag(*[pool_w[l, gi] for gi in range(len(POOL_WINDOWS))]).astype(BF16)
        pscale = pool_scale[l][None, :].astype(F32)
        wo = w_out[l].astype(BF16)

        naq, dfq, pool_in, fnet_in, gate, nakv, dfk, dfv = _inproj(
            x, gmix, sh1, sc1, w_bf, gq, gk, gdq, gdk, rope, min(INPROJ_ROWS, T))
        (cnaq, cdfq, cpool_in, cfnet_in, cgate, cnakv, cdfk, cdfv) = _inproj(
            ctx, gmix, csh1, csc1, w_bf, gq, gk, gdq, gdk, None, ctx_len)

        y_na = _na_latent(naq, nakv, cnakv, na_rpb[l])
        lamp = df_lambda[l].astype(F32)
        y_df = _diff_attn(dfq, jnp.concatenate([dfk, cdfk], axis=1), jnp.concatenate([dfv, cdfv], axis=2),
                          df_q_g[l], df_k_g[l], lamp, subg, lam_init, tq, DIFF_TK)
        f_real = _fourier_real(fnet_in)
        x_new = _merge(x, gt1, y_na, y_df, pool_in, f_real, gate, wbr, wpool, pscale, wf, wo, MERGE_ROWS)

        if not last:
            yc_na = _ctx_dense_attn(cnaq, cnakv)
            yc_df = _diff_attn(cdfq, cdfk, cdfv, df_q_g[l], df_k_g[l], lamp, subg, lam_init, ctx_len, ctx_len)
            fc_real = _fourier_real(cfnet_in)
            ctx_new = _merge(ctx, cgt1, yc_na, yc_df, cpool_in, fc_real, cgate, wbr, wpool, pscale, wf, wo,
                             ctx_len)

        x = x_new
        x = _ec_ffn(x, gt2, gffn, sh2, sc2, w_router[l], w_gate_e, w_up_e, w_down_e, l, min(ROUTER_ROWS, T),
                    EXPERT_ROWS)
        if not last:
            ctx = ctx_new
            ctx = _ec_ffn(ctx, cgt2, gffn, csh2, csc2, w_router[l], w_gate_e, w_up_e, w_down_e, l, ctx_len,
                          EXPERT_ROWS)
    return x
```

```python
import functools
import math

import jax
import jax.numpy as jnp
import numpy as np
from jax import lax
from jax.experimental import pallas as pl
from jax.experimental.pallas import tpu as pltpu

F32 = jnp.float32
BF16 = jnp.bfloat16

D_MODEL = 1024
DEPTH = 2
GRID_W = 64
EPS = 1e-6
ROPE_BASE = 10000.0

NA_HEADS = 4
NA_HEAD_DIM = 64
NA_WIN_R = 8
NA_WIN_C = 16
NA_W = NA_HEADS * NA_HEAD_DIM

DIFF_HEADS = 4
DIFF_HEAD_DIM = 32
DIFF_QK_W = DIFF_HEADS * 2 * DIFF_HEAD_DIM
DIFF_V_W = DIFF_HEADS * 2 * DIFF_HEAD_DIM

POOL_WINDOWS = (2, 4, 8, 16)
POOL_GROUP_W = 64
POOL_W = len(POOL_WINDOWS) * POOL_GROUP_W
FNET_W = 256
N_BRANCH = 4
BRANCH_W = 256

OFF_NA_Q = 0
OFF_DF_Q = OFF_NA_Q + NA_W
OFF_POOL = OFF_DF_Q + DIFF_QK_W
OFF_FNET = OFF_POOL + POOL_W
OFF_GATE = OFF_FNET + FNET_W
OFF_KV = OFF_GATE + N_BRANCH * D_MODEL
KV_W = 2 * NA_W + DIFF_QK_W + DIFF_V_W
IN_COLS = OFF_KV + KV_W

N_EXPERTS = 16
EC_FACTOR = 2
EXPERT_FF = 1408

VMEM_LIMIT_BYTES = 56 * 1024 * 1024
LANES = 128
NEG_BIG = -1e30

INPROJ_ROWS = 512
MERGE_ROWS = 1024
ROUTER_ROWS = 2048
EXPERT_ROWS = 512
DIFF_TQ, DIFF_TK = 2048, 1280


def _params(*sem):
    return pltpu.CompilerParams(dimension_semantics=sem, vmem_limit_bytes=VMEM_LIMIT_BYTES)


def _split_bf16(a):
    hi = a.astype(BF16)
    lo = (a - hi.astype(F32)).astype(BF16)
    return hi, lo


def _dot(a, b):
    return jnp.dot(a, b, preferred_element_type=F32)


def _dot_nt(a, b):
    return lax.dot_general(a, b, (((1,), (1,)), ((), ())), preferred_element_type=F32)


def _group_rmsnorm(p, bd_ref, g):
    hi, lo = _split_bf16(p * p)
    ms = _dot(hi, bd_ref[...]) + _dot(lo, bd_ref[...])
    return p * lax.rsqrt(ms + EPS) * g


def _rope256(y, cos, s_next, s_prev):
    outs = []
    for half in range(2):
        z = y[:, half * LANES:(half + 1) * LANES]
        outs.append(z * cos + pltpu.roll(z, LANES - 8, 1) * s_next + pltpu.roll(z, 8, 1) * s_prev)
    return jnp.concatenate(outs, axis=1)


def _inproj_kernel(*refs, use_rope):
    if use_rope:
        (x_ref, g_ref, sh_ref, sc_ref, w_ref, gq_ref, gk_ref, gdq_ref, gdk_ref, bd64_ref, bd32_ref,
         cos_ref, sn_ref, sp_ref,
         naq_ref, dfq_ref, pool_ref, fnet_ref, gate_ref, nakv_ref, dfk_ref, dfv_ref) = refs
    else:
        (x_ref, g_ref, sh_ref, sc_ref, w_ref, gq_ref, gk_ref, gdq_ref, gdk_ref, bd64_ref, bd32_ref,
         naq_ref, dfq_ref, pool_ref, fnet_ref, gate_ref, nakv_ref, dfk_ref, dfv_ref) = refs
    x = x_ref[0]
    ms = jnp.mean(x * x, axis=-1, keepdims=True)
    y = x * lax.rsqrt(ms + EPS) * g_ref[...]
    h = (y * (1.0 + sc_ref[0]) + sh_ref[0]).astype(BF16)

    def seg(lo, n):
        return _dot(h, w_ref[:, lo:lo + n])

    def rope(v):
        if not use_rope:
            return v
        return _rope256(v, cos_ref[...], sn_ref[...], sp_ref[...])

    naq = _group_rmsnorm(seg(OFF_NA_Q, NA_W), bd64_ref, gq_ref[...])
    naq_ref[0] = (naq * (NA_HEAD_DIM ** -0.5)).astype(BF16)
    dfq = rope(_group_rmsnorm(seg(OFF_DF_Q, DIFF_QK_W), bd32_ref, gdq_ref[...]))
    dfq_ref[0] = (dfq * (math.log2(math.e) * DIFF_HEAD_DIM ** -0.5)).T.astype(BF16)
    pool_ref[0] = seg(OFF_POOL, POOL_W)
    fnet_ref[0] = seg(OFF_FNET, FNET_W)
    for j in range(0, N_BRANCH * D_MODEL, 512):
        gate_ref[0, :, j:j + 512] = jax.nn.sigmoid(seg(OFF_GATE + j, 512)).astype(BF16)
    nak = _group_rmsnorm(seg(OFF_KV, NA_W), bd64_ref, gk_ref[...])
    nakv_ref[0, :, 0:NA_W] = nak.astype(BF16)
    nakv_ref[0, :, NA_W:2 * NA_W] = seg(OFF_KV + NA_W, NA_W).astype(BF16)
    dfk = rope(_group_rmsnorm(seg(OFF_KV + 2 * NA_W, DIFF_QK_W), bd32_ref, gdk_ref[...]))
    dfk_ref[0] = dfk.astype(BF16)
    dfv_ref[0] = seg(OFF_KV + 2 * NA_W + DIFF_QK_W, DIFF_V_W).T.astype(BF16)


def _block_diag_mean(width, group):
    i = jnp.arange(width)
    return jnp.where((i[:, None] // group) == (i[None, :] // group), 1.0 / group, 0.0).astype(BF16)


def _rope_tables(T):
    t = jnp.arange(T)
    j = jnp.arange(LANES)
    jj = j % DIFF_HEAD_DIM
    quarter = DIFF_HEAD_DIM // 4
    use_row = jj < DIFF_HEAD_DIM // 2
    first = (jj % (DIFF_HEAD_DIM // 2)) < quarter
    inv = ROPE_BASE ** (-(jj % quarter).astype(F32) / quarter)
    pos = jnp.where(use_row[None, :], (t // GRID_W)[:, None], (t % GRID_W)[:, None]).astype(F32)
    ang = pos * inv[None, :]
    cos, sin = jnp.cos(ang), jnp.sin(ang)
    s_next = jnp.where(first[None, :], -sin, 0.0)
    s_prev = jnp.where(first[None, :], 0.0, sin)
    return cos, s_next, s_prev


def _inproj(x, g, sh, sc, w_bf16, gq, gk, gdq, gdk, rope, tm):
    B, T, D = x.shape
    use_rope = rope is not None
    row = lambda b, i: (b, i, 0)
    const2 = lambda b, i: (0, 0)
    perb = lambda b, i: (b, 0, 0)
    in_specs = [
        pl.BlockSpec((1, tm, D), row),
        pl.BlockSpec((1, D), const2),
        pl.BlockSpec((1, 1, D), perb),
        pl.BlockSpec((1, 1, D), perb),
        pl.BlockSpec((D, IN_COLS), const2),
        pl.BlockSpec((1, NA_W), const2), pl.BlockSpec((1, NA_W), const2),
        pl.BlockSpec((1, DIFF_QK_W), const2), pl.BlockSpec((1, DIFF_QK_W), const2),
        pl.BlockSpec((NA_W, NA_W), const2), pl.BlockSpec((DIFF_QK_W, DIFF_QK_W), const2),
    ]
    args = [x, g, sh, sc, w_bf16, gq, gk, gdq, gdk,
            _block_diag_mean(NA_W, NA_HEAD_DIM), _block_diag_mean(DIFF_QK_W, DIFF_HEAD_DIM)]
    if use_rope:
        in_specs += [pl.BlockSpec((tm, LANES), lambda b, i: (i, 0))] * 3
        args += list(rope)
    outs = [(NA_W, BF16, False), (DIFF_QK_W, BF16, True), (POOL_W, F32, False), (FNET_W, F32, False),
            (N_BRANCH * D_MODEL, BF16, False), (2 * NA_W, BF16, False), (DIFF_QK_W, BF16, False),
            (DIFF_V_W, BF16, True)]
    col = lambda b, i: (b, 0, i)
    out_shape = [jax.ShapeDtypeStruct((B, w, T) if tr else (B, T, w), dt) for w, dt, tr in outs]
    out_specs = [pl.BlockSpec((1, w, tm), col) if tr else pl.BlockSpec((1, tm, w), row) for w, _, tr in outs]
    return pl.pallas_call(
        functools.partial(_inproj_kernel, use_rope=use_rope),
        grid=(B, T // tm), in_specs=in_specs, out_specs=out_specs, out_shape=out_shape,
        compiler_params=_params("parallel", "parallel"), name="inproj",
    )(*args)


NA_ROWS_PER_STEP = 4


def _na_kernel(q_ref, kv0_ref, kv1_ref, kv2_ref, ckv_ref, bias_ref, o_ref):
    q = q_ref[0]
    kv = jnp.concatenate([kv0_ref[0], kv1_ref[0], kv2_ref[0]], axis=0)
    ckv = ckv_ref[0]
    for h in range(NA_HEADS):
        ks = slice(h * NA_HEAD_DIM, (h + 1) * NA_HEAD_DIM)
        vs = slice(NA_W + h * NA_HEAD_DIM, NA_W + (h + 1) * NA_HEAD_DIM)
        qh = q[:, ks]
        s = _dot_nt(qh, kv[:, ks]) + bias_ref[0, h]
        sc = _dot_nt(qh, ckv[:, ks])
        m = jnp.maximum(jnp.max(s, axis=-1, keepdims=True), jnp.max(sc, axis=-1, keepdims=True))
        p = jnp.exp(s - m)
        pc = jnp.exp(sc - m)
        l = jnp.sum(p, axis=-1, keepdims=True) + jnp.sum(pc, axis=-1, keepdims=True)
        o = _dot(p.astype(BF16), kv[:, vs]) + _dot(pc.astype(BF16), ckv[:, vs])
        o_ref[0, :, ks] = o / l


def _na_bias_table(rpb, rows):
    R = NA_ROWS_PER_STEP
    nb = rows // R
    col = jnp.arange(GRID_W)
    cs = jnp.clip(col - NA_WIN_C // 2, 0, GRID_W - NA_WIN_C)
    col_ok = (col[None, :] >= cs[:, None]) & (col[None, :] < cs[:, None] + NA_WIN_C)
    ci = jnp.clip(col[None, :] - col[:, None] + (NA_WIN_C - 1), 0, 2 * NA_WIN_C - 2)
    j = jnp.array([0, 1, nb - 1])[:, None, None]
    r = R * j + jnp.arange(R)[None, :, None]
    krow = R * (j - 1) + jnp.arange(3 * R)[None, None, :]
    rs = jnp.clip(r - NA_WIN_R // 2, 0, rows - NA_WIN_R)
    row_ok = (krow >= rs) & (krow < rs + NA_WIN_R) & (krow >= 0) & (krow < rows)
    ri = jnp.clip(krow - r + (NA_WIN_R - 1), 0, 2 * NA_WIN_R - 2)
    tab = rpb.astype(F32)[:, ri][:, :, :, :, ci]
    ok = row_ok[None, :, :, :, None, None] & col_ok[None, None, None, None]
    tab = jnp.where(ok, tab, NEG_BIG)
    tab = tab.transpose(1, 0, 2, 4, 3, 5)
    return tab.reshape(3, NA_HEADS, R * GRID_W, 3 * R * GRID_W)


def _na_latent(naq, nakv, cnakv, rpb):
    B, T, _ = naq.shape
    rows = T // GRID_W
    R = NA_ROWS_PER_STEP
    assert rows % R == 0 and rows // R >= 3 and rows >= NA_WIN_R and R + NA_WIN_R <= 3 * R
    nb = rows // R
    tb = R * GRID_W
    ctx_len = cnakv.shape[1]
    pattern = lambda j: (j > 0).astype(jnp.int32) + (j == nb - 1).astype(jnp.int32)
    kv_spec = lambda d: pl.BlockSpec((1, tb, 2 * NA_W), lambda b, j: (b, jnp.clip(j + d, 0, nb - 1), 0))
    return pl.pallas_call(
        _na_kernel, grid=(B, nb),
        in_specs=[
            pl.BlockSpec((1, tb, NA_W), lambda b, j: (b, j, 0)),
            kv_spec(-1), kv_spec(0), kv_spec(1),
            pl.BlockSpec((1, ctx_len, 2 * NA_W), lambda b, j: (b, 0, 0)),
            pl.BlockSpec((1, NA_HEADS, tb, 3 * tb), lambda b, j: (pattern(j), 0, 0, 0)),
        ],
        out_specs=pl.BlockSpec((1, tb, NA_W), lambda b, j: (b, j, 0)),
        out_shape=jax.ShapeDtypeStruct((B, T, NA_W), F32),
        compiler_params=_params("parallel", "parallel"), name="na_latent",
    )(naq, nakv, nakv, nakv, cnakv, _na_bias_table(rpb, rows))


DIFF_BOUND_MAX = 60.0


def _diff_kernel(lam_ref, g_ref, qt_ref, k_ref, vt_ref, o_ref, qx_sc, m_sc, l_sc, acc_sc, *,
                 lam_init, tk, nk, online):
    h = pl.program_id(1)
    qt = qt_ref[0]
    tq = qt.shape[1]
    grp = lax.broadcasted_iota(jnp.int32, qt.shape, 0) // DIFF_HEAD_DIM
    for mi in range(2):
        qx_sc[mi] = jnp.where(grp == 2 * h + mi, qt, jnp.zeros_like(qt))
    m_sc[...] = jnp.full(m_sc.shape, -jnp.inf, F32)
    l_sc[...] = jnp.zeros(l_sc.shape, F32)
    acc_sc[...] = jnp.zeros(acc_sc.shape, F32)

    def body(k, carry):
        off = pl.multiple_of(k * tk, tk)
        kb = k_ref[0, pl.ds(off, tk), :]
        vtb = vt_ref[0, :, pl.ds(off, tk)]
        for mi in range(2):
            s = _dot(kb, qx_sc[mi])
            if online:
                m_prev = m_sc[mi]
                m_new = jnp.maximum(m_prev, jnp.max(s, axis=0, keepdims=True))
                alpha = jnp.exp2(m_prev - m_new)
                p = jnp.exp2(s - m_new)
                l_sc[mi] = alpha * l_sc[mi] + p.reshape(tk // 8, 8, tq).sum(axis=0)
                acc_sc[mi] = alpha * acc_sc[mi] + _dot(vtb, p.astype(BF16))
                m_sc[mi] = m_new
            else:
                p = jnp.exp2(s)
                l_sc[mi] += p.reshape(tk // 8, 8, tq).sum(axis=0)
                acc_sc[mi] += _dot(vtb, p.astype(BF16))
        return carry

    lax.fori_loop(0, nk, body, 0)

    lv = lam_ref[...]
    lam = (jnp.exp(jnp.sum(lv[0:1] * lv[1:2], axis=-1, keepdims=True))
           - jnp.exp(jnp.sum(lv[2:3] * lv[3:4], axis=-1, keepdims=True)) + lam_init)
    l0 = jnp.sum(l_sc[0], axis=0, keepdims=True)
    l1 = jnp.sum(l_sc[1], axis=0, keepdims=True)
    o = acc_sc[0] / l0 - lam * (acc_sc[1] / l1)
    ms = jnp.mean(o * o, axis=0, keepdims=True)
    o_ref[0] = o * lax.rsqrt(ms + EPS) * g_ref[...] * (1.0 - lam_init)


def _diff_call(qt, k, vt, lam_params, subln_g, lam_init, tq, tk, online):
    B, W, Tq = qt.shape
    Tk = k.shape[1]
    dv = W // DIFF_HEADS
    return pl.pallas_call(
        functools.partial(_diff_kernel, lam_init=lam_init, tk=tk, nk=Tk // tk, online=online),
        grid=(B, DIFF_HEADS, Tq // tq),
        in_specs=[
            pl.BlockSpec((4, DIFF_HEAD_DIM), lambda b, h, i: (0, 0)),
            pl.BlockSpec((dv, 1), lambda b, h, i: (0, 0)),
            pl.BlockSpec((1, W, tq), lambda b, h, i: (b, 0, i)),
            pl.BlockSpec((1, Tk, W), lambda b, h, i: (b, 0, 0)),
            pl.BlockSpec((1, dv, Tk), lambda b, h, i: (b, h, 0)),
        ],
        out_specs=pl.BlockSpec((1, dv, tq), lambda b, h, i: (b, h, i)),
        out_shape=jax.ShapeDtypeStruct((B, W, Tq), F32),
        scratch_shapes=[pltpu.VMEM((2, W, tq), BF16), pltpu.VMEM((2, 1, tq), F32),
                        pltpu.VMEM((2, 8, tq), F32), pltpu.VMEM((2, dv, tq), F32)],
        compiler_params=_params("parallel", "parallel", "parallel"),
        name="diff_attn_online" if online else "diff_attn_bounded",
    )(lam_params, subln_g, qt, k, vt)


def _diff_attn(qt, dfk, vt, gq, gk, lam_params, subln_g, lam_init, tq, tk):
    bound = (1.02 * math.log2(math.e) * DIFF_HEAD_DIM ** 0.5) * jnp.max(jnp.abs(gq)) * jnp.max(jnp.abs(gk))
    args = (qt, dfk, vt, lam_params, subln_g, lam_init, tq, tk)
    return lax.cond(bound <= DIFF_BOUND_MAX,
                    lambda: _diff_call(*args, online=False), lambda: _diff_call(*args, online=True))


POOL_HALO = 8


def _pool_mixer(u, prev, nxt, wp_ref, scale, n_total):
    tm = u.shape[0]
    i = pl.program_id(1)
    prev = jnp.where(i > 0, prev, 0.0)
    nxt = jnp.where(i < pl.num_programs(1) - 1, nxt, 0.0)
    ext = jnp.concatenate([prev, u, nxt], axis=0)
    sh = lambda d: ext[POOL_HALO + d:POOL_HALO + d + tm]
    t = i * tm + lax.broadcasted_iota(jnp.int32, (tm, 1), 0)
    lane_grp = lax.broadcasted_iota(jnp.int32, (1, POOL_W), 1) // POOL_GROUP_W
    win = None
    mean = None
    prev_half = 0
    for gi, w in enumerate(POOL_WINDOWS):
        half = w // 2
        for d in range(prev_half, half):
            add = sh(d) + sh(-d - 1)
            win = add if win is None else win + add
        prev_half = half
        cnt = (jnp.minimum(t + half, n_total) - jnp.maximum(t - half, 0)).astype(F32)
        m_w = win / cnt
        mean = m_w if mean is None else jnp.where(lane_grp >= gi, m_w, mean)
    return _dot((mean - u).astype(BF16), wp_ref[...]) * scale


def _merge_kernel(x_ref, gt_ref, yna_ref, ydf_ref, u_ref, up_ref, un_ref, f_ref, gate_ref, wbr_ref, wp_ref,
                  ps_ref, wf_ref, wo_ref, o_ref, *, n_total):
    yf = _dot(f_ref[0].astype(BF16), wf_ref[...])
    ypool = _pool_mixer(u_ref[0], up_ref[0], un_ref[0], wp_ref, ps_ref[...], n_total)
    ys = (yna_ref[0], ydf_ref[0].T, ypool, yf)
    acc = None
    for i, y in enumerate(ys):
        g = gate_ref[0, :, i * D_MODEL:(i + 1) * D_MODEL].astype(F32)
        term = g * _dot(y.astype(BF16), wbr_ref[i])
        acc = term if acc is None else acc + term
    o_ref[0] = x_ref[0] + gt_ref[0] * _dot(acc.astype(BF16), wo_ref[...])


def _merge(x, gt, y_na, y_df, pool_in, f_real, gate, w_br, w_pool_bd, pool_scale, w_f, w_o, tm):
    B, T, D = x.shape
    assert tm % POOL_HALO == 0 and POOL_HALO >= max(POOL_WINDOWS) // 2
    hb = tm // POOL_HALO
    row = lambda b, i: (b, i, 0)
    const2 = lambda b, i: (0, 0)
    return pl.pallas_call(
        functools.partial(_merge_kernel, n_total=T), grid=(B, T // tm),
        in_specs=[
            pl.BlockSpec((1, tm, D), row),
            pl.BlockSpec((1, 1, D), lambda b, i: (b, 0, 0)),
            pl.BlockSpec((1, tm, BRANCH_W), row), pl.BlockSpec((1, BRANCH_W, tm), lambda b, i: (b, 0, i)),
            pl.BlockSpec((1, tm, POOL_W), row),
            pl.BlockSpec((1, POOL_HALO, POOL_W), lambda b, i: (b, jnp.maximum(i * hb - 1, 0), 0)),
            pl.BlockSpec((1, POOL_HALO, POOL_W), lambda b, i: (b, jnp.minimum((i + 1) * hb, T // POOL_HALO - 1), 0)),
            pl.BlockSpec((1, tm, FNET_W), row),
            pl.BlockSpec((1, tm, N_BRANCH * D), row),
            pl.BlockSpec((N_BRANCH, BRANCH_W, D), lambda b, i: (0, 0, 0)),
            pl.BlockSpec((POOL_W, POOL_W), const2), pl.BlockSpec((1, POOL_W), const2),
            pl.BlockSpec((FNET_W, FNET_W), const2),
            pl.BlockSpec((D, D), const2),
        ],
        out_specs=pl.BlockSpec((1, tm, D), row),
        out_shape=jax.ShapeDtypeStruct((B, T, D), F32),
        compiler_params=_params("parallel", "parallel"), name="merge",
    )(x, gt, y_na, y_df, pool_in, pool_in, pool_in, f_real, gate, w_br, w_pool_bd, pool_scale, w_f, w_o)


TOK_BLOCK = LANES
SEL_GROUP = 8
COMBINE_WIN = 48
COMBINE_BUFFERS = 3
BF16_ROWS = 16
SLOT_SPLIT = 64.0


def _router_kernel(x_ref, g_ref, sh_ref, sc_ref, wrh_ref, wrl_ref, h_ref, aff_ref):
    x = x_ref[0]
    ms = jnp.mean(x * x, axis=-1, keepdims=True)
    h = x * lax.rsqrt(ms + EPS) * g_ref[...] * (1.0 + sc_ref[0]) + sh_ref[0]
    h_ref[0] = h.astype(BF16)
    hi, lo = _split_bf16(h)
    logits = _dot_nt(wrh_ref[...], hi) + _dot_nt(wrh_ref[...], lo) + _dot_nt(wrl_ref[...], hi)
    logits = logits - jnp.max(logits, axis=0, keepdims=True)
    e = jnp.exp(logits)
    aff = e / jnp.sum(e, axis=0, keepdims=True)
    for j in range(aff.shape[1] // TOK_BLOCK):
        aff_ref[0, :, j, :] = aff[:, j * TOK_BLOCK:(j + 1) * TOK_BLOCK]


def _router(x, g, sh, sc, w_router, tm):
    B, T, D = x.shape
    E = w_router.shape[1]
    wrh, wrl = _split_bf16(w_router.T)
    row = lambda b, i: (b, i, 0)
    return pl.pallas_call(
        _router_kernel, grid=(B, T // tm),
        in_specs=[
            pl.BlockSpec((1, tm, D), row), pl.BlockSpec((1, D), lambda b, i: (0, 0)),
            pl.BlockSpec((1, 1, D), lambda b, i: (b, 0, 0)), pl.BlockSpec((1, 1, D), lambda b, i: (b, 0, 0)),
            pl.BlockSpec((E, D), lambda b, i: (0, 0)), pl.BlockSpec((E, D), lambda b, i: (0, 0)),
        ],
        out_specs=[pl.BlockSpec((1, tm, D), row),
                   pl.BlockSpec((1, E, tm // TOK_BLOCK, TOK_BLOCK), lambda b, i: (b, 0, i, 0))],
        out_shape=[jax.ShapeDtypeStruct((B, T, D), BF16),
                   jax.ShapeDtypeStruct((B, E, T // TOK_BLOCK, TOK_BLOCK), F32)],
        compiler_params=_params("parallel", "parallel"), name="router",
    )(x, g, sh, sc, wrh, wrl)


def _select_kernel(aff_ref, u_ref, ls_ref, idx_ref, gate_ref, pos_ref, off_ref, tot_ref, *, cap):
    G, nb, _ = aff_ref.shape
    aff = aff_ref[...]
    bits = pltpu.bitcast(aff, jnp.int32)
    kf = float(cap)
    ones_f = lambda m: jnp.where(m, 1.0, 0.0)
    count = lambda m: jnp.sum(ones_f(m), axis=(1, 2), keepdims=True)

    def search(it, lo):
        cand = lo | jnp.left_shift(jnp.int32(1), 30 - it)
        return jnp.where(count(bits >= cand) >= kf, cand, lo)

    thr = lax.fori_loop(0, 31, search, jnp.zeros((G, 1, 1), jnp.int32))
    need = kf - count(bits > thr)

    u = u_ref[...]
    ls = ls_ref[...]
    ones_m = jnp.ones((LANES, LANES), BF16)

    def prefix(mf):
        mb = mf.astype(BF16)
        tot = _dot(mb, ones_m)
        off = _dot(ls, tot.astype(BF16))
        return _dot(mb, u) + off, off, tot

    blk_i = lax.broadcasted_iota(jnp.int32, (nb, LANES), 0)
    lane_i = lax.broadcasted_iota(jnp.int32, (nb, LANES), 1)
    s_row = lax.broadcasted_iota(jnp.int32, (1, cap), 1).astype(F32)
    col_nb = lax.broadcasted_iota(jnp.int32, (nb, 1), 0).astype(F32)
    col_l = lax.broadcasted_iota(jnp.int32, (LANES, 1), 0).astype(F32)
    for g in range(G):
        gt = bits[g] > thr[g]
        eq = bits[g] == thr[g]
        eqf = ones_f(eq)
        tie_before = prefix(eqf)[0] - eqf
        sel = gt | (eq & (tie_before < need[g]))
        pin, off, tot = prefix(ones_f(sel))
        pos_ref[g] = jnp.where(sel, pin - 1.0, -1.0).astype(jnp.int32)
        diag = blk_i == lane_i
        off_ref[g] = jnp.sum(jnp.where(diag, off, 0.0), axis=0, keepdims=True).astype(jnp.int32)
        tot_ref[g] = jnp.sum(jnp.where(diag, tot, 0.0), axis=0, keepdims=True).astype(jnp.int32)
        blk = jnp.sum(ones_f((off + tot)[:, 0:1] <= s_row), axis=0, keepdims=True)
        oh = jnp.where(col_nb == blk, 1.0, 0.0).astype(BF16)
        pin_t = pin.T
        p_hi = jnp.floor(pin_t * (1.0 / LANES))
        p_lo = pin_t - p_hi * LANES
        rows = _dot(p_hi.astype(BF16), oh) * LANES + _dot(p_lo.astype(BF16), oh)
        lane = jnp.sum(ones_f(rows <= s_row), axis=0, keepdims=True)
        idx_ref[g] = (blk * LANES + lane).astype(jnp.int32)
        a_t = aff[g].T
        a1 = a_t.astype(BF16)
        r1 = a_t - a1.astype(F32)
        a2 = r1.astype(BF16)
        a3 = (r1 - a2.astype(F32)).astype(BF16)
        arow = _dot(a1, oh) + _dot(a2, oh) + _dot(a3, oh)
        gate_ref[g] = jnp.sum(jnp.where(col_l == lane, arow, 0.0), axis=0, keepdims=True)


def _select(aff, cap):
    B, E, nb, _ = aff.shape
    R = B * E
    G = SEL_GROUP
    i = np.arange(LANES)
    u = jnp.asarray(i[:, None] <= i[None, :], BF16)
    j = np.arange(nb)
    ls = jnp.asarray(j[None, :] < j[:, None], BF16)
    grp = lambda r: (r, 0, 0)
    outs = pl.pallas_call(
        functools.partial(_select_kernel, cap=cap), grid=(R // G,),
        in_specs=[pl.BlockSpec((G, nb, LANES), grp), pl.BlockSpec((LANES, LANES), lambda r: (0, 0)),
                  pl.BlockSpec((nb, nb), lambda r: (0, 0))],
        out_specs=[pl.BlockSpec((G, 1, cap), grp), pl.BlockSpec((G, 1, cap), grp),
                   pl.BlockSpec((G, nb, LANES), grp), pl.BlockSpec((G, 1, LANES), grp),
                   pl.BlockSpec((G, 1, LANES), grp)],
        out_shape=[jax.ShapeDtypeStruct((R, 1, cap), jnp.int32), jax.ShapeDtypeStruct((R, 1, cap), F32),
                   jax.ShapeDtypeStruct((R, nb, LANES), jnp.int32), jax.ShapeDtypeStruct((R, 1, LANES), jnp.int32),
                   jax.ShapeDtypeStruct((R, 1, LANES), jnp.int32)],
        compiler_params=_params("parallel"), name="select",
    )(aff.reshape(R, nb, LANES), u, ls)
    idx, gate, pos, off, tot = outs
    return (idx.reshape(B, E, cap), gate.reshape(B, E, cap), pos.reshape(B, E, nb, LANES),
            off[:, 0, :nb].reshape(B, E, nb), tot[:, 0, :nb].reshape(B, E, nb))


def _combine_kernel(off_ref, tot_ref, x_ref, gt_ref, pos_ref, ex_ref, ye_hbm, o_ref, buf, xbuf, acc_sc, sem,
                    xsem, *, nt, cap, W):
    n = pl.program_id(0)
    E = N_EXPERTS

    def window(step, e):
        b = step // nt
        o = off_ref[(b * E + e) * nt + step % nt]
        return b, o, jnp.minimum((o // BF16_ROWS) * BF16_ROWS, cap - W)

    def fetch(step, slot):
        for e in range(E):
            b, _, st = window(step, e)
            pltpu.make_async_copy(ye_hbm.at[b, e, pl.ds(pl.multiple_of(st, BF16_ROWS), W)], buf.at[slot, e],
                                  sem.at[slot]).start()

    depth = buf.shape[0]
    ahead = depth - 1

    @pl.when(n == 0)
    def _():
        for s in range(ahead):
            @pl.when(s < pl.num_programs(0))
            def _():
                fetch(s, s)

    @pl.when(n + ahead < pl.num_programs(0))
    def _():
        fetch(n + ahead, (n + ahead) % depth)

    slot = n % depth
    for e in range(E):
        pltpu.make_async_copy(ye_hbm.at[0, 0, pl.ds(0, W)], buf.at[slot, e], sem.at[slot]).wait()

    pos = pos_ref[0, :, (n % nt) % pos_ref.shape[2], :].astype(F32).T
    p_hi = jnp.floor(pos * (1.0 / SLOT_SPLIT))
    p_lo = pos - SLOT_SPLIT * p_hi
    ex = ex_ref[...]
    posx = SLOT_SPLIT * _dot(p_hi.astype(BF16), ex) + _dot(p_lo.astype(BF16), ex)
    col = lax.broadcasted_iota(jnp.int32, (1, E * W), 1)
    tgt = jnp.zeros((1, E * W), jnp.int32)
    for e in range(E):
        _, _, st = window(n, e)
        tgt = jnp.where(col // W == e, st + col % W, tgt)
    onehot = jnp.where(posx == tgt.astype(F32), 1.0, 0.0).astype(BF16)
    acc_sc[...] = _dot(onehot, buf[slot].reshape(E * W, buf.shape[-1]))

    w_i = lax.broadcasted_iota(jnp.int32, (1, W), 1)
    for e in range(E):
        b, o, st = window(n, e)
        t = tot_ref[(b * E + e) * nt + n % nt]
        n_extra = jnp.maximum(o + t - (st + W) + (W - 1), 0) // W

        def extra(j, carry, e=e, b=b, st=st):
            base = st + W * (j + 1)
            src = jnp.minimum(base, cap - W)
            cp = pltpu.make_async_copy(ye_hbm.at[b, e, pl.ds(pl.multiple_of(src, BF16_ROWS), W)], xbuf,
                                       xsem.at[0])
            cp.start()
            cp.wait()
            slot_i = (src + w_i).astype(F32)
            oh = jnp.where((pos[:, e:e + 1] == slot_i) & (slot_i >= base.astype(F32)), 1.0, 0.0).astype(BF16)
            acc_sc[...] += _dot(oh, xbuf[...])
            return carry

        lax.fori_loop(0, n_extra, extra, 0)

    o_ref[0] = x_ref[0] + gt_ref[0] * acc_sc[...]


def _combine(x, gt, pos, off, tot, ye):
    B, T, D = x.shape
    E, cap = ye.shape[1], ye.shape[2]
    nt = T // TOK_BLOCK
    W = min(COMBINE_WIN, cap)
    pr = min(8, nt)
    assert nt % pr == 0
    assert W % BF16_ROWS == 0 and (cap - W) % BF16_ROWS == 0 and (E * W) % LANES == 0
    expand = jnp.asarray(np.arange(E)[:, None] == (np.arange(E * W)[None, :] // W), BF16)
    tile = lambda n, off_r, tot_r: (n // nt, n % nt, 0)
    grid_spec = pltpu.PrefetchScalarGridSpec(
        num_scalar_prefetch=2, grid=(B * nt,),
        in_specs=[
            pl.BlockSpec((1, TOK_BLOCK, D), tile),
            pl.BlockSpec((1, 1, D), lambda n, off_r, tot_r: (n // nt, 0, 0)),
            pl.BlockSpec((1, E, pr, TOK_BLOCK), lambda n, off_r, tot_r: (n // nt, 0, (n % nt) // pr, 0)),
            pl.BlockSpec((E, E * W), lambda n, off_r, tot_r: (0, 0)),
            pl.BlockSpec(memory_space=pl.ANY),
        ],
        out_specs=pl.BlockSpec((1, TOK_BLOCK, D), tile),
        scratch_shapes=[pltpu.VMEM((COMBINE_BUFFERS, E, W, D), BF16), pltpu.VMEM((W, D), BF16),
                        pltpu.VMEM((TOK_BLOCK, D), F32),
                        pltpu.SemaphoreType.DMA((COMBINE_BUFFERS,)), pltpu.SemaphoreType.DMA((1,))],
    )
    return pl.pallas_call(
        functools.partial(_combine_kernel, nt=nt, cap=cap, W=W), grid_spec=grid_spec,
        out_shape=jax.ShapeDtypeStruct((B, T, D), F32),
        compiler_params=_params("arbitrary"), name="combine",
    )(off.reshape(-1), tot.reshape(-1), x, gt, pos, expand, ye)


FF_CHUNKS = ((0, 512), (512, 512), (1024, 384))


def _expert_kernel(xe_ref, g_ref, wg_ref, wu_ref, wd_ref, o_ref):
    xe = xe_ref[0, 0]
    acc = None
    for lo, n in FF_CHUNKS:
        a = _dot(xe, wg_ref[0, 0, :, lo:lo + n].astype(BF16))
        u = _dot(xe, wu_ref[0, 0, :, lo:lo + n].astype(BF16))
        hid = (a * jax.nn.sigmoid(a) * u).astype(BF16)
        part = _dot(hid, wd_ref[0, 0, lo:lo + n, :].astype(BF16))
        acc = part if acc is None else acc + part
    g_col = jnp.broadcast_to(g_ref[0, 0], (8, acc.shape[0])).T[:, 0:1]
    o_ref[0, 0] = (acc * g_col).astype(o_ref.dtype)


def _experts(xe, g, wg, wu, wd, layer, tm):
    B, E, cap, D = xe.shape
    F = wg.shape[-1]
    assert F == EXPERT_FF
    tile = lambda e, b, i: (b, e, i, 0)
    wsel = lambda e, b, i: (layer, e, 0, 0)
    return pl.pallas_call(
        _expert_kernel, grid=(E, B, cap // tm),
        in_specs=[
            pl.BlockSpec((1, 1, tm, D), tile), pl.BlockSpec((1, 1, 1, tm), lambda e, b, i: (b, e, 0, i)),
            pl.BlockSpec((1, 1, D, F), wsel), pl.BlockSpec((1, 1, D, F), wsel), pl.BlockSpec((1, 1, F, D), wsel),
        ],
        out_specs=pl.BlockSpec((1, 1, tm, D), tile),
        out_shape=jax.ShapeDtypeStruct((B, E, cap, D), BF16),
        compiler_params=_params("parallel", "parallel", "parallel"), name="experts",
    )(xe, g, wg, wu, wd)


def _ec_ffn(x, gt, g, sh, sc, w_router, wg, wu, wd, layer, tm_router, tm_expert):
    B, N, D = x.shape
    cap = max(1, EC_FACTOR * N // N_EXPERTS)
    assert N <= TOK_BLOCK * LANES and N % TOK_BLOCK == 0
    nt = N // TOK_BLOCK
    h, aff = _router(x, g, sh, sc, w_router, tm_router)
    aff = jnp.pad(aff, ((0, 0), (0, 0), (0, LANES - nt), (0, 0)))
    idx, gsel, pos, off, tot = _select(aff, cap)
    flat = (idx + (jnp.arange(B) * N)[:, None, None]).reshape(-1)
    xe = jnp.take(h.reshape(B * N, D), flat, axis=0, mode="clip").reshape(B, N_EXPERTS, cap, D)
    ye = _experts(xe, gsel[:, :, None, :], wg, wu, wd, layer, min(tm_expert, cap))
    return _combine(x, gt, pos[:, :, :nt], off[:, :, :nt], tot[:, :, :nt], ye)


DFT_RADIX = 128


def _dot3(a_hi, a_lo, b_hi, b_lo):
    return _dot(a_hi, b_hi) + _dot(a_lo, b_hi) + _dot(a_hi, b_lo)


def _cos_sin(n_rows, n_cols, period):
    k = (np.arange(n_rows)[:, None] * np.arange(n_cols)[None, :]) % period
    ang = 2.0 * np.pi * k.astype(np.float64) / period
    return np.cos(ang), np.sin(ang)


def _const_split(m):
    return _split_bf16(jnp.asarray(m, F32))


def _chan_dft_kernel(u_ref, mh_ref, ml_ref, xr_ref, xi_ref):
    uh, ul = _split_bf16(u_ref[0])
    y = _dot3(uh, ul, mh_ref[...], ml_ref[...])
    c = xr_ref.shape[-1]
    xr_ref[0] = y[:, :c]
    xi_ref[0] = y[:, c:]


def _chan_dft(u, tm):
    B, N, C = u.shape
    cc, sc = _cos_sin(C, C, C)
    mh, ml = _const_split(np.concatenate([cc, -sc], axis=1))
    row = lambda b, i: (b, i, 0)
    const2 = lambda b, i: (0, 0)
    return pl.pallas_call(
        _chan_dft_kernel, grid=(B, N // tm),
        in_specs=[pl.BlockSpec((1, tm, C), row), pl.BlockSpec((C, 2 * C), const2), pl.BlockSpec((C, 2 * C), const2)],
        out_specs=[pl.BlockSpec((1, tm, C), row)] * 2,
        out_shape=[jax.ShapeDtypeStruct((B, N, C), F32)] * 2,
        compiler_params=_params("parallel", "parallel"), name="chan_dft",
    )(u, mh, ml)


def _dft_left_kernel(xr_ref, xi_ref, mh_ref, ml_ref, o_ref, *, scale):
    xh, xl = _split_bf16(jnp.concatenate([xr_ref[0], xi_ref[0]], axis=0))
    o_ref[0] = _dot3(mh_ref[...], ml_ref[...], xh, xl) * scale


def _dft_left(xr, xi, m, scale, tn):
    B, K, cols = xr.shape
    R = m.shape[0]
    mh, ml = _const_split(m)
    col = lambda b, j: (b, 0, j)
    const2 = lambda b, j: (0, 0)
    return pl.pallas_call(
        functools.partial(_dft_left_kernel, scale=scale), grid=(B, cols // tn),
        in_specs=[pl.BlockSpec((1, K, tn), col), pl.BlockSpec((1, K, tn), col),
                  pl.BlockSpec((R, 2 * K), const2), pl.BlockSpec((R, 2 * K), const2)],
        out_specs=pl.BlockSpec((1, R, tn), col),
        out_shape=jax.ShapeDtypeStruct((B, R, cols), F32),
        compiler_params=_params("parallel", "parallel"), name="dft_left",
    )(xr, xi, mh, ml)


DFT_N2_PER_STEP = 8


def _dft_outer_kernel(xr_ref, xi_ref, mh_ref, ml_ref, o_ref):
    n1 = xr_ref.shape[1]
    for s in range(DFT_N2_PER_STEP):
        xh, xl = _split_bf16(jnp.concatenate([xr_ref[0, :, s, :], xi_ref[0, :, s, :]], axis=0))
        y = _dot3(mh_ref[...], ml_ref[...], xh, xl)
        o_ref[0, 0, :, s, :] = y[:n1]
        o_ref[0, 1, :, s, :] = y[n1:]


def _dft_outer(xr, xi, m):
    B, n1, n2, C = xr.shape
    mh, ml = _const_split(m)
    blk = lambda b, j: (b, 0, j, 0)
    const2 = lambda b, j: (0, 0)
    return pl.pallas_call(
        _dft_outer_kernel, grid=(B, n2 // DFT_N2_PER_STEP),
        in_specs=[pl.BlockSpec((1, n1, DFT_N2_PER_STEP, C), blk), pl.BlockSpec((1, n1, DFT_N2_PER_STEP, C), blk),
                  pl.BlockSpec((2 * n1, 2 * n1), const2), pl.BlockSpec((2 * n1, 2 * n1), const2)],
        out_specs=pl.BlockSpec((1, 2, n1, DFT_N2_PER_STEP, C), lambda b, j: (b, 0, 0, j, 0)),
        out_shape=jax.ShapeDtypeStruct((B, 2, n1, n2, C), F32),
        compiler_params=_params("parallel", "parallel"), name="dft_outer",
    )(xr, xi, mh, ml)


DFT_K1_PER_STEP = 8


def _dft_twiddle_kernel(y_ref, tc_ref, ts_ref, dh_ref, dl_ref, o_ref, *, scale):
    tc = tc_ref[0]
    ts = ts_ref[0]
    for kk in range(DFT_K1_PER_STEP):
        yr = y_ref[0, 0, kk]
        yi = y_ref[0, 1, kk]
        c, s = tc[:, kk:kk + 1], ts[:, kk:kk + 1]
        zh, zl = _split_bf16(jnp.concatenate([yr * c + yi * s, yi * c - yr * s], axis=0))
        o_ref[0, :, kk, :] = _dot3(dh_ref[...], dl_ref[...], zh, zl) * scale


def _fourier_real(u):
    B, N, C = u.shape
    scale = 1.0 / math.sqrt(N * C)
    xr, xi = _chan_dft(u, min(N, 512))
    if N % (DFT_RADIX * DFT_K1_PER_STEP) != 0:
        assert N <= 1024
        cn, sn = _cos_sin(N, N, N)
        return _dft_left(xr, xi, np.concatenate([cn, sn], axis=1), scale, C)
    n1, n2 = N // DFT_RADIX, DFT_RADIX
    c1, s1 = _cos_sin(n1, n1, n1)
    m1 = np.block([[c1, s1], [-s1, c1]])
    yy = _dft_outer(xr.reshape(B, n1, n2, C), xi.reshape(B, n1, n2, C), m1)
    tcos, tsin = _cos_sin(n1, n2, N)
    grp = lambda t: jnp.asarray(t.reshape(n1 // DFT_K1_PER_STEP, DFT_K1_PER_STEP, n2).transpose(0, 2, 1), F32)
    c2, s2 = _cos_sin(n2, n2, n2)
    dh, dl = _const_split(np.concatenate([c2, s2], axis=1))
    fp = pl.pallas_call(
        functools.partial(_dft_twiddle_kernel, scale=scale), grid=(B, n1 // DFT_K1_PER_STEP),
        in_specs=[
            pl.BlockSpec((1, 2, DFT_K1_PER_STEP, n2, C), lambda b, g: (b, 0, g, 0, 0)),
            pl.BlockSpec((1, n2, DFT_K1_PER_STEP), lambda b, g: (g, 0, 0)),
            pl.BlockSpec((1, n2, DFT_K1_PER_STEP), lambda b, g: (g, 0, 0)),
            pl.BlockSpec((n2, 2 * n2), lambda b, g: (0, 0)), pl.BlockSpec((n2, 2 * n2), lambda b, g: (0, 0)),
        ],
        out_specs=pl.BlockSpec((1, n2, DFT_K1_PER_STEP, C), lambda b, g: (b, 0, g, 0)),
        out_shape=jax.ShapeDtypeStruct((B, n2, n1, C), F32),
        compiler_params=_params("parallel", "parallel"), name="dft_twiddle",
    )(yy, grp(tcos), grp(tsin), dh, dl)
    return fp.reshape(B, N, C)


def _ctx_attn_kernel(q_ref, kv_ref, o_ref):
    q = q_ref[0]
    kv = kv_ref[0]
    for h in range(NA_HEADS):
        ks = slice(h * NA_HEAD_DIM, (h + 1) * NA_HEAD_DIM)
        vs = slice(NA_W + h * NA_HEAD_DIM, NA_W + (h + 1) * NA_HEAD_DIM)
        s = _dot_nt(q[:, ks], kv[:, ks])
        p = jnp.exp(s - jnp.max(s, axis=-1, keepdims=True))
        l = jnp.sum(p, axis=-1, keepdims=True)
        o_ref[0, :, ks] = _dot(p.astype(BF16), kv[:, vs]) / l


def _ctx_dense_attn(q, kv):
    B, Q, _ = q.shape
    return pl.pallas_call(
        _ctx_attn_kernel, grid=(B,),
        in_specs=[pl.BlockSpec((1, Q, NA_W), lambda b: (b, 0, 0)),
                  pl.BlockSpec((1, kv.shape[1], 2 * NA_W), lambda b: (b, 0, 0))],
        out_specs=pl.BlockSpec((1, Q, NA_W), lambda b: (b, 0, 0)),
        out_shape=jax.ShapeDtypeStruct((B, Q, NA_W), F32),
        compiler_params=_params("parallel"), name="ctx_attn",
    )(q, kv)


ADA_ROWS = 8
ADA_TN = 512


def _ada_kernel(c_ref, w_ref, b_ref, o_ref):
    c = c_ref[...]
    sh, sl = _split_bf16(c * jax.nn.sigmoid(c))
    wh, wl = _split_bf16(w_ref[0])
    o_ref[...] = _dot3(sh, sl, wh, wl) + b_ref[0]


def _ada_mod(c_rows, w_ada, b_ada, layer):
    R, D = c_rows.shape
    N = w_ada.shape[-1]
    return pl.pallas_call(
        _ada_kernel, grid=(N // ADA_TN,),
        in_specs=[pl.BlockSpec((R, D), lambda j: (0, 0)),
                  pl.BlockSpec((1, D, ADA_TN), lambda j: (layer, 0, j)),
                  pl.BlockSpec((1, 1, ADA_TN), lambda j: (layer, 0, j))],
        out_specs=pl.BlockSpec((R, ADA_TN), lambda j: (0, j)),
        out_shape=jax.ShapeDtypeStruct((R, N), F32),
        compiler_params=_params("parallel"), name="ada_mod",
    )(c_rows, w_ada, b_ada[:, None, :])


def _tile(g, n):
    return jnp.tile(g.astype(F32), n)[None, :]


def kernel(x, c, ctx, c_ctx, w_ada, b_ada, g_mix, g_ffn, w_in, na_q_g, na_k_g, na_rpb, df_q_g, df_k_g,
           df_lambda, df_subln_g, pool_w, pool_scale, fnet_w, w_branch, w_out, w_router, w_gate_e, w_up_e,
           w_down_e):
    B, T, D = x.shape
    ctx_len = ctx.shape[1]
    rope = _rope_tables(T)
    assert B + 1 <= ADA_ROWS
    c_rows = jnp.concatenate([c, c_ctx[None, :], jnp.zeros((ADA_ROWS - B - 1, D), F32)], axis=0)
    tq = min(DIFF_TQ, T)
    for l in range(DEPTH):
        last = l == DEPTH - 1
        lam_init = 0.8 - 0.6 * math.exp(-0.3 * l)
        mods = _ada_mod(c_rows, w_ada, b_ada, l)
        sh1, sc1, gt1, sh2, sc2, gt2 = [m[:, None, :] for m in jnp.split(mods[:B], 6, axis=-1)]
        bc = lambda m: jnp.broadcast_to(m[None, None, :], (B, 1, D))
        csh1, csc1, cgt1, csh2, csc2, cgt2 = [bc(m) for m in jnp.split(mods[B], 6, axis=-1)]

        w_bf = w_in[l].astype(BF16)
        gq, gk = _tile(na_q_g[l], NA_HEADS), _tile(na_k_g[l], NA_HEADS)
        gdq, gdk = _tile(df_q_g[l], 2 * DIFF_HEADS), _tile(df_k_g[l], 2 * DIFF_HEADS)
        gmix = g_mix[l][None, :]
        gffn = g_ffn[l][None, :]
        subg = df_subln_g[l][:, None].astype(F32)
        wbr = w_branch[l].astype(BF16)
        wf = fnet_w[l].astype(BF16)
        wpool = jax.scipy.linalg.block_diag(*[pool_w[l, gi] for gi in range(len(POOL_WINDOWS))]).astype(BF16)
        pscale = pool_scale[l][None, :].astype(F32)
        wo = w_out[l].astype(BF16)

        naq, dfq, pool_in, fnet_in, gate, nakv, dfk, dfv = _inproj(
            x, gmix, sh1, sc1, w_bf, gq, gk, gdq, gdk, rope, min(INPROJ_ROWS, T))
        (cnaq, cdfq, cpool_in, cfnet_in, cgate, cnakv, cdfk, cdfv) = _inproj(
            ctx, gmix, csh1, csc1, w_bf, gq, gk, gdq, gdk, None, ctx_len)

        y_na = _na_latent(naq, nakv, cnakv, na_rpb[l])
        lamp = df_lambda[l].astype(F32)
        y_df = _diff_attn(dfq, jnp.concatenate([dfk, cdfk], axis=1), jnp.concatenate([dfv, cdfv], axis=2),
                          df_q_g[l], df_k_g[l], lamp, subg, lam_init, tq, DIFF_TK)
        f_real = _fourier_real(fnet_in)
        x_new = _merge(x, gt1, y_na, y_df, pool_in, f_real, gate, wbr, wpool, pscale, wf, wo, MERGE_ROWS)

        if not last:
            yc_na = _ctx_dense_attn(cnaq, cnakv)
            yc_df = _diff_attn(cdfq, cdfk, cdfv, df_q_g[l], df_k_g[l], lamp, subg, lam_init, ctx_len, ctx_len)
            fc_real = _fourier_real(cfnet_in)
            ctx_new = _merge(ctx, cgt1, yc_na, yc_df, cpool_in, fc_real, cgate, wbr, wpool, pscale, wf, wo,
                             ctx_len)

        x = x_new
        x = _ec_ffn(x, gt2, gffn, sh2, sc2, w_router[l], w_gate_e, w_up_e, w_down_e, l, min(ROUTER_ROWS, T),
                    EXPERT_ROWS)
        if not last:
            ctx = ctx_new
            ctx = _ec_ffn(ctx, cgt2, gffn, csh2, csc2, w_router[l], w_gate_e, w_up_e, w_down_e, l, ctx_len,
                          EXPERT_ROWS)
    return x
```

```python
import functools
import math

import jax
import jax.numpy as jnp
import numpy as np
from jax import lax
from jax.experimental import pallas as pl
from jax.experimental.pallas import tpu as pltpu

F32 = jnp.float32
BF16 = jnp.bfloat16

D_MODEL = 1024
DEPTH = 2
GRID_W = 64
EPS = 1e-6
ROPE_BASE = 10000.0

NA_HEADS = 4
NA_HEAD_DIM = 64
NA_WIN_R = 8
NA_WIN_C = 16
NA_W = NA_HEADS * NA_HEAD_DIM

DIFF_HEADS = 4
DIFF_HEAD_DIM = 32
DIFF_QK_W = DIFF_HEADS * 2 * DIFF_HEAD_DIM
DIFF_V_W = DIFF_HEADS * 2 * DIFF_HEAD_DIM

POOL_WINDOWS = (2, 4, 8, 16)
POOL_GROUP_W = 64
POOL_W = len(POOL_WINDOWS) * POOL_GROUP_W
FNET_W = 256
N_BRANCH = 4
BRANCH_W = 256

OFF_NA_Q = 0
OFF_DF_Q = OFF_NA_Q + NA_W
OFF_POOL = OFF_DF_Q + DIFF_QK_W
OFF_FNET = OFF_POOL + POOL_W
OFF_GATE = OFF_FNET + FNET_W
OFF_KV = OFF_GATE + N_BRANCH * D_MODEL
KV_W = 2 * NA_W + DIFF_QK_W + DIFF_V_W
IN_COLS = OFF_KV + KV_W

N_EXPERTS = 16
EC_FACTOR = 2
EXPERT_FF = 1408

VMEM_LIMIT_BYTES = 56 * 1024 * 1024
LANES = 128
NEG_BIG = -1e30

INPROJ_ROWS = 1024
MERGE_ROWS = 1024
ROUTER_ROWS = 2048
EXPERT_ROWS = 512
DIFF_TQ = 1024
DIFF_TK_CHOICES = (3328, 1280, 256)


def _params(*sem):
    return pltpu.CompilerParams(dimension_semantics=sem, vmem_limit_bytes=VMEM_LIMIT_BYTES)


def _split_bf16(a):
    hi = a.astype(BF16)
    lo = (a - hi.astype(F32)).astype(BF16)
    return hi, lo


def _dot(a, b):
    return jnp.dot(a, b, preferred_element_type=F32)


def _dot_nt(a, b):
    return lax.dot_general(a, b, (((1,), (1,)), ((), ())), preferred_element_type=F32)


def _group_rmsnorm(p, bd_ref, g):
    hi, lo = _split_bf16(p * p)
    ms = _dot(hi, bd_ref[...]) + _dot(lo, bd_ref[...])
    return p * lax.rsqrt(ms + EPS) * g


def _rope256(y, cos, s_next, s_prev):
    outs = []
    for half in range(2):
        z = y[:, half * LANES:(half + 1) * LANES]
        outs.append(z * cos + pltpu.roll(z, LANES - 8, 1) * s_next + pltpu.roll(z, 8, 1) * s_prev)
    return jnp.concatenate(outs, axis=1)


def _inproj_kernel(*refs, use_rope):
    if use_rope:
        (x_ref, g_ref, sh_ref, sc_ref, w_ref, gq_ref, gk_ref, gdq_ref, gdk_ref, bd64_ref, bd32_ref,
         cos_ref, sn_ref, sp_ref,
         naq_ref, dfq_ref, pool_ref, fnet_ref, gate_ref, nakv_ref, dfk_ref, dfv_ref) = refs
    else:
        (x_ref, g_ref, sh_ref, sc_ref, w_ref, gq_ref, gk_ref, gdq_ref, gdk_ref, bd64_ref, bd32_ref,
         naq_ref, dfq_ref, pool_ref, fnet_ref, gate_ref, nakv_ref, dfk_ref, dfv_ref) = refs
    x = x_ref[0]
    ms = jnp.mean(x * x, axis=-1, keepdims=True)
    y = x * lax.rsqrt(ms + EPS) * g_ref[...]
    h = (y * (1.0 + sc_ref[0]) + sh_ref[0]).astype(BF16)

    def seg(lo, n):
        return _dot(h, w_ref[:, lo:lo + n])

    def rope(v):
        if not use_rope:
            return v
        return _rope256(v, cos_ref[...], sn_ref[...], sp_ref[...])

    naq = _group_rmsnorm(seg(OFF_NA_Q, NA_W), bd64_ref, gq_ref[...])
    naq_ref[0] = (naq * (NA_HEAD_DIM ** -0.5)).astype(BF16)
    dfq = rope(_group_rmsnorm(seg(OFF_DF_Q, DIFF_QK_W), bd32_ref, gdq_ref[...]))
    dfq_ref[0] = (dfq * (math.log2(math.e) * DIFF_HEAD_DIM ** -0.5)).T.astype(BF16)
    pool_ref[0] = seg(OFF_POOL, POOL_W)
    fnet_ref[0] = seg(OFF_FNET, FNET_W)
    for j in range(0, N_BRANCH * D_MODEL, 512):
        gate_ref[0, :, j:j + 512] = jax.nn.sigmoid(seg(OFF_GATE + j, 512)).astype(BF16)
    nak = _group_rmsnorm(seg(OFF_KV, NA_W), bd64_ref, gk_ref[...])
    nakv_ref[0, :, 0:NA_W] = nak.astype(BF16)
    nakv_ref[0, :, NA_W:2 * NA_W] = seg(OFF_KV + NA_W, NA_W).astype(BF16)
    dfk = rope(_group_rmsnorm(seg(OFF_KV + 2 * NA_W, DIFF_QK_W), bd32_ref, gdk_ref[...]))
    dfk_ref[0] = dfk.astype(BF16)
    dfv_ref[0] = seg(OFF_KV + 2 * NA_W + DIFF_QK_W, DIFF_V_W).T.astype(BF16)


def _block_diag_mean(width, group):
    i = jnp.arange(width)
    return jnp.where((i[:, None] // group) == (i[None, :] // group), 1.0 / group, 0.0).astype(BF16)


def _rope_tables(T):
    t = jnp.arange(T)
    j = jnp.arange(LANES)
    jj = j % DIFF_HEAD_DIM
    quarter = DIFF_HEAD_DIM // 4
    use_row = jj < DIFF_HEAD_DIM // 2
    first = (jj % (DIFF_HEAD_DIM // 2)) < quarter
    inv = ROPE_BASE ** (-(jj % quarter).astype(F32) / quarter)
    pos = jnp.where(use_row[None, :], (t // GRID_W)[:, None], (t % GRID_W)[:, None]).astype(F32)
    ang = pos * inv[None, :]
    cos, sin = jnp.cos(ang), jnp.sin(ang)
    s_next = jnp.where(first[None, :], -sin, 0.0)
    s_prev = jnp.where(first[None, :], 0.0, sin)
    return cos, s_next, s_prev


def _inproj(x, g, sh, sc, w_bf16, gq, gk, gdq, gdk, rope, tm):
    B, T, D = x.shape
    use_rope = rope is not None
    row = lambda b, i: (b, i, 0)
    const2 = lambda b, i: (0, 0)
    perb = lambda b, i: (b, 0, 0)
    in_specs = [
        pl.BlockSpec((1, tm, D), row),
        pl.BlockSpec((1, D), const2),
        pl.BlockSpec((1, 1, D), perb),
        pl.BlockSpec((1, 1, D), perb),
        pl.BlockSpec((D, IN_COLS), const2, pipeline_mode=pl.Buffered(1)),
        pl.BlockSpec((1, NA_W), const2), pl.BlockSpec((1, NA_W), const2),
        pl.BlockSpec((1, DIFF_QK_W), const2), pl.BlockSpec((1, DIFF_QK_W), const2),
        pl.BlockSpec((NA_W, NA_W), const2), pl.BlockSpec((DIFF_QK_W, DIFF_QK_W), const2),
    ]
    args = [x, g, sh, sc, w_bf16, gq, gk, gdq, gdk,
            _block_diag_mean(NA_W, NA_HEAD_DIM), _block_diag_mean(DIFF_QK_W, DIFF_HEAD_DIM)]
    if use_rope:
        in_specs += [pl.BlockSpec((tm, LANES), lambda b, i: (i, 0))] * 3
        args += list(rope)
    outs = [(NA_W, BF16, False), (DIFF_QK_W, BF16, True), (POOL_W, F32, False), (FNET_W, F32, False),
            (N_BRANCH * D_MODEL, BF16, False), (2 * NA_W, BF16, False), (DIFF_QK_W, BF16, False),
            (DIFF_V_W, BF16, True)]
    col = lambda b, i: (b, 0, i)
    out_shape = [jax.ShapeDtypeStruct((B, w, T) if tr else (B, T, w), dt) for w, dt, tr in outs]
    out_specs = [pl.BlockSpec((1, w, tm), col) if tr else pl.BlockSpec((1, tm, w), row) for w, _, tr in outs]
    return pl.pallas_call(
        functools.partial(_inproj_kernel, use_rope=use_rope),
        grid=(B, T // tm), in_specs=in_specs, out_specs=out_specs, out_shape=out_shape,
        compiler_params=_params("parallel", "parallel"), name="inproj",
    )(*args)


NA_ROWS_PER_STEP = 4


def _na_kernel(q_ref, kv0_ref, kv1_ref, kv2_ref, ckv_ref, bias_ref, o_ref):
    q = q_ref[0]
    kv = jnp.concatenate([kv0_ref[0], kv1_ref[0], kv2_ref[0]], axis=0)
    ckv = ckv_ref[0]
    for h in range(NA_HEADS):
        ks = slice(h * NA_HEAD_DIM, (h + 1) * NA_HEAD_DIM)
        vs = slice(NA_W + h * NA_HEAD_DIM, NA_W + (h + 1) * NA_HEAD_DIM)
        qh = q[:, ks]
        s = _dot_nt(qh, kv[:, ks]) + bias_ref[0, h]
        sc = _dot_nt(qh, ckv[:, ks])
        m = jnp.maximum(jnp.max(s, axis=-1, keepdims=True), jnp.max(sc, axis=-1, keepdims=True))
        p = jnp.exp(s - m)
        pc = jnp.exp(sc - m)
        l = jnp.sum(p, axis=-1, keepdims=True) + jnp.sum(pc, axis=-1, keepdims=True)
        o = _dot(p.astype(BF16), kv[:, vs]) + _dot(pc.astype(BF16), ckv[:, vs])
        o_ref[0, :, ks] = o / l


def _na_bias_table(rpb, rows):
    R = NA_ROWS_PER_STEP
    nb = rows // R
    col = jnp.arange(GRID_W)
    cs = jnp.clip(col - NA_WIN_C // 2, 0, GRID_W - NA_WIN_C)
    col_ok = (col[None, :] >= cs[:, None]) & (col[None, :] < cs[:, None] + NA_WIN_C)
    ci = jnp.clip(col[None, :] - col[:, None] + (NA_WIN_C - 1), 0, 2 * NA_WIN_C - 2)
    j = jnp.array([0, 1, nb - 1])[:, None, None]
    r = R * j + jnp.arange(R)[None, :, None]
    krow = R * (j - 1) + jnp.arange(3 * R)[None, None, :]
    rs = jnp.clip(r - NA_WIN_R // 2, 0, rows - NA_WIN_R)
    row_ok = (krow >= rs) & (krow < rs + NA_WIN_R) & (krow >= 0) & (krow < rows)
    ri = jnp.clip(krow - r + (NA_WIN_R - 1), 0, 2 * NA_WIN_R - 2)
    tab = rpb.astype(F32)[:, ri][:, :, :, :, ci]
    ok = row_ok[None, :, :, :, None, None] & col_ok[None, None, None, None]
    tab = jnp.where(ok, tab, NEG_BIG)
    tab = tab.transpose(1, 0, 2, 4, 3, 5)
    return tab.reshape(3, NA_HEADS, R * GRID_W, 3 * R * GRID_W)


def _na_latent(naq, nakv, cnakv, rpb):
    B, T, _ = naq.shape
    rows = T // GRID_W
    R = NA_ROWS_PER_STEP
    assert rows % R == 0 and rows // R >= 3 and rows >= NA_WIN_R and R + NA_WIN_R <= 3 * R
    nb = rows // R
    tb = R * GRID_W
    ctx_len = cnakv.shape[1]
    pattern = lambda j: (j > 0).astype(jnp.int32) + (j == nb - 1).astype(jnp.int32)
    kv_spec = lambda d: pl.BlockSpec((1, tb, 2 * NA_W), lambda b, j: (b, jnp.clip(j + d, 0, nb - 1), 0))
    return pl.pallas_call(
        _na_kernel, grid=(B, nb),
        in_specs=[
            pl.BlockSpec((1, tb, NA_W), lambda b, j: (b, j, 0)),
            kv_spec(-1), kv_spec(0), kv_spec(1),
            pl.BlockSpec((1, ctx_len, 2 * NA_W), lambda b, j: (b, 0, 0)),
            pl.BlockSpec((1, NA_HEADS, tb, 3 * tb), lambda b, j: (pattern(j), 0, 0, 0)),
        ],
        out_specs=pl.BlockSpec((1, tb, NA_W), lambda b, j: (b, j, 0)),
        out_shape=jax.ShapeDtypeStruct((B, T, NA_W), F32),
        compiler_params=_params("parallel", "parallel"), name="na_latent",
    )(naq, nakv, nakv, nakv, cnakv, _na_bias_table(rpb, rows))


DIFF_BOUND_MAX = 60.0


def _diff_kernel(lam_ref, g_ref, qt_ref, k_ref, vt_ref, o_ref, qx_sc, m_sc, l_sc, acc_sc, *,
                 lam_init, tk, nk, online):
    h = pl.program_id(1)
    qt = qt_ref[0]
    tq = qt.shape[1]
    grp = lax.broadcasted_iota(jnp.int32, qt.shape, 0) // DIFF_HEAD_DIM
    for mi in range(2):
        qx_sc[mi] = jnp.where(grp == 2 * h + mi, qt, jnp.zeros_like(qt))
    m_sc[...] = jnp.full(m_sc.shape, -jnp.inf, F32)
    l_sc[...] = jnp.zeros(l_sc.shape, F32)
    acc_sc[...] = jnp.zeros(acc_sc.shape, F32)

    def body(k, carry):
        off = pl.multiple_of(k * tk, tk)
        kb = k_ref[0, pl.ds(off, tk), :]
        vtb = vt_ref[0, :, pl.ds(off, tk)]
        for mi in range(2):
            s = _dot(kb, qx_sc[mi])
            if online:
                m_prev = m_sc[mi]
                m_new = jnp.maximum(m_prev, jnp.max(s, axis=0, keepdims=True))
                alpha = jnp.exp2(m_prev - m_new)
                p = jnp.exp2(s - m_new)
                l_sc[mi] = alpha * l_sc[mi] + p.reshape(tk // 8, 8, tq).sum(axis=0)
                acc_sc[mi] = alpha * acc_sc[mi] + _dot(vtb, p.astype(BF16))
                m_sc[mi] = m_new
            else:
                p = jnp.exp2(s)
                l_sc[mi] += p.reshape(tk // 8, 8, tq).sum(axis=0)
                acc_sc[mi] += _dot(vtb, p.astype(BF16))
        return carry

    lax.fori_loop(0, nk, body, 0)

    lv = lam_ref[...]
    lam = (jnp.exp(jnp.sum(lv[0:1] * lv[1:2], axis=-1, keepdims=True))
           - jnp.exp(jnp.sum(lv[2:3] * lv[3:4], axis=-1, keepdims=True)) + lam_init)
    l0 = jnp.sum(l_sc[0], axis=0, keepdims=True)
    l1 = jnp.sum(l_sc[1], axis=0, keepdims=True)
    o = acc_sc[0] / l0 - lam * (acc_sc[1] / l1)
    ms = jnp.mean(o * o, axis=0, keepdims=True)
    o_ref[0] = o * lax.rsqrt(ms + EPS) * g_ref[...] * (1.0 - lam_init)


def _diff_call(qt, k, vt, lam_params, subln_g, lam_init, tq, tk, online):
    B, W, Tq = qt.shape
    Tk = k.shape[1]
    dv = W // DIFF_HEADS
    return pl.pallas_call(
        functools.partial(_diff_kernel, lam_init=lam_init, tk=tk, nk=Tk // tk, online=online),
        grid=(B, DIFF_HEADS, Tq // tq),
        in_specs=[
            pl.BlockSpec((4, DIFF_HEAD_DIM), lambda b, h, i: (0, 0)),
            pl.BlockSpec((dv, 1), lambda b, h, i: (0, 0)),
            pl.BlockSpec((1, W, tq), lambda b, h, i: (b, 0, i)),
            pl.BlockSpec((1, Tk, W), lambda b, h, i: (b, 0, 0)),
            pl.BlockSpec((1, dv, Tk), lambda b, h, i: (b, h, 0)),
        ],
        out_specs=pl.BlockSpec((1, dv, tq), lambda b, h, i: (b, h, i)),
        out_shape=jax.ShapeDtypeStruct((B, W, Tq), F32),
        scratch_shapes=[pltpu.VMEM((2, W, tq), BF16), pltpu.VMEM((2, 1, tq), F32),
                        pltpu.VMEM((2, 8, tq), F32), pltpu.VMEM((2, dv, tq), F32)],
        compiler_params=_params("parallel", "parallel", "parallel"),
        name="diff_attn_online" if online else "diff_attn_bounded",
    )(lam_params, subln_g, qt, k, vt)


def _diff_attn(qt, dfk, vt, gq, gk, lam_params, subln_g, lam_init, tq, tk):
    bound = (1.02 * math.log2(math.e) * DIFF_HEAD_DIM ** 0.5) * jnp.max(jnp.abs(gq)) * jnp.max(jnp.abs(gk))
    args = (qt, dfk, vt, lam_params, subln_g, lam_init, tq, tk)
    return lax.cond(bound <= DIFF_BOUND_MAX,
                    lambda: _diff_call(*args, online=False), lambda: _diff_call(*args, online=True))


POOL_HALO = 8


def _pool_mixer(u, prev, nxt, wp_ref, scale, n_total):
    tm = u.shape[0]
    i = pl.program_id(1)
    prev = jnp.where(i > 0, prev, 0.0)
    nxt = jnp.where(i < pl.num_programs(1) - 1, nxt, 0.0)
    ext = jnp.concatenate([prev, u, nxt], axis=0)
    sh = lambda d: ext[POOL_HALO + d:POOL_HALO + d + tm]
    t = i * tm + lax.broadcasted_iota(jnp.int32, (tm, 1), 0)
    lane_grp = lax.broadcasted_iota(jnp.int32, (1, POOL_W), 1) // POOL_GROUP_W
    win = None
    mean = None
    prev_half = 0
    for gi, w in enumerate(POOL_WINDOWS):
        half = w // 2
        for d in range(prev_half, half):
            add = sh(d) + sh(-d - 1)
            win = add if win is None else win + add
        prev_half = half
        cnt = (jnp.minimum(t + half, n_total) - jnp.maximum(t - half, 0)).astype(F32)
        m_w = win / cnt
        mean = m_w if mean is None else jnp.where(lane_grp >= gi, m_w, mean)
    return _dot((mean - u).astype(BF16), wp_ref[...]) * scale


def _merge_kernel(x_ref, gt_ref, yna_ref, ydf_ref, u_ref, up_ref, un_ref, f_ref, gate_ref, wbr_ref, wp_ref,
                  ps_ref, wf_ref, wo_ref, o_ref, *, n_total):
    yf = _dot(f_ref[0].astype(BF16), wf_ref[...])
    ypool = _pool_mixer(u_ref[0], up_ref[0], un_ref[0], wp_ref, ps_ref[...], n_total)
    ys = (yna_ref[0], ydf_ref[0].T, ypool, yf)
    acc = None
    for i, y in enumerate(ys):
        g = gate_ref[0, :, i * D_MODEL:(i + 1) * D_MODEL].astype(F32)
        term = g * _dot(y.astype(BF16), wbr_ref[i])
        acc = term if acc is None else acc + term
    o_ref[0] = x_ref[0] + gt_ref[0] * _dot(acc.astype(BF16), wo_ref[...])


def _merge(x, gt, y_na, y_df, pool_in, f_real, gate, w_br, w_pool_bd, pool_scale, w_f, w_o, tm):
    B, T, D = x.shape
    assert tm % POOL_HALO == 0 and POOL_HALO >= max(POOL_WINDOWS) // 2
    hb = tm // POOL_HALO
    row = lambda b, i: (b, i, 0)
    const2 = lambda b, i: (0, 0)
    return pl.pallas_call(
        functools.partial(_merge_kernel, n_total=T), grid=(B, T // tm),
        in_specs=[
            pl.BlockSpec((1, tm, D), row),
            pl.BlockSpec((1, 1, D), lambda b, i: (b, 0, 0)),
            pl.BlockSpec((1, tm, BRANCH_W), row), pl.BlockSpec((1, BRANCH_W, tm), lambda b, i: (b, 0, i)),
            pl.BlockSpec((1, tm, POOL_W), row),
            pl.BlockSpec((1, POOL_HALO, POOL_W), lambda b, i: (b, jnp.maximum(i * hb - 1, 0), 0)),
            pl.BlockSpec((1, POOL_HALO, POOL_W), lambda b, i: (b, jnp.minimum((i + 1) * hb, T // POOL_HALO - 1), 0)),
            pl.BlockSpec((1, tm, FNET_W), row),
            pl.BlockSpec((1, tm, N_BRANCH * D), row),
            pl.BlockSpec((N_BRANCH, BRANCH_W, D), lambda b, i: (0, 0, 0)),
            pl.BlockSpec((POOL_W, POOL_W), const2), pl.BlockSpec((1, POOL_W), const2),
            pl.BlockSpec((FNET_W, FNET_W), const2),
            pl.BlockSpec((D, D), const2),
        ],
        out_specs=pl.BlockSpec((1, tm, D), row),
        out_shape=jax.ShapeDtypeStruct((B, T, D), F32),
        compiler_params=_params("parallel", "parallel"), name="merge",
    )(x, gt, y_na, y_df, pool_in, pool_in, pool_in, f_real, gate, w_br, w_pool_bd, pool_scale, w_f, w_o)


TOK_BLOCK = LANES
SEL_GROUP = 8
COMBINE_WIN = 48
COMBINE_BUFFERS = 3
BF16_ROWS = 16
SLOT_SPLIT = 64.0


def _router_kernel(x_ref, g_ref, sh_ref, sc_ref, wrh_ref, wrl_ref, h_ref, aff_ref):
    x = x_ref[0]
    ms = jnp.mean(x * x, axis=-1, keepdims=True)
    h = x * lax.rsqrt(ms + EPS) * g_ref[...] * (1.0 + sc_ref[0]) + sh_ref[0]
    h_ref[0] = h.astype(BF16)
    hi, lo = _split_bf16(h)
    logits = _dot_nt(wrh_ref[...], hi) + _dot_nt(wrh_ref[...], lo) + _dot_nt(wrl_ref[...], hi)
    logits = logits - jnp.max(logits, axis=0, keepdims=True)
    e = jnp.exp(logits)
    aff = e / jnp.sum(e, axis=0, keepdims=True)
    for j in range(aff.shape[1] // TOK_BLOCK):
        aff_ref[0, :, j, :] = aff[:, j * TOK_BLOCK:(j + 1) * TOK_BLOCK]


def _router(x, g, sh, sc, w_router, tm):
    B, T, D = x.shape
    E = w_router.shape[1]
    wrh, wrl = _split_bf16(w_router.T)
    row = lambda b, i: (b, i, 0)
    return pl.pallas_call(
        _router_kernel, grid=(B, T // tm),
        in_specs=[
            pl.BlockSpec((1, tm, D), row), pl.BlockSpec((1, D), lambda b, i: (0, 0)),
            pl.BlockSpec((1, 1, D), lambda b, i: (b, 0, 0)), pl.BlockSpec((1, 1, D), lambda b, i: (b, 0, 0)),
            pl.BlockSpec((E, D), lambda b, i: (0, 0)), pl.BlockSpec((E, D), lambda b, i: (0, 0)),
        ],
        out_specs=[pl.BlockSpec((1, tm, D), row),
                   pl.BlockSpec((1, E, tm // TOK_BLOCK, TOK_BLOCK), lambda b, i: (b, 0, i, 0))],
        out_shape=[jax.ShapeDtypeStruct((B, T, D), BF16),
                   jax.ShapeDtypeStruct((B, E, T // TOK_BLOCK, TOK_BLOCK), F32)],
        compiler_params=_params("parallel", "parallel"), name="router",
    )(x, g, sh, sc, wrh, wrl)


def _select_kernel(aff_ref, u_ref, ls_ref, idx_ref, gate_ref, pos_ref, off_ref, tot_ref, *, cap):
    G, nb, _ = aff_ref.shape
    aff = aff_ref[...]
    bits = pltpu.bitcast(aff, jnp.int32)
    kf = float(cap)
    ones_f = lambda m: jnp.where(m, 1.0, 0.0)
    count = lambda m: jnp.sum(ones_f(m), axis=(1, 2), keepdims=True)

    def search(it, lo):
        cand = lo | jnp.left_shift(jnp.int32(1), 30 - it)
        return jnp.where(count(bits >= cand) >= kf, cand, lo)

    thr = lax.fori_loop(0, 31, search, jnp.zeros((G, 1, 1), jnp.int32))
    need = kf - count(bits > thr)

    u = u_ref[...]
    ls = ls_ref[...]
    ones_m = jnp.ones((LANES, LANES), BF16)

    def prefix(mf):
        mb = mf.astype(BF16)
        tot = _dot(mb, ones_m)
        off = _dot(ls, tot.astype(BF16))
        return _dot(mb, u) + off, off, tot

    blk_i = lax.broadcasted_iota(jnp.int32, (nb, LANES), 0)
    lane_i = lax.broadcasted_iota(jnp.int32, (nb, LANES), 1)
    s_row = lax.broadcasted_iota(jnp.int32, (1, cap), 1).astype(F32)
    col_nb = lax.broadcasted_iota(jnp.int32, (nb, 1), 0).astype(F32)
    col_l = lax.broadcasted_iota(jnp.int32, (LANES, 1), 0).astype(F32)
    for g in range(G):
        gt = bits[g] > thr[g]
        eq = bits[g] == thr[g]
        eqf = ones_f(eq)
        tie_before = prefix(eqf)[0] - eqf
        sel = gt | (eq & (tie_before < need[g]))
        pin, off, tot = prefix(ones_f(sel))
        pos_ref[g] = jnp.where(sel, pin - 1.0, -1.0).astype(jnp.int32)
        diag = blk_i == lane_i
        off_ref[g] = jnp.sum(jnp.where(diag, off, 0.0), axis=0, keepdims=True).astype(jnp.int32)
        tot_ref[g] = jnp.sum(jnp.where(diag, tot, 0.0), axis=0, keepdims=True).astype(jnp.int32)
        blk = jnp.sum(ones_f((off + tot)[:, 0:1] <= s_row), axis=0, keepdims=True)
        oh = jnp.where(col_nb == blk, 1.0, 0.0).astype(BF16)
        pin_t = pin.T
        p_hi = jnp.floor(pin_t * (1.0 / LANES))
        p_lo = pin_t - p_hi * LANES
        rows = _dot(p_hi.astype(BF16), oh) * LANES + _dot(p_lo.astype(BF16), oh)
        lane = jnp.sum(ones_f(rows <= s_row), axis=0, keepdims=True)
        idx_ref[g] = (blk * LANES + lane).astype(jnp.int32)
        a_t = aff[g].T
        a1 = a_t.astype(BF16)
        r1 = a_t - a1.astype(F32)
        a2 = r1.astype(BF16)
        a3 = (r1 - a2.astype(F32)).astype(BF16)
        arow = _dot(a1, oh) + _dot(a2, oh) + _dot(a3, oh)
        gate_ref[g] = jnp.sum(jnp.where(col_l == lane, arow, 0.0), axis=0, keepdims=True)


def _select(aff, cap):
    B, E, nb, _ = aff.shape
    R = B * E
    G = SEL_GROUP
    i = np.arange(LANES)
    u = jnp.asarray(i[:, None] <= i[None, :], BF16)
    j = np.arange(nb)
    ls = jnp.asarray(j[None, :] < j[:, None], BF16)
    grp = lambda r: (r, 0, 0)
    outs = pl.pallas_call(
        functools.partial(_select_kernel, cap=cap), grid=(R // G,),
        in_specs=[pl.BlockSpec((G, nb, LANES), grp), pl.BlockSpec((LANES, LANES), lambda r: (0, 0)),
                  pl.BlockSpec((nb, nb), lambda r: (0, 0))],
        out_specs=[pl.BlockSpec((G, 1, cap), grp), pl.BlockSpec((G, 1, cap), grp),
                   pl.BlockSpec((G, nb, LANES), grp), pl.BlockSpec((G, 1, LANES), grp),
                   pl.BlockSpec((G, 1, LANES), grp)],
        out_shape=[jax.ShapeDtypeStruct((R, 1, cap), jnp.int32), jax.ShapeDtypeStruct((R, 1, cap), F32),
                   jax.ShapeDtypeStruct((R, nb, LANES), jnp.int32), jax.ShapeDtypeStruct((R, 1, LANES), jnp.int32),
                   jax.ShapeDtypeStruct((R, 1, LANES), jnp.int32)],
        compiler_params=_params("parallel"), name="select",
    )(aff.reshape(R, nb, LANES), u, ls)
    idx, gate, pos, off, tot = outs
    return (idx.reshape(B, E, cap), gate.reshape(B, E, cap), pos.reshape(B, E, nb, LANES),
            off[:, 0, :nb].reshape(B, E, nb), tot[:, 0, :nb].reshape(B, E, nb))


def _combine_kernel(off_ref, tot_ref, x_ref, gt_ref, pos_ref, ex_ref, ye_hbm, o_ref, buf, xbuf, acc_sc, sem,
                    xsem, *, nt, cap, W):
    n = pl.program_id(0)
    E = N_EXPERTS

    def window(step, e):
        b = step // nt
        o = off_ref[(b * E + e) * nt + step % nt]
        return b, o, jnp.minimum((o // BF16_ROWS) * BF16_ROWS, cap - W)

    def fetch(step, slot):
        for e in range(E):
            b, _, st = window(step, e)
            pltpu.make_async_copy(ye_hbm.at[b, e, pl.ds(pl.multiple_of(st, BF16_ROWS), W)], buf.at[slot, e],
                                  sem.at[slot]).start()

    depth = buf.shape[0]
    ahead = depth - 1

    @pl.when(n == 0)
    def _():
        for s in range(ahead):
            @pl.when(s < pl.num_programs(0))
            def _():
                fetch(s, s)

    @pl.when(n + ahead < pl.num_programs(0))
    def _():
        fetch(n + ahead, (n + ahead) % depth)

    slot = n % depth
    for e in range(E):
        pltpu.make_async_copy(ye_hbm.at[0, 0, pl.ds(0, W)], buf.at[slot, e], sem.at[slot]).wait()

    pos = pos_ref[0, :, (n % nt) % pos_ref.shape[2], :].astype(F32).T
    p_hi = jnp.floor(pos * (1.0 / SLOT_SPLIT))
    p_lo = pos - SLOT_SPLIT * p_hi
    ex = ex_ref[...]
    posx = SLOT_SPLIT * _dot(p_hi.astype(BF16), ex) + _dot(p_lo.astype(BF16), ex)
    col = lax.broadcasted_iota(jnp.int32, (1, E * W), 1)
    tgt = jnp.zeros((1, E * W), jnp.int32)
    for e in range(E):
        _, _, st = window(n, e)
        tgt = jnp.where(col // W == e, st + col % W, tgt)
    onehot = jnp.where(posx == tgt.astype(F32), 1.0, 0.0).astype(BF16)
    acc_sc[...] = _dot(onehot, buf[slot].reshape(E * W, buf.shape[-1]))

    w_i = lax.broadcasted_iota(jnp.int32, (1, W), 1)
    for e in range(E):
        b, o, st = window(n, e)
        t = tot_ref[(b * E + e) * nt + n % nt]
        n_extra = jnp.maximum(o + t - (st + W) + (W - 1), 0) // W

        def extra(j, carry, e=e, b=b, st=st):
            base = st + W * (j + 1)
            src = jnp.minimum(base, cap - W)
            cp = pltpu.make_async_copy(ye_hbm.at[b, e, pl.ds(pl.multiple_of(src, BF16_ROWS), W)], xbuf,
                                       xsem.at[0])
            cp.start()
            cp.wait()
            slot_i = (src + w_i).astype(F32)
            oh = jnp.where((pos[:, e:e + 1] == slot_i) & (slot_i >= base.astype(F32)), 1.0, 0.0).astype(BF16)
            acc_sc[...] += _dot(oh, xbuf[...])
            return carry

        lax.fori_loop(0, n_extra, extra, 0)

    o_ref[0] = x_ref[0] + gt_ref[0] * acc_sc[...]


def _combine(x, gt, pos, off, tot, ye):
    B, T, D = x.shape
    E, cap = ye.shape[1], ye.shape[2]
    nt = T // TOK_BLOCK
    W = min(COMBINE_WIN, cap)
    pr = min(8, nt)
    assert nt % pr == 0
    assert W % BF16_ROWS == 0 and (cap - W) % BF16_ROWS == 0 and (E * W) % LANES == 0
    expand = jnp.asarray(np.arange(E)[:, None] == (np.arange(E * W)[None, :] // W), BF16)
    tile = lambda n, off_r, tot_r: (n // nt, n % nt, 0)
    grid_spec = pltpu.PrefetchScalarGridSpec(
        num_scalar_prefetch=2, grid=(B * nt,),
        in_specs=[
            pl.BlockSpec((1, TOK_BLOCK, D), tile),
            pl.BlockSpec((1, 1, D), lambda n, off_r, tot_r: (n // nt, 0, 0)),
            pl.BlockSpec((1, E, pr, TOK_BLOCK), lambda n, off_r, tot_r: (n // nt, 0, (n % nt) // pr, 0)),
            pl.BlockSpec((E, E * W), lambda n, off_r, tot_r: (0, 0)),
            pl.BlockSpec(memory_space=pl.ANY),
        ],
        out_specs=pl.BlockSpec((1, TOK_BLOCK, D), tile),
        scratch_shapes=[pltpu.VMEM((COMBINE_BUFFERS, E, W, D), BF16), pltpu.VMEM((W, D), BF16),
                        pltpu.VMEM((TOK_BLOCK, D), F32),
                        pltpu.SemaphoreType.DMA((COMBINE_BUFFERS,)), pltpu.SemaphoreType.DMA((1,))],
    )
    return pl.pallas_call(
        functools.partial(_combine_kernel, nt=nt, cap=cap, W=W), grid_spec=grid_spec,
        out_shape=jax.ShapeDtypeStruct((B, T, D), F32),
        compiler_params=_params("arbitrary"), name="combine",
    )(off.reshape(-1), tot.reshape(-1), x, gt, pos, expand, ye)


FF_CHUNKS = ((0, 512), (512, 512), (1024, 384))


def _expert_kernel(xe_ref, g_ref, wg_ref, wu_ref, wd_ref, o_ref):
    xe = xe_ref[0, 0]
    acc = None
    for lo, n in FF_CHUNKS:
        a = _dot(xe, wg_ref[0, 0, :, lo:lo + n].astype(BF16))
        u = _dot(xe, wu_ref[0, 0, :, lo:lo + n].astype(BF16))
        hid = (a * jax.nn.sigmoid(a) * u).astype(BF16)
        part = _dot(hid, wd_ref[0, 0, lo:lo + n, :].astype(BF16))
        acc = part if acc is None else acc + part
    g_col = jnp.broadcast_to(g_ref[0, 0], (8, acc.shape[0])).T[:, 0:1]
    o_ref[0, 0] = (acc * g_col).astype(o_ref.dtype)


def _experts(xe, g, wg, wu, wd, layer, tm):
    B, E, cap, D = xe.shape
    F = wg.shape[-1]
    assert F == EXPERT_FF
    tile = lambda e, b, i: (b, e, i, 0)
    wsel = lambda e, b, i: (layer, e, 0, 0)
    return pl.pallas_call(
        _expert_kernel, grid=(E, B, cap // tm),
        in_specs=[
            pl.BlockSpec((1, 1, tm, D), tile), pl.BlockSpec((1, 1, 1, tm), lambda e, b, i: (b, e, 0, i)),
            pl.BlockSpec((1, 1, D, F), wsel), pl.BlockSpec((1, 1, D, F), wsel), pl.BlockSpec((1, 1, F, D), wsel),
        ],
        out_specs=pl.BlockSpec((1, 1, tm, D), tile),
        out_shape=jax.ShapeDtypeStruct((B, E, cap, D), BF16),
        compiler_params=_params("parallel", "parallel", "parallel"), name="experts",
    )(xe, g, wg, wu, wd)


def _ec_ffn(x, gt, g, sh, sc, w_router, wg, wu, wd, layer, tm_router, tm_expert):
    B, N, D = x.shape
    cap = max(1, EC_FACTOR * N // N_EXPERTS)
    assert N <= TOK_BLOCK * LANES and N % TOK_BLOCK == 0
    nt = N // TOK_BLOCK
    h, aff = _router(x, g, sh, sc, w_router, tm_router)
    aff = jnp.pad(aff, ((0, 0), (0, 0), (0, LANES - nt), (0, 0)))
    idx, gsel, pos, off, tot = _select(aff, cap)
    flat = (idx + (jnp.arange(B) * N)[:, None, None]).reshape(-1)
    xe = jnp.take(h.reshape(B * N, D), flat, axis=0, mode="clip").reshape(B, N_EXPERTS, cap, D)
    ye = _experts(xe, gsel[:, :, None, :], wg, wu, wd, layer, min(tm_expert, cap))
    return _combine(x, gt, pos[:, :, :nt], off[:, :, :nt], tot[:, :, :nt], ye)


DFT_RADIX = 128


def _dot3(a_hi, a_lo, b_hi, b_lo):
    return _dot(a_hi, b_hi) + _dot(a_lo, b_hi) + _dot(a_hi, b_lo)


def _cos_sin(n_rows, n_cols, period):
    k = (np.arange(n_rows)[:, None] * np.arange(n_cols)[None, :]) % period
    ang = 2.0 * np.pi * k.astype(np.float64) / period
    return np.cos(ang), np.sin(ang)


def _const_split(m):
    return _split_bf16(jnp.asarray(m, F32))


def _chan_dft_kernel(u_ref, mh_ref, ml_ref, xr_ref, xi_ref):
    uh, ul = _split_bf16(u_ref[0])
    y = _dot3(uh, ul, mh_ref[...], ml_ref[...])
    c = xr_ref.shape[-1]
    xr_ref[0] = y[:, :c]
    xi_ref[0] = y[:, c:]


def _chan_dft(u, tm):
    B, N, C = u.shape
    cc, sc = _cos_sin(C, C, C)
    mh, ml = _const_split(np.concatenate([cc, -sc], axis=1))
    row = lambda b, i: (b, i, 0)
    const2 = lambda b, i: (0, 0)
    return pl.pallas_call(
        _chan_dft_kernel, grid=(B, N // tm),
        in_specs=[pl.BlockSpec((1, tm, C), row), pl.BlockSpec((C, 2 * C), const2), pl.BlockSpec((C, 2 * C), const2)],
        out_specs=[pl.BlockSpec((1, tm, C), row)] * 2,
        out_shape=[jax.ShapeDtypeStruct((B, N, C), F32)] * 2,
        compiler_params=_params("parallel", "parallel"), name="chan_dft",
    )(u, mh, ml)


def _dft_left_kernel(xr_ref, xi_ref, mh_ref, ml_ref, o_ref, *, scale):
    xh, xl = _split_bf16(jnp.concatenate([xr_ref[0], xi_ref[0]], axis=0))
    o_ref[0] = _dot3(mh_ref[...], ml_ref[...], xh, xl) * scale


def _dft_left(xr, xi, m, scale, tn):
    B, K, cols = xr.shape
    R = m.shape[0]
    mh, ml = _const_split(m)
    col = lambda b, j: (b, 0, j)
    const2 = lambda b, j: (0, 0)
    return pl.pallas_call(
        functools.partial(_dft_left_kernel, scale=scale), grid=(B, cols // tn),
        in_specs=[pl.BlockSpec((1, K, tn), col), pl.BlockSpec((1, K, tn), col),
                  pl.BlockSpec((R, 2 * K), const2), pl.BlockSpec((R, 2 * K), const2)],
        out_specs=pl.BlockSpec((1, R, tn), col),
        out_shape=jax.ShapeDtypeStruct((B, R, cols), F32),
        compiler_params=_params("parallel", "parallel"), name="dft_left",
    )(xr, xi, mh, ml)


DFT_N2_PER_STEP = 8


def _dft_outer_kernel(xr_ref, xi_ref, mh_ref, ml_ref, o_ref):
    n1 = xr_ref.shape[1]
    for s in range(DFT_N2_PER_STEP):
        xh, xl = _split_bf16(jnp.concatenate([xr_ref[0, :, s, :], xi_ref[0, :, s, :]], axis=0))
        y = _dot3(mh_ref[...], ml_ref[...], xh, xl)
        o_ref[0, 0, :, s, :] = y[:n1]
        o_ref[0, 1, :, s, :] = y[n1:]


def _dft_outer(xr, xi, m):
    B, n1, n2, C = xr.shape
    mh, ml = _const_split(m)
    blk = lambda b, j: (b, 0, j, 0)
    const2 = lambda b, j: (0, 0)
    return pl.pallas_call(
        _dft_outer_kernel, grid=(B, n2 // DFT_N2_PER_STEP),
        in_specs=[pl.BlockSpec((1, n1, DFT_N2_PER_STEP, C), blk), pl.BlockSpec((1, n1, DFT_N2_PER_STEP, C), blk),
                  pl.BlockSpec((2 * n1, 2 * n1), const2), pl.BlockSpec((2 * n1, 2 * n1), const2)],
        out_specs=pl.BlockSpec((1, 2, n1, DFT_N2_PER_STEP, C), lambda b, j: (b, 0, 0, j, 0)),
        out_shape=jax.ShapeDtypeStruct((B, 2, n1, n2, C), F32),
        compiler_params=_params("parallel", "parallel"), name="dft_outer",
    )(xr, xi, mh, ml)


DFT_K1_PER_STEP = 8


def _dft_twiddle_kernel(y_ref, tc_ref, ts_ref, dh_ref, dl_ref, o_ref, *, scale):
    tc = tc_ref[0]
    ts = ts_ref[0]
    for kk in range(DFT_K1_PER_STEP):
        yr = y_ref[0, 0, kk]
        yi = y_ref[0, 1, kk]
        c, s = tc[:, kk:kk + 1], ts[:, kk:kk + 1]
        zh, zl = _split_bf16(jnp.concatenate([yr * c + yi * s, yi * c - yr * s], axis=0))
        o_ref[0, :, kk, :] = _dot3(dh_ref[...], dl_ref[...], zh, zl) * scale


def _fourier_real(u):
    B, N, C = u.shape
    scale = 1.0 / math.sqrt(N * C)
    xr, xi = _chan_dft(u, min(N, 512))
    if N % (DFT_RADIX * DFT_K1_PER_STEP) != 0:
        assert N <= 1024
        cn, sn = _cos_sin(N, N, N)
        return _dft_left(xr, xi, np.concatenate([cn, sn], axis=1), scale, C)
    n1, n2 = N // DFT_RADIX, DFT_RADIX
    c1, s1 = _cos_sin(n1, n1, n1)
    m1 = np.block([[c1, s1], [-s1, c1]])
    yy = _dft_outer(xr.reshape(B, n1, n2, C), xi.reshape(B, n1, n2, C), m1)
    tcos, tsin = _cos_sin(n1, n2, N)
    grp = lambda t: jnp.asarray(t.reshape(n1 // DFT_K1_PER_STEP, DFT_K1_PER_STEP, n2).transpose(0, 2, 1), F32)
    c2, s2 = _cos_sin(n2, n2, n2)
    dh, dl = _const_split(np.concatenate([c2, s2], axis=1))
    fp = pl.pallas_call(
        functools.partial(_dft_twiddle_kernel, scale=scale), grid=(B, n1 // DFT_K1_PER_STEP),
        in_specs=[
            pl.BlockSpec((1, 2, DFT_K1_PER_STEP, n2, C), lambda b, g: (b, 0, g, 0, 0)),
            pl.BlockSpec((1, n2, DFT_K1_PER_STEP), lambda b, g: (g, 0, 0)),
            pl.BlockSpec((1, n2, DFT_K1_PER_STEP), lambda b, g: (g, 0, 0)),
            pl.BlockSpec((n2, 2 * n2), lambda b, g: (0, 0)), pl.BlockSpec((n2, 2 * n2), lambda b, g: (0, 0)),
        ],
        out_specs=pl.BlockSpec((1, n2, DFT_K1_PER_STEP, C), lambda b, g: (b, 0, g, 0)),
        out_shape=jax.ShapeDtypeStruct((B, n2, n1, C), F32),
        compiler_params=_params("parallel", "parallel"), name="dft_twiddle",
    )(yy, grp(tcos), grp(tsin), dh, dl)
    return fp.reshape(B, N, C)


def _ctx_attn_kernel(q_ref, kv_ref, o_ref):
    q = q_ref[0]
    kv = kv_ref[0]
    for h in range(NA_HEADS):
        ks = slice(h * NA_HEAD_DIM, (h + 1) * NA_HEAD_DIM)
        vs = slice(NA_W + h * NA_HEAD_DIM, NA_W + (h + 1) * NA_HEAD_DIM)
        s = _dot_nt(q[:, ks], kv[:, ks])
        p = jnp.exp(s - jnp.max(s, axis=-1, keepdims=True))
        l = jnp.sum(p, axis=-1, keepdims=True)
        o_ref[0, :, ks] = _dot(p.astype(BF16), kv[:, vs]) / l


def _ctx_dense_attn(q, kv):
    B, Q, _ = q.shape
    return pl.pallas_call(
        _ctx_attn_kernel, grid=(B,),
        in_specs=[pl.BlockSpec((1, Q, NA_W), lambda b: (b, 0, 0)),
                  pl.BlockSpec((1, kv.shape[1], 2 * NA_W), lambda b: (b, 0, 0))],
        out_specs=pl.BlockSpec((1, Q, NA_W), lambda b: (b, 0, 0)),
        out_shape=jax.ShapeDtypeStruct((B, Q, NA_W), F32),
        compiler_params=_params("parallel"), name="ctx_attn",
    )(q, kv)


ADA_ROWS = 8
ADA_TN = 512


def _ada_kernel(c_ref, w_ref, b_ref, o_ref):
    c = c_ref[...]
    sh, sl = _split_bf16(c * jax.nn.sigmoid(c))
    wh, wl = _split_bf16(w_ref[0])
    o_ref[...] = _dot3(sh, sl, wh, wl) + b_ref[0]


def _ada_mod(c_rows, w_ada, b_ada, layer):
    R, D = c_rows.shape
    N = w_ada.shape[-1]
    return pl.pallas_call(
        _ada_kernel, grid=(N // ADA_TN,),
        in_specs=[pl.BlockSpec((R, D), lambda j: (0, 0)),
                  pl.BlockSpec((1, D, ADA_TN), lambda j: (layer, 0, j)),
                  pl.BlockSpec((1, 1, ADA_TN), lambda j: (layer, 0, j))],
        out_specs=pl.BlockSpec((R, ADA_TN), lambda j: (0, j)),
        out_shape=jax.ShapeDtypeStruct((R, N), F32),
        compiler_params=_params("parallel"), name="ada_mod",
    )(c_rows, w_ada, b_ada[:, None, :])


def _tile(g, n):
    return jnp.tile(g.astype(F32), n)[None, :]


def kernel(x, c, ctx, c_ctx, w_ada, b_ada, g_mix, g_ffn, w_in, na_q_g, na_k_g, na_rpb, df_q_g, df_k_g,
           df_lambda, df_subln_g, pool_w, pool_scale, fnet_w, w_branch, w_out, w_router, w_gate_e, w_up_e,
           w_down_e):
    B, T, D = x.shape
    ctx_len = ctx.shape[1]
    rope = _rope_tables(T)
    assert B + 1 <= ADA_ROWS
    c_rows = jnp.concatenate([c, c_ctx[None, :], jnp.zeros((ADA_ROWS - B - 1, D), F32)], axis=0)
    tq = min(DIFF_TQ, T)
    tk = next(c for c in DIFF_TK_CHOICES if (T + ctx_len) % c == 0)
    for l in range(DEPTH):
        last = l == DEPTH - 1
        lam_init = 0.8 - 0.6 * math.exp(-0.3 * l)
        mods = _ada_mod(c_rows, w_ada, b_ada, l)
        sh1, sc1, gt1, sh2, sc2, gt2 = [m[:, None, :] for m in jnp.split(mods[:B], 6, axis=-1)]
        bc = lambda m: jnp.broadcast_to(m[None, None, :], (B, 1, D))
        csh1, csc1, cgt1, csh2, csc2, cgt2 = [bc(m) for m in jnp.split(mods[B], 6, axis=-1)]

        w_bf = w_in[l].astype(BF16)
        gq, gk = _tile(na_q_g[l], NA_HEADS), _tile(na_k_g[l], NA_HEADS)
        gdq, gdk = _tile(df_q_g[l], 2 * DIFF_HEADS), _tile(df_k_g[l], 2 * DIFF_HEADS)
        gmix = g_mix[l][None, :]
        gffn = g_ffn[l][None, :]
        subg = df_subln_g[l][:, None].astype(F32)
        wbr = w_branch[l].astype(BF16)
        wf = fnet_w[l].astype(BF16)
        wpool = jax.scipy.linalg.block_diag(*[pool_w[l, gi] for gi in range(len(POOL_WINDOWS))]).astype(BF16)
        pscale = pool_scale[l][None, :].astype(F32)
        wo = w_out[l].astype(BF16)

        naq, dfq, pool_in, fnet_in, gate, nakv, dfk, dfv = _inproj(
            x, gmix, sh1, sc1, w_bf, gq, gk, gdq, gdk, rope, min(INPROJ_ROWS, T))
        (cnaq, cdfq, cpool_in, cfnet_in, cgate, cnakv, cdfk, cdfv) = _inproj(
            ctx, gmix, csh1, csc1, w_bf, gq, gk, gdq, gdk, None, ctx_len)

        y_na = _na_latent(naq, nakv, cnakv, na_rpb[l])
        lamp = df_lambda[l].astype(F32)
        y_df = _diff_attn(dfq, jnp.concatenate([dfk, cdfk], axis=1), jnp.concatenate([dfv, cdfv], axis=2),
                          df_q_g[l], df_k_g[l], lamp, subg, lam_init, tq, tk)
        f_real = _fourier_real(fnet_in)
        x_new = _merge(x, gt1, y_na, y_df, pool_in, f_real, gate, wbr, wpool, pscale, wf, wo, MERGE_ROWS)

        if not last:
            yc_na = _ctx_dense_attn(cnaq, cnakv)
            yc_df = _diff_attn(cdfq, cdfk, cdfv, df_q_g[l], df_k_g[l], lamp, subg, lam_init, ctx_len, ctx_len)
            fc_real = _fourier_real(cfnet_in)
            ctx_new = _merge(ctx, cgt1, yc_na, yc_df, cpool_in, fc_real, cgate, wbr, wpool, pscale, wf, wo,
                             ctx_len)

        x = x_new
        x = _ec_ffn(x, gt2, gffn, sh2, sc2, w_router[l], w_gate_e, w_up_e, w_down_e, l, min(ROUTER_ROWS, T),
                    EXPERT_ROWS)
        if not last:
            ctx = ctx_new
            ctx = _ec_ffn(ctx, cgt2, gffn, csh2, csc2, w_router[l], w_gate_e, w_up_e, w_down_e, l, ctx_len,
                          EXPERT_ROWS)
    return x
```

```python
import functools
import math

import jax
import jax.numpy as jnp
import numpy as np
from jax import lax
from jax.experimental import pallas as pl
from jax.experimental.pallas import tpu as pltpu

F32 = jnp.float32
BF16 = jnp.bfloat16

D_MODEL = 1024
DEPTH = 2
GRID_W = 64
EPS = 1e-6
ROPE_BASE = 10000.0

NA_HEADS = 4
NA_HEAD_DIM = 64
NA_WIN_R = 8
NA_WIN_C = 16
NA_W = NA_HEADS * NA_HEAD_DIM

DIFF_HEADS = 4
DIFF_HEAD_DIM = 32
DIFF_QK_W = DIFF_HEADS * 2 * DIFF_HEAD_DIM
DIFF_V_W = DIFF_HEADS * 2 * DIFF_HEAD_DIM

POOL_WINDOWS = (2, 4, 8, 16)
POOL_GROUP_W = 64
POOL_W = len(POOL_WINDOWS) * POOL_GROUP_W
FNET_W = 256
N_BRANCH = 4
BRANCH_W = 256

OFF_NA_Q = 0
OFF_DF_Q = OFF_NA_Q + NA_W
OFF_POOL = OFF_DF_Q + DIFF_QK_W
OFF_FNET = OFF_POOL + POOL_W
OFF_GATE = OFF_FNET + FNET_W
OFF_KV = OFF_GATE + N_BRANCH * D_MODEL
KV_W = 2 * NA_W + DIFF_QK_W + DIFF_V_W
IN_COLS = OFF_KV + KV_W

N_EXPERTS = 16
EC_FACTOR = 2
EXPERT_FF = 1408

VMEM_LIMIT_BYTES = 56 * 1024 * 1024
LANES = 128
NEG_BIG = -1e30

INPROJ_ROWS = 1024
MERGE_ROWS = 1024
ROUTER_ROWS = 2048
EXPERT_ROWS = 512
DIFF_TQ = 2048
DIFF_TK_CHOICES = (3328, 1280, 256)


def _params(*sem):
    return pltpu.CompilerParams(dimension_semantics=sem, vmem_limit_bytes=VMEM_LIMIT_BYTES)


def _split_bf16(a):
    hi = a.astype(BF16)
    lo = (a - hi.astype(F32)).astype(BF16)
    return hi, lo


def _dot(a, b):
    return jnp.dot(a, b, preferred_element_type=F32)


def _dot_nt(a, b):
    return lax.dot_general(a, b, (((1,), (1,)), ((), ())), preferred_element_type=F32)


def _group_rmsnorm(p, bd_ref, g):
    hi, lo = _split_bf16(p * p)
    ms = _dot(hi, bd_ref[...]) + _dot(lo, bd_ref[...])
    return p * lax.rsqrt(ms + EPS) * g


def _rope256(y, cos, s_next, s_prev):
    outs = []
    for half in range(2):
        z = y[:, half * LANES:(half + 1) * LANES]
        outs.append(z * cos + pltpu.roll(z, LANES - 8, 1) * s_next + pltpu.roll(z, 8, 1) * s_prev)
    return jnp.concatenate(outs, axis=1)


def _inproj_kernel(*refs, use_rope):
    if use_rope:
        (x_ref, g_ref, sh_ref, sc_ref, w_ref, gq_ref, gk_ref, gdq_ref, gdk_ref, bd64_ref, bd32_ref,
         cos_ref, sn_ref, sp_ref,
         naq_ref, dfq_ref, pool_ref, fnet_ref, gate_ref, nakv_ref, dfk_ref, dfv_ref) = refs
    else:
        (x_ref, g_ref, sh_ref, sc_ref, w_ref, gq_ref, gk_ref, gdq_ref, gdk_ref, bd64_ref, bd32_ref,
         naq_ref, dfq_ref, pool_ref, fnet_ref, gate_ref, nakv_ref, dfk_ref, dfv_ref) = refs
    x = x_ref[0]
    ms = jnp.mean(x * x, axis=-1, keepdims=True)
    y = x * lax.rsqrt(ms + EPS) * g_ref[...]
    h = (y * (1.0 + sc_ref[0]) + sh_ref[0]).astype(BF16)

    def seg(lo, n):
        return _dot(h, w_ref[:, lo:lo + n])

    def rope(v):
        if not use_rope:
            return v
        return _rope256(v, cos_ref[...], sn_ref[...], sp_ref[...])

    naq = _group_rmsnorm(seg(OFF_NA_Q, NA_W), bd64_ref, gq_ref[...])
    naq_ref[0] = (naq * (NA_HEAD_DIM ** -0.5)).astype(BF16)
    dfq = rope(_group_rmsnorm(seg(OFF_DF_Q, DIFF_QK_W), bd32_ref, gdq_ref[...]))
    dfq_ref[0] = (dfq * (math.log2(math.e) * DIFF_HEAD_DIM ** -0.5)).T.astype(BF16)
    pool_ref[0] = seg(OFF_POOL, POOL_W)
    fnet_ref[0] = seg(OFF_FNET, FNET_W)
    for j in range(0, N_BRANCH * D_MODEL, 512):
        gate_ref[0, :, j:j + 512] = jax.nn.sigmoid(seg(OFF_GATE + j, 512)).astype(BF16)
    nak = _group_rmsnorm(seg(OFF_KV, NA_W), bd64_ref, gk_ref[...])
    nakv_ref[0, :, 0:NA_W] = nak.astype(BF16)
    nakv_ref[0, :, NA_W:2 * NA_W] = seg(OFF_KV + NA_W, NA_W).astype(BF16)
    dfk = rope(_group_rmsnorm(seg(OFF_KV + 2 * NA_W, DIFF_QK_W), bd32_ref, gdk_ref[...]))
    dfk_ref[0] = dfk.astype(BF16)
    dfv_ref[0] = seg(OFF_KV + 2 * NA_W + DIFF_QK_W, DIFF_V_W).T.astype(BF16)


def _block_diag_mean(width, group):
    i = jnp.arange(width)
    return jnp.where((i[:, None] // group) == (i[None, :] // group), 1.0 / group, 0.0).astype(BF16)


def _rope_tables(T):
    t = jnp.arange(T)
    j = jnp.arange(LANES)
    jj = j % DIFF_HEAD_DIM
    quarter = DIFF_HEAD_DIM // 4
    use_row = jj < DIFF_HEAD_DIM // 2
    first = (jj % (DIFF_HEAD_DIM // 2)) < quarter
    inv = ROPE_BASE ** (-(jj % quarter).astype(F32) / quarter)
    pos = jnp.where(use_row[None, :], (t // GRID_W)[:, None], (t % GRID_W)[:, None]).astype(F32)
    ang = pos * inv[None, :]
    cos, sin = jnp.cos(ang), jnp.sin(ang)
    s_next = jnp.where(first[None, :], -sin, 0.0)
    s_prev = jnp.where(first[None, :], 0.0, sin)
    return cos, s_next, s_prev


def _inproj(x, g, sh, sc, w_bf16, gq, gk, gdq, gdk, rope, tm):
    B, T, D = x.shape
    use_rope = rope is not None
    row = lambda b, i: (b, i, 0)
    const2 = lambda b, i: (0, 0)
    perb = lambda b, i: (b, 0, 0)
    in_specs = [
        pl.BlockSpec((1, tm, D), row),
        pl.BlockSpec((1, D), const2),
        pl.BlockSpec((1, 1, D), perb),
        pl.BlockSpec((1, 1, D), perb),
        pl.BlockSpec((D, IN_COLS), const2, pipeline_mode=pl.Buffered(1)),
        pl.BlockSpec((1, NA_W), const2), pl.BlockSpec((1, NA_W), const2),
        pl.BlockSpec((1, DIFF_QK_W), const2), pl.BlockSpec((1, DIFF_QK_W), const2),
        pl.BlockSpec((NA_W, NA_W), const2), pl.BlockSpec((DIFF_QK_W, DIFF_QK_W), const2),
    ]
    args = [x, g, sh, sc, w_bf16, gq, gk, gdq, gdk,
            _block_diag_mean(NA_W, NA_HEAD_DIM), _block_diag_mean(DIFF_QK_W, DIFF_HEAD_DIM)]
    if use_rope:
        in_specs += [pl.BlockSpec((tm, LANES), lambda b, i: (i, 0))] * 3
        args += list(rope)
    outs = [(NA_W, BF16, False), (DIFF_QK_W, BF16, True), (POOL_W, F32, False), (FNET_W, F32, False),
            (N_BRANCH * D_MODEL, BF16, False), (2 * NA_W, BF16, False), (DIFF_QK_W, BF16, False),
            (DIFF_V_W, BF16, True)]
    col = lambda b, i: (b, 0, i)
    out_shape = [jax.ShapeDtypeStruct((B, w, T) if tr else (B, T, w), dt) for w, dt, tr in outs]
    out_specs = [pl.BlockSpec((1, w, tm), col) if tr else pl.BlockSpec((1, tm, w), row) for w, _, tr in outs]
    return pl.pallas_call(
        functools.partial(_inproj_kernel, use_rope=use_rope),
        grid=(B, T // tm), in_specs=in_specs, out_specs=out_specs, out_shape=out_shape,
        compiler_params=_params("parallel", "parallel"), name="inproj",
    )(*args)


NA_ROWS_PER_STEP = 4


def _na_kernel(q_ref, kv0_ref, kv1_ref, kv2_ref, ckv_ref, bias_ref, o_ref):
    q = q_ref[0]
    kv = jnp.concatenate([kv0_ref[0], kv1_ref[0], kv2_ref[0]], axis=0)
    ckv = ckv_ref[0]
    for h in range(NA_HEADS):
        ks = slice(h * NA_HEAD_DIM, (h + 1) * NA_HEAD_DIM)
        vs = slice(NA_W + h * NA_HEAD_DIM, NA_W + (h + 1) * NA_HEAD_DIM)
        qh = q[:, ks]
        s = _dot_nt(qh, kv[:, ks]) + bias_ref[0, h]
        sc = _dot_nt(qh, ckv[:, ks])
        m = jnp.maximum(jnp.max(s, axis=-1, keepdims=True), jnp.max(sc, axis=-1, keepdims=True))
        p = jnp.exp(s - m)
        pc = jnp.exp(sc - m)
        l = jnp.sum(p, axis=-1, keepdims=True) + jnp.sum(pc, axis=-1, keepdims=True)
        o = _dot(p.astype(BF16), kv[:, vs]) + _dot(pc.astype(BF16), ckv[:, vs])
        o_ref[0, :, ks] = o / l


def _na_bias_table(rpb, rows):
    R = NA_ROWS_PER_STEP
    nb = rows // R
    col = jnp.arange(GRID_W)
    cs = jnp.clip(col - NA_WIN_C // 2, 0, GRID_W - NA_WIN_C)
    col_ok = (col[None, :] >= cs[:, None]) & (col[None, :] < cs[:, None] + NA_WIN_C)
    ci = jnp.clip(col[None, :] - col[:, None] + (NA_WIN_C - 1), 0, 2 * NA_WIN_C - 2)
    j = jnp.array([0, 1, nb - 1])[:, None, None]
    r = R * j + jnp.arange(R)[None, :, None]
    krow = R * (j - 1) + jnp.arange(3 * R)[None, None, :]
    rs = jnp.clip(r - NA_WIN_R // 2, 0, rows - NA_WIN_R)
    row_ok = (krow >= rs) & (krow < rs + NA_WIN_R) & (krow >= 0) & (krow < rows)
    ri = jnp.clip(krow - r + (NA_WIN_R - 1), 0, 2 * NA_WIN_R - 2)
    tab = rpb.astype(F32)[:, ri][:, :, :, :, ci]
    ok = row_ok[None, :, :, :, None, None] & col_ok[None, None, None, None]
    tab = jnp.where(ok, tab, NEG_BIG)
    tab = tab.transpose(1, 0, 2, 4, 3, 5)
    return tab.reshape(3, NA_HEADS, R * GRID_W, 3 * R * GRID_W)


def _na_latent(naq, nakv, cnakv, rpb):
    B, T, _ = naq.shape
    rows = T // GRID_W
    R = NA_ROWS_PER_STEP
    assert rows % R == 0 and rows // R >= 3 and rows >= NA_WIN_R and R + NA_WIN_R <= 3 * R
    nb = rows // R
    tb = R * GRID_W
    ctx_len = cnakv.shape[1]
    pattern = lambda j: (j > 0).astype(jnp.int32) + (j == nb - 1).astype(jnp.int32)
    kv_spec = lambda d: pl.BlockSpec((1, tb, 2 * NA_W), lambda b, j: (b, jnp.clip(j + d, 0, nb - 1), 0))
    return pl.pallas_call(
        _na_kernel, grid=(B, nb),
        in_specs=[
            pl.BlockSpec((1, tb, NA_W), lambda b, j: (b, j, 0)),
            kv_spec(-1), kv_spec(0), kv_spec(1),
            pl.BlockSpec((1, ctx_len, 2 * NA_W), lambda b, j: (b, 0, 0)),
            pl.BlockSpec((1, NA_HEADS, tb, 3 * tb), lambda b, j: (pattern(j), 0, 0, 0)),
        ],
        out_specs=pl.BlockSpec((1, tb, NA_W), lambda b, j: (b, j, 0)),
        out_shape=jax.ShapeDtypeStruct((B, T, NA_W), F32),
        compiler_params=_params("parallel", "parallel"), name="na_latent",
    )(naq, nakv, nakv, nakv, cnakv, _na_bias_table(rpb, rows))


DIFF_BOUND_MAX = 60.0


def _diff_kernel(lam_ref, g_ref, qt_ref, k_ref, vt_ref, o_ref, qx_sc, m_sc, l_sc, acc_sc, *,
                 lam_init, tk, nk, online):
    h = pl.program_id(1)
    qt = qt_ref[0]
    tq = qt.shape[1]
    grp = lax.broadcasted_iota(jnp.int32, qt.shape, 0) // DIFF_HEAD_DIM
    for mi in range(2):
        qx_sc[mi] = jnp.where(grp == 2 * h + mi, qt, jnp.zeros_like(qt))
    m_sc[...] = jnp.full(m_sc.shape, -jnp.inf, F32)
    l_sc[...] = jnp.zeros(l_sc.shape, F32)
    acc_sc[...] = jnp.zeros(acc_sc.shape, F32)

    def body(k, carry):
        off = pl.multiple_of(k * tk, tk)
        kb = k_ref[0, pl.ds(off, tk), :]
        vtb = vt_ref[0, :, pl.ds(off, tk)]
        for mi in range(2):
            s = _dot(kb, qx_sc[mi])
            if online:
                m_prev = m_sc[mi]
                m_new = jnp.maximum(m_prev, jnp.max(s, axis=0, keepdims=True))
                alpha = jnp.exp2(m_prev - m_new)
                p = jnp.exp2(s - m_new)
                l_sc[mi] = alpha * l_sc[mi] + p.reshape(tk // 8, 8, tq).sum(axis=0)
                acc_sc[mi] = alpha * acc_sc[mi] + _dot(vtb, p.astype(BF16))
                m_sc[mi] = m_new
            else:
                p = jnp.exp2(s)
                l_sc[mi] += p.reshape(tk // 8, 8, tq).sum(axis=0)
                acc_sc[mi] += _dot(vtb, p.astype(BF16))
        return carry

    lax.fori_loop(0, nk, body, 0)

    lv = lam_ref[...]
    lam = (jnp.exp(jnp.sum(lv[0:1] * lv[1:2], axis=-1, keepdims=True))
           - jnp.exp(jnp.sum(lv[2:3] * lv[3:4], axis=-1, keepdims=True)) + lam_init)
    l0 = jnp.sum(l_sc[0], axis=0, keepdims=True)
    l1 = jnp.sum(l_sc[1], axis=0, keepdims=True)
    o = acc_sc[0] / l0 - lam * (acc_sc[1] / l1)
    ms = jnp.mean(o * o, axis=0, keepdims=True)
    o_ref[0] = o * lax.rsqrt(ms + EPS) * g_ref[...] * (1.0 - lam_init)


def _diff_call(qt, k, vt, lam_params, subln_g, lam_init, tq, tk, online):
    B, W, Tq = qt.shape
    Tk = k.shape[1]
    dv = W // DIFF_HEADS
    return pl.pallas_call(
        functools.partial(_diff_kernel, lam_init=lam_init, tk=tk, nk=Tk // tk, online=online),
        grid=(B, DIFF_HEADS, Tq // tq),
        in_specs=[
            pl.BlockSpec((4, DIFF_HEAD_DIM), lambda b, h, i: (0, 0)),
            pl.BlockSpec((dv, 1), lambda b, h, i: (0, 0)),
            pl.BlockSpec((1, W, tq), lambda b, h, i: (b, 0, i)),
            pl.BlockSpec((1, Tk, W), lambda b, h, i: (b, 0, 0), pipeline_mode=pl.Buffered(1)),
            pl.BlockSpec((1, dv, Tk), lambda b, h, i: (b, h, 0)),
        ],
        out_specs=pl.BlockSpec((1, dv, tq), lambda b, h, i: (b, h, i)),
        out_shape=jax.ShapeDtypeStruct((B, W, Tq), F32),
        scratch_shapes=[pltpu.VMEM((2, W, tq), BF16), pltpu.VMEM((2, 1, tq), F32),
                        pltpu.VMEM((2, 8, tq), F32), pltpu.VMEM((2, dv, tq), F32)],
        compiler_params=_params("parallel", "parallel", "parallel"),
        name="diff_attn_online" if online else "diff_attn_bounded",
    )(lam_params, subln_g, qt, k, vt)


def _diff_attn(qt, dfk, vt, gq, gk, lam_params, subln_g, lam_init, tq, tk):
    bound = (1.02 * math.log2(math.e) * DIFF_HEAD_DIM ** 0.5) * jnp.max(jnp.abs(gq)) * jnp.max(jnp.abs(gk))
    args = (qt, dfk, vt, lam_params, subln_g, lam_init, tq, tk)
    return lax.cond(bound <= DIFF_BOUND_MAX,
                    lambda: _diff_call(*args, online=False), lambda: _diff_call(*args, online=True))


POOL_HALO = 8


def _pool_mixer(u, prev, nxt, wp_ref, scale, n_total):
    tm = u.shape[0]
    i = pl.program_id(1)
    prev = jnp.where(i > 0, prev, 0.0)
    nxt = jnp.where(i < pl.num_programs(1) - 1, nxt, 0.0)
    ext = jnp.concatenate([prev, u, nxt], axis=0)
    sh = lambda d: ext[POOL_HALO + d:POOL_HALO + d + tm]
    t = i * tm + lax.broadcasted_iota(jnp.int32, (tm, 1), 0)
    lane_grp = lax.broadcasted_iota(jnp.int32, (1, POOL_W), 1) // POOL_GROUP_W
    win = None
    mean = None
    prev_half = 0
    for gi, w in enumerate(POOL_WINDOWS):
        half = w // 2
        for d in range(prev_half, half):
            add = sh(d) + sh(-d - 1)
            win = add if win is None else win + add
        prev_half = half
        cnt = (jnp.minimum(t + half, n_total) - jnp.maximum(t - half, 0)).astype(F32)
        m_w = win / cnt
        mean = m_w if mean is None else jnp.where(lane_grp >= gi, m_w, mean)
    return _dot((mean - u).astype(BF16), wp_ref[...]) * scale


def _merge_kernel(x_ref, gt_ref, yna_ref, ydf_ref, u_ref, up_ref, un_ref, f_ref, gate_ref, wbr_ref, wp_ref,
                  ps_ref, wf_ref, wo_ref, o_ref, *, n_total):
    yf = _dot(f_ref[0].astype(BF16), wf_ref[...])
    ypool = _pool_mixer(u_ref[0], up_ref[0], un_ref[0], wp_ref, ps_ref[...], n_total)
    ys = (yna_ref[0], ydf_ref[0].T, ypool, yf)
    acc = None
    for i, y in enumerate(ys):
        g = gate_ref[0, :, i * D_MODEL:(i + 1) * D_MODEL].astype(F32)
        term = g * _dot(y.astype(BF16), wbr_ref[i])
        acc = term if acc is None else acc + term
    o_ref[0] = x_ref[0] + gt_ref[0] * _dot(acc.astype(BF16), wo_ref[...])


def _merge(x, gt, y_na, y_df, pool_in, f_real, gate, w_br, w_pool_bd, pool_scale, w_f, w_o, tm):
    B, T, D = x.shape
    assert tm % POOL_HALO == 0 and POOL_HALO >= max(POOL_WINDOWS) // 2
    hb = tm // POOL_HALO
    row = lambda b, i: (b, i, 0)
    const2 = lambda b, i: (0, 0)
    return pl.pallas_call(
        functools.partial(_merge_kernel, n_total=T), grid=(B, T // tm),
        in_specs=[
            pl.BlockSpec((1, tm, D), row),
            pl.BlockSpec((1, 1, D), lambda b, i: (b, 0, 0)),
            pl.BlockSpec((1, tm, BRANCH_W), row), pl.BlockSpec((1, BRANCH_W, tm), lambda b, i: (b, 0, i)),
            pl.BlockSpec((1, tm, POOL_W), row),
            pl.BlockSpec((1, POOL_HALO, POOL_W), lambda b, i: (b, jnp.maximum(i * hb - 1, 0), 0)),
            pl.BlockSpec((1, POOL_HALO, POOL_W), lambda b, i: (b, jnp.minimum((i + 1) * hb, T // POOL_HALO - 1), 0)),
            pl.BlockSpec((1, tm, FNET_W), row),
            pl.BlockSpec((1, tm, N_BRANCH * D), row),
            pl.BlockSpec((N_BRANCH, BRANCH_W, D), lambda b, i: (0, 0, 0)),
            pl.BlockSpec((POOL_W, POOL_W), const2), pl.BlockSpec((1, POOL_W), const2),
            pl.BlockSpec((FNET_W, FNET_W), const2),
            pl.BlockSpec((D, D), const2),
        ],
        out_specs=pl.BlockSpec((1, tm, D), row),
        out_shape=jax.ShapeDtypeStruct((B, T, D), F32),
        compiler_params=_params("parallel", "parallel"), name="merge",
    )(x, gt, y_na, y_df, pool_in, pool_in, pool_in, f_real, gate, w_br, w_pool_bd, pool_scale, w_f, w_o)


TOK_BLOCK = LANES
SEL_GROUP = 8
COMBINE_WIN = 48
COMBINE_BUFFERS = 3
BF16_ROWS = 16
SLOT_SPLIT = 64.0


def _router_kernel(x_ref, g_ref, sh_ref, sc_ref, wrh_ref, wrl_ref, h_ref, aff_ref):
    x = x_ref[0]
    ms = jnp.mean(x * x, axis=-1, keepdims=True)
    h = x * lax.rsqrt(ms + EPS) * g_ref[...] * (1.0 + sc_ref[0]) + sh_ref[0]
    h_ref[0] = h.astype(BF16)
    hi, lo = _split_bf16(h)
    logits = _dot_nt(wrh_ref[...], hi) + _dot_nt(wrh_ref[...], lo) + _dot_nt(wrl_ref[...], hi)
    logits = logits - jnp.max(logits, axis=0, keepdims=True)
    e = jnp.exp(logits)
    aff = e / jnp.sum(e, axis=0, keepdims=True)
    for j in range(aff.shape[1] // TOK_BLOCK):
        aff_ref[0, :, j, :] = aff[:, j * TOK_BLOCK:(j + 1) * TOK_BLOCK]


def _router(x, g, sh, sc, w_router, tm):
    B, T, D = x.shape
    E = w_router.shape[1]
    wrh, wrl = _split_bf16(w_router.T)
    row = lambda b, i: (b, i, 0)
    return pl.pallas_call(
        _router_kernel, grid=(B, T // tm),
        in_specs=[
            pl.BlockSpec((1, tm, D), row), pl.BlockSpec((1, D), lambda b, i: (0, 0)),
            pl.BlockSpec((1, 1, D), lambda b, i: (b, 0, 0)), pl.BlockSpec((1, 1, D), lambda b, i: (b, 0, 0)),
            pl.BlockSpec((E, D), lambda b, i: (0, 0)), pl.BlockSpec((E, D), lambda b, i: (0, 0)),
        ],
        out_specs=[pl.BlockSpec((1, tm, D), row),
                   pl.BlockSpec((1, E, tm // TOK_BLOCK, TOK_BLOCK), lambda b, i: (b, 0, i, 0))],
        out_shape=[jax.ShapeDtypeStruct((B, T, D), BF16),
                   jax.ShapeDtypeStruct((B, E, T // TOK_BLOCK, TOK_BLOCK), F32)],
        compiler_params=_params("parallel", "parallel"), name="router",
    )(x, g, sh, sc, wrh, wrl)


def _select_kernel(aff_ref, u_ref, ls_ref, idx_ref, gate_ref, pos_ref, off_ref, tot_ref, *, cap):
    G, nb, _ = aff_ref.shape
    aff = aff_ref[...]
    bits = pltpu.bitcast(aff, jnp.int32)
    kf = float(cap)
    ones_f = lambda m: jnp.where(m, 1.0, 0.0)
    count = lambda m: jnp.sum(ones_f(m), axis=(1, 2), keepdims=True)

    def search(it, lo):
        cand = lo | jnp.left_shift(jnp.int32(1), 30 - it)
        return jnp.where(count(bits >= cand) >= kf, cand, lo)

    thr = lax.fori_loop(0, 31, search, jnp.zeros((G, 1, 1), jnp.int32))
    need = kf - count(bits > thr)

    u = u_ref[...]
    ls = ls_ref[...]
    ones_m = jnp.ones((LANES, LANES), BF16)

    def prefix(mf):
        mb = mf.astype(BF16)
        tot = _dot(mb, ones_m)
        off = _dot(ls, tot.astype(BF16))
        return _dot(mb, u) + off, off, tot

    blk_i = lax.broadcasted_iota(jnp.int32, (nb, LANES), 0)
    lane_i = lax.broadcasted_iota(jnp.int32, (nb, LANES), 1)
    s_row = lax.broadcasted_iota(jnp.int32, (1, cap), 1).astype(F32)
    col_nb = lax.broadcasted_iota(jnp.int32, (nb, 1), 0).astype(F32)
    col_l = lax.broadcasted_iota(jnp.int32, (LANES, 1), 0).astype(F32)
    for g in range(G):
        gt = bits[g] > thr[g]
        eq = bits[g] == thr[g]
        eqf = ones_f(eq)
        tie_before = prefix(eqf)[0] - eqf
        sel = gt | (eq & (tie_before < need[g]))
        pin, off, tot = prefix(ones_f(sel))
        pos_ref[g] = jnp.where(sel, pin - 1.0, -1.0).astype(jnp.int32)
        diag = blk_i == lane_i
        off_ref[g] = jnp.sum(jnp.where(diag, off, 0.0), axis=0, keepdims=True).astype(jnp.int32)
        tot_ref[g] = jnp.sum(jnp.where(diag, tot, 0.0), axis=0, keepdims=True).astype(jnp.int32)
        blk = jnp.sum(ones_f((off + tot)[:, 0:1] <= s_row), axis=0, keepdims=True)
        oh = jnp.where(col_nb == blk, 1.0, 0.0).astype(BF16)
        pin_t = pin.T
        p_hi = jnp.floor(pin_t * (1.0 / LANES))
        p_lo = pin_t - p_hi * LANES
        rows = _dot(p_hi.astype(BF16), oh) * LANES + _dot(p_lo.astype(BF16), oh)
        lane = jnp.sum(ones_f(rows <= s_row), axis=0, keepdims=True)
        idx_ref[g] = (blk * LANES + lane).astype(jnp.int32)
        a_t = aff[g].T
        a1 = a_t.astype(BF16)
        r1 = a_t - a1.astype(F32)
        a2 = r1.astype(BF16)
        a3 = (r1 - a2.astype(F32)).astype(BF16)
        arow = _dot(a1, oh) + _dot(a2, oh) + _dot(a3, oh)
        gate_ref[g] = jnp.sum(jnp.where(col_l == lane, arow, 0.0), axis=0, keepdims=True)


def _select(aff, cap):
    B, E, nb, _ = aff.shape
    R = B * E
    G = SEL_GROUP
    i = np.arange(LANES)
    u = jnp.asarray(i[:, None] <= i[None, :], BF16)
    j = np.arange(nb)
    ls = jnp.asarray(j[None, :] < j[:, None], BF16)
    grp = lambda r: (r, 0, 0)
    outs = pl.pallas_call(
        functools.partial(_select_kernel, cap=cap), grid=(R // G,),
        in_specs=[pl.BlockSpec((G, nb, LANES), grp), pl.BlockSpec((LANES, LANES), lambda r: (0, 0)),
                  pl.BlockSpec((nb, nb), lambda r: (0, 0))],
        out_specs=[pl.BlockSpec((G, 1, cap), grp), pl.BlockSpec((G, 1, cap), grp),
                   pl.BlockSpec((G, nb, LANES), grp), pl.BlockSpec((G, 1, LANES), grp),
                   pl.BlockSpec((G, 1, LANES), grp)],
        out_shape=[jax.ShapeDtypeStruct((R, 1, cap), jnp.int32), jax.ShapeDtypeStruct((R, 1, cap), F32),
                   jax.ShapeDtypeStruct((R, nb, LANES), jnp.int32), jax.ShapeDtypeStruct((R, 1, LANES), jnp.int32),
                   jax.ShapeDtypeStruct((R, 1, LANES), jnp.int32)],
        compiler_params=_params("parallel"), name="select",
    )(aff.reshape(R, nb, LANES), u, ls)
    idx, gate, pos, off, tot = outs
    return (idx.reshape(B, E, cap), gate.reshape(B, E, cap), pos.reshape(B, E, nb, LANES),
            off[:, 0, :nb].reshape(B, E, nb), tot[:, 0, :nb].reshape(B, E, nb))


def _combine_kernel(off_ref, tot_ref, x_ref, gt_ref, pos_ref, ex_ref, ye_hbm, o_ref, buf, xbuf, acc_sc, sem,
                    xsem, *, nt, cap, W):
    n = pl.program_id(0)
    E = N_EXPERTS

    def window(step, e):
        b = step // nt
        o = off_ref[(b * E + e) * nt + step % nt]
        return b, o, jnp.minimum((o // BF16_ROWS) * BF16_ROWS, cap - W)

    def fetch(step, slot):
        for e in range(E):
            b, _, st = window(step, e)
            pltpu.make_async_copy(ye_hbm.at[b, e, pl.ds(pl.multiple_of(st, BF16_ROWS), W)], buf.at[slot, e],
                                  sem.at[slot]).start()

    depth = buf.shape[0]
    ahead = depth - 1

    @pl.when(n == 0)
    def _():
        for s in range(ahead):
            @pl.when(s < pl.num_programs(0))
            def _():
                fetch(s, s)

    @pl.when(n + ahead < pl.num_programs(0))
    def _():
        fetch(n + ahead, (n + ahead) % depth)

    slot = n % depth
    for e in range(E):
        pltpu.make_async_copy(ye_hbm.at[0, 0, pl.ds(0, W)], buf.at[slot, e], sem.at[slot]).wait()

    pos = pos_ref[0, :, (n % nt) % pos_ref.shape[2], :].astype(F32).T
    p_hi = jnp.floor(pos * (1.0 / SLOT_SPLIT))
    p_lo = pos - SLOT_SPLIT * p_hi
    ex = ex_ref[...]
    posx = SLOT_SPLIT * _dot(p_hi.astype(BF16), ex) + _dot(p_lo.astype(BF16), ex)
    col = lax.broadcasted_iota(jnp.int32, (1, E * W), 1)
    tgt = jnp.zeros((1, E * W), jnp.int32)
    for e in range(E):
        _, _, st = window(n, e)
        tgt = jnp.where(col // W == e, st + col % W, tgt)
    onehot = jnp.where(posx == tgt.astype(F32), 1.0, 0.0).astype(BF16)
    acc_sc[...] = _dot(onehot, buf[slot].reshape(E * W, buf.shape[-1]))

    w_i = lax.broadcasted_iota(jnp.int32, (1, W), 1)
    for e in range(E):
        b, o, st = window(n, e)
        t = tot_ref[(b * E + e) * nt + n % nt]
        n_extra = jnp.maximum(o + t - (st + W) + (W - 1), 0) // W

        def extra(j, carry, e=e, b=b, st=st):
            base = st + W * (j + 1)
            src = jnp.minimum(base, cap - W)
            cp = pltpu.make_async_copy(ye_hbm.at[b, e, pl.ds(pl.multiple_of(src, BF16_ROWS), W)], xbuf,
                                       xsem.at[0])
            cp.start()
            cp.wait()
            slot_i = (src + w_i).astype(F32)
            oh = jnp.where((pos[:, e:e + 1] == slot_i) & (slot_i >= base.astype(F32)), 1.0, 0.0).astype(BF16)
            acc_sc[...] += _dot(oh, xbuf[...])
            return carry

        lax.fori_loop(0, n_extra, extra, 0)

    o_ref[0] = x_ref[0] + gt_ref[0] * acc_sc[...]


def _combine(x, gt, pos, off, tot, ye):
    B, T, D = x.shape
    E, cap = ye.shape[1], ye.shape[2]
    nt = T // TOK_BLOCK
    W = min(COMBINE_WIN, cap)
    pr = min(8, nt)
    assert nt % pr == 0
    assert W % BF16_ROWS == 0 and (cap - W) % BF16_ROWS == 0 and (E * W) % LANES == 0
    expand = jnp.asarray(np.arange(E)[:, None] == (np.arange(E * W)[None, :] // W), BF16)
    tile = lambda n, off_r, tot_r: (n // nt, n % nt, 0)
    grid_spec = pltpu.PrefetchScalarGridSpec(
        num_scalar_prefetch=2, grid=(B * nt,),
        in_specs=[
            pl.BlockSpec((1, TOK_BLOCK, D), tile),
            pl.BlockSpec((1, 1, D), lambda n, off_r, tot_r: (n // nt, 0, 0)),
            pl.BlockSpec((1, E, pr, TOK_BLOCK), lambda n, off_r, tot_r: (n // nt, 0, (n % nt) // pr, 0)),
            pl.BlockSpec((E, E * W), lambda n, off_r, tot_r: (0, 0)),
            pl.BlockSpec(memory_space=pl.ANY),
        ],
        out_specs=pl.BlockSpec((1, TOK_BLOCK, D), tile),
        scratch_shapes=[pltpu.VMEM((COMBINE_BUFFERS, E, W, D), BF16), pltpu.VMEM((W, D), BF16),
                        pltpu.VMEM((TOK_BLOCK, D), F32),
                        pltpu.SemaphoreType.DMA((COMBINE_BUFFERS,)), pltpu.SemaphoreType.DMA((1,))],
    )
    return pl.pallas_call(
        functools.partial(_combine_kernel, nt=nt, cap=cap, W=W), grid_spec=grid_spec,
        out_shape=jax.ShapeDtypeStruct((B, T, D), F32),
        compiler_params=_params("arbitrary"), name="combine",
    )(off.reshape(-1), tot.reshape(-1), x, gt, pos, expand, ye)


FF_CHUNKS = ((0, 512), (512, 512), (1024, 384))


def _expert_kernel(xe_ref, g_ref, wg_ref, wu_ref, wd_ref, o_ref):
    xe = xe_ref[0, 0]
    acc = None
    for lo, n in FF_CHUNKS:
        a = _dot(xe, wg_ref[0, 0, :, lo:lo + n].astype(BF16))
        u = _dot(xe, wu_ref[0, 0, :, lo:lo + n].astype(BF16))
        hid = (a * jax.nn.sigmoid(a) * u).astype(BF16)
        part = _dot(hid, wd_ref[0, 0, lo:lo + n, :].astype(BF16))
        acc = part if acc is None else acc + part
    g_col = jnp.broadcast_to(g_ref[0, 0], (8, acc.shape[0])).T[:, 0:1]
    o_ref[0, 0] = (acc * g_col).astype(o_ref.dtype)


def _experts(xe, g, wg, wu, wd, layer, tm):
    B, E, cap, D = xe.shape
    F = wg.shape[-1]
    assert F == EXPERT_FF
    tile = lambda e, b, i: (b, e, i, 0)
    wsel = lambda e, b, i: (layer, e, 0, 0)
    return pl.pallas_call(
        _expert_kernel, grid=(E, B, cap // tm),
        in_specs=[
            pl.BlockSpec((1, 1, tm, D), tile), pl.BlockSpec((1, 1, 1, tm), lambda e, b, i: (b, e, 0, i)),
            pl.BlockSpec((1, 1, D, F), wsel), pl.BlockSpec((1, 1, D, F), wsel), pl.BlockSpec((1, 1, F, D), wsel),
        ],
        out_specs=pl.BlockSpec((1, 1, tm, D), tile),
        out_shape=jax.ShapeDtypeStruct((B, E, cap, D), BF16),
        compiler_params=_params("parallel", "parallel", "parallel"), name="experts",
    )(xe, g, wg, wu, wd)


def _ec_ffn(x, gt, g, sh, sc, w_router, wg, wu, wd, layer, tm_router, tm_expert):
    B, N, D = x.shape
    cap = max(1, EC_FACTOR * N // N_EXPERTS)
    assert N <= TOK_BLOCK * LANES and N % TOK_BLOCK == 0
    nt = N // TOK_BLOCK
    h, aff = _router(x, g, sh, sc, w_router, tm_router)
    aff = jnp.pad(aff, ((0, 0), (0, 0), (0, LANES - nt), (0, 0)))
    idx, gsel, pos, off, tot = _select(aff, cap)
    flat = (idx + (jnp.arange(B) * N)[:, None, None]).reshape(-1)
    xe = jnp.take(h.reshape(B * N, D), flat, axis=0, mode="clip").reshape(B, N_EXPERTS, cap, D)
    ye = _experts(xe, gsel[:, :, None, :], wg, wu, wd, layer, min(tm_expert, cap))
    return _combine(x, gt, pos[:, :, :nt], off[:, :, :nt], tot[:, :, :nt], ye)


DFT_RADIX = 128


def _dot3(a_hi, a_lo, b_hi, b_lo):
    return _dot(a_hi, b_hi) + _dot(a_lo, b_hi) + _dot(a_hi, b_lo)


def _cos_sin(n_rows, n_cols, period):
    k = (np.arange(n_rows)[:, None] * np.arange(n_cols)[None, :]) % period
    ang = 2.0 * np.pi * k.astype(np.float64) / period
    return np.cos(ang), np.sin(ang)


def _const_split(m):
    return _split_bf16(jnp.asarray(m, F32))


def _chan_dft_kernel(u_ref, mh_ref, ml_ref, xr_ref, xi_ref):
    uh, ul = _split_bf16(u_ref[0])
    y = _dot3(uh, ul, mh_ref[...], ml_ref[...])
    c = xr_ref.shape[-1]
    xr_ref[0] = y[:, :c]
    xi_ref[0] = y[:, c:]


def _chan_dft(u, tm):
    B, N, C = u.shape
    cc, sc = _cos_sin(C, C, C)
    mh, ml = _const_split(np.concatenate([cc, -sc], axis=1))
    row = lambda b, i: (b, i, 0)
    const2 = lambda b, i: (0, 0)
    return pl.pallas_call(
        _chan_dft_kernel, grid=(B, N // tm),
        in_specs=[pl.BlockSpec((1, tm, C), row), pl.BlockSpec((C, 2 * C), const2), pl.BlockSpec((C, 2 * C), const2)],
        out_specs=[pl.BlockSpec((1, tm, C), row)] * 2,
        out_shape=[jax.ShapeDtypeStruct((B, N, C), F32)] * 2,
        compiler_params=_params("parallel", "parallel"), name="chan_dft",
    )(u, mh, ml)


def _dft_left_kernel(xr_ref, xi_ref, mh_ref, ml_ref, o_ref, *, scale):
    xh, xl = _split_bf16(jnp.concatenate([xr_ref[0], xi_ref[0]], axis=0))
    o_ref[0] = _dot3(mh_ref[...], ml_ref[...], xh, xl) * scale


def _dft_left(xr, xi, m, scale, tn):
    B, K, cols = xr.shape
    R = m.shape[0]
    mh, ml = _const_split(m)
    col = lambda b, j: (b, 0, j)
    const2 = lambda b, j: (0, 0)
    return pl.pallas_call(
        functools.partial(_dft_left_kernel, scale=scale), grid=(B, cols // tn),
        in_specs=[pl.BlockSpec((1, K, tn), col), pl.BlockSpec((1, K, tn), col),
                  pl.BlockSpec((R, 2 * K), const2), pl.BlockSpec((R, 2 * K), const2)],
        out_specs=pl.BlockSpec((1, R, tn), col),
        out_shape=jax.ShapeDtypeStruct((B, R, cols), F32),
        compiler_params=_params("parallel", "parallel"), name="dft_left",
    )(xr, xi, mh, ml)


DFT_N2_PER_STEP = 8


def _dft_outer_kernel(xr_ref, xi_ref, mh_ref, ml_ref, o_ref):
    n1 = xr_ref.shape[1]
    for s in range(DFT_N2_PER_STEP):
        xh, xl = _split_bf16(jnp.concatenate([xr_ref[0, :, s, :], xi_ref[0, :, s, :]], axis=0))
        y = _dot3(mh_ref[...], ml_ref[...], xh, xl)
        o_ref[0, 0, :, s, :] = y[:n1]
        o_ref[0, 1, :, s, :] = y[n1:]


def _dft_outer(xr, xi, m):
    B, n1, n2, C = xr.shape
    mh, ml = _const_split(m)
    blk = lambda b, j: (b, 0, j, 0)
    const2 = lambda b, j: (0, 0)
    return pl.pallas_call(
        _dft_outer_kernel, grid=(B, n2 // DFT_N2_PER_STEP),
        in_specs=[pl.BlockSpec((1, n1, DFT_N2_PER_STEP, C), blk), pl.BlockSpec((1, n1, DFT_N2_PER_STEP, C), blk),
                  pl.BlockSpec((2 * n1, 2 * n1), const2), pl.BlockSpec((2 * n1, 2 * n1), const2)],
        out_specs=pl.BlockSpec((1, 2, n1, DFT_N2_PER_STEP, C), lambda b, j: (b, 0, 0, j, 0)),
        out_shape=jax.ShapeDtypeStruct((B, 2, n1, n2, C), F32),
        compiler_params=_params("parallel", "parallel"), name="dft_outer",
    )(xr, xi, mh, ml)


DFT_K1_PER_STEP = 8


def _dft_twiddle_kernel(y_ref, tc_ref, ts_ref, dh_ref, dl_ref, o_ref, *, scale):
    tc = tc_ref[0]
    ts = ts_ref[0]
    for kk in range(DFT_K1_PER_STEP):
        yr = y_ref[0, 0, kk]
        yi = y_ref[0, 1, kk]
        c, s = tc[:, kk:kk + 1], ts[:, kk:kk + 1]
        zh, zl = _split_bf16(jnp.concatenate([yr * c + yi * s, yi * c - yr * s], axis=0))
        o_ref[0, :, kk, :] = _dot3(dh_ref[...], dl_ref[...], zh, zl) * scale


def _fourier_real(u):
    B, N, C = u.shape
    scale = 1.0 / math.sqrt(N * C)
    xr, xi = _chan_dft(u, min(N, 512))
    if N % (DFT_RADIX * DFT_K1_PER_STEP) != 0:
        assert N <= 1024
        cn, sn = _cos_sin(N, N, N)
        return _dft_left(xr, xi, np.concatenate([cn, sn], axis=1), scale, C)
    n1, n2 = N // DFT_RADIX, DFT_RADIX
    c1, s1 = _cos_sin(n1, n1, n1)
    m1 = np.block([[c1, s1], [-s1, c1]])
    yy = _dft_outer(xr.reshape(B, n1, n2, C), xi.reshape(B, n1, n2, C), m1)
    tcos, tsin = _cos_sin(n1, n2, N)
    grp = lambda t: jnp.asarray(t.reshape(n1 // DFT_K1_PER_STEP, DFT_K1_PER_STEP, n2).transpose(0, 2, 1), F32)
    c2, s2 = _cos_sin(n2, n2, n2)
    dh, dl = _const_split(np.concatenate([c2, s2], axis=1))
    fp = pl.pallas_call(
        functools.partial(_dft_twiddle_kernel, scale=scale), grid=(B, n1 // DFT_K1_PER_STEP),
        in_specs=[
            pl.BlockSpec((1, 2, DFT_K1_PER_STEP, n2, C), lambda b, g: (b, 0, g, 0, 0)),
            pl.BlockSpec((1, n2, DFT_K1_PER_STEP), lambda b, g: (g, 0, 0)),
            pl.BlockSpec((1, n2, DFT_K1_PER_STEP), lambda b, g: (g, 0, 0)),
            pl.BlockSpec((n2, 2 * n2), lambda b, g: (0, 0)), pl.BlockSpec((n2, 2 * n2), lambda b, g: (0, 0)),
        ],
        out_specs=pl.BlockSpec((1, n2, DFT_K1_PER_STEP, C), lambda b, g: (b, 0, g, 0)),
        out_shape=jax.ShapeDtypeStruct((B, n2, n1, C), F32),
        compiler_params=_params("parallel", "parallel"), name="dft_twiddle",
    )(yy, grp(tcos), grp(tsin), dh, dl)
    return fp.reshape(B, N, C)


def _ctx_attn_kernel(q_ref, kv_ref, o_ref):
    q = q_ref[0]
    kv = kv_ref[0]
    for h in range(NA_HEADS):
        ks = slice(h * NA_HEAD_DIM, (h + 1) * NA_HEAD_DIM)
        vs = slice(NA_W + h * NA_HEAD_DIM, NA_W + (h + 1) * NA_HEAD_DIM)
        s = _dot_nt(q[:, ks], kv[:, ks])
        p = jnp.exp(s - jnp.max(s, axis=-1, keepdims=True))
        l = jnp.sum(p, axis=-1, keepdims=True)
        o_ref[0, :, ks] = _dot(p.astype(BF16), kv[:, vs]) / l


def _ctx_dense_attn(q, kv):
    B, Q, _ = q.shape
    return pl.pallas_call(
        _ctx_attn_kernel, grid=(B,),
        in_specs=[pl.BlockSpec((1, Q, NA_W), lambda b: (b, 0, 0)),
                  pl.BlockSpec((1, kv.shape[1], 2 * NA_W), lambda b: (b, 0, 0))],
        out_specs=pl.BlockSpec((1, Q, NA_W), lambda b: (b, 0, 0)),
        out_shape=jax.ShapeDtypeStruct((B, Q, NA_W), F32),
        compiler_params=_params("parallel"), name="ctx_attn",
    )(q, kv)


ADA_ROWS = 8
ADA_TN = 512


def _ada_kernel(c_ref, w_ref, b_ref, o_ref):
    c = c_ref[...]
    sh, sl = _split_bf16(c * jax.nn.sigmoid(c))
    wh, wl = _split_bf16(w_ref[0])
    o_ref[...] = _dot3(sh, sl, wh, wl) + b_ref[0]


def _ada_mod(c_rows, w_ada, b_ada, layer):
    R, D = c_rows.shape
    N = w_ada.shape[-1]
    return pl.pallas_call(
        _ada_kernel, grid=(N // ADA_TN,),
        in_specs=[pl.BlockSpec((R, D), lambda j: (0, 0)),
                  pl.BlockSpec((1, D, ADA_TN), lambda j: (layer, 0, j)),
                  pl.BlockSpec((1, 1, ADA_TN), lambda j: (layer, 0, j))],
        out_specs=pl.BlockSpec((R, ADA_TN), lambda j: (0, j)),
        out_shape=jax.ShapeDtypeStruct((R, N), F32),
        compiler_params=_params("parallel"), name="ada_mod",
    )(c_rows, w_ada, b_ada[:, None, :])


def _tile(g, n):
    return jnp.tile(g.astype(F32), n)[None, :]


def kernel(x, c, ctx, c_ctx, w_ada, b_ada, g_mix, g_ffn, w_in, na_q_g, na_k_g, na_rpb, df_q_g, df_k_g,
           df_lambda, df_subln_g, pool_w, pool_scale, fnet_w, w_branch, w_out, w_router, w_gate_e, w_up_e,
           w_down_e):
    B, T, D = x.shape
    ctx_len = ctx.shape[1]
    rope = _rope_tables(T)
    assert B + 1 <= ADA_ROWS
    c_rows = jnp.concatenate([c, c_ctx[None, :], jnp.zeros((ADA_ROWS - B - 1, D), F32)], axis=0)
    tq = min(DIFF_TQ, T)
    tk = next(c for c in DIFF_TK_CHOICES if (T + ctx_len) % c == 0)
    for l in range(DEPTH):
        last = l == DEPTH - 1
        lam_init = 0.8 - 0.6 * math.exp(-0.3 * l)
        mods = _ada_mod(c_rows, w_ada, b_ada, l)
        sh1, sc1, gt1, sh2, sc2, gt2 = [m[:, None, :] for m in jnp.split(mods[:B], 6, axis=-1)]
        bc = lambda m: jnp.broadcast_to(m[None, None, :], (B, 1, D))
        csh1, csc1, cgt1, csh2, csc2, cgt2 = [bc(m) for m in jnp.split(mods[B], 6, axis=-1)]

        w_bf = w_in[l].astype(BF16)
        gq, gk = _tile(na_q_g[l], NA_HEADS), _tile(na_k_g[l], NA_HEADS)
        gdq, gdk = _tile(df_q_g[l], 2 * DIFF_HEADS), _tile(df_k_g[l], 2 * DIFF_HEADS)
        gmix = g_mix[l][None, :]
        gffn = g_ffn[l][None, :]
        subg = df_subln_g[l][:, None].astype(F32)
        wbr = w_branch[l].astype(BF16)
        wf = fnet_w[l].astype(BF16)
        wpool = jax.scipy.linalg.block_diag(*[pool_w[l, gi] for gi in range(len(POOL_WINDOWS))]).astype(BF16)
        pscale = pool_scale[l][None, :].astype(F32)
        wo = w_out[l].astype(BF16)

        naq, dfq, pool_in, fnet_in, gate, nakv, dfk, dfv = _inproj(
            x, gmix, sh1, sc1, w_bf, gq, gk, gdq, gdk, rope, min(INPROJ_ROWS, T))
        (cnaq, cdfq, cpool_in, cfnet_in, cgate, cnakv, cdfk, cdfv) = _inproj(
            ctx, gmix, csh1, csc1, w_bf, gq, gk, gdq, gdk, None, ctx_len)

        y_na = _na_latent(naq, nakv, cnakv, na_rpb[l])
        lamp = df_lambda[l].astype(F32)
        y_df = _diff_attn(dfq, jnp.concatenate([dfk, cdfk], axis=1), jnp.concatenate([dfv, cdfv], axis=2),
                          df_q_g[l], df_k_g[l], lamp, subg, lam_init, tq, tk)
        f_real = _fourier_real(fnet_in)
        x_new = _merge(x, gt1, y_na, y_df, pool_in, f_real, gate, wbr, wpool, pscale, wf, wo, MERGE_ROWS)

        if not last:
            yc_na = _ctx_dense_attn(cnaq, cnakv)
            yc_df = _diff_attn(cdfq, cdfk, cdfv, df_q_g[l], df_k_g[l], lamp, subg, lam_init, ctx_len, ctx_len)
            fc_real = _fourier_real(cfnet_in)
            ctx_new = _merge(ctx, cgt1, yc_na, yc_df, cpool_in, fc_real, cgate, wbr, wpool, pscale, wf, wo,
                             ctx_len)

        x = x_new
        x = _ec_ffn(x, gt2, gffn, sh2, sc2, w_router[l], w_gate_e, w_up_e, w_down_e, l, min(ROUTER_ROWS, T),
                    EXPERT_ROWS)
        if not last:
            ctx = ctx_new
            ctx = _ec_ffn(ctx, cgt2, gffn, csh2, csc2, w_router[l], w_gate_e, w_up_e, w_down_e, l, ctx_len,
                          EXPERT_ROWS)
    return x
```
